```python
import jax, jax.numpy as jnp
from jax import lax
import numpy as np

D_MODEL = 1024
BATCH = 4
SEQ = 8192
DEPTH = 2
DEC_BATCH = 32
DEC_SEQ = 16
PAST_LEN = 2048

CHUNK = 64
N_RET = (DEPTH + 1) // 2
N_ATT = DEPTH // 2
RET_HEADS = 4
RET_DK = D_MODEL // RET_HEADS
RET_DV = 2 * D_MODEL // RET_HEADS
ROPE_BASE = 10000.0
ATT_HEADS = 16
ATT_DH = D_MODEL // ATT_HEADS
BAND_CHUNKS = 8
REL_CLIP = 256
N_GROUPS = 4
EXP_PER_GROUP = 8
N_EXPERTS = N_GROUPS * EXP_PER_GROUP
TOP_K = 2
D_EXPERT = 512
PLE_DIM = 256
DN_ALPHA = (2 * DEPTH) ** 0.25
DN_BETA = (8 * DEPTH) ** -0.25
LN_EPS = 1e-5
NEG_INF = -1e30

kernel_name = 'hybrid_retention_chunkband_hmoe_step'


def layer_norm(x, g, b):
    x32 = x.astype(jnp.float32)
    mu = jnp.mean(x32, -1, keepdims=True)
    var = jnp.mean(jnp.square(x32 - mu), -1, keepdims=True)
    y = (x32 - mu) * lax.rsqrt(var + LN_EPS) * g.astype(jnp.float32) + b.astype(jnp.float32)
    return y.astype(x.dtype)


def rotary(t, pos):
    half = t.shape[-1] // 2
    inv_freq = ROPE_BASE ** (-jnp.arange(half, dtype=jnp.float32) / half)
    ang = pos.astype(jnp.float32)[:, None] * inv_freq[None, :]
    cos = jnp.cos(ang)[:, None, :]
    sin = jnp.sin(ang)[:, None, :]
    t32 = t.astype(jnp.float32)
    t1, t2 = t32[..., :half], t32[..., half:]
    return jnp.concatenate([t1 * cos - t2 * sin, t1 * sin + t2 * cos], axis=-1)


def ret_log_gamma():
    return jnp.log(1.0 - 2.0 ** (-5.0 - jnp.arange(RET_HEADS, dtype=jnp.float32)))


def retention_block(S, q, k, v):
    L = q.shape[1]
    lg = ret_log_gamma()
    i = jnp.arange(L, dtype=jnp.float32)
    diff = i[:, None] - i[None, :]
    dmask = jnp.where(diff >= 0, jnp.exp(lg[:, None, None] * jnp.maximum(diff, 0.0)), 0.0)
    scores = jnp.einsum('blhk,bmhk->bhlm', q, k) * dmask
    inner = jnp.einsum('bhlm,bmhv->blhv', scores, v)
    q_dec = q * jnp.exp(lg[None, :] * (i[:, None] + 1.0))[None, :, :, None]
    cross = jnp.einsum('blhk,bhkv->blhv', q_dec, S)
    k_dec = k * jnp.exp(lg[None, :] * (L - 1.0 - i)[:, None])[None, :, :, None]
    S_new = jnp.exp(lg * L)[None, :, None, None] * S + jnp.einsum('blhk,blhv->bhkv', k_dec, v)
    return S_new, inner + cross


def retention_mixer(x, S0, pos, w_in, gn_g, w_o):
    B, L, _ = x.shape
    hk, hv = RET_HEADS * RET_DK, RET_HEADS * RET_DV
    q, k, v, g = jnp.split(x @ w_in, [hk, 2 * hk, 2 * hk + hv], axis=-1)
    q = rotary(q.reshape(B, L, RET_HEADS, RET_DK), pos)
    k = rotary(k.reshape(B, L, RET_HEADS, RET_DK), pos) * RET_DK ** -0.5
    v = v.reshape(B, L, RET_HEADS, RET_DV).astype(jnp.float32)
    blk = min(L, CHUNK)
    nb = L // blk

    def to_blocks(t):
        return jnp.moveaxis(t.reshape(B, nb, blk, t.shape[2], t.shape[3]), 1, 0)

    S, o = lax.scan(lambda s, qkv: retention_block(s, qkv[0], qkv[1], qkv[2]),
                    S0.astype(jnp.float32), (to_blocks(q), to_blocks(k), to_blocks(v)))
    o = jnp.moveaxis(o, 0, 1).reshape(B, L, RET_HEADS, RET_DV)
    mu = jnp.mean(o, -1, keepdims=True)
    var = jnp.mean(jnp.square(o - mu), -1, keepdims=True)
    o = ((o - mu) * lax.rsqrt(var + LN_EPS)).reshape(B, L, hv) * gn_g.astype(jnp.float32)
    out = (jax.nn.silu(g) * o.astype(x.dtype)) @ w_o
    return out, S


def rel_index(d):
    return jnp.clip(d, -REL_CLIP, REL_CLIP) + REL_CLIP


def attend(q, k, v, bias, valid):
    s = jnp.einsum('bqhd,bkhd->bhqk', q, k).astype(jnp.float32) * ATT_DH ** -0.5
    s = s + bias.astype(jnp.float32)[None]
    if valid is not None:
        s = jnp.where(valid, s, NEG_INF)
    pr = jax.nn.softmax(s, axis=-1).astype(v.dtype)
    return jnp.einsum('bhqk,bkhd->bqhd', pr, v)


def band_attention_prompt(q, k, v, rel_table):
    B, S, H, dh = q.shape
    nc = S // CHUNK
    pad = BAND_CHUNKS * CHUNK
    band = pad + CHUNK
    kp = jnp.pad(k, ((0, 0), (pad, 0), (0, 0), (0, 0)))
    vp = jnp.pad(v, ((0, 0), (pad, 0), (0, 0), (0, 0)))
    qc = jnp.moveaxis(q.reshape(B, nc, CHUNK, H, dh), 1, 0)
    i = jnp.arange(CHUNK)
    j = jnp.arange(band)
    bias = rel_table[:, rel_index(i[:, None] + pad - j[None, :])]

    def one_chunk(args):
        c, qb = args
        kb = lax.dynamic_slice_in_dim(kp, c * CHUNK, band, axis=1)
        vb = lax.dynamic_slice_in_dim(vp, c * CHUNK, band, axis=1)
        valid = (c * CHUNK - pad + j) >= 0
        return attend(qb, kb, vb, bias, valid)

    out = lax.map(one_chunk, (jnp.arange(nc), qc))
    return jnp.moveaxis(out, 0, 1).reshape(B, S, H, dh)


def chunk_attention_mixer(x, k_cache, v_cache, pos, w_qkv, rel_table, w_o):
    B, L, _ = x.shape
    q, k, v = jnp.split(x @ w_qkv, 3, axis=-1)
    q = q.reshape(B, L, ATT_HEADS, ATT_DH)
    k = k.reshape(B, L, ATT_HEADS, ATT_DH)
    v = v.reshape(B, L, ATT_HEADS, ATT_DH)
    if k_cache is None:
        o = band_attention_prompt(q, k, v, rel_table)
        keep = min(BAND_CHUNKS * CHUNK, L)
        k_rows, v_rows = k[:, L - keep:], v[:, L - keep:]
    else:
        A = k_cache.shape[1]
        kk = jnp.concatenate([k_cache.astype(k.dtype), k], axis=1)
        vv = jnp.concatenate([v_cache.astype(v.dtype), v], axis=1)
        kpos = jnp.concatenate([pos[0] - A + jnp.arange(A), pos])
        bias = rel_table[:, rel_index(pos[:, None] - kpos[None, :])]
        o = attend(q, kk, vv, bias, None)
        k_rows, v_rows = k, v
    out = o.reshape(B, L, D_MODEL) @ w_o
    return out, k_rows, v_rows


def hier_moe(x, w_grp, b_grp, w_exp, b_exp, w_gate, w_up, w_down):
    B, L, D = x.shape
    xt = x.reshape(B * L, D)
    glog = (xt @ w_grp).astype(jnp.float32) + b_grp.astype(jnp.float32)
    gprob = jax.nn.softmax(glog, axis=-1)
    gsel = jnp.argmax(glog, axis=-1)
    gp = jnp.take_along_axis(gprob, gsel[:, None], axis=-1)[:, 0]
    elog_all = jnp.einsum('td,gde->tge', xt, w_exp).astype(jnp.float32) + b_exp.astype(jnp.float32)
    elog = jnp.take_along_axis(elog_all, gsel[:, None, None], axis=1)[:, 0]
    top_v, top_i = lax.top_k(elog, TOP_K)
    top_w = jax.nn.softmax(top_v, axis=-1) * gp[:, None]
    eid = gsel[:, None] * EXP_PER_GROUP + top_i
    gates = jnp.einsum('tk,tke->te', top_w, jax.nn.one_hot(eid, N_EXPERTS, dtype=jnp.float32))
    gates = gates.astype(xt.dtype)
    out = jnp.zeros_like(xt)
    for e in range(N_EXPERTS):
        h = jax.nn.silu(xt @ w_gate[e]) * (xt @ w_up[e])
        out = out + gates[:, e:e + 1] * (h @ w_down[e])
    return out.reshape(B, L, D)


def trunk(x, p, pos, state_in, k_in, v_in, ret_w_in, ret_gn_g, ret_w_o, att_w_qkv, att_rel_bias,
          att_w_o, ln1_g, ln1_b, ln2_g, ln2_b, moe_w_grp, moe_b_grp, moe_w_exp, moe_b_exp,
          moe_w_gate, moe_w_up, moe_w_down, ple_w_proj, ple_w_gate):
    B = x.shape[0]
    states, k_rows, v_rows = [], [], []
    for i in range(DEPTH):
        j = i // 2
        if i % 2 == 0:
            if state_in is None:
                S0 = jnp.zeros((B, RET_HEADS, RET_DK, RET_DV), jnp.float32)
            else:
                S0 = state_in[j]
            mix, S = retention_mixer(x, S0, pos, ret_w_in[j], ret_gn_g[j], ret_w_o[j])
            states.append(S)
        else:
            kc = None if k_in is None else k_in[j]
            vc = None if v_in is None else v_in[j]
            mix, kr, vr = chunk_attention_mixer(x, kc, vc, pos, att_w_qkv[j], att_rel_bias[j], att_w_o[j])
            k_rows.append(kr)
            v_rows.append(vr)
        h = layer_norm(DN_ALPHA * x + mix, ln1_g[i], ln1_b[i])
        ffn = hier_moe(h, moe_w_grp[i], moe_b_grp[i], moe_w_exp[i], moe_b_exp[i],
                       moe_w_gate[i], moe_w_up[i], moe_w_down[i])
        h = layer_norm(DN_ALPHA * h + ffn, ln2_g[i], ln2_b[i])
        x = h + jax.nn.sigmoid(h @ ple_w_gate[i]) * (p[i] @ ple_w_proj[i])
    return x, jnp.stack(states), jnp.stack(k_rows), jnp.stack(v_rows)


def setup_inputs(seed: int = 0) -> dict:
    key = jax.random.key(seed)
    ks = jax.random.split(key, 32)
    f32 = jnp.float32

    def nrm(k, shape, scale):
        return jax.random.normal(k, shape, f32) * scale

    att_rows = min(BAND_CHUNKS * CHUNK, PAST_LEN)
    hk, hv = RET_HEADS * RET_DK, RET_HEADS * RET_DV
    return {
        'x_prompt': nrm(ks[0], (BATCH, SEQ, D_MODEL), 1.0),
        'x_sample': nrm(ks[1], (DEC_BATCH, DEC_SEQ, D_MODEL), 1.0),
        'p_prompt': nrm(ks[2], (DEPTH, BATCH, SEQ, PLE_DIM), 1.0),
        'p_sample': nrm(ks[3], (DEPTH, DEC_BATCH, DEC_SEQ, PLE_DIM), 1.0),
        'state_ret': nrm(ks[4], (N_RET, DEC_BATCH, RET_HEADS, RET_DK, RET_DV), 0.5),
        'cache_att_k': nrm(ks[5], (N_ATT, DEC_BATCH, att_rows, ATT_HEADS, ATT_DH), 1.0),
        'cache_att_v': nrm(ks[6], (N_ATT, DEC_BATCH, att_rows, ATT_HEADS, ATT_DH), 1.0),
        'ret_w_in': nrm(ks[7], (N_RET, D_MODEL, 2 * hk + 2 * hv), D_MODEL ** -0.5),
        'ret_gn_g': 1.0 + nrm(ks[8], (N_RET, hv), 0.02),
        'ret_w_o': nrm(ks[9], (N_RET, hv, D_MODEL), hv ** -0.5 * DN_BETA),
        'att_w_qkv': nrm(ks[10], (N_ATT, D_MODEL, 3 * D_MODEL), D_MODEL ** -0.5),
        'att_rel_bias': nrm(ks[11], (N_ATT, ATT_HEADS, 2 * REL_CLIP + 1), 0.5),
        'att_w_o': nrm(ks[12], (N_ATT, D_MODEL, D_MODEL), D_MODEL ** -0.5 * DN_BETA),
        'ln1_g': 1.0 + nrm(ks[13], (DEPTH, D_MODEL), 0.02),
        'ln1_b': nrm(ks[14], (DEPTH, D_MODEL), 0.02),
        'ln2_g': 1.0 + nrm(ks[15], (DEPTH, D_MODEL), 0.02),
        'ln2_b': nrm(ks[16], (DEPTH, D_MODEL), 0.02),
        'moe_w_grp': nrm(ks[17], (DEPTH, D_MODEL, N_GROUPS), D_MODEL ** -0.5),
        'moe_b_grp': nrm(ks[18], (DEPTH, N_GROUPS), 0.01),
        'moe_w_exp': nrm(ks[19], (DEPTH, N_GROUPS, D_MODEL, EXP_PER_GROUP), D_MODEL ** -0.5),
        'moe_b_exp': nrm(ks[20], (DEPTH, N_GROUPS, EXP_PER_GROUP), 0.01),
        'moe_w_gate': nrm(ks[21], (DEPTH, N_EXPERTS, D_MODEL, D_EXPERT), D_MODEL ** -0.5),
        'moe_w_up': nrm(ks[22], (DEPTH, N_EXPERTS, D_MODEL, D_EXPERT), D_MODEL ** -0.5),
        'moe_w_down': nrm(ks[23], (DEPTH, N_EXPERTS, D_EXPERT, D_MODEL), D_EXPERT ** -0.5 * DN_BETA),
        'ple_w_proj': nrm(ks[24], (DEPTH, PLE_DIM, D_MODEL), PLE_DIM ** -0.5),
        'ple_w_gate': nrm(ks[25], (DEPTH, D_MODEL, D_MODEL), D_MODEL ** -0.5),
    }


def reference(x_prompt, x_sample, p_prompt, p_sample, state_ret, cache_att_k, cache_att_v,
              ret_w_in, ret_gn_g, ret_w_o, att_w_qkv, att_rel_bias, att_w_o,
              ln1_g, ln1_b, ln2_g, ln2_b, moe_w_grp, moe_b_grp, moe_w_exp, moe_b_exp,
              moe_w_gate, moe_w_up, moe_w_down, ple_w_proj, ple_w_gate):
    pos_prompt = jnp.arange(x_prompt.shape[1], dtype=jnp.int32)
    pos_sample = PAST_LEN + jnp.arange(x_sample.shape[1], dtype=jnp.int32)
    y_prompt, state_ret_prompt, k_rows_prompt, v_rows_prompt = trunk(
        x_prompt, p_prompt, pos_prompt, None, None, None,
        ret_w_in, ret_gn_g, ret_w_o, att_w_qkv, att_rel_bias, att_w_o,
        ln1_g, ln1_b, ln2_g, ln2_b, moe_w_grp, moe_b_grp, moe_w_exp, moe_b_exp,
        moe_w_gate, moe_w_up, moe_w_down, ple_w_proj, ple_w_gate)
    y_sample, state_ret_sample, k_rows_sample, v_rows_sample = trunk(
        x_sample, p_sample, pos_sample, state_ret, cache_att_k, cache_att_v,
        ret_w_in, ret_gn_g, ret_w_o, att_w_qkv, att_rel_bias, att_w_o,
        ln1_g, ln1_b, ln2_g, ln2_b, moe_w_grp, moe_b_grp, moe_w_exp, moe_b_exp,
        moe_w_gate, moe_w_up, moe_w_down, ple_w_proj, ple_w_gate)
    return (y_prompt, y_sample, state_ret_prompt, state_ret_sample,
            k_rows_prompt, v_rows_prompt, k_rows_sample, v_rows_sample)
```

```python
import functools
import math

import numpy as np
import jax
import jax.numpy as jnp
from jax import lax
from jax.experimental import pallas as pl
from jax.experimental.pallas import tpu as pltpu
from jax.experimental.pallas import tpu_sc as plsc

CHUNK = 64
PAST_LEN = 2048
RET_HEADS = 4
ROPE_BASE = 10000.0
ATT_HEADS = 16
BAND_CHUNKS = 8
REL_CLIP = 256
N_GROUPS = 4
EXP_PER_GROUP = 8
N_EXPERTS = N_GROUPS * EXP_PER_GROUP
LN_EPS = 1e-5
NEG_INF = -1e30

LANES = 128
SC_WORKERS = 32
SC_ROWS_PER_CHUNK = 32
VMEM_LIMIT = 56 * 1024 * 1024

RET_BLOCK = 256
ATT_BLOCK = 4 * CHUNK
TOKEN_TILE = 256
GEMM_TILE = 256

F32 = jnp.float32
BF16 = jnp.bfloat16


def _dot(a, b):
    return jnp.dot(a, b, preferred_element_type=F32)


def _dot_nt(a, b):
    return lax.dot_general(a, b, (((1,), (1,)), ((), ())), preferred_element_type=F32)


def _dot_tn(a, b):
    return lax.dot_general(a, b, (((0,), (0,)), ((), ())), preferred_element_type=F32)


def _layer_norm(x, g, b):
    mu = jnp.mean(x, axis=-1, keepdims=True)
    xc = x - mu
    var = jnp.mean(xc * xc, axis=-1, keepdims=True)
    return xc * lax.rsqrt(var + LN_EPS) * g + b


def _router_logits(h, wr_hi_ref, wr_lo_ref, br_ref):
    h_hi = h.astype(BF16)
    h_lo = (h - h_hi.astype(F32)).astype(BF16)
    w_hi = wr_hi_ref[...]
    return _dot(h_hi, w_hi) + _dot(h_lo, w_hi) + _dot(h_hi, wr_lo_ref[...]) + br_ref[...]


def _const_spec(shape):
    nd = len(shape)
    return pl.BlockSpec(shape, lambda *_: (0,) * nd, pipeline_mode=pl.Buffered(1))


def _ret_log_gamma():
    h = np.arange(RET_HEADS, dtype=np.float32)
    return np.log(np.float32(1.0) - np.float32(2.0) ** (np.float32(-5.0) - h)).astype(np.float32)


def _ret_kernel(*refs, nb, blk, nblk, has_state, alpha):
    if has_state:
        (x_ref, cos_ref, sin_ref, dmask_ref, win_ref, wo_ref, gn_ref, lng_ref, lnb_ref,
         wrh_ref, wrl_ref, br_ref, s_in_ref, h_ref, lgt_ref, s_out_ref, gated_ref) = refs
    else:
        (x_ref, cos_ref, sin_ref, dmask_ref, win_ref, wo_ref, gn_ref, lng_ref, lnb_ref,
         wrh_ref, wrl_ref, br_ref, h_ref, lgt_ref, s_out_ref, gated_ref) = refs
        s_in_ref = None
    heads = RET_HEADS
    d_model = x_ref.shape[1]
    dk = d_model // heads
    dv = 2 * d_model // heads
    hk, hv = heads * dk, heads * dv
    half = dk // 2
    lg = _ret_log_gamma()

    x = x_ref[...]
    xb = x.astype(BF16)
    q_all = _dot(xb, win_ref[:, 0:hk])
    k_all = _dot(xb, win_ref[:, hk:2 * hk])
    v_all = _dot(xb, win_ref[:, 2 * hk:2 * hk + hv])
    g_all = _dot(xb, win_ref[:, 2 * hk + hv:2 * hk + 2 * hv])
    cos = cos_ref[...]
    sin = sin_ref[...]
    rowf = lax.broadcasted_iota(jnp.int32, (blk, 1), 0).astype(F32)

    def rot(t):
        t1, t2 = t[:, :half], t[:, half:]
        return jnp.concatenate([t1 * cos - t2 * sin, t1 * sin + t2 * cos], axis=1)

    if not has_state:
        @pl.when(pl.program_id(1) == 0)
        def _():
            s_out_ref[...] = jnp.zeros(s_out_ref.shape, F32)

    s_prev_ref = s_in_ref if has_state else s_out_ref
    for s in range(nb):
        r0 = s * blk
        for h in range(heads):
            lgh = float(lg[h])
            q = rot(q_all[r0:r0 + blk, h * dk:(h + 1) * dk])
            k = rot(k_all[r0:r0 + blk, h * dk:(h + 1) * dk]) * (dk ** -0.5)
            v = v_all[r0:r0 + blk, h * dv:(h + 1) * dv]
            g = g_all[r0:r0 + blk, h * dv:(h + 1) * dv]
            vb = v.astype(BF16)
            scores = _dot_nt(q.astype(BF16), k.astype(BF16)) * dmask_ref[h]
            inner = _dot(scores.astype(BF16), vb)
            s_prev = s_prev_ref[s, h]
            q_dec = q * jnp.exp(lgh * (rowf + 1.0))
            cross = _dot(q_dec.astype(BF16), s_prev.astype(BF16))
            k_dec = k * jnp.exp(lgh * (float(blk - 1) - rowf))
            s_out_ref[s, h] = math.exp(lgh * blk) * s_prev + _dot_tn(k_dec.astype(BF16), vb)
            o = inner + cross
            mu = jnp.mean(o, axis=-1, keepdims=True)
            oc = o - mu
            var = jnp.mean(oc * oc, axis=-1, keepdims=True)
            on = oc * lax.rsqrt(var + LN_EPS) * gn_ref[:, h * dv:(h + 1) * dv]
            gated_ref[r0:r0 + blk, h * dv:(h + 1) * dv] = (jax.nn.silu(g) * on).astype(BF16)

    mix = _dot(gated_ref[...], wo_ref[...])
    hh = _layer_norm(alpha * x + mix, lng_ref[...], lnb_ref[...])
    h_ref[...] = hh
    lgt_ref[...] = _router_logits(hh, wrh_ref, wrl_ref, br_ref)


def _ret_mixer(x2d, row_block0, n_seq, seq_len, pos0, state_in, w_in, w_o, gn_g, ln_g, ln_b,
               wr_hi, wr_lo, br, alpha, nb, blk):
    d_model = x2d.shape[1]
    heads = RET_HEADS
    dk, dv = d_model // heads, 2 * d_model // heads
    half = dk // 2
    nblk = seq_len // blk
    has_state = state_in is not None
    assert seq_len % blk == 0 and n_seq % nb == 0
    assert (not has_state) or nblk == 1
    assert nb == 1 or nblk == 1
    rows = nb * blk
    n_tok = n_seq * seq_len

    pos = (pos0 + jnp.arange(seq_len, dtype=jnp.int32)).astype(F32)
    inv_freq = ROPE_BASE ** (-jnp.arange(half, dtype=F32) / half)
    ang = pos[:, None] * inv_freq[None, :]
    cos, sin = jnp.cos(ang), jnp.sin(ang)
    lg = jnp.asarray(_ret_log_gamma())
    ii = jnp.arange(blk, dtype=F32)
    diff = ii[:, None] - ii[None, :]
    dmask = jnp.where(diff >= 0, jnp.exp(lg[:, None, None] * jnp.maximum(diff, 0.0)), 0.0)

    in_specs = [
        pl.BlockSpec((rows, d_model), lambda g, i: (row_block0 + g * nblk + i, 0)),
        pl.BlockSpec((blk, half), lambda g, i: (i, 0)),
        pl.BlockSpec((blk, half), lambda g, i: (i, 0)),
        _const_spec(dmask.shape),
        _const_spec(w_in.shape),
        _const_spec(w_o.shape),
        _const_spec(gn_g.shape),
        _const_spec(ln_g.shape),
        _const_spec(ln_b.shape),
        _const_spec(wr_hi.shape),
        _const_spec(wr_lo.shape),
        _const_spec(br.shape),
    ]
    args = [x2d, cos, sin, dmask, w_in, w_o, gn_g, ln_g, ln_b, wr_hi, wr_lo, br]
    state_spec = pl.BlockSpec((nb, heads, dk, dv), lambda g, i: (g, 0, 0, 0))
    if has_state:
        in_specs.append(state_spec)
        args.append(state_in)
    out_shape = (
        jax.ShapeDtypeStruct((n_tok, d_model), F32),
        jax.ShapeDtypeStruct((n_tok, LANES), F32),
        jax.ShapeDtypeStruct((n_seq, heads, dk, dv), F32),
    )
    out_specs = (
        pl.BlockSpec((rows, d_model), lambda g, i: (g * nblk + i, 0)),
        pl.BlockSpec((rows, LANES), lambda g, i: (g * nblk + i, 0)),
        state_spec,
    )
    return pl.pallas_call(
        functools.partial(_ret_kernel, nb=nb, blk=blk, nblk=nblk, has_state=has_state, alpha=alpha),
        grid=(n_seq // nb, nblk),
        in_specs=in_specs,
        out_specs=out_specs,
        out_shape=out_shape,
        scratch_shapes=[pltpu.VMEM((rows, heads * dv), BF16)],
        compiler_params=pltpu.CompilerParams(
            dimension_semantics=("arbitrary", "arbitrary"), vmem_limit_bytes=VMEM_LIMIT),
        name="ret_mixer_state" if has_state else "ret_mixer",
    )(*args)


_RING = 3


def _att_prompt_kernel(x_ref, wqkv_ref, wo_ref, bias_ref, lng_ref, lnb_ref, wrh_ref, wrl_ref, br_ref,
                       h_ref, lgt_ref, krow_ref, vrow_ref, kring, vring, o_scr, *, blk, alpha):
    i = pl.program_id(1)
    d_model = x_ref.shape[1]
    dh = d_model // ATT_HEADS
    x = x_ref[...]
    xb = x.astype(BF16)
    q = _dot(xb, wqkv_ref[:, 0:d_model]) * (dh ** -0.5)
    k = _dot(xb, wqkv_ref[:, d_model:2 * d_model])
    v = _dot(xb, wqkv_ref[:, 2 * d_model:3 * d_model])
    krow_ref[...] = k
    vrow_ref[...] = v

    @pl.when(i == 0)
    def _():
        kring[...] = jnp.zeros(kring.shape, BF16)
        vring[...] = jnp.zeros(vring.shape, BF16)

    kring[i % _RING] = k.astype(BF16)
    vring[i % _RING] = v.astype(BF16)
    qb = q.astype(BF16)
    lane = lax.broadcasted_iota(jnp.int32, (1, LANES), 1)
    slots = [(i + _RING - d) % _RING for d in range(_RING)]

    for pair in range(ATT_HEADS * dh // LANES):
        c0 = pair * LANES
        q_pair = qb[:, c0:c0 + LANES]
        heads_in_pair = LANES // dh
        o_pair = None
        for sub in range(heads_in_pair):
            hd = pair * heads_in_pair + sub
            in_head = (lane >= sub * dh) & (lane < (sub + 1) * dh)
            qm = jnp.where(in_head, q_pair, jnp.zeros_like(q_pair))
            s_list = []
            for d in range(_RING):
                sc = _dot_nt(qm, kring[slots[d], :, c0:c0 + LANES]) + bias_ref[d, hd]
                if d > 0:
                    sc = jnp.where(i >= d, sc, NEG_INF)
                s_list.append(sc)
            m = s_list[0].max(axis=-1, keepdims=True)
            for d in range(1, _RING):
                m = jnp.maximum(m, s_list[d].max(axis=-1, keepdims=True))
            o = None
            l = None
            for d in range(_RING):
                p = jnp.exp(s_list[d] - m)
                ls = p.sum(axis=-1, keepdims=True)
                l = ls if l is None else l + ls
                od = _dot(p.astype(BF16), vring[slots[d], :, c0:c0 + LANES])
                o = od if o is None else o + od
            o = o * (1.0 / l)
            o_pair = o if o_pair is None else jnp.where(in_head, o, o_pair)
        o_scr[:, c0:c0 + LANES] = o_pair.astype(BF16)

    mix = _dot(o_scr[...], wo_ref[...])
    hh = _layer_norm(alpha * x + mix, lng_ref[...], lnb_ref[...])
    h_ref[...] = hh
    lgt_ref[...] = _router_logits(hh, wrh_ref, wrl_ref, br_ref)


def _att_prompt_bias(rel_table, blk):
    i = jnp.arange(blk)
    out = []
    for d in range(_RING):
        dist = i[:, None] - i[None, :] + d * blk
        b = rel_table[:, jnp.clip(dist, -REL_CLIP, REL_CLIP) + REL_CLIP]
        cd = (i[:, None] // CHUNK) - (i[None, :] // CHUNK) + d * (blk // CHUNK)
        ok = (cd >= 0) & (cd <= BAND_CHUNKS)
        out.append(jnp.where(ok[None], b, NEG_INF))
    return jnp.stack(out).astype(F32)


def _att_prompt(x2d, n_seq, seq_len, w_qkv, w_o, rel_table, ln_g, ln_b, wr_hi, wr_lo, br, alpha):
    d_model = x2d.shape[1]
    blk = ATT_BLOCK
    nblk = seq_len // blk
    keep = min(BAND_CHUNKS * CHUNK, seq_len)
    assert seq_len % blk == 0 and keep % blk == 0
    assert (_RING - 1) * blk >= BAND_CHUNKS * CHUNK
    kb = keep // blk
    n_tok = n_seq * seq_len
    bias = _att_prompt_bias(rel_table, blk)
    row_spec = pl.BlockSpec((None, blk, d_model), lambda b, i: (b, jnp.maximum(i - (nblk - kb), 0), 0))
    return pl.pallas_call(
        functools.partial(_att_prompt_kernel, blk=blk, alpha=alpha),
        grid=(n_seq, nblk),
        in_specs=[
            pl.BlockSpec((blk, d_model), lambda b, i: (b * nblk + i, 0)),
            _const_spec(w_qkv.shape),
            _const_spec(w_o.shape),
            _const_spec(bias.shape),
            _const_spec(ln_g.shape),
            _const_spec(ln_b.shape),
            _const_spec(wr_hi.shape),
            _const_spec(wr_lo.shape),
            _const_spec(br.shape),
        ],
        out_specs=(
            pl.BlockSpec((blk, d_model), lambda b, i: (b * nblk + i, 0)),
            pl.BlockSpec((blk, LANES), lambda b, i: (b * nblk + i, 0)),
            row_spec,
            row_spec,
        ),
        out_shape=(
            jax.ShapeDtypeStruct((n_tok, d_model), F32),
            jax.ShapeDtypeStruct((n_tok, LANES), F32),
            jax.ShapeDtypeStruct((n_seq, keep, d_model), F32),
            jax.ShapeDtypeStruct((n_seq, keep, d_model), F32),
        ),
        scratch_shapes=[
            pltpu.VMEM((_RING, blk, d_model), BF16),
            pltpu.VMEM((_RING, blk, d_model), BF16),
            pltpu.VMEM((blk, d_model), BF16),
        ],
        compiler_params=pltpu.CompilerParams(
            dimension_semantics=("arbitrary", "arbitrary"), vmem_limit_bytes=VMEM_LIMIT),
        name="att_mixer",
    )(x2d, w_qkv, w_o, bias, ln_g, ln_b, wr_hi, wr_lo, br)


def _att_sample_kernel(x_ref, kc_ref, vc_ref, wqkv_ref, wo_ref, bias_c_ref, bias_n_ref, lng_ref, lnb_ref,
                       wrh_ref, wrl_ref, br_ref, h_ref, lgt_ref, krow_ref, vrow_ref, o_scr, *, nb, blk, alpha):
    d_model = x_ref.shape[1]
    heads = ATT_HEADS
    dh = d_model // heads
    x = x_ref[...]
    xb = x.astype(BF16)
    q = _dot(xb, wqkv_ref[:, 0:d_model]) * (dh ** -0.5)
    k = _dot(xb, wqkv_ref[:, d_model:2 * d_model])
    v = _dot(xb, wqkv_ref[:, 2 * d_model:3 * d_model])
    krow_ref[...] = k.reshape(nb, blk, d_model)
    vrow_ref[...] = v.reshape(nb, blk, d_model)
    lane_head = lax.broadcasted_iota(jnp.int32, (heads, 1, d_model), 2) // dh
    head_id = lax.broadcasted_iota(jnp.int32, (heads, 1, d_model), 0)
    head_mask = (lane_head == head_id).astype(F32)

    for s in range(nb):
        r0 = s * blk
        qs = q[r0:r0 + blk]
        q_bd = (qs[None, :, :] * head_mask).reshape(heads * blk, d_model).astype(BF16)
        kn = k[r0:r0 + blk].astype(BF16)
        vn = v[r0:r0 + blk].astype(BF16)
        s_c = _dot_nt(q_bd, kc_ref[s].astype(BF16)) + bias_c_ref[...]
        s_n = _dot_nt(q_bd, kn) + bias_n_ref[...]
        m = jnp.maximum(s_c.max(axis=-1, keepdims=True), s_n.max(axis=-1, keepdims=True))
        p_c = jnp.exp(s_c - m)
        p_n = jnp.exp(s_n - m)
        l = p_c.sum(axis=-1, keepdims=True) + p_n.sum(axis=-1, keepdims=True)
        o_full = _dot(p_c.astype(BF16), vc_ref[s].astype(BF16)) + _dot(p_n.astype(BF16), vn)
        o_full = o_full * (1.0 / l)
        o = (o_full.reshape(heads, blk, d_model) * head_mask).sum(axis=0)
        o_scr[r0:r0 + blk, :] = o.astype(BF16)

    mix = _dot(o_scr[...], wo_ref[...])
    hh = _layer_norm(alpha * x + mix, lng_ref[...], lnb_ref[...])
    h_ref[...] = hh
    lgt_ref[...] = _router_logits(hh, wrh_ref, wrl_ref, br_ref)


def _att_sample(x2d, row_block0, n_seq, seq_len, k_cache, v_cache, w_qkv, w_o, rel_table, ln_g, ln_b,
                wr_hi, wr_lo, br, alpha, nb):
    d_model = x2d.shape[1]
    heads = ATT_HEADS
    blk = seq_len
    n_cache = k_cache.shape[1]
    n_tok = n_seq * seq_len
    rows = nb * blk
    assert n_seq % nb == 0
    i = jnp.arange(blk)
    j = jnp.arange(n_cache)
    dist_c = i[:, None] - j[None, :] + n_cache
    bias_c = rel_table[:, jnp.clip(dist_c, -REL_CLIP, REL_CLIP) + REL_CLIP].reshape(heads * blk, n_cache)
    dist_n = i[:, None] - i[None, :]
    bias_n = rel_table[:, jnp.clip(dist_n, -REL_CLIP, REL_CLIP) + REL_CLIP].reshape(heads * blk, blk)
    kc = k_cache.reshape(n_seq, n_cache, d_model)
    vc = v_cache.reshape(n_seq, n_cache, d_model)
    cache_spec = pl.BlockSpec((nb, n_cache, d_model), lambda g: (g, 0, 0))
    row_spec = pl.BlockSpec((nb, blk, d_model), lambda g: (g, 0, 0))
    return pl.pallas_call(
        functools.partial(_att_sample_kernel, nb=nb, blk=blk, alpha=alpha),
        grid=(n_seq // nb,),
        in_specs=[
            pl.BlockSpec((rows, d_model), lambda g: (row_block0 + g, 0)),
            cache_spec,
            cache_spec,
            _const_spec(w_qkv.shape),
            _const_spec(w_o.shape),
            _const_spec(bias_c.shape),
            _const_spec(bias_n.shape),
            _const_spec(ln_g.shape),
            _const_spec(ln_b.shape),
            _const_spec(wr_hi.shape),
            _const_spec(wr_lo.shape),
            _const_spec(br.shape),
        ],
        out_specs=(
            pl.BlockSpec((rows, d_model), lambda g: (g, 0)),
            pl.BlockSpec((rows, LANES), lambda g: (g, 0)),
            row_spec,
            row_spec,
        ),
        out_shape=(
            jax.ShapeDtypeStruct((n_tok, d_model), F32),
            jax.ShapeDtypeStruct((n_tok, LANES), F32),
            jax.ShapeDtypeStruct((n_seq, blk, d_model), F32),
            jax.ShapeDtypeStruct((n_seq, blk, d_model), F32),
        ),
        scratch_shapes=[pltpu.VMEM((rows, d_model), BF16)],
        compiler_params=pltpu.CompilerParams(
            dimension_semantics=("arbitrary",), vmem_limit_bytes=VMEM_LIMIT),
        name="att_mixer_cache",
    )(x2d, kc, vc, w_qkv, w_o, bias_c, bias_n, ln_g, ln_b, wr_hi, wr_lo, br)


def _route_kernel(lgt_ref, meta_i_ref, meta_w_ref, count_ref, carry_ref):
    j = pl.program_id(0)
    tm = lgt_ref.shape[0]

    @pl.when(j == 0)
    def _():
        carry_ref[...] = jnp.zeros(carry_ref.shape, F32)

    lg = lgt_ref[...]
    lane = lax.broadcasted_iota(jnp.int32, (tm, LANES), 1)
    big = jnp.int32(LANES)
    is_grp = lane < N_GROUPS
    glog = jnp.where(is_grp, lg, NEG_INF)
    gmax = glog.max(axis=-1, keepdims=True)
    gsel = jnp.where(glog == gmax, lane, big).min(axis=-1, keepdims=True)
    gp = 1.0 / jnp.exp(glog - gmax).sum(axis=-1, keepdims=True)
    eidx = lane - N_GROUPS
    in_grp = (eidx >= gsel * EXP_PER_GROUP) & (eidx < (gsel + 1) * EXP_PER_GROUP)
    elog = jnp.where(in_grp, lg, NEG_INF)
    v1 = elog.max(axis=-1, keepdims=True)
    l1 = jnp.where(elog == v1, lane, big).min(axis=-1, keepdims=True)
    elog2 = jnp.where(lane == l1, NEG_INF, elog)
    v2 = elog2.max(axis=-1, keepdims=True)
    l2 = jnp.where(elog2 == v2, lane, big).min(axis=-1, keepdims=True)
    e2 = jnp.exp(v2 - v1)
    w1 = gp / (1.0 + e2)
    w2 = gp * e2 / (1.0 + e2)

    oh1 = (lane == l1).astype(F32)
    oh2 = (lane == l2).astype(F32)
    oh = oh1 + oh2
    r = lax.broadcasted_iota(jnp.int32, (tm, tm), 0)
    c = lax.broadcasted_iota(jnp.int32, (tm, tm), 1)
    tri = (c < r).astype(BF16)
    before = _dot(tri, oh.astype(BF16)) + carry_ref[...]
    rank1 = (before * oh1).sum(axis=-1, keepdims=True)
    rank2 = (before * oh2).sum(axis=-1, keepdims=True)
    carry_ref[...] = carry_ref[...] + oh.sum(axis=0, keepdims=True)
    count_ref[...] = carry_ref[...].astype(jnp.int32)

    eid1 = l1 - N_GROUPS
    eid2 = l2 - N_GROUPS
    mi = jnp.where(lane == 0, eid1, 0) + jnp.where(lane == 1, eid2, 0)
    mi = mi + jnp.where(lane == 2, rank1.astype(jnp.int32), 0) + jnp.where(lane == 3, rank2.astype(jnp.int32), 0)
    meta_i_ref[...] = mi
    meta_w_ref[...] = jnp.where(lane == 0, w1, 0.0) + jnp.where(lane == 1, w2, 0.0)


def _route(logits):
    n_tok = logits.shape[0]
    tm = TOKEN_TILE
    assert n_tok % tm == 0
    tile = pl.BlockSpec((tm, LANES), lambda j: (j, 0))
    return pl.pallas_call(
        _route_kernel,
        grid=(n_tok // tm,),
        in_specs=[tile],
        out_specs=(tile, tile, pl.BlockSpec((1, LANES), lambda j: (0, 0))),
        out_shape=(
            jax.ShapeDtypeStruct((n_tok, LANES), jnp.int32),
            jax.ShapeDtypeStruct((n_tok, LANES), F32),
            jax.ShapeDtypeStruct((1, LANES), jnp.int32),
        ),
        scratch_shapes=[pltpu.VMEM((1, LANES), F32)],
        compiler_params=pltpu.CompilerParams(dimension_semantics=("arbitrary",)),
        name="moe_route",
    )(logits)


def _gather_rows(table, idx):
    m = idx.shape[0]
    width = table.shape[1]
    r = SC_ROWS_PER_CHUNK
    assert m % (SC_WORKERS * r) == 0
    per_w = m // SC_WORKERS
    n_chunk = per_w // r
    mesh = plsc.VectorSubcoreMesh(core_axis_name="c", subcore_axis_name="s")

    @functools.partial(
        pl.kernel,
        mesh=mesh,
        out_type=jax.ShapeDtypeStruct((m, width), table.dtype),
        scratch_types=[
            pltpu.VMEM((per_w,), jnp.int32),
            pltpu.VMEM((r, width), table.dtype),
            pltpu.SemaphoreType.DMA,
        ],
    )
    def gather(table_hbm, idx_hbm, out_hbm, idx_v, rows_v, sem):
        wid = lax.axis_index("s") * 2 + lax.axis_index("c")
        base = wid * per_w
        pltpu.sync_copy(idx_hbm.at[pl.ds(base, per_w)], idx_v)

        def body(c, carry):
            off = pl.multiple_of(c * r, r)
            pltpu.async_copy(table_hbm.at[idx_v.at[pl.ds(off, r)]], rows_v, sem).wait()
            pltpu.sync_copy(rows_v, out_hbm.at[pl.ds(base + off, r)])
            return carry

        lax.fori_loop(0, n_chunk, body, 0)

    return gather(table, idx)


def _gemm_kernel(tile_expert_ref, n_used_ref, xs_ref, wg_ref, wu_ref, wd_ref, ys_ref):
    j = pl.program_id(0)

    @pl.when(j < n_used_ref[0])
    def _():
        x = xs_ref[...].astype(BF16)
        g = _dot(x, wg_ref[...])
        u = _dot(x, wu_ref[...])
        hmid = (jax.nn.silu(g) * u).astype(BF16)
        ys_ref[...] = _dot(hmid, wd_ref[...])


def _expert_gemm(xs, tile_expert, n_used, w_gate, w_up, w_down):
    n_rows, d_model = xs.shape
    d_exp = w_gate.shape[2]
    tm = GEMM_TILE
    n_tiles = n_rows // tm

    def row_map(j, te, nu):
        return (jnp.minimum(j, nu[0] - 1), 0)

    def w_map(j, te, nu):
        return (te[j], 0, 0)

    grid_spec = pltpu.PrefetchScalarGridSpec(
        num_scalar_prefetch=2,
        grid=(n_tiles,),
        in_specs=[
            pl.BlockSpec((tm, d_model), row_map),
            pl.BlockSpec((None, d_model, d_exp), w_map),
            pl.BlockSpec((None, d_model, d_exp), w_map),
            pl.BlockSpec((None, d_exp, d_model), w_map),
        ],
        out_specs=pl.BlockSpec((tm, d_model), row_map),
    )
    return pl.pallas_call(
        _gemm_kernel,
        grid_spec=grid_spec,
        out_shape=jax.ShapeDtypeStruct((n_rows, d_model), F32),
        compiler_params=pltpu.CompilerParams(
            dimension_semantics=("arbitrary",), vmem_limit_bytes=VMEM_LIMIT),
        name="moe_gemm",
    )(tile_expert, n_used, xs, w_gate, w_up, w_down)


def _combine_kernel(yg_ref, mw_ref, h_ref, pp_ref, ps_ref, lng_ref, lnb_ref, wproj_ref, wgate_ref, out_ref,
                    *, n_prompt_tiles, alpha):
    j = pl.program_id(0)
    d_model = h_ref.shape[1]
    mw = mw_ref[...]
    ffn = mw[:, 0:1] * yg_ref[:, 0:d_model] + mw[:, 1:2] * yg_ref[:, d_model:2 * d_model]
    h = h_ref[...]
    h2 = _layer_norm(alpha * h + ffn, lng_ref[...], lnb_ref[...])
    p = jnp.where(j < n_prompt_tiles, pp_ref[...], ps_ref[...]).astype(BF16)
    gate = jax.nn.sigmoid(_dot(h2.astype(BF16), wgate_ref[...]))
    out_ref[...] = h2 + gate * _dot(p, wproj_ref[...])


def _combine(yg, meta_w, h, p_prompt, p_sample, layer, ln_g, ln_b, w_proj, w_gate, alpha):
    n_tok, d_model = h.shape
    tm = TOKEN_TILE
    n_tp = p_prompt.shape[1] // tm
    n_ts = p_sample.shape[1] // tm
    assert n_tp * tm == p_prompt.shape[1] and n_ts * tm == p_sample.shape[1]
    ple = p_prompt.shape[2]
    return pl.pallas_call(
        functools.partial(_combine_kernel, n_prompt_tiles=n_tp, alpha=alpha),
        grid=(n_tok // tm,),
        in_specs=[
            pl.BlockSpec((tm, 2 * d_model), lambda j: (j, 0)),
            pl.BlockSpec((tm, LANES), lambda j: (j, 0)),
            pl.BlockSpec((tm, d_model), lambda j: (j, 0)),
            pl.BlockSpec((None, tm, ple), lambda j: (layer, jnp.minimum(j, n_tp - 1), 0)),
            pl.BlockSpec((None, tm, ple), lambda j: (layer, jnp.maximum(j - n_tp, 0), 0)),
            _const_spec(ln_g.shape),
            _const_spec(ln_b.shape),
            _const_spec(w_proj.shape),
            _const_spec(w_gate.shape),
        ],
        out_specs=pl.BlockSpec((tm, d_model), lambda j: (j, 0)),
        out_shape=jax.ShapeDtypeStruct((n_tok, d_model), F32),
        compiler_params=pltpu.CompilerParams(
            dimension_semantics=("arbitrary",), vmem_limit_bytes=VMEM_LIMIT),
        name="moe_combine",
    )(yg, meta_w, h, p_prompt, p_sample, ln_g, ln_b, w_proj, w_gate)


def _router_weights(w_grp, b_grp, w_exp, b_exp):
    d_model = w_grp.shape[0]
    w = jnp.concatenate([w_grp, jnp.transpose(w_exp, (1, 0, 2)).reshape(d_model, N_EXPERTS)], axis=1)
    w = jnp.pad(w, ((0, 0), (0, LANES - w.shape[1])))
    b = jnp.concatenate([b_grp, b_exp.reshape(N_EXPERTS)])
    b = jnp.pad(b, (0, LANES - b.shape[0])).reshape(1, LANES)
    w_hi = w.astype(BF16)
    w_lo = (w - w_hi.astype(F32)).astype(BF16)
    return w_hi, w_lo, b


def _moe(h, logits, w_gate, w_up, w_down):
    n_tok, d_model = h.shape
    tm = GEMM_TILE
    meta_i, meta_w, counts = _route(logits)
    counts = counts[0, N_GROUPS:N_GROUPS + N_EXPERTS]
    eid = meta_i[:, 0:2]
    rank = meta_i[:, 2:4]
    tiles_per_expert = (counts + tm - 1) // tm
    tile_end = jnp.cumsum(tiles_per_expert)
    row_start = (tile_end - tiles_per_expert) * tm
    pos = (row_start[eid] + rank).reshape(-1)
    gather_quant = SC_WORKERS * SC_ROWS_PER_CHUNK
    n_tiles = -(-(2 * n_tok) // tm) + N_EXPERTS
    n_rows = -(-(n_tiles * tm) // gather_quant) * gather_quant
    n_tiles = n_rows // tm
    src = jnp.zeros((n_rows,), jnp.int32).at[pos].set(jnp.arange(2 * n_tok, dtype=jnp.int32) // 2)
    n_used = tile_end[-1:].astype(jnp.int32)
    tile_ids = jnp.arange(n_tiles, dtype=jnp.int32)
    tile_expert = jnp.searchsorted(tile_end, jnp.minimum(tile_ids, n_used[0] - 1), side="right").astype(jnp.int32)
    xs = _gather_rows(h, src)
    ys = _expert_gemm(xs, tile_expert, n_used, w_gate, w_up, w_down)
    yg = _gather_rows(ys, pos.astype(jnp.int32)).reshape(n_tok, 2 * d_model)
    return yg, meta_w


def kernel(x_prompt, x_sample, p_prompt, p_sample, state_ret, cache_att_k, cache_att_v, ret_w_in, ret_gn_g,
           ret_w_o, att_w_qkv, att_rel_bias, att_w_o, ln1_g, ln1_b, ln2_g, ln2_b, moe_w_grp, moe_b_grp,
           moe_w_exp, moe_b_exp, moe_w_gate, moe_w_up, moe_w_down, ple_w_proj, ple_w_gate):
    n_p, len_p, d_model = x_prompt.shape
    n_s, len_s, _ = x_sample.shape
    depth = ln1_g.shape[0]
    alpha = float((2 * depth) ** 0.25)
    tok_p, tok_s = n_p * len_p, n_s * len_s
    pp = p_prompt.reshape(depth, tok_p, -1)
    ps = p_sample.reshape(depth, tok_s, -1)

    x_p = x_prompt.reshape(tok_p, d_model)
    x_s = x_sample.reshape(tok_s, d_model)
    x_all = None
    states_p, states_s, k_p, v_p, k_s, v_s = [], [], [], [], [], []
    for i in range(depth):
        jj = i // 2
        wr_hi, wr_lo, br = _router_weights(moe_w_grp[i], moe_b_grp[i], moe_w_exp[i], moe_b_exp[i])
        lng, lnb = ln1_g[i].reshape(1, d_model), ln1_b[i].reshape(1, d_model)
        if x_all is None:
            src_p, off_p, src_s, off_s = x_p, 0, x_s, 0
        else:
            src_p, off_p, src_s, off_s = x_all, 0, x_all, tok_p
        if i % 2 == 0:
            w_in = ret_w_in[jj].astype(BF16)
            w_o = ret_w_o[jj].astype(BF16)
            gn = ret_gn_g[jj].reshape(1, -1)
            blk_p = min(RET_BLOCK, len_p)
            h_p, lg_p, st_p = _ret_mixer(src_p, off_p // blk_p, n_p, len_p, 0, None, w_in, w_o, gn, lng, lnb,
                                         wr_hi, wr_lo, br, alpha, nb=1, blk=blk_p)
            nb_s = 2
            h_s, lg_s, st_s = _ret_mixer(src_s, off_s // (nb_s * len_s), n_s, len_s, PAST_LEN, state_ret[jj],
                                         w_in, w_o, gn, lng, lnb, wr_hi, wr_lo, br, alpha, nb=nb_s, blk=len_s)
            states_p.append(st_p)
            states_s.append(st_s)
        else:
            w_qkv = att_w_qkv[jj].astype(BF16)
            w_o = att_w_o[jj].astype(BF16)
            assert off_p == 0
            h_p, lg_p, kr, vr = _att_prompt(src_p, n_p, len_p, w_qkv, w_o, att_rel_bias[jj], lng, lnb,
                                            wr_hi, wr_lo, br, alpha)
            k_p.append(kr.reshape(n_p, -1, ATT_HEADS, d_model // ATT_HEADS))
            v_p.append(vr.reshape(n_p, -1, ATT_HEADS, d_model // ATT_HEADS))
            nb_s = 2
            h_s, lg_s, kr, vr = _att_sample(src_s, off_s // (nb_s * len_s), n_s, len_s, cache_att_k[jj],
                                            cache_att_v[jj], w_qkv, w_o, att_rel_bias[jj], lng, lnb,
                                            wr_hi, wr_lo, br, alpha, nb=nb_s)
            k_s.append(kr.reshape(n_s, len_s, ATT_HEADS, d_model // ATT_HEADS))
            v_s.append(vr.reshape(n_s, len_s, ATT_HEADS, d_model // ATT_HEADS))
        h_all = jnp.concatenate([h_p, h_s], axis=0)
        lg_all = jnp.concatenate([lg_p, lg_s], axis=0)
        yg, meta_w = _moe(h_all, lg_all, moe_w_gate[i].astype(BF16), moe_w_up[i].astype(BF16),
                          moe_w_down[i].astype(BF16))
        x_all = _combine(yg, meta_w, h_all, pp, ps, i, ln2_g[i].reshape(1, d_model), ln2_b[i].reshape(1, d_model),
                         ple_w_proj[i].astype(BF16), ple_w_gate[i].astype(BF16), alpha)

    y_prompt = x_all[:tok_p].reshape(n_p, len_p, d_model)
    y_sample = x_all[tok_p:].reshape(n_s, len_s, d_model)
    return (y_prompt, y_sample, jnp.stack(states_p), jnp.stack(states_s),
            jnp.stack(k_p), jnp.stack(v_p), jnp.stack(k_s), jnp.stack(v_s))
```

```python
import functools
import math

import numpy as np
import jax
import jax.numpy as jnp
from jax import lax
from jax.experimental import pallas as pl
from jax.experimental.pallas import tpu as pltpu
from jax.experimental.pallas import tpu_sc as plsc

CHUNK = 64
PAST_LEN = 2048
RET_HEADS = 4
ROPE_BASE = 10000.0
ATT_HEADS = 16
BAND_CHUNKS = 8
REL_CLIP = 256
N_GROUPS = 4
EXP_PER_GROUP = 8
N_EXPERTS = N_GROUPS * EXP_PER_GROUP
LN_EPS = 1e-5
NEG_INF = -1e30

LANES = 128
SUBLANES = 8
SC_WORKERS = 32
SC_LANES = 16
SC_ROWS_PER_CHUNK = 32
VMEM_LIMIT = 56 * 1024 * 1024

RET_BLOCK = 256
ATT_BLOCK = 4 * CHUNK
TOKEN_TILE = 256
GEMM_TILE = 256

F32 = jnp.float32
BF16 = jnp.bfloat16
U32 = jnp.uint32
HI_MASK = 0xFFFF0000


def _dot(a, b):
    return jnp.dot(a, b, preferred_element_type=F32)


def _dot_nt(a, b):
    return lax.dot_general(a, b, (((1,), (1,)), ((), ())), preferred_element_type=F32)


def _dot_tn(a, b):
    return lax.dot_general(a, b, (((0,), (0,)), ((), ())), preferred_element_type=F32)


def _layer_norm(x, g, b):
    mu = jnp.mean(x, axis=-1, keepdims=True)
    xc = x - mu
    var = jnp.mean(xc * xc, axis=-1, keepdims=True)
    return xc * lax.rsqrt(var + LN_EPS) * g + b


def _pack_rows(x):
    half = x.shape[1] // 2
    lo = lax.bitcast_convert_type(x[:, :half].astype(BF16).astype(F32), U32) >> 16
    hi = lax.bitcast_convert_type(x[:, half:].astype(BF16).astype(F32), U32) & U32(HI_MASK)
    return lo | hi


def _unpack_rows(p):
    lo = lax.bitcast_convert_type(p << 16, F32)
    hi = lax.bitcast_convert_type(p & U32(HI_MASK), F32)
    return jnp.concatenate([lo, hi], axis=1)


def _router_logits(h, wr_hi_ref, wr_lo_ref, br_ref):
    h_hi = h.astype(BF16)
    h_lo = (h - h_hi.astype(F32)).astype(BF16)
    w_hi = wr_hi_ref[...]
    return _dot(h_hi, w_hi) + _dot(h_lo, w_hi) + _dot(h_hi, wr_lo_ref[...]) + br_ref[...]


def _finish_tokens(x, mix, alpha, lng_ref, lnb_ref, wrh_ref, wrl_ref, br_ref, h_ref, hp_ref, lgt_ref):
    hh = _layer_norm(alpha * x + mix, lng_ref[...], lnb_ref[...])
    h_ref[...] = hh
    hp_ref[...] = _pack_rows(hh)
    lgt_ref[...] = _router_logits(hh, wrh_ref, wrl_ref, br_ref)


def _zero_tokens(h_ref, hp_ref, lgt_ref):
    h_ref[...] = jnp.zeros(h_ref.shape, h_ref.dtype)
    hp_ref[...] = jnp.zeros(hp_ref.shape, hp_ref.dtype)
    lgt_ref[...] = jnp.zeros(lgt_ref.shape, lgt_ref.dtype)


def _const_spec(shape):
    nd = len(shape)
    return pl.BlockSpec(shape, lambda *_: (0,) * nd, pipeline_mode=pl.Buffered(1))


def _token_out_shapes(n_all, d_model):
    return (
        jax.ShapeDtypeStruct((n_all, d_model), F32),
        jax.ShapeDtypeStruct((n_all, d_model // 2), U32),
        jax.ShapeDtypeStruct((n_all, LANES), F32),
    )


def _token_out_specs(rows, d_model, row_map):
    return (
        pl.BlockSpec((rows, d_model), row_map),
        pl.BlockSpec((rows, d_model // 2), row_map),
        pl.BlockSpec((rows, LANES), row_map),
    )


def _alias_dst(args, in_specs, dst):
    if dst is None:
        return {}
    aliases = {}
    for k, arr in enumerate(dst):
        aliases[len(args)] = k
        args.append(arr)
        in_specs.append(pl.BlockSpec(memory_space=pl.ANY))
    return aliases


def _ret_log_gamma():
    h = np.arange(RET_HEADS, dtype=np.float32)
    return np.log(np.float32(1.0) - np.float32(2.0) ** (np.float32(-5.0) - h)).astype(np.float32)


def _ret_kernel(*refs, nb, blk, nblk, n_steps, n_fill, has_state, n_alias, alpha):
    step = pl.program_id(0)
    h_ref, hp_ref, lgt_ref = refs[12 + int(has_state) + n_alias:][:3]

    @pl.when(step < n_steps)
    def _():
        _ret_step(*refs, first_block=step % nblk == 0, nb=nb, blk=blk, has_state=has_state,
                  n_alias=n_alias, alpha=alpha)

    if n_fill:
        @pl.when(step >= n_steps)
        def _():
            _zero_tokens(h_ref, hp_ref, lgt_ref)


def _ret_step(*refs, first_block, nb, blk, has_state, n_alias, alpha):
    (x_ref, cos_ref, sin_ref, dmask_ref, win_ref, wo_ref, gn_ref, lng_ref, lnb_ref,
     wrh_ref, wrl_ref, br_ref) = refs[:12]
    s_in_ref = refs[12] if has_state else None
    h_ref, hp_ref, lgt_ref, s_out_ref, gated_ref = refs[12 + int(has_state) + n_alias:]
    heads = RET_HEADS
    d_model = x_ref.shape[1]
    dk = d_model // heads
    dv = 2 * d_model // heads
    hk, hv = heads * dk, heads * dv
    half = dk // 2
    lg = _ret_log_gamma()

    x = x_ref[...]
    xb = x.astype(BF16)
    q_all = _dot(xb, win_ref[:, 0:hk])
    k_all = _dot(xb, win_ref[:, hk:2 * hk])
    v_all = _dot(xb, win_ref[:, 2 * hk:2 * hk + hv])
    g_all = _dot(xb, win_ref[:, 2 * hk + hv:2 * hk + 2 * hv])
    cos = cos_ref[...]
    sin = sin_ref[...]
    rowf = lax.broadcasted_iota(jnp.int32, (blk, 1), 0).astype(F32)

    def rot(t):
        t1, t2 = t[:, :half], t[:, half:]
        return jnp.concatenate([t1 * cos - t2 * sin, t1 * sin + t2 * cos], axis=1)

    if not has_state:
        @pl.when(first_block)
        def _():
            s_out_ref[...] = jnp.zeros(s_out_ref.shape, F32)

    s_prev_ref = s_in_ref if has_state else s_out_ref
    for s in range(nb):
        r0 = s * blk
        for h in range(heads):
            lgh = float(lg[h])
            q = rot(q_all[r0:r0 + blk, h * dk:(h + 1) * dk])
            k = rot(k_all[r0:r0 + blk, h * dk:(h + 1) * dk]) * (dk ** -0.5)
            v = v_all[r0:r0 + blk, h * dv:(h + 1) * dv]
            g = g_all[r0:r0 + blk, h * dv:(h + 1) * dv]
            vb = v.astype(BF16)
            scores = _dot_nt(q.astype(BF16), k.astype(BF16)) * dmask_ref[h]
            inner = _dot(scores.astype(BF16), vb)
            s_prev = s_prev_ref[s, h]
            q_dec = q * jnp.exp(lgh * (rowf + 1.0))
            cross = _dot(q_dec.astype(BF16), s_prev.astype(BF16))
            k_dec = k * jnp.exp(lgh * (float(blk - 1) - rowf))
            s_out_ref[s, h] = math.exp(lgh * blk) * s_prev + _dot_tn(k_dec.astype(BF16), vb)
            o = inner + cross
            mu = jnp.mean(o, axis=-1, keepdims=True)
            oc = o - mu
            var = jnp.mean(oc * oc, axis=-1, keepdims=True)
            on = oc * lax.rsqrt(var + LN_EPS) * gn_ref[:, h * dv:(h + 1) * dv]
            gated_ref[r0:r0 + blk, h * dv:(h + 1) * dv] = (jax.nn.silu(g) * on).astype(BF16)

    mix = _dot(gated_ref[...], wo_ref[...])
    _finish_tokens(x, mix, alpha, lng_ref, lnb_ref, wrh_ref, wrl_ref, br_ref, h_ref, hp_ref, lgt_ref)


def _ret_mixer(x2d, in_row0, n_seq, seq_len, pos0, state_in, w_in, w_o, gn_g, ln_g, ln_b,
               wr_hi, wr_lo, br, alpha, nb, blk, n_all, out_row0, dst):
    d_model = x2d.shape[1]
    heads = RET_HEADS
    dk, dv = d_model // heads, 2 * d_model // heads
    half = dk // 2
    nblk = seq_len // blk
    has_state = state_in is not None
    rows = nb * blk
    assert seq_len % blk == 0 and n_seq % nb == 0
    assert (not has_state) or nblk == 1
    assert nb == 1 or nblk == 1
    assert in_row0 % rows == 0 and out_row0 % rows == 0
    in_b0, out_b0 = in_row0 // rows, out_row0 // rows

    pos = (pos0 + jnp.arange(seq_len, dtype=jnp.int32)).astype(F32)
    inv_freq = ROPE_BASE ** (-jnp.arange(half, dtype=F32) / half)
    ang = pos[:, None] * inv_freq[None, :]
    cos, sin = jnp.cos(ang), jnp.sin(ang)
    lg = jnp.asarray(_ret_log_gamma())
    ii = jnp.arange(blk, dtype=F32)
    diff = ii[:, None] - ii[None, :]
    dmask = jnp.where(diff >= 0, jnp.exp(lg[:, None, None] * jnp.maximum(diff, 0.0)), 0.0)

    n_steps = (n_seq // nb) * nblk
    n_fill = 0 if dst is not None else (n_all - n_seq * seq_len) // rows
    assert dst is not None or (out_row0 == 0 and n_fill * rows == n_all - n_seq * seq_len)

    def work(t):
        return jnp.minimum(t, n_steps - 1)

    in_specs = [
        pl.BlockSpec((rows, d_model), lambda t: (in_b0 + work(t), 0)),
        pl.BlockSpec((blk, half), lambda t: (work(t) % nblk, 0)),
        pl.BlockSpec((blk, half), lambda t: (work(t) % nblk, 0)),
        _const_spec(dmask.shape),
        _const_spec(w_in.shape),
        _const_spec(w_o.shape),
        _const_spec(gn_g.shape),
        _const_spec(ln_g.shape),
        _const_spec(ln_b.shape),
        _const_spec(wr_hi.shape),
        _const_spec(wr_lo.shape),
        _const_spec(br.shape),
    ]
    args = [x2d, cos, sin, dmask, w_in, w_o, gn_g, ln_g, ln_b, wr_hi, wr_lo, br]
    state_spec = pl.BlockSpec((nb, heads, dk, dv), lambda t: (work(t) // nblk, 0, 0, 0))
    if has_state:
        in_specs.append(state_spec)
        args.append(state_in)
    aliases = _alias_dst(args, in_specs, dst)
    out_shape = _token_out_shapes(n_all, d_model) + (jax.ShapeDtypeStruct((n_seq, heads, dk, dv), F32),)
    out_specs = _token_out_specs(rows, d_model, lambda t: (out_b0 + t, 0)) + (state_spec,)
    return pl.pallas_call(
        functools.partial(_ret_kernel, nb=nb, blk=blk, nblk=nblk, n_steps=n_steps, n_fill=n_fill,
                          has_state=has_state, n_alias=len(aliases), alpha=alpha),
        grid=(n_steps + n_fill,),
        in_specs=in_specs,
        out_specs=out_specs,
        out_shape=out_shape,
        input_output_aliases=aliases,
        scratch_shapes=[pltpu.VMEM((rows, heads * dv), BF16)],
        compiler_params=pltpu.CompilerParams(
            dimension_semantics=("arbitrary",), vmem_limit_bytes=VMEM_LIMIT),
        name="ret_mixer_state" if has_state else "ret_mixer",
    )(*args)


_RING = 3


def _att_prompt_kernel(*refs, blk, nblk, n_steps, n_fill, alpha):
    step = pl.program_id(0)
    h_ref, hp_ref, lgt_ref = refs[9:12]

    @pl.when(step < n_steps)
    def _():
        _att_prompt_step(*refs, i=step % nblk, blk=blk, alpha=alpha)

    if n_fill:
        @pl.when(step >= n_steps)
        def _():
            _zero_tokens(h_ref, hp_ref, lgt_ref)


def _att_prompt_step(x_ref, wqkv_ref, wo_ref, bias_ref, lng_ref, lnb_ref, wrh_ref, wrl_ref, br_ref,
                     h_ref, hp_ref, lgt_ref, krow_ref, vrow_ref, kring, vring, o_scr, *, i, blk, alpha):
    d_model = x_ref.shape[1]
    dh = d_model // ATT_HEADS
    x = x_ref[...]
    xb = x.astype(BF16)
    q = _dot(xb, wqkv_ref[:, 0:d_model]) * (dh ** -0.5)
    k = _dot(xb, wqkv_ref[:, d_model:2 * d_model])
    v = _dot(xb, wqkv_ref[:, 2 * d_model:3 * d_model])
    krow_ref[...] = k
    vrow_ref[...] = v

    @pl.when(i == 0)
    def _():
        kring[...] = jnp.zeros(kring.shape, BF16)
        vring[...] = jnp.zeros(vring.shape, BF16)

    kring[i % _RING] = k.astype(BF16)
    vring[i % _RING] = v.astype(BF16)
    qb = q.astype(BF16)
    lane = lax.broadcasted_iota(jnp.int32, (1, LANES), 1)
    slots = [(i + _RING - d) % _RING for d in range(_RING)]

    for pair in range(ATT_HEADS * dh // LANES):
        c0 = pair * LANES
        q_pair = qb[:, c0:c0 + LANES]
        heads_in_pair = LANES // dh
        o_pair = None
        for sub in range(heads_in_pair):
            hd = pair * heads_in_pair + sub
            in_head = (lane >= sub * dh) & (lane < (sub + 1) * dh)
            qm = jnp.where(in_head, q_pair, jnp.zeros_like(q_pair))
            s_list = []
            for d in range(_RING):
                sc = _dot_nt(qm, kring[slots[d], :, c0:c0 + LANES]) + bias_ref[d, hd]
                if d > 0:
                    sc = jnp.where(i >= d, sc, NEG_INF)
                s_list.append(sc)
            m = s_list[0].max(axis=-1, keepdims=True)
            for d in range(1, _RING):
                m = jnp.maximum(m, s_list[d].max(axis=-1, keepdims=True))
            o = None
            l = None
            for d in range(_RING):
                p = jnp.exp(s_list[d] - m)
                ls = p.sum(axis=-1, keepdims=True)
                l = ls if l is None else l + ls
                od = _dot(p.astype(BF16), vring[slots[d], :, c0:c0 + LANES])
                o = od if o is None else o + od
            o = o * (1.0 / l)
            o_pair = o if o_pair is None else jnp.where(in_head, o, o_pair)
        o_scr[:, c0:c0 + LANES] = o_pair.astype(BF16)

    mix = _dot(o_scr[...], wo_ref[...])
    _finish_tokens(x, mix, alpha, lng_ref, lnb_ref, wrh_ref, wrl_ref, br_ref, h_ref, hp_ref, lgt_ref)


def _att_prompt_bias(rel_table, blk):
    i = jnp.arange(blk)
    out = []
    for d in range(_RING):
        dist = i[:, None] - i[None, :] + d * blk
        b = rel_table[:, jnp.clip(dist, -REL_CLIP, REL_CLIP) + REL_CLIP]
        cd = (i[:, None] // CHUNK) - (i[None, :] // CHUNK) + d * (blk // CHUNK)
        ok = (cd >= 0) & (cd <= BAND_CHUNKS)
        out.append(jnp.where(ok[None], b, NEG_INF))
    return jnp.stack(out).astype(F32)


def _att_prompt(x2d, n_seq, seq_len, w_qkv, w_o, rel_table, ln_g, ln_b, wr_hi, wr_lo, br, alpha, n_all):
    d_model = x2d.shape[1]
    blk = ATT_BLOCK
    nblk = seq_len // blk
    keep = min(BAND_CHUNKS * CHUNK, seq_len)
    assert seq_len % blk == 0 and keep % blk == 0
    assert (_RING - 1) * blk >= BAND_CHUNKS * CHUNK
    kb = keep // blk
    bias = _att_prompt_bias(rel_table, blk)
    n_steps = n_seq * nblk
    n_fill = (n_all - n_seq * seq_len) // blk
    assert n_fill * blk == n_all - n_seq * seq_len

    def work(t):
        return jnp.minimum(t, n_steps - 1)

    row_spec = pl.BlockSpec(
        (None, blk, d_model), lambda t: (work(t) // nblk, jnp.maximum(work(t) % nblk - (nblk - kb), 0), 0))
    rows_out = jax.ShapeDtypeStruct((n_seq, keep, d_model), F32)
    return pl.pallas_call(
        functools.partial(_att_prompt_kernel, blk=blk, nblk=nblk, n_steps=n_steps, n_fill=n_fill, alpha=alpha),
        grid=(n_steps + n_fill,),
        in_specs=[
            pl.BlockSpec((blk, d_model), lambda t: (work(t), 0)),
            _const_spec(w_qkv.shape),
            _const_spec(w_o.shape),
            _const_spec(bias.shape),
            _const_spec(ln_g.shape),
            _const_spec(ln_b.shape),
            _const_spec(wr_hi.shape),
            _const_spec(wr_lo.shape),
            _const_spec(br.shape),
        ],
        out_specs=_token_out_specs(blk, d_model, lambda t: (t, 0)) + (row_spec, row_spec),
        out_shape=_token_out_shapes(n_all, d_model) + (rows_out, rows_out),
        scratch_shapes=[
            pltpu.VMEM((_RING, blk, d_model), BF16),
            pltpu.VMEM((_RING, blk, d_model), BF16),
            pltpu.VMEM((blk, d_model), BF16),
        ],
        compiler_params=pltpu.CompilerParams(
            dimension_semantics=("arbitrary",), vmem_limit_bytes=VMEM_LIMIT),
        name="att_mixer",
    )(x2d, w_qkv, w_o, bias, ln_g, ln_b, wr_hi, wr_lo, br)


def _att_sample_kernel(*refs, nb, blk, alpha):
    (x_ref, kc_ref, vc_ref, wqkv_ref, wo_ref, bias_c_ref, bias_n_ref, lng_ref, lnb_ref,
     wrh_ref, wrl_ref, br_ref) = refs[:12]
    h_ref, hp_ref, lgt_ref, krow_ref, vrow_ref, o_scr = refs[-6:]
    d_model = x_ref.shape[1]
    heads = ATT_HEADS
    dh = d_model // heads
    x = x_ref[...]
    xb = x.astype(BF16)
    q = _dot(xb, wqkv_ref[:, 0:d_model]) * (dh ** -0.5)
    k = _dot(xb, wqkv_ref[:, d_model:2 * d_model])
    v = _dot(xb, wqkv_ref[:, 2 * d_model:3 * d_model])
    krow_ref[...] = k.reshape(nb, blk, d_model)
    vrow_ref[...] = v.reshape(nb, blk, d_model)
    lane_head = lax.broadcasted_iota(jnp.int32, (heads, 1, d_model), 2) // dh
    head_id = lax.broadcasted_iota(jnp.int32, (heads, 1, d_model), 0)
    head_mask = (lane_head == head_id).astype(F32)

    for s in range(nb):
        r0 = s * blk
        qs = q[r0:r0 + blk]
        q_bd = (qs[None, :, :] * head_mask).reshape(heads * blk, d_model).astype(BF16)
        kn = k[r0:r0 + blk].astype(BF16)
        vn = v[r0:r0 + blk].astype(BF16)
        s_c = _dot_nt(q_bd, kc_ref[s].astype(BF16)) + bias_c_ref[...]
        s_n = _dot_nt(q_bd, kn) + bias_n_ref[...]
        m = jnp.maximum(s_c.max(axis=-1, keepdims=True), s_n.max(axis=-1, keepdims=True))
        p_c = jnp.exp(s_c - m)
        p_n = jnp.exp(s_n - m)
        l = p_c.sum(axis=-1, keepdims=True) + p_n.sum(axis=-1, keepdims=True)
        o_full = _dot(p_c.astype(BF16), vc_ref[s].astype(BF16)) + _dot(p_n.astype(BF16), vn)
        o_full = o_full * (1.0 / l)
        o = (o_full.reshape(heads, blk, d_model) * head_mask).sum(axis=0)
        o_scr[r0:r0 + blk, :] = o.astype(BF16)

    mix = _dot(o_scr[...], wo_ref[...])
    _finish_tokens(x, mix, alpha, lng_ref, lnb_ref, wrh_ref, wrl_ref, br_ref, h_ref, hp_ref, lgt_ref)


def _att_sample(x2d, in_row0, n_seq, seq_len, k_cache, v_cache, w_qkv, w_o, rel_table, ln_g, ln_b,
                wr_hi, wr_lo, br, alpha, nb, n_all, out_row0, dst):
    d_model = x2d.shape[1]
    heads = ATT_HEADS
    blk = seq_len
    n_cache = k_cache.shape[1]
    rows = nb * blk
    assert n_seq % nb == 0 and in_row0 % rows == 0 and out_row0 % rows == 0
    in_b0, out_b0 = in_row0 // rows, out_row0 // rows
    i = jnp.arange(blk)
    j = jnp.arange(n_cache)
    dist_c = i[:, None] - j[None, :] + n_cache
    bias_c = rel_table[:, jnp.clip(dist_c, -REL_CLIP, REL_CLIP) + REL_CLIP].reshape(heads * blk, n_cache)
    dist_n = i[:, None] - i[None, :]
    bias_n = rel_table[:, jnp.clip(dist_n, -REL_CLIP, REL_CLIP) + REL_CLIP].reshape(heads * blk, blk)
    kc = k_cache.reshape(n_seq, n_cache, d_model)
    vc = v_cache.reshape(n_seq, n_cache, d_model)
    cache_spec = pl.BlockSpec((nb, n_cache, d_model), lambda g: (g, 0, 0))
    row_spec = pl.BlockSpec((nb, blk, d_model), lambda g: (g, 0, 0))
    rows_out = jax.ShapeDtypeStruct((n_seq, blk, d_model), F32)
    in_specs = [
        pl.BlockSpec((rows, d_model), lambda g: (in_b0 + g, 0)),
        cache_spec,
        cache_spec,
        _const_spec(w_qkv.shape),
        _const_spec(w_o.shape),
        _const_spec(bias_c.shape),
        _const_spec(bias_n.shape),
        _const_spec(ln_g.shape),
        _const_spec(ln_b.shape),
        _const_spec(wr_hi.shape),
        _const_spec(wr_lo.shape),
        _const_spec(br.shape),
    ]
    args = [x2d, kc, vc, w_qkv, w_o, bias_c, bias_n, ln_g, ln_b, wr_hi, wr_lo, br]
    aliases = _alias_dst(args, in_specs, dst)
    return pl.pallas_call(
        functools.partial(_att_sample_kernel, nb=nb, blk=blk, alpha=alpha),
        grid=(n_seq // nb,),
        in_specs=in_specs,
        out_specs=_token_out_specs(rows, d_model, lambda g: (out_b0 + g, 0)) + (row_spec, row_spec),
        out_shape=_token_out_shapes(n_all, d_model) + (rows_out, rows_out),
        input_output_aliases=aliases,
        scratch_shapes=[pltpu.VMEM((rows, d_model), BF16)],
        compiler_params=pltpu.CompilerParams(
            dimension_semantics=("arbitrary",), vmem_limit_bytes=VMEM_LIMIT),
        name="att_mixer_cache",
    )(*args)


def _route_kernel(lgt_ref, meta_t_ref, meta_w_ref, count_ref, carry_ref):
    j = pl.program_id(0)
    tm = lgt_ref.shape[0]

    @pl.when(j == 0)
    def _():
        carry_ref[...] = jnp.zeros(carry_ref.shape, F32)

    lg = lgt_ref[...]
    lane = lax.broadcasted_iota(jnp.int32, (tm, LANES), 1)
    big = jnp.int32(LANES)
    is_grp = lane < N_GROUPS
    glog = jnp.where(is_grp, lg, NEG_INF)
    gmax = glog.max(axis=-1, keepdims=True)
    gsel = jnp.where(glog == gmax, lane, big).min(axis=-1, keepdims=True)
    gp = 1.0 / jnp.exp(glog - gmax).sum(axis=-1, keepdims=True)
    eidx = lane - N_GROUPS
    in_grp = (eidx >= gsel * EXP_PER_GROUP) & (eidx < (gsel + 1) * EXP_PER_GROUP)
    elog = jnp.where(in_grp, lg, NEG_INF)
    v1 = elog.max(axis=-1, keepdims=True)
    l1 = jnp.where(elog == v1, lane, big).min(axis=-1, keepdims=True)
    elog2 = jnp.where(lane == l1, NEG_INF, elog)
    v2 = elog2.max(axis=-1, keepdims=True)
    l2 = jnp.where(elog2 == v2, lane, big).min(axis=-1, keepdims=True)
    e2 = jnp.exp(v2 - v1)
    w1 = gp / (1.0 + e2)
    w2 = gp * e2 / (1.0 + e2)

    oh1 = (lane == l1).astype(F32)
    oh2 = (lane == l2).astype(F32)
    oh = oh1 + oh2
    r = lax.broadcasted_iota(jnp.int32, (tm, tm), 0)
    c = lax.broadcasted_iota(jnp.int32, (tm, tm), 1)
    tri = (c < r).astype(BF16)
    before = _dot(tri, oh.astype(BF16)) + carry_ref[...]
    rank1 = (before * oh1).sum(axis=-1, keepdims=True)
    rank2 = (before * oh2).sum(axis=-1, keepdims=True)
    carry_ref[...] = carry_ref[...] + oh.sum(axis=0, keepdims=True)
    count_ref[...] = carry_ref[...].astype(jnp.int32)

    mf = (jnp.where(lane == 0, (l1 - N_GROUPS).astype(F32), 0.0) + jnp.where(lane == 1, (l2 - N_GROUPS).astype(F32), 0.0)
          + jnp.where(lane == 2, rank1, 0.0) + jnp.where(lane == 3, rank2, 0.0))
    meta_t_ref[...] = jnp.transpose(mf)[0:SUBLANES, :].astype(jnp.int32)
    meta_w_ref[...] = jnp.where(lane == 0, w1, 0.0) + jnp.where(lane == 1, w2, 0.0)


def _route(logits):
    n_tok = logits.shape[0]
    tm = TOKEN_TILE
    assert n_tok % tm == 0
    tile = pl.BlockSpec((tm, LANES), lambda j: (j, 0))
    return pl.pallas_call(
        _route_kernel,
        grid=(n_tok // tm,),
        in_specs=[tile],
        out_specs=(pl.BlockSpec((SUBLANES, tm), lambda j: (0, j)), tile, pl.BlockSpec((1, LANES), lambda j: (0, 0))),
        out_shape=(
            jax.ShapeDtypeStruct((SUBLANES, n_tok), jnp.int32),
            jax.ShapeDtypeStruct((n_tok, LANES), F32),
            jax.ShapeDtypeStruct((1, LANES), jnp.int32),
        ),
        scratch_shapes=[pltpu.VMEM((1, LANES), F32)],
        compiler_params=pltpu.CompilerParams(dimension_semantics=("arbitrary",)),
        name="moe_route",
    )(logits)


def _sc_gather_loop(table_hbm, idx_v, out_hbm, rows_v, sem, base, n_chunk, r):
    def body(c, carry):
        off = pl.multiple_of(c * r, r)
        pltpu.async_copy(table_hbm.at[idx_v.at[pl.ds(off, r)]], rows_v, sem).wait()
        pltpu.sync_copy(rows_v, out_hbm.at[pl.ds(base + off, r)])
        return carry

    lax.fori_loop(0, n_chunk, body, 0)


def _gather_rows(table, idx):
    m = idx.shape[0]
    width = table.shape[1]
    r = SC_ROWS_PER_CHUNK
    assert m % (SC_WORKERS * r) == 0
    per_w = m // SC_WORKERS
    mesh = plsc.VectorSubcoreMesh(core_axis_name="c", subcore_axis_name="s")

    @functools.partial(
        pl.kernel,
        mesh=mesh,
        out_type=jax.ShapeDtypeStruct((m, width), table.dtype),
        scratch_types=[
            pltpu.VMEM((per_w,), jnp.int32),
            pltpu.VMEM((r, width), table.dtype),
            pltpu.SemaphoreType.DMA,
        ],
    )
    def gather(table_hbm, idx_hbm, out_hbm, idx_v, rows_v, sem):
        wid = lax.axis_index("s") * 2 + lax.axis_index("c")
        base = wid * per_w
        pltpu.sync_copy(idx_hbm.at[pl.ds(base, per_w)], idx_v)
        _sc_gather_loop(table_hbm, idx_v, out_hbm, rows_v, sem, base, per_w // r, r)

    return gather(table, idx)


def _dispatch_rows(table, pos, n_rows):
    n_tok, width = table.shape
    n_pairs = pos.shape[0]
    r = SC_ROWS_PER_CHUNK
    lanes = SC_LANES
    assert n_rows % (SC_WORKERS * r) == 0
    per_w = n_rows // SC_WORKERS
    n_stage = 16
    stage = n_pairs // n_stage
    assert stage * n_stage == n_pairs and stage % lanes == 0 and per_w % lanes == 0
    assert n_rows < 3 * n_tok
    mesh = plsc.VectorSubcoreMesh(core_axis_name="c", subcore_axis_name="s")

    @functools.partial(
        pl.kernel,
        mesh=mesh,
        out_type=jax.ShapeDtypeStruct((n_rows, width), table.dtype),
        scratch_types=[
            pltpu.VMEM((per_w,), jnp.int32),
            pltpu.VMEM((stage,), jnp.int32),
            pltpu.VMEM((r, width), table.dtype),
            pltpu.SemaphoreType.DMA,
        ],
        compiler_params=pltpu.CompilerParams(needs_layout_passes=False),
    )
    def dispatch(table_hbm, pos_hbm, out_hbm, src_v, pos_v, rows_v, sem):
        wid = lax.axis_index("s") * 2 + lax.axis_index("c")
        base = wid * per_w
        lane = lax.iota(jnp.int32, lanes)

        def wrap(t):
            t = jnp.where(t >= n_tok, t - n_tok, t)
            return jnp.where(t >= n_tok, t - n_tok, t)

        def init(i, carry):
            off = pl.multiple_of(i * lanes, lanes)
            src_v[pl.ds(off, lanes)] = wrap(base + off + lane)
            return carry

        lax.fori_loop(0, per_w // lanes, init, 0)

        def scan_stage(sidx, carry):
            pair0 = sidx * stage
            pltpu.sync_copy(pos_hbm.at[pl.ds(pl.multiple_of(pair0, 8), stage)], pos_v)

            def scan(i, c2):
                off = pl.multiple_of(i * lanes, lanes)
                local = pos_v[pl.ds(off, lanes)] - base
                mine = (local >= 0) & (local < per_w)
                plsc.store_scatter(src_v, [jnp.where(mine, local, 0)], wrap(pair0 + off + lane), mask=mine)
                return c2

            lax.fori_loop(0, stage // lanes, scan, 0)
            return carry

        lax.fori_loop(0, n_stage, scan_stage, 0)
        _sc_gather_loop(table_hbm, src_v, out_hbm, rows_v, sem, base, per_w // r, r)

    return dispatch(table, pos)


def _gemm_kernel(tile_expert_ref, n_used_ref, xs_ref, wg_ref, wu_ref, wd_ref, ys_ref, wg_b, wu_b, wd_b):
    j = pl.program_id(0)

    @pl.when(j < n_used_ref[0])
    def _():
        new_expert = (j == 0) | (tile_expert_ref[j] != tile_expert_ref[jnp.maximum(j - 1, 0)])

        @pl.when(new_expert)
        def _():
            wg_b[...] = wg_ref[...].astype(BF16)
            wu_b[...] = wu_ref[...].astype(BF16)
            wd_b[...] = wd_ref[...].astype(BF16)

        x = _unpack_rows(xs_ref[...]).astype(BF16)
        g = _dot(x, wg_b[...])
        u = _dot(x, wu_b[...])
        hmid = (jax.nn.silu(g) * u).astype(BF16)
        ys_ref[...] = _pack_rows(_dot(hmid, wd_b[...]))


def _expert_gemm(xs, tile_expert, n_used, w_gate, w_up, w_down):
    n_rows, half = xs.shape
    d_model, d_exp = w_gate.shape[1], w_gate.shape[2]
    tm = GEMM_TILE
    n_tiles = n_rows // tm

    def row_map(j, te, nu):
        return (jnp.minimum(j, nu[0] - 1), 0)

    def w_map(j, te, nu):
        return (te[j], 0, 0)

    grid_spec = pltpu.PrefetchScalarGridSpec(
        num_scalar_prefetch=2,
        grid=(n_tiles,),
        in_specs=[
            pl.BlockSpec((tm, half), row_map),
            pl.BlockSpec((None, d_model, d_exp), w_map),
            pl.BlockSpec((None, d_model, d_exp), w_map),
            pl.BlockSpec((None, d_exp, d_model), w_map),
        ],
        out_specs=pl.BlockSpec((tm, half), row_map),
        scratch_shapes=[
            pltpu.VMEM((d_model, d_exp), BF16),
            pltpu.VMEM((d_model, d_exp), BF16),
            pltpu.VMEM((d_exp, d_model), BF16),
        ],
    )
    return pl.pallas_call(
        _gemm_kernel,
        grid_spec=grid_spec,
        out_shape=jax.ShapeDtypeStruct((n_rows, half), U32),
        compiler_params=pltpu.CompilerParams(
            dimension_semantics=("arbitrary",), vmem_limit_bytes=VMEM_LIMIT),
        name="moe_gemm",
    )(tile_expert, n_used, xs, w_gate, w_up, w_down)


def _combine_kernel(y0_ref, y1_ref, mw_ref, h_ref, pp_ref, ps_ref, lng_ref, lnb_ref, wproj_ref, wgate_ref,
                    *out_refs, n_prompt_tiles, alpha):
    j = pl.program_id(0)
    mw = mw_ref[...]
    ffn = mw[:, 0:1] * _unpack_rows(y0_ref[...]) + mw[:, 1:2] * _unpack_rows(y1_ref[...])
    h2 = _layer_norm(alpha * h_ref[...] + ffn, lng_ref[...], lnb_ref[...])
    p = jnp.where(j < n_prompt_tiles, pp_ref[...], ps_ref[...]).astype(BF16)
    gate = jax.nn.sigmoid(_dot(h2.astype(BF16), wgate_ref[...]))
    out = h2 + gate * _dot(p, wproj_ref[...])
    if len(out_refs) == 1:
        out_refs[0][...] = out
    else:
        @pl.when(j < n_prompt_tiles)
        def _():
            out_refs[0][...] = out

        @pl.when(j >= n_prompt_tiles)
        def _():
            out_refs[1][...] = out


def _combine(yg, meta_w, h, p_prompt, p_sample, layer, ln_g, ln_b, w_proj, w_gate, alpha, split_out):
    n_tok, d_model = h.shape
    tm = TOKEN_TILE
    n_t = n_tok // tm
    n_tp = p_prompt.shape[1] // tm
    n_ts = p_sample.shape[1] // tm
    assert n_tp * tm == p_prompt.shape[1] and n_ts * tm == p_sample.shape[1] and n_tp + n_ts == n_t
    ple = p_prompt.shape[2]
    tile = pl.BlockSpec((tm, d_model), lambda j: (j, 0))
    if split_out:
        out_specs = (pl.BlockSpec((tm, d_model), lambda j: (jnp.minimum(j, n_tp - 1), 0)),
                     pl.BlockSpec((tm, d_model), lambda j: (jnp.maximum(j - n_tp, 0), 0)))
        out_shape = (jax.ShapeDtypeStruct((n_tp * tm, d_model), F32),
                     jax.ShapeDtypeStruct((n_ts * tm, d_model), F32))
    else:
        out_specs = tile
        out_shape = jax.ShapeDtypeStruct((n_tok, d_model), F32)
    return pl.pallas_call(
        functools.partial(_combine_kernel, n_prompt_tiles=n_tp, alpha=alpha),
        grid=(n_t,),
        in_specs=[
            pl.BlockSpec((tm, d_model // 2), lambda j: (j, 0)),
            pl.BlockSpec((tm, d_model // 2), lambda j: (j + n_t, 0)),
            pl.BlockSpec((tm, LANES), lambda j: (j, 0)),
            tile,
            pl.BlockSpec((None, tm, ple), lambda j: (layer, jnp.minimum(j, n_tp - 1), 0)),
            pl.BlockSpec((None, tm, ple), lambda j: (layer, jnp.maximum(j - n_tp, 0), 0)),
            _const_spec(ln_g.shape),
            _const_spec(ln_b.shape),
            _const_spec(w_proj.shape),
            _const_spec(w_gate.shape),
        ],
        out_specs=out_specs,
        out_shape=out_shape,
        compiler_params=pltpu.CompilerParams(
            dimension_semantics=("arbitrary",), vmem_limit_bytes=VMEM_LIMIT),
        name="moe_combine",
    )(yg, yg, meta_w, h, p_prompt, p_sample, ln_g, ln_b, w_proj, w_gate)


def _router_weights(w_grp, b_grp, w_exp, b_exp):
    d_model = w_grp.shape[0]
    w = jnp.concatenate([w_grp, jnp.transpose(w_exp, (1, 0, 2)).reshape(d_model, N_EXPERTS)], axis=1)
    w = jnp.pad(w, ((0, 0), (0, LANES - w.shape[1])))
    b = jnp.concatenate([b_grp, b_exp.reshape(N_EXPERTS)])
    b = jnp.pad(b, (0, LANES - b.shape[0])).reshape(1, LANES)
    w_hi = w.astype(BF16)
    w_lo = (w - w_hi.astype(F32)).astype(BF16)
    return w_hi, w_lo, b


def _moe(h_packed, logits, w_gate, w_up, w_down):
    n_tok = h_packed.shape[0]
    tm = GEMM_TILE
    meta_t, meta_w, counts = _route(logits)
    counts = counts[0, N_GROUPS:N_GROUPS + N_EXPERTS]
    tiles_per_expert = (counts + tm - 1) // tm
    tile_end = jnp.cumsum(tiles_per_expert)
    row_start = (tile_end - tiles_per_expert) * tm
    eid = meta_t[0:2]
    rank = meta_t[2:4]
    experts = jnp.arange(N_EXPERTS, dtype=jnp.int32)
    start = jnp.sum(jnp.where(eid[:, :, None] == experts, row_start, 0), axis=-1)
    pos = (start + rank).reshape(-1).astype(jnp.int32)
    gather_quant = SC_WORKERS * SC_ROWS_PER_CHUNK
    n_tiles = -(-(2 * n_tok) // tm) + N_EXPERTS
    n_rows = -(-(n_tiles * tm) // gather_quant) * gather_quant
    n_tiles = n_rows // tm
    n_used = tile_end[-1:].astype(jnp.int32)
    tile_ids = jnp.minimum(jnp.arange(n_tiles, dtype=jnp.int32), n_used[0] - 1)
    tile_expert = jnp.sum(tile_end[None, :] <= tile_ids[:, None], axis=1).astype(jnp.int32)
    xs = _dispatch_rows(h_packed, pos, n_rows)
    ys = _expert_gemm(xs, tile_expert, n_used, w_gate, w_up, w_down)
    return _gather_rows(ys, pos), meta_w


def kernel(x_prompt, x_sample, p_prompt, p_sample, state_ret, cache_att_k, cache_att_v, ret_w_in, ret_gn_g,
           ret_w_o, att_w_qkv, att_rel_bias, att_w_o, ln1_g, ln1_b, ln2_g, ln2_b, moe_w_grp, moe_b_grp,
           moe_w_exp, moe_b_exp, moe_w_gate, moe_w_up, moe_w_down, ple_w_proj, ple_w_gate):
    n_p, len_p, d_model = x_prompt.shape
    n_s, len_s, _ = x_sample.shape
    depth = ln1_g.shape[0]
    alpha = float((2 * depth) ** 0.25)
    tok_p, tok_s = n_p * len_p, n_s * len_s
    n_all = tok_p + tok_s
    dh = d_model // ATT_HEADS
    pp = p_prompt.reshape(depth, tok_p, -1)
    ps = p_sample.reshape(depth, tok_s, -1)
    nb_s = 2

    x_all = None
    y_prompt = y_sample = None
    states_p, states_s, k_p, v_p, k_s, v_s = [], [], [], [], [], []
    for i in range(depth):
        jj = i // 2
        wr_hi, wr_lo, br = _router_weights(moe_w_grp[i], moe_b_grp[i], moe_w_exp[i], moe_b_exp[i])
        lng, lnb = ln1_g[i].reshape(1, d_model), ln1_b[i].reshape(1, d_model)
        if x_all is None:
            src_p, src_s, row_s = x_prompt.reshape(tok_p, d_model), x_sample.reshape(tok_s, d_model), 0
        else:
            src_p, src_s, row_s = x_all, x_all, tok_p
        if i % 2 == 0:
            w_in = ret_w_in[jj].astype(BF16)
            w_o = ret_w_o[jj].astype(BF16)
            gn = ret_gn_g[jj].reshape(1, -1)
            h, hp, lgt, st_p = _ret_mixer(src_p, 0, n_p, len_p, 0, None, w_in, w_o, gn, lng, lnb,
                                          wr_hi, wr_lo, br, alpha, nb=1, blk=min(RET_BLOCK, len_p),
                                          n_all=n_all, out_row0=0, dst=None)
            h, hp, lgt, st_s = _ret_mixer(src_s, row_s, n_s, len_s, PAST_LEN, state_ret[jj], w_in, w_o, gn,
                                          lng, lnb, wr_hi, wr_lo, br, alpha, nb=nb_s, blk=len_s,
                                          n_all=n_all, out_row0=tok_p, dst=(h, hp, lgt))
            states_p.append(st_p)
            states_s.append(st_s)
        else:
            w_qkv = att_w_qkv[jj].astype(BF16)
            w_o = att_w_o[jj].astype(BF16)
            h, hp, lgt, kr, vr = _att_prompt(src_p, n_p, len_p, w_qkv, w_o, att_rel_bias[jj], lng, lnb,
                                             wr_hi, wr_lo, br, alpha, n_all)
            k_p.append(kr.reshape(n_p, -1, ATT_HEADS, dh))
            v_p.append(vr.reshape(n_p, -1, ATT_HEADS, dh))
            h, hp, lgt, kr, vr = _att_sample(src_s, row_s, n_s, len_s, cache_att_k[jj], cache_att_v[jj],
                                             w_qkv, w_o, att_rel_bias[jj], lng, lnb, wr_hi, wr_lo, br, alpha,
                                             nb=nb_s, n_all=n_all, out_row0=tok_p, dst=(h, hp, lgt))
            k_s.append(kr.reshape(n_s, len_s, ATT_HEADS, dh))
            v_s.append(vr.reshape(n_s, len_s, ATT_HEADS, dh))
        yg, meta_w = _moe(hp, lgt, moe_w_gate[i], moe_w_up[i], moe_w_down[i])
        last = i == depth - 1
        out = _combine(yg, meta_w, h, pp, ps, i, ln2_g[i].reshape(1, d_model), ln2_b[i].reshape(1, d_model),
                       ple_w_proj[i].astype(BF16), ple_w_gate[i].astype(BF16), alpha, split_out=last)
        if last:
            y_prompt = out[0].reshape(n_p, len_p, d_model)
            y_sample = out[1].reshape(n_s, len_s, d_model)
        else:
            x_all = out

    return (y_prompt, y_sample, jnp.stack(states_p), jnp.stack(states_s),
            jnp.stack(k_p), jnp.stack(v_p), jnp.stack(k_s), jnp.stack(v_s))
```

```python
import functools
import math

import numpy as np
import jax
import jax.numpy as jnp
from jax import lax
from jax.experimental import pallas as pl
from jax.experimental.pallas import tpu as pltpu
from jax.experimental.pallas import tpu_sc as plsc

CHUNK = 64
PAST_LEN = 2048
RET_HEADS = 4
ROPE_BASE = 10000.0
ATT_HEADS = 16
BAND_CHUNKS = 8
REL_CLIP = 256
N_GROUPS = 4
EXP_PER_GROUP = 8
N_EXPERTS = N_GROUPS * EXP_PER_GROUP
LN_EPS = 1e-5
NEG_INF = -1e30

LANES = 128
SUBLANES = 8
SC_WORKERS = 32
SC_LANES = 16
SC_ROWS_PER_CHUNK = 32
VMEM_LIMIT = 56 * 1024 * 1024

RET_BLOCK = 256
ATT_BLOCK = 4 * CHUNK
TOKEN_TILE = 256
GEMM_TILE = 512

F32 = jnp.float32
BF16 = jnp.bfloat16
U32 = jnp.uint32
HI_MASK = 0xFFFF0000


def _dot(a, b):
    return jnp.dot(a, b, preferred_element_type=F32)


def _dot_nt(a, b):
    return lax.dot_general(a, b, (((1,), (1,)), ((), ())), preferred_element_type=F32)


def _dot_tn(a, b):
    return lax.dot_general(a, b, (((0,), (0,)), ((), ())), preferred_element_type=F32)


def _layer_norm(x, g, b):
    mu = jnp.mean(x, axis=-1, keepdims=True)
    xc = x - mu
    var = jnp.mean(xc * xc, axis=-1, keepdims=True)
    return xc * lax.rsqrt(var + LN_EPS) * g + b


def _pack_rows(x):
    half = x.shape[1] // 2
    lo = lax.bitcast_convert_type(x[:, :half].astype(BF16).astype(F32), U32) >> 16
    hi = lax.bitcast_convert_type(x[:, half:].astype(BF16).astype(F32), U32) & U32(HI_MASK)
    return lo | hi


def _unpack_rows(p):
    lo = lax.bitcast_convert_type(p << 16, F32)
    hi = lax.bitcast_convert_type(p & U32(HI_MASK), F32)
    return jnp.concatenate([lo, hi], axis=1)


def _router_logits(h, wr_hi_ref, wr_lo_ref, br_ref):
    h_hi = h.astype(BF16)
    h_lo = (h - h_hi.astype(F32)).astype(BF16)
    w_hi = wr_hi_ref[...]
    return _dot(h_hi, w_hi) + _dot(h_lo, w_hi) + _dot(h_hi, wr_lo_ref[...]) + br_ref[...]


def _finish_tokens(x, mix, alpha, lng_ref, lnb_ref, wrh_ref, wrl_ref, br_ref, h_ref, hp_ref, lgt_ref):
    hh = _layer_norm(alpha * x + mix, lng_ref[...], lnb_ref[...])
    h_ref[...] = hh
    hp_ref[...] = _pack_rows(hh)
    lgt_ref[...] = _router_logits(hh, wrh_ref, wrl_ref, br_ref)


def _zero_tokens(h_ref, hp_ref, lgt_ref):
    h_ref[...] = jnp.zeros(h_ref.shape, h_ref.dtype)
    hp_ref[...] = jnp.zeros(hp_ref.shape, hp_ref.dtype)
    lgt_ref[...] = jnp.zeros(lgt_ref.shape, lgt_ref.dtype)


def _const_spec(shape):
    nd = len(shape)
    return pl.BlockSpec(shape, lambda *_: (0,) * nd, pipeline_mode=pl.Buffered(1))


def _token_out_shapes(n_all, d_model):
    return (
        jax.ShapeDtypeStruct((n_all, d_model), F32),
        jax.ShapeDtypeStruct((n_all, d_model // 2), U32),
        jax.ShapeDtypeStruct((n_all, LANES), F32),
    )


def _token_out_specs(rows, d_model, row_map):
    return (
        pl.BlockSpec((rows, d_model), row_map),
        pl.BlockSpec((rows, d_model // 2), row_map),
        pl.BlockSpec((rows, LANES), row_map),
    )


def _alias_dst(args, in_specs, dst):
    if dst is None:
        return {}
    aliases = {}
    for k, arr in enumerate(dst):
        aliases[len(args)] = k
        args.append(arr)
        in_specs.append(pl.BlockSpec(memory_space=pl.ANY))
    return aliases


def _ret_log_gamma():
    h = np.arange(RET_HEADS, dtype=np.float32)
    return np.log(np.float32(1.0) - np.float32(2.0) ** (np.float32(-5.0) - h)).astype(np.float32)


def _ret_kernel(*refs, nb, blk, nblk, n_steps, n_fill, has_state, n_alias, alpha):
    step = pl.program_id(0)
    h_ref, hp_ref, lgt_ref = refs[12 + int(has_state) + n_alias:][:3]

    @pl.when(step < n_steps)
    def _():
        _ret_step(*refs, first_block=step % nblk == 0, nb=nb, blk=blk, has_state=has_state,
                  n_alias=n_alias, alpha=alpha)

    if n_fill:
        @pl.when(step >= n_steps)
        def _():
            _zero_tokens(h_ref, hp_ref, lgt_ref)


def _ret_step(*refs, first_block, nb, blk, has_state, n_alias, alpha):
    (x_ref, cos_ref, sin_ref, dmask_ref, win_ref, wo_ref, gn_ref, lng_ref, lnb_ref,
     wrh_ref, wrl_ref, br_ref) = refs[:12]
    s_in_ref = refs[12] if has_state else None
    h_ref, hp_ref, lgt_ref, s_out_ref, gated_ref = refs[12 + int(has_state) + n_alias:]
    heads = RET_HEADS
    d_model = x_ref.shape[1]
    dk = d_model // heads
    dv = 2 * d_model // heads
    hk, hv = heads * dk, heads * dv
    half = dk // 2
    lg = _ret_log_gamma()

    x = x_ref[...]
    xb = x.astype(BF16)
    q_all = _dot(xb, win_ref[:, 0:hk])
    k_all = _dot(xb, win_ref[:, hk:2 * hk])
    v_all = _dot(xb, win_ref[:, 2 * hk:2 * hk + hv])
    g_all = _dot(xb, win_ref[:, 2 * hk + hv:2 * hk + 2 * hv])
    cos = cos_ref[...]
    sin = sin_ref[...]
    rowf = lax.broadcasted_iota(jnp.int32, (blk, 1), 0).astype(F32)

    def rot(t):
        t1, t2 = t[:, :half], t[:, half:]
        return jnp.concatenate([t1 * cos - t2 * sin, t1 * sin + t2 * cos], axis=1)

    if not has_state:
        @pl.when(first_block)
        def _():
            s_out_ref[...] = jnp.zeros(s_out_ref.shape, F32)

    s_prev_ref = s_in_ref if has_state else s_out_ref
    for s in range(nb):
        r0 = s * blk
        for h in range(heads):
            lgh = float(lg[h])
            q = rot(q_all[r0:r0 + blk, h * dk:(h + 1) * dk])
            k = rot(k_all[r0:r0 + blk, h * dk:(h + 1) * dk]) * (dk ** -0.5)
            v = v_all[r0:r0 + blk, h * dv:(h + 1) * dv]
            g = g_all[r0:r0 + blk, h * dv:(h + 1) * dv]
            vb = v.astype(BF16)
            scores = _dot_nt(q.astype(BF16), k.astype(BF16)) * dmask_ref[h]
            inner = _dot(scores.astype(BF16), vb)
            s_prev = s_prev_ref[s, h]
            q_dec = q * jnp.exp(lgh * (rowf + 1.0))
            cross = _dot(q_dec.astype(BF16), s_prev.astype(BF16))
            k_dec = k * jnp.exp(lgh * (float(blk - 1) - rowf))
            s_out_ref[s, h] = math.exp(lgh * blk) * s_prev + _dot_tn(k_dec.astype(BF16), vb)
            o = inner + cross
            mu = jnp.mean(o, axis=-1, keepdims=True)
            oc = o - mu
            var = jnp.mean(oc * oc, axis=-1, keepdims=True)
            on = oc * lax.rsqrt(var + LN_EPS) * gn_ref[:, h * dv:(h + 1) * dv]
            gated_ref[r0:r0 + blk, h * dv:(h + 1) * dv] = (jax.nn.silu(g) * on).astype(BF16)

    mix = _dot(gated_ref[...], wo_ref[...])
    _finish_tokens(x, mix, alpha, lng_ref, lnb_ref, wrh_ref, wrl_ref, br_ref, h_ref, hp_ref, lgt_ref)


def _ret_mixer(x2d, in_row0, n_seq, seq_len, pos0, state_in, w_in, w_o, gn_g, ln_g, ln_b,
               wr_hi, wr_lo, br, alpha, nb, blk, n_all, out_row0, dst):
    d_model = x2d.shape[1]
    heads = RET_HEADS
    dk, dv = d_model // heads, 2 * d_model // heads
    half = dk // 2
    nblk = seq_len // blk
    has_state = state_in is not None
    rows = nb * blk
    assert seq_len % blk == 0 and n_seq % nb == 0
    assert (not has_state) or nblk == 1
    assert nb == 1 or nblk == 1
    assert in_row0 % rows == 0 and out_row0 % rows == 0
    in_b0, out_b0 = in_row0 // rows, out_row0 // rows

    pos = (pos0 + jnp.arange(seq_len, dtype=jnp.int32)).astype(F32)
    inv_freq = ROPE_BASE ** (-jnp.arange(half, dtype=F32) / half)
    ang = pos[:, None] * inv_freq[None, :]
    cos, sin = jnp.cos(ang), jnp.sin(ang)
    lg = jnp.asarray(_ret_log_gamma())
    ii = jnp.arange(blk, dtype=F32)
    diff = ii[:, None] - ii[None, :]
    dmask = jnp.where(diff >= 0, jnp.exp(lg[:, None, None] * jnp.maximum(diff, 0.0)), 0.0)

    n_steps = (n_seq // nb) * nblk
    n_fill = 0 if dst is not None else (n_all - n_seq * seq_len) // rows
    assert dst is not None or (out_row0 == 0 and n_fill * rows == n_all - n_seq * seq_len)

    def work(t):
        return jnp.minimum(t, n_steps - 1)

    in_specs = [
        pl.BlockSpec((rows, d_model), lambda t: (in_b0 + work(t), 0)),
        pl.BlockSpec((blk, half), lambda t: (work(t) % nblk, 0)),
        pl.BlockSpec((blk, half), lambda t: (work(t) % nblk, 0)),
        _const_spec(dmask.shape),
        _const_spec(w_in.shape),
        _const_spec(w_o.shape),
        _const_spec(gn_g.shape),
        _const_spec(ln_g.shape),
        _const_spec(ln_b.shape),
        _const_spec(wr_hi.shape),
        _const_spec(wr_lo.shape),
        _const_spec(br.shape),
    ]
    args = [x2d, cos, sin, dmask, w_in, w_o, gn_g, ln_g, ln_b, wr_hi, wr_lo, br]
    state_spec = pl.BlockSpec((nb, heads, dk, dv), lambda t: (work(t) // nblk, 0, 0, 0))
    if has_state:
        in_specs.append(state_spec)
        args.append(state_in)
    aliases = _alias_dst(args, in_specs, dst)
    out_shape = _token_out_shapes(n_all, d_model) + (jax.ShapeDtypeStruct((n_seq, heads, dk, dv), F32),)
    out_specs = _token_out_specs(rows, d_model, lambda t: (out_b0 + t, 0)) + (state_spec,)
    return pl.pallas_call(
        functools.partial(_ret_kernel, nb=nb, blk=blk, nblk=nblk, n_steps=n_steps, n_fill=n_fill,
                          has_state=has_state, n_alias=len(aliases), alpha=alpha),
        grid=(n_steps + n_fill,),
        in_specs=in_specs,
        out_specs=out_specs,
        out_shape=out_shape,
        input_output_aliases=aliases,
        scratch_shapes=[pltpu.VMEM((rows, heads * dv), BF16)],
        compiler_params=pltpu.CompilerParams(
            dimension_semantics=("arbitrary",), vmem_limit_bytes=VMEM_LIMIT),
        name="ret_mixer_state" if has_state else "ret_mixer",
    )(*args)


_RING = 3


def _att_prompt_kernel(*refs, blk, nblk, n_steps, n_fill, alpha):
    step = pl.program_id(0)
    h_ref, hp_ref, lgt_ref = refs[9:12]

    @pl.when(step < n_steps)
    def _():
        _att_prompt_step(*refs, i=step % nblk, blk=blk, alpha=alpha)

    if n_fill:
        @pl.when(step >= n_steps)
        def _():
            _zero_tokens(h_ref, hp_ref, lgt_ref)


def _att_prompt_step(x_ref, wqkv_ref, wo_ref, bias_ref, lng_ref, lnb_ref, wrh_ref, wrl_ref, br_ref,
                     h_ref, hp_ref, lgt_ref, krow_ref, vrow_ref, kring, vring, o_scr, *, i, blk, alpha):
    d_model = x_ref.shape[1]
    dh = d_model // ATT_HEADS
    x = x_ref[...]
    xb = x.astype(BF16)
    q = _dot(xb, wqkv_ref[:, 0:d_model]) * (dh ** -0.5)
    k = _dot(xb, wqkv_ref[:, d_model:2 * d_model])
    v = _dot(xb, wqkv_ref[:, 2 * d_model:3 * d_model])
    krow_ref[...] = k
    vrow_ref[...] = v

    @pl.when(i == 0)
    def _():
        kring[...] = jnp.zeros(kring.shape, BF16)
        vring[...] = jnp.zeros(vring.shape, BF16)

    kring[i % _RING] = k.astype(BF16)
    vring[i % _RING] = jnp.transpose(v).astype(BF16)
    qb = q.astype(BF16)
    lane = lax.broadcasted_iota(jnp.int32, (1, LANES), 1)
    slots = [(i + _RING - d) % _RING for d in range(_RING)]

    heads_per_group = LANES // dh

    def scores(hd):
        c0 = (hd // heads_per_group) * LANES
        sub = hd % heads_per_group
        in_head = (lane >= sub * dh) & (lane < (sub + 1) * dh)
        q_pair = qb[:, c0:c0 + LANES]
        qm = jnp.where(in_head, q_pair, jnp.zeros_like(q_pair))
        s_list = []
        for d in range(_RING):
            sc = _dot_nt(kring[slots[d], :, c0:c0 + LANES], qm) + bias_ref[d, hd]
            if d > 0:
                sc = jnp.where(i >= d, sc, NEG_INF)
            s_list.append(sc)
        return s_list

    def probs(s_list):
        m = s_list[0].max(axis=0, keepdims=True)
        for d in range(1, _RING):
            m = jnp.maximum(m, s_list[d].max(axis=0, keepdims=True))
        p_list = [jnp.exp(sc - m) for sc in s_list]
        l = p_list[0].sum(axis=0, keepdims=True)
        for d in range(1, _RING):
            l = l + p_list[d].sum(axis=0, keepdims=True)
        return [p.astype(BF16) for p in p_list], 1.0 / l

    def values(hd, p_list, inv_l):
        o = _dot(vring[slots[0], hd * dh:(hd + 1) * dh, :], p_list[0])
        for d in range(1, _RING):
            o = o + _dot(vring[slots[d], hd * dh:(hd + 1) * dh, :], p_list[d])
        o_scr[hd * dh:(hd + 1) * dh, :] = (o * inv_l).astype(BF16)

    s_next = scores(0)
    pending = None
    for hd in range(ATT_HEADS):
        s_cur = s_next
        if hd + 1 < ATT_HEADS:
            s_next = scores(hd + 1)
        p_list, inv_l = probs(s_cur)
        if pending is not None:
            values(*pending)
        pending = (hd, p_list, inv_l)
    values(*pending)

    mix = _dot_tn(o_scr[...], wo_ref[...])
    _finish_tokens(x, mix, alpha, lng_ref, lnb_ref, wrh_ref, wrl_ref, br_ref, h_ref, hp_ref, lgt_ref)


def _rel_bias(rel_table, n_rows, n_cols, offset, sign):
    heads = rel_table.shape[0]
    period = n_rows + n_cols
    m = jnp.arange(period)
    c_minus_r = jnp.where(m < n_cols, m, m - period)
    w = rel_table[:, jnp.clip(offset - sign * c_minus_r, -REL_CLIP, REL_CLIP) + REL_CLIP]
    flat = jnp.broadcast_to(w[:, None, :], (heads, n_rows, period)).reshape(heads, n_rows * period)
    return flat[:, :n_rows * (period - 1)].reshape(heads, n_rows, period - 1)[:, :, :n_cols]


def _att_prompt_bias(rel_table, blk):
    i = jnp.arange(blk)
    out = []
    for d in range(_RING):
        b = _rel_bias(rel_table, blk, blk, d * blk, -1)
        cd = (i[None, :] // CHUNK) - (i[:, None] // CHUNK) + d * (blk // CHUNK)
        ok = (cd >= 0) & (cd <= BAND_CHUNKS)
        out.append(jnp.where(ok[None], b, NEG_INF))
    return jnp.stack(out).astype(F32)


def _att_prompt(x2d, n_seq, seq_len, w_qkv, w_o, rel_table, ln_g, ln_b, wr_hi, wr_lo, br, alpha, n_all):
    d_model = x2d.shape[1]
    blk = ATT_BLOCK
    nblk = seq_len // blk
    keep = min(BAND_CHUNKS * CHUNK, seq_len)
    assert seq_len % blk == 0 and keep % blk == 0
    assert (_RING - 1) * blk >= BAND_CHUNKS * CHUNK
    kb = keep // blk
    bias = _att_prompt_bias(rel_table, blk)
    n_steps = n_seq * nblk
    n_fill = (n_all - n_seq * seq_len) // blk
    assert n_fill * blk == n_all - n_seq * seq_len

    def work(t):
        return jnp.minimum(t, n_steps - 1)

    row_spec = pl.BlockSpec(
        (None, blk, d_model), lambda t: (work(t) // nblk, jnp.maximum(work(t) % nblk - (nblk - kb), 0), 0))
    rows_out = jax.ShapeDtypeStruct((n_seq, keep, d_model), F32)
    return pl.pallas_call(
        functools.partial(_att_prompt_kernel, blk=blk, nblk=nblk, n_steps=n_steps, n_fill=n_fill, alpha=alpha),
        grid=(n_steps + n_fill,),
        in_specs=[
            pl.BlockSpec((blk, d_model), lambda t: (work(t), 0)),
            _const_spec(w_qkv.shape),
            _const_spec(w_o.shape),
            _const_spec(bias.shape),
            _const_spec(ln_g.shape),
            _const_spec(ln_b.shape),
            _const_spec(wr_hi.shape),
            _const_spec(wr_lo.shape),
            _const_spec(br.shape),
        ],
        out_specs=_token_out_specs(blk, d_model, lambda t: (t, 0)) + (row_spec, row_spec),
        out_shape=_token_out_shapes(n_all, d_model) + (rows_out, rows_out),
        scratch_shapes=[
            pltpu.VMEM((_RING, blk, d_model), BF16),
            pltpu.VMEM((_RING, d_model, blk), BF16),
            pltpu.VMEM((d_model, blk), BF16),
        ],
        compiler_params=pltpu.CompilerParams(
            dimension_semantics=("arbitrary",), vmem_limit_bytes=VMEM_LIMIT),
        name="att_mixer",
    )(x2d, w_qkv, w_o, bias, ln_g, ln_b, wr_hi, wr_lo, br)


def _att_sample_kernel(*refs, nb, blk, alpha):
    (x_ref, kc_ref, vc_ref, wqkv_ref, wo_ref, bias_c_ref, bias_n_ref, lng_ref, lnb_ref,
     wrh_ref, wrl_ref, br_ref) = refs[:12]
    h_ref, hp_ref, lgt_ref, krow_ref, vrow_ref, o_scr = refs[-6:]
    d_model = x_ref.shape[1]
    heads = ATT_HEADS
    dh = d_model // heads
    x = x_ref[...]
    xb = x.astype(BF16)
    q = _dot(xb, wqkv_ref[:, 0:d_model]) * (dh ** -0.5)
    k = _dot(xb, wqkv_ref[:, d_model:2 * d_model])
    v = _dot(xb, wqkv_ref[:, 2 * d_model:3 * d_model])
    krow_ref[...] = k.reshape(nb, blk, d_model)
    vrow_ref[...] = v.reshape(nb, blk, d_model)
    lane_head = lax.broadcasted_iota(jnp.int32, (heads, 1, d_model), 2) // dh
    head_id = lax.broadcasted_iota(jnp.int32, (heads, 1, d_model), 0)
    head_mask = (lane_head == head_id).astype(F32)

    for s in range(nb):
        r0 = s * blk
        qs = q[r0:r0 + blk]
        q_bd = (qs[None, :, :] * head_mask).reshape(heads * blk, d_model).astype(BF16)
        kn = k[r0:r0 + blk].astype(BF16)
        vn = v[r0:r0 + blk].astype(BF16)
        s_c = _dot_nt(q_bd, kc_ref[s].astype(BF16)) + bias_c_ref[...]
        s_n = _dot_nt(q_bd, kn) + bias_n_ref[...]
        m = jnp.maximum(s_c.max(axis=-1, keepdims=True), s_n.max(axis=-1, keepdims=True))
        p_c = jnp.exp(s_c - m)
        p_n = jnp.exp(s_n - m)
        l = p_c.sum(axis=-1, keepdims=True) + p_n.sum(axis=-1, keepdims=True)
        o_full = _dot(p_c.astype(BF16), vc_ref[s].astype(BF16)) + _dot(p_n.astype(BF16), vn)
        o_full = o_full * (1.0 / l)
        o = (o_full.reshape(heads, blk, d_model) * head_mask).sum(axis=0)
        o_scr[r0:r0 + blk, :] = o.astype(BF16)

    mix = _dot(o_scr[...], wo_ref[...])
    _finish_tokens(x, mix, alpha, lng_ref, lnb_ref, wrh_ref, wrl_ref, br_ref, h_ref, hp_ref, lgt_ref)


def _att_sample(x2d, in_row0, n_seq, seq_len, k_cache, v_cache, w_qkv, w_o, rel_table, ln_g, ln_b,
                wr_hi, wr_lo, br, alpha, nb, n_all, out_row0, dst):
    d_model = x2d.shape[1]
    heads = ATT_HEADS
    blk = seq_len
    n_cache = k_cache.shape[1]
    rows = nb * blk
    assert n_seq % nb == 0 and in_row0 % rows == 0 and out_row0 % rows == 0
    in_b0, out_b0 = in_row0 // rows, out_row0 // rows
    bias_c = _rel_bias(rel_table, blk, n_cache, n_cache, 1).reshape(heads * blk, n_cache)
    bias_n = _rel_bias(rel_table, blk, blk, 0, 1).reshape(heads * blk, blk)
    kc = k_cache.reshape(n_seq, n_cache, d_model)
    vc = v_cache.reshape(n_seq, n_cache, d_model)
    cache_spec = pl.BlockSpec((nb, n_cache, d_model), lambda g: (g, 0, 0))
    row_spec = pl.BlockSpec((nb, blk, d_model), lambda g: (g, 0, 0))
    rows_out = jax.ShapeDtypeStruct((n_seq, blk, d_model), F32)
    in_specs = [
        pl.BlockSpec((rows, d_model), lambda g: (in_b0 + g, 0)),
        cache_spec,
        cache_spec,
        _const_spec(w_qkv.shape),
        _const_spec(w_o.shape),
        _const_spec(bias_c.shape),
        _const_spec(bias_n.shape),
        _const_spec(ln_g.shape),
        _const_spec(ln_b.shape),
        _const_spec(wr_hi.shape),
        _const_spec(wr_lo.shape),
        _const_spec(br.shape),
    ]
    args = [x2d, kc, vc, w_qkv, w_o, bias_c, bias_n, ln_g, ln_b, wr_hi, wr_lo, br]
    aliases = _alias_dst(args, in_specs, dst)
    return pl.pallas_call(
        functools.partial(_att_sample_kernel, nb=nb, blk=blk, alpha=alpha),
        grid=(n_seq // nb,),
        in_specs=in_specs,
        out_specs=_token_out_specs(rows, d_model, lambda g: (out_b0 + g, 0)) + (row_spec, row_spec),
        out_shape=_token_out_shapes(n_all, d_model) + (rows_out, rows_out),
        input_output_aliases=aliases,
        scratch_shapes=[pltpu.VMEM((rows, d_model), BF16)],
        compiler_params=pltpu.CompilerParams(
            dimension_semantics=("arbitrary",), vmem_limit_bytes=VMEM_LIMIT),
        name="att_mixer_cache",
    )(*args)


def _route_kernel(lgt_ref, meta_t_ref, meta_w_ref, count_ref, carry_ref):
    j = pl.program_id(0)
    tm = lgt_ref.shape[0]

    @pl.when(j == 0)
    def _():
        carry_ref[...] = jnp.zeros(carry_ref.shape, F32)

    lg = lgt_ref[...]
    lane = lax.broadcasted_iota(jnp.int32, (tm, LANES), 1)
    big = jnp.int32(LANES)
    is_grp = lane < N_GROUPS
    glog = jnp.where(is_grp, lg, NEG_INF)
    gmax = glog.max(axis=-1, keepdims=True)
    gsel = jnp.where(glog == gmax, lane, big).min(axis=-1, keepdims=True)
    gp = 1.0 / jnp.exp(glog - gmax).sum(axis=-1, keepdims=True)
    eidx = lane - N_GROUPS
    in_grp = (eidx >= gsel * EXP_PER_GROUP) & (eidx < (gsel + 1) * EXP_PER_GROUP)
    elog = jnp.where(in_grp, lg, NEG_INF)
    v1 = elog.max(axis=-1, keepdims=True)
    l1 = jnp.where(elog == v1, lane, big).min(axis=-1, keepdims=True)
    elog2 = jnp.where(lane == l1, NEG_INF, elog)
    v2 = elog2.max(axis=-1, keepdims=True)
    l2 = jnp.where(elog2 == v2, lane, big).min(axis=-1, keepdims=True)
    e2 = jnp.exp(v2 - v1)
    w1 = gp / (1.0 + e2)
    w2 = gp * e2 / (1.0 + e2)

    oh1 = (lane == l1).astype(F32)
    oh2 = (lane == l2).astype(F32)
    oh = oh1 + oh2
    r = lax.broadcasted_iota(jnp.int32, (tm, tm), 0)
    c = lax.broadcasted_iota(jnp.int32, (tm, tm), 1)
    tri = (c < r).astype(BF16)
    before = _dot(tri, oh.astype(BF16)) + carry_ref[...]
    rank1 = (before * oh1).sum(axis=-1, keepdims=True)
    rank2 = (before * oh2).sum(axis=-1, keepdims=True)
    carry_ref[...] = carry_ref[...] + oh.sum(axis=0, keepdims=True)
    count_ref[...] = carry_ref[...].astype(jnp.int32)

    mf = (jnp.where(lane == 0, (l1 - N_GROUPS).astype(F32), 0.0) + jnp.where(lane == 1, (l2 - N_GROUPS).astype(F32), 0.0)
          + jnp.where(lane == 2, rank1, 0.0) + jnp.where(lane == 3, rank2, 0.0))
    meta_t_ref[...] = jnp.transpose(mf)[0:SUBLANES, :].astype(jnp.int32)
    meta_w_ref[...] = jnp.where(lane == 0, w1, 0.0) + jnp.where(lane == 1, w2, 0.0)


def _route(logits):
    n_tok = logits.shape[0]
    tm = TOKEN_TILE
    assert n_tok % tm == 0
    tile = pl.BlockSpec((tm, LANES), lambda j: (j, 0))
    return pl.pallas_call(
        _route_kernel,
        grid=(n_tok // tm,),
        in_specs=[tile],
        out_specs=(pl.BlockSpec((SUBLANES, tm), lambda j: (0, j)), tile, pl.BlockSpec((1, LANES), lambda j: (0, 0))),
        out_shape=(
            jax.ShapeDtypeStruct((SUBLANES, n_tok), jnp.int32),
            jax.ShapeDtypeStruct((n_tok, LANES), F32),
            jax.ShapeDtypeStruct((1, LANES), jnp.int32),
        ),
        scratch_shapes=[pltpu.VMEM((1, LANES), F32)],
        compiler_params=pltpu.CompilerParams(dimension_semantics=("arbitrary",)),
        name="moe_route",
    )(logits)


def _sc_gather_loop(table_hbm, idx_v, out_hbm, rows_v, sem, base, n_chunk, r):
    def body(c, carry):
        off = pl.multiple_of(c * r, r)
        pltpu.async_copy(table_hbm.at[idx_v.at[pl.ds(off, r)]], rows_v, sem).wait()
        pltpu.sync_copy(rows_v, out_hbm.at[pl.ds(base + off, r)])
        return carry

    lax.fori_loop(0, n_chunk, body, 0)


def _gather_rows(table, idx):
    m = idx.shape[0]
    width = table.shape[1]
    r = SC_ROWS_PER_CHUNK
    assert m % (SC_WORKERS * r) == 0
    per_w = m // SC_WORKERS
    mesh = plsc.VectorSubcoreMesh(core_axis_name="c", subcore_axis_name="s")

    @functools.partial(
        pl.kernel,
        mesh=mesh,
        out_type=jax.ShapeDtypeStruct((m, width), table.dtype),
        scratch_types=[
            pltpu.VMEM((per_w,), jnp.int32),
            pltpu.VMEM((r, width), table.dtype),
            pltpu.SemaphoreType.DMA,
        ],
    )
    def gather(table_hbm, idx_hbm, out_hbm, idx_v, rows_v, sem):
        wid = lax.axis_index("s") * 2 + lax.axis_index("c")
        base = wid * per_w
        pltpu.sync_copy(idx_hbm.at[pl.ds(base, per_w)], idx_v)
        _sc_gather_loop(table_hbm, idx_v, out_hbm, rows_v, sem, base, per_w // r, r)

    return gather(table, idx)


def _dispatch_rows(table, pos, n_rows):
    n_tok, width = table.shape
    n_pairs = pos.shape[0]
    r = SC_ROWS_PER_CHUNK
    lanes = SC_LANES
    assert n_rows % (SC_WORKERS * r) == 0
    per_w = n_rows // SC_WORKERS
    n_stage = 16
    stage = n_pairs // n_stage
    assert stage * n_stage == n_pairs and stage % lanes == 0 and per_w % lanes == 0
    assert n_rows < 3 * n_tok
    mesh = plsc.VectorSubcoreMesh(core_axis_name="c", subcore_axis_name="s")

    @functools.partial(
        pl.kernel,
        mesh=mesh,
        out_type=jax.ShapeDtypeStruct((n_rows, width), table.dtype),
        scratch_types=[
            pltpu.VMEM((per_w,), jnp.int32),
            pltpu.VMEM((stage,), jnp.int32),
            pltpu.VMEM((r, width), table.dtype),
            pltpu.SemaphoreType.DMA,
        ],
        compiler_params=pltpu.CompilerParams(needs_layout_passes=False),
    )
    def dispatch(table_hbm, pos_hbm, out_hbm, src_v, pos_v, rows_v, sem):
        wid = lax.axis_index("s") * 2 + lax.axis_index("c")
        base = wid * per_w
        lane = lax.iota(jnp.int32, lanes)

        def wrap(t):
            t = jnp.where(t >= n_tok, t - n_tok, t)
            return jnp.where(t >= n_tok, t - n_tok, t)

        def init(i, carry):
            off = pl.multiple_of(i * lanes, lanes)
            src_v[pl.ds(off, lanes)] = wrap(base + off + lane)
            return carry

        lax.fori_loop(0, per_w // lanes, init, 0)

        def scan_stage(sidx, carry):
            pair0 = sidx * stage
            pltpu.sync_copy(pos_hbm.at[pl.ds(pl.multiple_of(pair0, 8), stage)], pos_v)

            def scan(i, c2):
                off = pl.multiple_of(i * lanes, lanes)
                local = pos_v[pl.ds(off, lanes)] - base
                mine = (local >= 0) & (local < per_w)
                plsc.store_scatter(src_v, [jnp.where(mine, local, 0)], wrap(pair0 + off + lane), mask=mine)
                return c2

            lax.fori_loop(0, stage // lanes, scan, 0)
            return carry

        lax.fori_loop(0, n_stage, scan_stage, 0)
        _sc_gather_loop(table_hbm, src_v, out_hbm, rows_v, sem, base, per_w // r, r)

    return dispatch(table, pos)


def _gemm_kernel(tile_expert_ref, n_used_ref, xs_ref, wg_ref, wu_ref, wd_ref, ys_ref, wg_b, wu_b, wd_b):
    j = pl.program_id(0)

    @pl.when(j < n_used_ref[0])
    def _():
        new_expert = (j == 0) | (tile_expert_ref[j] != tile_expert_ref[jnp.maximum(j - 1, 0)])

        @pl.when(new_expert)
        def _():
            wg_b[...] = wg_ref[...].astype(BF16)
            wu_b[...] = wu_ref[...].astype(BF16)
            wd_b[...] = wd_ref[...].astype(BF16)

        x = _unpack_rows(xs_ref[...]).astype(BF16)
        g = _dot(x, wg_b[...])
        u = _dot(x, wu_b[...])
        hmid = (jax.nn.silu(g) * u).astype(BF16)
        ys_ref[...] = _pack_rows(_dot(hmid, wd_b[...]))


def _expert_gemm(xs, tile_expert, n_used, layer, w_gate, w_up, w_down):
    n_rows, half = xs.shape
    d_model, d_exp = w_gate.shape[2], w_gate.shape[3]
    tm = GEMM_TILE
    n_tiles = n_rows // tm

    def row_map(j, te, nu):
        return (jnp.minimum(j, nu[0] - 1), 0)

    def w_map(j, te, nu):
        return (layer, te[j], 0, 0)

    grid_spec = pltpu.PrefetchScalarGridSpec(
        num_scalar_prefetch=2,
        grid=(n_tiles,),
        in_specs=[
            pl.BlockSpec((tm, half), row_map),
            pl.BlockSpec((None, None, d_model, d_exp), w_map),
            pl.BlockSpec((None, None, d_model, d_exp), w_map),
            pl.BlockSpec((None, None, d_exp, d_model), w_map),
        ],
        out_specs=pl.BlockSpec((tm, half), row_map),
        scratch_shapes=[
            pltpu.VMEM((d_model, d_exp), BF16),
            pltpu.VMEM((d_model, d_exp), BF16),
            pltpu.VMEM((d_exp, d_model), BF16),
        ],
    )
    return pl.pallas_call(
        _gemm_kernel,
        grid_spec=grid_spec,
        out_shape=jax.ShapeDtypeStruct((n_rows, half), U32),
        compiler_params=pltpu.CompilerParams(
            dimension_semantics=("arbitrary",), vmem_limit_bytes=VMEM_LIMIT),
        name="moe_gemm",
    )(tile_expert, n_used, xs, w_gate, w_up, w_down)


def _combine_kernel(y0_ref, y1_ref, mw_ref, h_ref, pp_ref, ps_ref, lng_ref, lnb_ref, wproj_ref, wgate_ref,
                    *out_refs, n_prompt_tiles, alpha):
    j = pl.program_id(0)
    mw = mw_ref[...]
    ffn = mw[:, 0:1] * _unpack_rows(y0_ref[...]) + mw[:, 1:2] * _unpack_rows(y1_ref[...])
    h2 = _layer_norm(alpha * h_ref[...] + ffn, lng_ref[...], lnb_ref[...])
    p = jnp.where(j < n_prompt_tiles, pp_ref[...], ps_ref[...]).astype(BF16)
    gate = jax.nn.sigmoid(_dot(h2.astype(BF16), wgate_ref[...]))
    out = h2 + gate * _dot(p, wproj_ref[...])
    if len(out_refs) == 1:
        out_refs[0][...] = out
    else:
        @pl.when(j < n_prompt_tiles)
        def _():
            out_refs[0][...] = out

        @pl.when(j >= n_prompt_tiles)
        def _():
            out_refs[1][...] = out


def _combine(yg, meta_w, h, p_prompt, p_sample, layer, ln_g, ln_b, w_proj, w_gate, alpha, split_out):
    n_tok, d_model = h.shape
    tm = TOKEN_TILE
    n_t = n_tok // tm
    n_tp = p_prompt.shape[1] // tm
    n_ts = p_sample.shape[1] // tm
    assert n_tp * tm == p_prompt.shape[1] and n_ts * tm == p_sample.shape[1] and n_tp + n_ts == n_t
    ple = p_prompt.shape[2]
    tile = pl.BlockSpec((tm, d_model), lambda j: (j, 0))
    if split_out:
        out_specs = (pl.BlockSpec((tm, d_model), lambda j: (jnp.minimum(j, n_tp - 1), 0)),
                     pl.BlockSpec((tm, d_model), lambda j: (jnp.maximum(j - n_tp, 0), 0)))
        out_shape = (jax.ShapeDtypeStruct((n_tp * tm, d_model), F32),
                     jax.ShapeDtypeStruct((n_ts * tm, d_model), F32))
    else:
        out_specs = tile
        out_shape = jax.ShapeDtypeStruct((n_tok, d_model), F32)
    return pl.pallas_call(
        functools.partial(_combine_kernel, n_prompt_tiles=n_tp, alpha=alpha),
        grid=(n_t,),
        in_specs=[
            pl.BlockSpec((tm, d_model // 2), lambda j: (j, 0)),
            pl.BlockSpec((tm, d_model // 2), lambda j: (j + n_t, 0)),
            pl.BlockSpec((tm, LANES), lambda j: (j, 0)),
            tile,
            pl.BlockSpec((None, tm, ple), lambda j: (layer, jnp.minimum(j, n_tp - 1), 0)),
            pl.BlockSpec((None, tm, ple), lambda j: (layer, jnp.maximum(j - n_tp, 0), 0)),
            _const_spec(ln_g.shape),
            _const_spec(ln_b.shape),
            _const_spec(w_proj.shape),
            _const_spec(w_gate.shape),
        ],
        out_specs=out_specs,
        out_shape=out_shape,
        compiler_params=pltpu.CompilerParams(
            dimension_semantics=("arbitrary",), vmem_limit_bytes=VMEM_LIMIT),
        name="moe_combine",
    )(yg, yg, meta_w, h, p_prompt, p_sample, ln_g, ln_b, w_proj, w_gate)


def _router_weights(w_grp, b_grp, w_exp, b_exp):
    d_model = w_grp.shape[0]
    w = jnp.concatenate([w_grp, jnp.transpose(w_exp, (1, 0, 2)).reshape(d_model, N_EXPERTS)], axis=1)
    w = jnp.pad(w, ((0, 0), (0, LANES - w.shape[1])))
    b = jnp.concatenate([b_grp, b_exp.reshape(N_EXPERTS)])
    b = jnp.pad(b, (0, LANES - b.shape[0])).reshape(1, LANES)
    w_hi = w.astype(BF16)
    w_lo = (w - w_hi.astype(F32)).astype(BF16)
    return w_hi, w_lo, b


def _moe(h_packed, logits, layer, w_gate, w_up, w_down):
    n_tok = h_packed.shape[0]
    tm = GEMM_TILE
    meta_t, meta_w, counts = _route(logits)
    counts = counts[0, N_GROUPS:N_GROUPS + N_EXPERTS]
    tiles_per_expert = (counts + tm - 1) // tm
    tile_end = jnp.cumsum(tiles_per_expert)
    row_start = (tile_end - tiles_per_expert) * tm
    eid = meta_t[0:2]
    rank = meta_t[2:4]
    experts = jnp.arange(N_EXPERTS, dtype=jnp.int32)
    start = jnp.sum(jnp.where(eid[:, :, None] == experts, row_start, 0), axis=-1)
    pos = (start + rank).reshape(-1).astype(jnp.int32)
    gather_quant = SC_WORKERS * SC_ROWS_PER_CHUNK
    n_tiles = -(-(2 * n_tok) // tm) + N_EXPERTS
    n_rows = -(-(n_tiles * tm) // gather_quant) * gather_quant
    n_tiles = n_rows // tm
    n_used = tile_end[-1:].astype(jnp.int32)
    tile_ids = jnp.minimum(jnp.arange(n_tiles, dtype=jnp.int32), n_used[0] - 1)
    tile_expert = jnp.sum(tile_end[None, :] <= tile_ids[:, None], axis=1).astype(jnp.int32)
    xs = _dispatch_rows(h_packed, pos, n_rows)
    ys = _expert_gemm(xs, tile_expert, n_used, layer, w_gate, w_up, w_down)
    return _gather_rows(ys, pos), meta_w


def kernel(x_prompt, x_sample, p_prompt, p_sample, state_ret, cache_att_k, cache_att_v, ret_w_in, ret_gn_g,
           ret_w_o, att_w_qkv, att_rel_bias, att_w_o, ln1_g, ln1_b, ln2_g, ln2_b, moe_w_grp, moe_b_grp,
           moe_w_exp, moe_b_exp, moe_w_gate, moe_w_up, moe_w_down, ple_w_proj, ple_w_gate):
    n_p, len_p, d_model = x_prompt.shape
    n_s, len_s, _ = x_sample.shape
    depth = ln1_g.shape[0]
    alpha = float((2 * depth) ** 0.25)
    tok_p, tok_s = n_p * len_p, n_s * len_s
    n_all = tok_p + tok_s
    dh = d_model // ATT_HEADS
    pp = p_prompt.reshape(depth, tok_p, -1)
    ps = p_sample.reshape(depth, tok_s, -1)
    nb_s = 2

    x_all = None
    y_prompt = y_sample = None
    states_p, states_s, k_p, v_p, k_s, v_s = [], [], [], [], [], []
    for i in range(depth):
        jj = i // 2
        wr_hi, wr_lo, br = _router_weights(moe_w_grp[i], moe_b_grp[i], moe_w_exp[i], moe_b_exp[i])
        lng, lnb = ln1_g[i].reshape(1, d_model), ln1_b[i].reshape(1, d_model)
        if x_all is None:
            src_p, src_s, row_s = x_prompt.reshape(tok_p, d_model), x_sample.reshape(tok_s, d_model), 0
        else:
            src_p, src_s, row_s = x_all, x_all, tok_p
        if i % 2 == 0:
            w_in = ret_w_in[jj].astype(BF16)
            w_o = ret_w_o[jj].astype(BF16)
            gn = ret_gn_g[jj].reshape(1, -1)
            h, hp, lgt, st_p = _ret_mixer(src_p, 0, n_p, len_p, 0, None, w_in, w_o, gn, lng, lnb,
                                          wr_hi, wr_lo, br, alpha, nb=1, blk=min(RET_BLOCK, len_p),
                                          n_all=n_all, out_row0=0, dst=None)
            h, hp, lgt, st_s = _ret_mixer(src_s, row_s, n_s, len_s, PAST_LEN, state_ret[jj], w_in, w_o, gn,
                                          lng, lnb, wr_hi, wr_lo, br, alpha, nb=nb_s, blk=len_s,
                                          n_all=n_all, out_row0=tok_p, dst=(h, hp, lgt))
            states_p.append(st_p)
            states_s.append(st_s)
        else:
            w_qkv = att_w_qkv[jj].astype(BF16)
            w_o = att_w_o[jj].astype(BF16)
            h, hp, lgt, kr, vr = _att_prompt(src_p, n_p, len_p, w_qkv, w_o, att_rel_bias[jj], lng, lnb,
                                             wr_hi, wr_lo, br, alpha, n_all)
            k_p.append(kr.reshape(n_p, -1, ATT_HEADS, dh))
            v_p.append(vr.reshape(n_p, -1, ATT_HEADS, dh))
            h, hp, lgt, kr, vr = _att_sample(src_s, row_s, n_s, len_s, cache_att_k[jj], cache_att_v[jj],
                                             w_qkv, w_o, att_rel_bias[jj], lng, lnb, wr_hi, wr_lo, br, alpha,
                                             nb=nb_s, n_all=n_all, out_row0=tok_p, dst=(h, hp, lgt))
            k_s.append(kr.reshape(n_s, len_s, ATT_HEADS, dh))
            v_s.append(vr.reshape(n_s, len_s, ATT_HEADS, dh))
        yg, meta_w = _moe(hp, lgt, i, moe_w_gate, moe_w_up, moe_w_down)
        last = i == depth - 1
        out = _combine(yg, meta_w, h, pp, ps, i, ln2_g[i].reshape(1, d_model), ln2_b[i].reshape(1, d_model),
                       ple_w_proj[i].astype(BF16), ple_w_gate[i].astype(BF16), alpha, split_out=last)
        if last:
            y_prompt = out[0].reshape(n_p, len_p, d_model)
            y_sample = out[1].reshape(n_s, len_s, d_model)
        else:
            x_all = out

    return (y_prompt, y_sample, jnp.stack(states_p), jnp.stack(states_s),
            jnp.stack(k_p), jnp.stack(v_p), jnp.stack(k_s), jnp.stack(v_s))
```

```python
import functools
import math

import numpy as np
import jax
import jax.numpy as jnp
from jax import lax
from jax.experimental import pallas as pl
from jax.experimental.pallas import tpu as pltpu
from jax.experimental.pallas import tpu_sc as plsc

CHUNK = 64
PAST_LEN = 2048
RET_HEADS = 4
ROPE_BASE = 10000.0
ATT_HEADS = 16
BAND_CHUNKS = 8
REL_CLIP = 256
N_GROUPS = 4
EXP_PER_GROUP = 8
N_EXPERTS = N_GROUPS * EXP_PER_GROUP
LN_EPS = 1e-5
NEG_INF = -1e30

LANES = 128
SUBLANES = 8
SC_WORKERS = 32
SC_LANES = 16
SC_ROWS_PER_CHUNK = 32
VMEM_LIMIT = 56 * 1024 * 1024

RET_BLOCK = 256
ATT_BLOCK = 4 * CHUNK
TOKEN_TILE = 512
GEMM_TILE = 512

F32 = jnp.float32
BF16 = jnp.bfloat16
U32 = jnp.uint32
HI_MASK = 0xFFFF0000


def _dot(a, b):
    return jnp.dot(a, b, preferred_element_type=F32)


def _dot_nt(a, b):
    return lax.dot_general(a, b, (((1,), (1,)), ((), ())), preferred_element_type=F32)


def _dot_tn(a, b):
    return lax.dot_general(a, b, (((0,), (0,)), ((), ())), preferred_element_type=F32)


def _layer_norm(x, g, b):
    mu = jnp.mean(x, axis=-1, keepdims=True)
    xc = x - mu
    var = jnp.mean(xc * xc, axis=-1, keepdims=True)
    return xc * lax.rsqrt(var + LN_EPS) * g + b


def _pack_rows(x):
    half = x.shape[1] // 2
    lo = lax.bitcast_convert_type(x[:, :half].astype(BF16).astype(F32), U32) >> 16
    hi = lax.bitcast_convert_type(x[:, half:].astype(BF16).astype(F32), U32) & U32(HI_MASK)
    return lo | hi


def _unpack_rows(p):
    lo = lax.bitcast_convert_type(p << 16, F32)
    hi = lax.bitcast_convert_type(p & U32(HI_MASK), F32)
    return jnp.concatenate([lo, hi], axis=1)


def _router_logits(h, wr_hi_ref, wr_lo_ref, br_ref):
    h_hi = h.astype(BF16)
    h_lo = (h - h_hi.astype(F32)).astype(BF16)
    w_hi = wr_hi_ref[...]
    return _dot(h_hi, w_hi) + _dot(h_lo, w_hi) + _dot(h_hi, wr_lo_ref[...]) + br_ref[...]


def _finish_tokens(x, mix, alpha, lng_ref, lnb_ref, wrh_ref, wrl_ref, br_ref, h_ref, hp_ref, lgt_ref):
    hh = _layer_norm(alpha * x + mix, lng_ref[...], lnb_ref[...])
    h_ref[...] = hh
    hp_ref[...] = _pack_rows(hh)
    lgt_ref[...] = _router_logits(hh, wrh_ref, wrl_ref, br_ref)


def _zero_tokens(h_ref, hp_ref, lgt_ref):
    h_ref[...] = jnp.zeros(h_ref.shape, h_ref.dtype)
    hp_ref[...] = jnp.zeros(hp_ref.shape, hp_ref.dtype)
    lgt_ref[...] = jnp.zeros(lgt_ref.shape, lgt_ref.dtype)


def _const_spec(shape):
    nd = len(shape)
    return pl.BlockSpec(shape, lambda *_: (0,) * nd, pipeline_mode=pl.Buffered(1))


def _token_out_shapes(n_all, d_model):
    return (
        jax.ShapeDtypeStruct((n_all, d_model), F32),
        jax.ShapeDtypeStruct((n_all, d_model // 2), U32),
        jax.ShapeDtypeStruct((n_all, LANES), F32),
    )


def _token_out_specs(rows, d_model, row_map):
    return (
        pl.BlockSpec((rows, d_model), row_map),
        pl.BlockSpec((rows, d_model // 2), row_map),
        pl.BlockSpec((rows, LANES), row_map),
    )


def _alias_dst(args, in_specs, dst):
    if dst is None:
        return {}
    aliases = {}
    for k, arr in enumerate(dst):
        aliases[len(args)] = k
        args.append(arr)
        in_specs.append(pl.BlockSpec(memory_space=pl.ANY))
    return aliases


def _ret_log_gamma():
    h = np.arange(RET_HEADS, dtype=np.float32)
    return np.log(np.float32(1.0) - np.float32(2.0) ** (np.float32(-5.0) - h)).astype(np.float32)


def _ret_kernel(*refs, nb, blk, nblk, n_steps, n_fill, has_state, n_alias, alpha):
    step = pl.program_id(0)
    h_ref, hp_ref, lgt_ref = refs[12 + int(has_state) + n_alias:][:3]

    @pl.when(step < n_steps)
    def _():
        _ret_step(*refs, first_block=step % nblk == 0, nb=nb, blk=blk, has_state=has_state,
                  n_alias=n_alias, alpha=alpha)

    if n_fill:
        @pl.when(step >= n_steps)
        def _():
            _zero_tokens(h_ref, hp_ref, lgt_ref)


def _ret_step(*refs, first_block, nb, blk, has_state, n_alias, alpha):
    (x_ref, cos_ref, sin_ref, dmask_ref, win_ref, wo_ref, gn_ref, lng_ref, lnb_ref,
     wrh_ref, wrl_ref, br_ref) = refs[:12]
    s_in_ref = refs[12] if has_state else None
    h_ref, hp_ref, lgt_ref, s_out_ref, gated_ref = refs[12 + int(has_state) + n_alias:]
    heads = RET_HEADS
    d_model = x_ref.shape[1]
    dk = d_model // heads
    dv = 2 * d_model // heads
    hk, hv = heads * dk, heads * dv
    half = dk // 2
    lg = _ret_log_gamma()

    x = x_ref[...]
    xb = x.astype(BF16)
    q_all = _dot(xb, win_ref[:, 0:hk])
    k_all = _dot(xb, win_ref[:, hk:2 * hk])
    v_all = _dot(xb, win_ref[:, 2 * hk:2 * hk + hv])
    g_all = _dot(xb, win_ref[:, 2 * hk + hv:2 * hk + 2 * hv])
    cos = cos_ref[...]
    sin = sin_ref[...]
    rowf = lax.broadcasted_iota(jnp.int32, (blk, 1), 0).astype(F32)

    def rot(t):
        t1, t2 = t[:, :half], t[:, half:]
        return jnp.concatenate([t1 * cos - t2 * sin, t1 * sin + t2 * cos], axis=1)

    if not has_state:
        @pl.when(first_block)
        def _():
            s_out_ref[...] = jnp.zeros(s_out_ref.shape, F32)

    s_prev_ref = s_in_ref if has_state else s_out_ref
    for s in range(nb):
        r0 = s * blk
        for h in range(heads):
            lgh = float(lg[h])
            q = rot(q_all[r0:r0 + blk, h * dk:(h + 1) * dk])
            k = rot(k_all[r0:r0 + blk, h * dk:(h + 1) * dk]) * (dk ** -0.5)
            v = v_all[r0:r0 + blk, h * dv:(h + 1) * dv]
            g = g_all[r0:r0 + blk, h * dv:(h + 1) * dv]
            vb = v.astype(BF16)
            scores = _dot_nt(q.astype(BF16), k.astype(BF16)) * dmask_ref[h]
            inner = _dot(scores.astype(BF16), vb)
            s_prev = s_prev_ref[s, h]
            q_dec = q * jnp.exp(lgh * (rowf + 1.0))
            cross = _dot(q_dec.astype(BF16), s_prev.astype(BF16))
            k_dec = k * jnp.exp(lgh * (float(blk - 1) - rowf))
            s_out_ref[s, h] = math.exp(lgh * blk) * s_prev + _dot_tn(k_dec.astype(BF16), vb)
            o = inner + cross
            mu = jnp.mean(o, axis=-1, keepdims=True)
            oc = o - mu
            var = jnp.mean(oc * oc, axis=-1, keepdims=True)
            on = oc * lax.rsqrt(var + LN_EPS) * gn_ref[:, h * dv:(h + 1) * dv]
            gated_ref[r0:r0 + blk, h * dv:(h + 1) * dv] = (jax.nn.silu(g) * on).astype(BF16)

    mix = _dot(gated_ref[...], wo_ref[...])
    _finish_tokens(x, mix, alpha, lng_ref, lnb_ref, wrh_ref, wrl_ref, br_ref, h_ref, hp_ref, lgt_ref)


def _ret_mixer(x2d, in_row0, n_seq, seq_len, pos0, state_in, w_in, w_o, gn_g, ln_g, ln_b,
               wr_hi, wr_lo, br, alpha, nb, blk, n_all, out_row0, dst):
    d_model = x2d.shape[1]
    heads = RET_HEADS
    dk, dv = d_model // heads, 2 * d_model // heads
    half = dk // 2
    nblk = seq_len // blk
    has_state = state_in is not None
    rows = nb * blk
    assert seq_len % blk == 0 and n_seq % nb == 0
    assert (not has_state) or nblk == 1
    assert nb == 1 or nblk == 1
    assert in_row0 % rows == 0 and out_row0 % rows == 0
    in_b0, out_b0 = in_row0 // rows, out_row0 // rows

    pos = (pos0 + jnp.arange(seq_len, dtype=jnp.int32)).astype(F32)
    inv_freq = ROPE_BASE ** (-jnp.arange(half, dtype=F32) / half)
    ang = pos[:, None] * inv_freq[None, :]
    cos, sin = jnp.cos(ang), jnp.sin(ang)
    lg = jnp.asarray(_ret_log_gamma())
    ii = jnp.arange(blk, dtype=F32)
    diff = ii[:, None] - ii[None, :]
    dmask = jnp.where(diff >= 0, jnp.exp(lg[:, None, None] * jnp.maximum(diff, 0.0)), 0.0)

    n_steps = (n_seq // nb) * nblk
    n_fill = 0 if dst is not None else (n_all - n_seq * seq_len) // rows
    assert dst is not None or (out_row0 == 0 and n_fill * rows == n_all - n_seq * seq_len)

    def work(t):
        return jnp.minimum(t, n_steps - 1)

    in_specs = [
        pl.BlockSpec((rows, d_model), lambda t: (in_b0 + work(t), 0)),
        pl.BlockSpec((blk, half), lambda t: (work(t) % nblk, 0)),
        pl.BlockSpec((blk, half), lambda t: (work(t) % nblk, 0)),
        _const_spec(dmask.shape),
        _const_spec(w_in.shape),
        _const_spec(w_o.shape),
        _const_spec(gn_g.shape),
        _const_spec(ln_g.shape),
        _const_spec(ln_b.shape),
        _const_spec(wr_hi.shape),
        _const_spec(wr_lo.shape),
        _const_spec(br.shape),
    ]
    args = [x2d, cos, sin, dmask, w_in, w_o, gn_g, ln_g, ln_b, wr_hi, wr_lo, br]
    state_spec = pl.BlockSpec((nb, heads, dk, dv), lambda t: (work(t) // nblk, 0, 0, 0))
    if has_state:
        in_specs.append(state_spec)
        args.append(state_in)
    aliases = _alias_dst(args, in_specs, dst)
    out_shape = _token_out_shapes(n_all, d_model) + (jax.ShapeDtypeStruct((n_seq, heads, dk, dv), F32),)
    out_specs = _token_out_specs(rows, d_model, lambda t: (out_b0 + t, 0)) + (state_spec,)
    return pl.pallas_call(
        functools.partial(_ret_kernel, nb=nb, blk=blk, nblk=nblk, n_steps=n_steps, n_fill=n_fill,
                          has_state=has_state, n_alias=len(aliases), alpha=alpha),
        grid=(n_steps + n_fill,),
        in_specs=in_specs,
        out_specs=out_specs,
        out_shape=out_shape,
        input_output_aliases=aliases,
        scratch_shapes=[pltpu.VMEM((rows, heads * dv), BF16)],
        compiler_params=pltpu.CompilerParams(
            dimension_semantics=("arbitrary",), vmem_limit_bytes=VMEM_LIMIT),
        name="ret_mixer_state" if has_state else "ret_mixer",
    )(*args)


_RING = 3


def _att_prompt_kernel(*refs, blk, nblk, n_steps, n_fill, alpha):
    step = pl.program_id(0)
    h_ref, hp_ref, lgt_ref = refs[9:12]

    @pl.when(step < n_steps)
    def _():
        _att_prompt_step(*refs, i=step % nblk, blk=blk, alpha=alpha)

    if n_fill:
        @pl.when(step >= n_steps)
        def _():
            _zero_tokens(h_ref, hp_ref, lgt_ref)


def _att_prompt_step(x_ref, wqkv_ref, wo_ref, bias_ref, lng_ref, lnb_ref, wrh_ref, wrl_ref, br_ref,
                     h_ref, hp_ref, lgt_ref, krow_ref, vrow_ref, kring, vring, o_scr, *, i, blk, alpha):
    d_model = x_ref.shape[1]
    dh = d_model // ATT_HEADS
    x = x_ref[...]
    xb = x.astype(BF16)
    q = _dot(xb, wqkv_ref[:, 0:d_model]) * (dh ** -0.5)
    k = _dot(xb, wqkv_ref[:, d_model:2 * d_model])
    v = _dot(xb, wqkv_ref[:, 2 * d_model:3 * d_model])
    krow_ref[...] = k
    vrow_ref[...] = v

    @pl.when(i == 0)
    def _():
        kring[...] = jnp.zeros(kring.shape, BF16)
        vring[...] = jnp.zeros(vring.shape, BF16)

    kring[i % _RING] = k.astype(BF16)
    vring[i % _RING] = jnp.transpose(v).astype(BF16)
    qb = q.astype(BF16)
    lane = lax.broadcasted_iota(jnp.int32, (1, LANES), 1)
    slots = [(i + _RING - d) % _RING for d in range(_RING)]

    heads_per_group = LANES // dh

    def scores(hd):
        c0 = (hd // heads_per_group) * LANES
        sub = hd % heads_per_group
        in_head = (lane >= sub * dh) & (lane < (sub + 1) * dh)
        q_pair = qb[:, c0:c0 + LANES]
        qm = jnp.where(in_head, q_pair, jnp.zeros_like(q_pair))
        s_list = []
        for d in range(_RING):
            sc = _dot_nt(kring[slots[d], :, c0:c0 + LANES], qm) + bias_ref[d, hd]
            if d > 0:
                sc = jnp.where(i >= d, sc, NEG_INF)
            s_list.append(sc)
        return s_list

    def probs(s_list):
        m = s_list[0].max(axis=0, keepdims=True)
        for d in range(1, _RING):
            m = jnp.maximum(m, s_list[d].max(axis=0, keepdims=True))
        p_list = [jnp.exp(sc - m) for sc in s_list]
        l = p_list[0].sum(axis=0, keepdims=True)
        for d in range(1, _RING):
            l = l + p_list[d].sum(axis=0, keepdims=True)
        return [p.astype(BF16) for p in p_list], 1.0 / l

    def values(hd, p_list, inv_l):
        o = _dot(vring[slots[0], hd * dh:(hd + 1) * dh, :], p_list[0])
        for d in range(1, _RING):
            o = o + _dot(vring[slots[d], hd * dh:(hd + 1) * dh, :], p_list[d])
        o_scr[hd * dh:(hd + 1) * dh, :] = (o * inv_l).astype(BF16)

    s_next = scores(0)
    pending = None
    for hd in range(ATT_HEADS):
        s_cur = s_next
        if hd + 1 < ATT_HEADS:
            s_next = scores(hd + 1)
        p_list, inv_l = probs(s_cur)
        if pending is not None:
            values(*pending)
        pending = (hd, p_list, inv_l)
    values(*pending)

    mix = _dot_tn(o_scr[...], wo_ref[...])
    _finish_tokens(x, mix, alpha, lng_ref, lnb_ref, wrh_ref, wrl_ref, br_ref, h_ref, hp_ref, lgt_ref)


def _rel_bias(rel_table, n_rows, n_cols, offset, sign):
    heads = rel_table.shape[0]
    period = n_rows + n_cols
    m = jnp.arange(period)
    c_minus_r = jnp.where(m < n_cols, m, m - period)
    w = rel_table[:, jnp.clip(offset - sign * c_minus_r, -REL_CLIP, REL_CLIP) + REL_CLIP]
    flat = jnp.broadcast_to(w[:, None, :], (heads, n_rows, period)).reshape(heads, n_rows * period)
    return flat[:, :n_rows * (period - 1)].reshape(heads, n_rows, period - 1)[:, :, :n_cols]


def _att_prompt_bias(rel_table, blk):
    i = jnp.arange(blk)
    out = []
    for d in range(_RING):
        b = _rel_bias(rel_table, blk, blk, d * blk, -1)
        cd = (i[None, :] // CHUNK) - (i[:, None] // CHUNK) + d * (blk // CHUNK)
        ok = (cd >= 0) & (cd <= BAND_CHUNKS)
        out.append(jnp.where(ok[None], b, NEG_INF))
    return jnp.stack(out).astype(F32)


def _att_prompt(x2d, n_seq, seq_len, w_qkv, w_o, rel_table, ln_g, ln_b, wr_hi, wr_lo, br, alpha, n_all):
    d_model = x2d.shape[1]
    blk = ATT_BLOCK
    nblk = seq_len // blk
    keep = min(BAND_CHUNKS * CHUNK, seq_len)
    assert seq_len % blk == 0 and keep % blk == 0
    assert (_RING - 1) * blk >= BAND_CHUNKS * CHUNK
    kb = keep // blk
    bias = _att_prompt_bias(rel_table, blk)
    n_steps = n_seq * nblk
    n_fill = (n_all - n_seq * seq_len) // blk
    assert n_fill * blk == n_all - n_seq * seq_len

    def work(t):
        return jnp.minimum(t, n_steps - 1)

    row_spec = pl.BlockSpec(
        (None, blk, d_model), lambda t: (work(t) // nblk, jnp.maximum(work(t) % nblk - (nblk - kb), 0), 0))
    rows_out = jax.ShapeDtypeStruct((n_seq, keep, d_model), F32)
    return pl.pallas_call(
        functools.partial(_att_prompt_kernel, blk=blk, nblk=nblk, n_steps=n_steps, n_fill=n_fill, alpha=alpha),
        grid=(n_steps + n_fill,),
        in_specs=[
            pl.BlockSpec((blk, d_model), lambda t: (work(t), 0)),
            _const_spec(w_qkv.shape),
            _const_spec(w_o.shape),
            _const_spec(bias.shape),
            _const_spec(ln_g.shape),
            _const_spec(ln_b.shape),
            _const_spec(wr_hi.shape),
            _const_spec(wr_lo.shape),
            _const_spec(br.shape),
        ],
        out_specs=_token_out_specs(blk, d_model, lambda t: (t, 0)) + (row_spec, row_spec),
        out_shape=_token_out_shapes(n_all, d_model) + (rows_out, rows_out),
        scratch_shapes=[
            pltpu.VMEM((_RING, blk, d_model), BF16),
            pltpu.VMEM((_RING, d_model, blk), BF16),
            pltpu.VMEM((d_model, blk), BF16),
        ],
        compiler_params=pltpu.CompilerParams(
            dimension_semantics=("arbitrary",), vmem_limit_bytes=VMEM_LIMIT),
        name="att_mixer",
    )(x2d, w_qkv, w_o, bias, ln_g, ln_b, wr_hi, wr_lo, br)


def _att_sample_kernel(*refs, nb, blk, alpha):
    (x_ref, kc_ref, vc_ref, wqkv_ref, wo_ref, bias_c_ref, bias_n_ref, lng_ref, lnb_ref,
     wrh_ref, wrl_ref, br_ref) = refs[:12]
    h_ref, hp_ref, lgt_ref, krow_ref, vrow_ref, o_scr = refs[-6:]
    d_model = x_ref.shape[1]
    heads = ATT_HEADS
    dh = d_model // heads
    x = x_ref[...]
    xb = x.astype(BF16)
    q = _dot(xb, wqkv_ref[:, 0:d_model]) * (dh ** -0.5)
    k = _dot(xb, wqkv_ref[:, d_model:2 * d_model])
    v = _dot(xb, wqkv_ref[:, 2 * d_model:3 * d_model])
    krow_ref[...] = k.reshape(nb, blk, d_model)
    vrow_ref[...] = v.reshape(nb, blk, d_model)
    lane_head = lax.broadcasted_iota(jnp.int32, (heads, 1, d_model), 2) // dh
    head_id = lax.broadcasted_iota(jnp.int32, (heads, 1, d_model), 0)
    head_mask = (lane_head == head_id).astype(F32)

    for s in range(nb):
        r0 = s * blk
        qs = q[r0:r0 + blk]
        q_bd = (qs[None, :, :] * head_mask).reshape(heads * blk, d_model).astype(BF16)
        kn = k[r0:r0 + blk].astype(BF16)
        vn = v[r0:r0 + blk].astype(BF16)
        s_c = _dot_nt(q_bd, kc_ref[s].astype(BF16)) + bias_c_ref[...]
        s_n = _dot_nt(q_bd, kn) + bias_n_ref[...]
        m = jnp.maximum(s_c.max(axis=-1, keepdims=True), s_n.max(axis=-1, keepdims=True))
        p_c = jnp.exp(s_c - m)
        p_n = jnp.exp(s_n - m)
        l = p_c.sum(axis=-1, keepdims=True) + p_n.sum(axis=-1, keepdims=True)
        o_full = _dot(p_c.astype(BF16), vc_ref[s].astype(BF16)) + _dot(p_n.astype(BF16), vn)
        o_full = o_full * (1.0 / l)
        o = (o_full.reshape(heads, blk, d_model) * head_mask).sum(axis=0)
        o_scr[r0:r0 + blk, :] = o.astype(BF16)

    mix = _dot(o_scr[...], wo_ref[...])
    _finish_tokens(x, mix, alpha, lng_ref, lnb_ref, wrh_ref, wrl_ref, br_ref, h_ref, hp_ref, lgt_ref)


def _att_sample(x2d, in_row0, n_seq, seq_len, k_cache, v_cache, w_qkv, w_o, rel_table, ln_g, ln_b,
                wr_hi, wr_lo, br, alpha, nb, n_all, out_row0, dst):
    d_model = x2d.shape[1]
    heads = ATT_HEADS
    blk = seq_len
    n_cache = k_cache.shape[1]
    rows = nb * blk
    assert n_seq % nb == 0 and in_row0 % rows == 0 and out_row0 % rows == 0
    in_b0, out_b0 = in_row0 // rows, out_row0 // rows
    bias_c = _rel_bias(rel_table, blk, n_cache, n_cache, 1).reshape(heads * blk, n_cache)
    bias_n = _rel_bias(rel_table, blk, blk, 0, 1).reshape(heads * blk, blk)
    kc = k_cache.reshape(n_seq, n_cache, d_model)
    vc = v_cache.reshape(n_seq, n_cache, d_model)
    cache_spec = pl.BlockSpec((nb, n_cache, d_model), lambda g: (g, 0, 0))
    row_spec = pl.BlockSpec((nb, blk, d_model), lambda g: (g, 0, 0))
    rows_out = jax.ShapeDtypeStruct((n_seq, blk, d_model), F32)
    in_specs = [
        pl.BlockSpec((rows, d_model), lambda g: (in_b0 + g, 0)),
        cache_spec,
        cache_spec,
        _const_spec(w_qkv.shape),
        _const_spec(w_o.shape),
        _const_spec(bias_c.shape),
        _const_spec(bias_n.shape),
        _const_spec(ln_g.shape),
        _const_spec(ln_b.shape),
        _const_spec(wr_hi.shape),
        _const_spec(wr_lo.shape),
        _const_spec(br.shape),
    ]
    args = [x2d, kc, vc, w_qkv, w_o, bias_c, bias_n, ln_g, ln_b, wr_hi, wr_lo, br]
    aliases = _alias_dst(args, in_specs, dst)
    return pl.pallas_call(
        functools.partial(_att_sample_kernel, nb=nb, blk=blk, alpha=alpha),
        grid=(n_seq // nb,),
        in_specs=in_specs,
        out_specs=_token_out_specs(rows, d_model, lambda g: (out_b0 + g, 0)) + (row_spec, row_spec),
        out_shape=_token_out_shapes(n_all, d_model) + (rows_out, rows_out),
        input_output_aliases=aliases,
        scratch_shapes=[pltpu.VMEM((rows, d_model), BF16)],
        compiler_params=pltpu.CompilerParams(
            dimension_semantics=("arbitrary",), vmem_limit_bytes=VMEM_LIMIT),
        name="att_mixer_cache",
    )(*args)


ROUTE_ROWS = 40


def _route_kernel(lgt_ref, meta_t_ref, meta_w_ref, count_ref, carry_ref, tri_ref):
    j = pl.program_id(0)
    tm = lgt_ref.shape[0]

    @pl.when(j == 0)
    def _():
        carry_ref[...] = jnp.zeros(carry_ref.shape, F32)
        r = lax.broadcasted_iota(jnp.int32, (tm, tm), 0)
        c = lax.broadcasted_iota(jnp.int32, (tm, tm), 1)
        tri_ref[...] = (r < c).astype(BF16)

    a = jnp.transpose(lgt_ref[...])[0:ROUTE_ROWS, :]
    row = lax.broadcasted_iota(jnp.int32, (ROUTE_ROWS, tm), 0)
    big = jnp.int32(LANES)
    glog = jnp.where(row < N_GROUPS, a, NEG_INF)
    gmax = glog.max(axis=0, keepdims=True)
    gsel = jnp.where(glog == gmax, row, big).min(axis=0, keepdims=True)
    gp = 1.0 / jnp.exp(glog - gmax).sum(axis=0, keepdims=True)
    first = N_GROUPS + gsel * EXP_PER_GROUP
    in_grp = (row >= first) & (row < first + EXP_PER_GROUP)
    elog = jnp.where(in_grp, a, NEG_INF)
    v1 = elog.max(axis=0, keepdims=True)
    l1 = jnp.where(elog == v1, row, big).min(axis=0, keepdims=True)
    elog2 = jnp.where(row == l1, NEG_INF, elog)
    v2 = elog2.max(axis=0, keepdims=True)
    l2 = jnp.where(elog2 == v2, row, big).min(axis=0, keepdims=True)
    e2 = jnp.exp(v2 - v1)
    w1 = gp / (1.0 + e2)
    w2 = gp * e2 / (1.0 + e2)

    is1 = row == l1
    is2 = row == l2
    oh = (is1 | is2).astype(F32)
    before = _dot(oh.astype(BF16), tri_ref[...]) + carry_ref[:, 0:1]
    rank1 = jnp.where(is1, before, 0.0).sum(axis=0, keepdims=True)
    rank2 = jnp.where(is2, before, 0.0).sum(axis=0, keepdims=True)
    carry_ref[...] = carry_ref[...] + oh.sum(axis=1, keepdims=True)
    count_ref[...] = carry_ref[...].astype(jnp.int32)

    pad = jnp.zeros((SUBLANES - 4, tm), jnp.int32)
    meta_t_ref[...] = jnp.concatenate(
        [l1 - N_GROUPS, l2 - N_GROUPS, rank1.astype(jnp.int32), rank2.astype(jnp.int32), pad], axis=0)
    wt = jnp.concatenate([w1, w2, jnp.zeros((LANES - 2, tm), F32)], axis=0)
    meta_w_ref[...] = jnp.transpose(wt)


def _route(logits):
    n_tok = logits.shape[0]
    tm = TOKEN_TILE
    assert n_tok % tm == 0
    tile = pl.BlockSpec((tm, LANES), lambda j: (j, 0))
    return pl.pallas_call(
        _route_kernel,
        grid=(n_tok // tm,),
        in_specs=[tile],
        out_specs=(pl.BlockSpec((SUBLANES, tm), lambda j: (0, j)), tile,
                   pl.BlockSpec((ROUTE_ROWS, LANES), lambda j: (0, 0))),
        out_shape=(
            jax.ShapeDtypeStruct((SUBLANES, n_tok), jnp.int32),
            jax.ShapeDtypeStruct((n_tok, LANES), F32),
            jax.ShapeDtypeStruct((ROUTE_ROWS, LANES), jnp.int32),
        ),
        scratch_shapes=[pltpu.VMEM((ROUTE_ROWS, LANES), F32), pltpu.VMEM((tm, tm), BF16)],
        compiler_params=pltpu.CompilerParams(dimension_semantics=("arbitrary",)),
        name="moe_route",
    )(logits)


def _sc_gather_loop(table_hbm, idx_v, out_hbm, rows_v, sems, base, n_chunk, r):
    def gather(c, slot):
        off = pl.multiple_of(c * r, r)
        return pltpu.make_async_copy(table_hbm.at[idx_v.at[pl.ds(off, r)]], rows_v.at[slot], sems.at[slot])

    def finish(c, slot):
        gather(c, slot).wait()
        pltpu.sync_copy(rows_v.at[slot], out_hbm.at[pl.ds(base + pl.multiple_of(c * r, r), r)])

    gather(0, 0).start()
    if n_chunk > 1:
        gather(1, 1).start()

    def body(pair, carry):
        c = 2 * pair
        for slot in range(2):
            finish(c + slot, slot)

            @pl.when(c + slot + 2 < n_chunk)
            def _():
                gather(c + slot + 2, slot).start()
        return carry

    lax.fori_loop(0, n_chunk // 2, body, 0)
    if n_chunk % 2:
        finish(n_chunk - 1, 0)


def _gather_rows(table, idx):
    m = idx.shape[0]
    width = table.shape[1]
    r = SC_ROWS_PER_CHUNK
    assert m % (SC_WORKERS * r) == 0
    per_w = m // SC_WORKERS
    mesh = plsc.VectorSubcoreMesh(core_axis_name="c", subcore_axis_name="s")

    @functools.partial(
        pl.kernel,
        mesh=mesh,
        out_type=jax.ShapeDtypeStruct((m, width), table.dtype),
        scratch_types=[
            pltpu.VMEM((per_w,), jnp.int32),
            pltpu.VMEM((2, r, width), table.dtype),
            pltpu.SemaphoreType.DMA((2,)),
        ],
    )
    def gather(table_hbm, idx_hbm, out_hbm, idx_v, rows_v, sem):
        wid = lax.axis_index("s") * 2 + lax.axis_index("c")
        base = wid * per_w
        pltpu.sync_copy(idx_hbm.at[pl.ds(base, per_w)], idx_v)
        _sc_gather_loop(table_hbm, idx_v, out_hbm, rows_v, sem, base, per_w // r, r)

    return gather(table, idx)


def _dispatch_rows(table, pos, n_rows):
    n_tok, width = table.shape
    n_pairs = pos.shape[0]
    r = SC_ROWS_PER_CHUNK
    lanes = SC_LANES
    assert n_rows % (SC_WORKERS * r) == 0
    per_w = n_rows // SC_WORKERS
    n_stage = 16
    stage = n_pairs // n_stage
    assert stage * n_stage == n_pairs and stage % lanes == 0 and per_w % lanes == 0
    assert n_rows < 3 * n_tok
    mesh = plsc.VectorSubcoreMesh(core_axis_name="c", subcore_axis_name="s")

    @functools.partial(
        pl.kernel,
        mesh=mesh,
        out_type=jax.ShapeDtypeStruct((n_rows, width), table.dtype),
        scratch_types=[
            pltpu.VMEM((per_w,), jnp.int32),
            pltpu.VMEM((stage,), jnp.int32),
            pltpu.VMEM((2, r, width), table.dtype),
            pltpu.SemaphoreType.DMA((2,)),
        ],
        compiler_params=pltpu.CompilerParams(needs_layout_passes=False),
    )
    def dispatch(table_hbm, pos_hbm, out_hbm, src_v, pos_v, rows_v, sem):
        wid = lax.axis_index("s") * 2 + lax.axis_index("c")
        base = wid * per_w
        lane = lax.iota(jnp.int32, lanes)

        def wrap(t):
            t = jnp.where(t >= n_tok, t - n_tok, t)
            return jnp.where(t >= n_tok, t - n_tok, t)

        def init(i, carry):
            off = pl.multiple_of(i * lanes, lanes)
            src_v[pl.ds(off, lanes)] = wrap(base + off + lane)
            return carry

        lax.fori_loop(0, per_w // lanes, init, 0)

        def scan_stage(sidx, carry):
            pair0 = sidx * stage
            pltpu.sync_copy(pos_hbm.at[pl.ds(pl.multiple_of(pair0, 8), stage)], pos_v)

            def scan(i, c2):
                off = pl.multiple_of(i * lanes, lanes)
                local = pos_v[pl.ds(off, lanes)] - base
                mine = (local >= 0) & (local < per_w)
                plsc.store_scatter(src_v, [jnp.where(mine, local, 0)], wrap(pair0 + off + lane), mask=mine)
                return c2

            lax.fori_loop(0, stage // lanes, scan, 0)
            return carry

        lax.fori_loop(0, n_stage, scan_stage, 0)
        _sc_gather_loop(table_hbm, src_v, out_hbm, rows_v, sem, base, per_w // r, r)

    return dispatch(table, pos)


def _gemm_kernel(tile_expert_ref, n_used_ref, xs_ref, wg_ref, wu_ref, wd_ref, ys_ref, wg_b, wu_b, wd_b):
    j = pl.program_id(0)

    @pl.when(j < n_used_ref[0])
    def _():
        new_expert = (j == 0) | (tile_expert_ref[j] != tile_expert_ref[jnp.maximum(j - 1, 0)])

        @pl.when(new_expert)
        def _():
            wg_b[...] = wg_ref[...].astype(BF16)
            wu_b[...] = wu_ref[...].astype(BF16)
            wd_b[...] = wd_ref[...].astype(BF16)

        x = _unpack_rows(xs_ref[...]).astype(BF16)
        g = _dot(x, wg_b[...])
        u = _dot(x, wu_b[...])
        hmid = (jax.nn.silu(g) * u).astype(BF16)
        ys_ref[...] = _pack_rows(_dot(hmid, wd_b[...]))


def _expert_gemm(xs, tile_expert, n_used, layer, w_gate, w_up, w_down):
    n_rows, half = xs.shape
    d_model, d_exp = w_gate.shape[2], w_gate.shape[3]
    tm = GEMM_TILE
    n_tiles = n_rows // tm

    def row_map(j, te, nu):
        return (jnp.minimum(j, nu[0] - 1), 0)

    def w_map(j, te, nu):
        return (layer, te[j], 0, 0)

    grid_spec = pltpu.PrefetchScalarGridSpec(
        num_scalar_prefetch=2,
        grid=(n_tiles,),
        in_specs=[
            pl.BlockSpec((tm, half), row_map),
            pl.BlockSpec((None, None, d_model, d_exp), w_map),
            pl.BlockSpec((None, None, d_model, d_exp), w_map),
            pl.BlockSpec((None, None, d_exp, d_model), w_map),
        ],
        out_specs=pl.BlockSpec((tm, half), row_map),
        scratch_shapes=[
            pltpu.VMEM((d_model, d_exp), BF16),
            pltpu.VMEM((d_model, d_exp), BF16),
            pltpu.VMEM((d_exp, d_model), BF16),
        ],
    )
    return pl.pallas_call(
        _gemm_kernel,
        grid_spec=grid_spec,
        out_shape=jax.ShapeDtypeStruct((n_rows, half), U32),
        compiler_params=pltpu.CompilerParams(
            dimension_semantics=("arbitrary",), vmem_limit_bytes=VMEM_LIMIT),
        name="moe_gemm",
    )(tile_expert, n_used, xs, w_gate, w_up, w_down)


def _combine_kernel(y0_ref, y1_ref, mw_ref, h_ref, pp_ref, ps_ref, lng_ref, lnb_ref, wproj_ref, wgate_ref,
                    *out_refs, n_prompt_tiles, alpha):
    j = pl.program_id(0)
    mw = mw_ref[...]
    ffn = mw[:, 0:1] * _unpack_rows(y0_ref[...]) + mw[:, 1:2] * _unpack_rows(y1_ref[...])
    h2 = _layer_norm(alpha * h_ref[...] + ffn, lng_ref[...], lnb_ref[...])
    p = jnp.where(j < n_prompt_tiles, pp_ref[...], ps_ref[...]).astype(BF16)
    gate = jax.nn.sigmoid(_dot(h2.astype(BF16), wgate_ref[...]))
    out = h2 + gate * _dot(p, wproj_ref[...])
    if len(out_refs) == 1:
        out_refs[0][...] = out
    else:
        @pl.when(j < n_prompt_tiles)
        def _():
            out_refs[0][...] = out

        @pl.when(j >= n_prompt_tiles)
        def _():
            out_refs[1][...] = out


def _combine(yg, meta_w, h, p_prompt, p_sample, layer, ln_g, ln_b, w_proj, w_gate, alpha, split_out):
    n_tok, d_model = h.shape
    tm = TOKEN_TILE
    n_t = n_tok // tm
    n_tp = p_prompt.shape[1] // tm
    n_ts = p_sample.shape[1] // tm
    assert n_tp * tm == p_prompt.shape[1] and n_ts * tm == p_sample.shape[1] and n_tp + n_ts == n_t
    ple = p_prompt.shape[2]
    tile = pl.BlockSpec((tm, d_model), lambda j: (j, 0))
    if split_out:
        out_specs = (pl.BlockSpec((tm, d_model), lambda j: (jnp.minimum(j, n_tp - 1), 0)),
                     pl.BlockSpec((tm, d_model), lambda j: (jnp.maximum(j - n_tp, 0), 0)))
        out_shape = (jax.ShapeDtypeStruct((n_tp * tm, d_model), F32),
                     jax.ShapeDtypeStruct((n_ts * tm, d_model), F32))
    else:
        out_specs = tile
        out_shape = jax.ShapeDtypeStruct((n_tok, d_model), F32)
    return pl.pallas_call(
        functools.partial(_combine_kernel, n_prompt_tiles=n_tp, alpha=alpha),
        grid=(n_t,),
        in_specs=[
            pl.BlockSpec((tm, d_model // 2), lambda j: (j, 0)),
            pl.BlockSpec((tm, d_model // 2), lambda j: (j + n_t, 0)),
            pl.BlockSpec((tm, LANES), lambda j: (j, 0)),
            tile,
            pl.BlockSpec((None, tm, ple), lambda j: (layer, jnp.minimum(j, n_tp - 1), 0)),
            pl.BlockSpec((None, tm, ple), lambda j: (layer, jnp.maximum(j - n_tp, 0), 0)),
            _const_spec(ln_g.shape),
            _const_spec(ln_b.shape),
            _const_spec(w_proj.shape),
            _const_spec(w_gate.shape),
        ],
        out_specs=out_specs,
        out_shape=out_shape,
        compiler_params=pltpu.CompilerParams(
            dimension_semantics=("arbitrary",), vmem_limit_bytes=VMEM_LIMIT),
        name="moe_combine",
    )(yg, yg, meta_w, h, p_prompt, p_sample, ln_g, ln_b, w_proj, w_gate)


def _router_weights(w_grp, b_grp, w_exp, b_exp):
    d_model = w_grp.shape[0]
    w = jnp.concatenate([w_grp, jnp.transpose(w_exp, (1, 0, 2)).reshape(d_model, N_EXPERTS)], axis=1)
    w = jnp.pad(w, ((0, 0), (0, LANES - w.shape[1])))
    b = jnp.concatenate([b_grp, b_exp.reshape(N_EXPERTS)])
    b = jnp.pad(b, (0, LANES - b.shape[0])).reshape(1, LANES)
    w_hi = w.astype(BF16)
    w_lo = (w - w_hi.astype(F32)).astype(BF16)
    return w_hi, w_lo, b


def _moe(h_packed, logits, layer, w_gate, w_up, w_down):
    n_tok = h_packed.shape[0]
    tm = GEMM_TILE
    meta_t, meta_w, counts = _route(logits)
    counts = counts[N_GROUPS:N_GROUPS + N_EXPERTS, 0]
    tiles_per_expert = (counts + tm - 1) // tm
    tile_end = jnp.cumsum(tiles_per_expert)
    row_start = (tile_end - tiles_per_expert) * tm
    eid = meta_t[0:2]
    rank = meta_t[2:4]
    experts = jnp.arange(N_EXPERTS, dtype=jnp.int32)
    start = jnp.sum(jnp.where(eid[:, :, None] == experts, row_start, 0), axis=-1)
    pos = (start + rank).reshape(-1).astype(jnp.int32)
    gather_quant = SC_WORKERS * SC_ROWS_PER_CHUNK
    n_tiles = -(-(2 * n_tok) // tm) + N_EXPERTS
    n_rows = -(-(n_tiles * tm) // gather_quant) * gather_quant
    n_tiles = n_rows // tm
    n_used = tile_end[-1:].astype(jnp.int32)
    tile_ids = jnp.minimum(jnp.arange(n_tiles, dtype=jnp.int32), n_used[0] - 1)
    tile_expert = jnp.sum(tile_end[None, :] <= tile_ids[:, None], axis=1).astype(jnp.int32)
    xs = _dispatch_rows(h_packed, pos, n_rows)
    ys = _expert_gemm(xs, tile_expert, n_used, layer, w_gate, w_up, w_down)
    return _gather_rows(ys, pos), meta_w


def kernel(x_prompt, x_sample, p_prompt, p_sample, state_ret, cache_att_k, cache_att_v, ret_w_in, ret_gn_g,
           ret_w_o, att_w_qkv, att_rel_bias, att_w_o, ln1_g, ln1_b, ln2_g, ln2_b, moe_w_grp, moe_b_grp,
           moe_w_exp, moe_b_exp, moe_w_gate, moe_w_up, moe_w_down, ple_w_proj, ple_w_gate):
    n_p, len_p, d_model = x_prompt.shape
    n_s, len_s, _ = x_sample.shape
    depth = ln1_g.shape[0]
    alpha = float((2 * depth) ** 0.25)
    tok_p, tok_s = n_p * len_p, n_s * len_s
    n_all = tok_p + tok_s
    dh = d_model // ATT_HEADS
    pp = p_prompt.reshape(depth, tok_p, -1)
    ps = p_sample.reshape(depth, tok_s, -1)
    nb_s = 2

    x_all = None
    y_prompt = y_sample = None
    states_p, states_s, k_p, v_p, k_s, v_s = [], [], [], [], [], []
    for i in range(depth):
        jj = i // 2
        wr_hi, wr_lo, br = _router_weights(moe_w_grp[i], moe_b_grp[i], moe_w_exp[i], moe_b_exp[i])
        lng, lnb = ln1_g[i].reshape(1, d_model), ln1_b[i].reshape(1, d_model)
        if x_all is None:
            src_p, src_s, row_s = x_prompt.reshape(tok_p, d_model), x_sample.reshape(tok_s, d_model), 0
        else:
            src_p, src_s, row_s = x_all, x_all, tok_p
        if i % 2 == 0:
            w_in = ret_w_in[jj].astype(BF16)
            w_o = ret_w_o[jj].astype(BF16)
            gn = ret_gn_g[jj].reshape(1, -1)
            h, hp, lgt, st_p = _ret_mixer(src_p, 0, n_p, len_p, 0, None, w_in, w_o, gn, lng, lnb,
                                          wr_hi, wr_lo, br, alpha, nb=1, blk=min(RET_BLOCK, len_p),
                                          n_all=n_all, out_row0=0, dst=None)
            h, hp, lgt, st_s = _ret_mixer(src_s, row_s, n_s, len_s, PAST_LEN, state_ret[jj], w_in, w_o, gn,
                                          lng, lnb, wr_hi, wr_lo, br, alpha, nb=nb_s, blk=len_s,
                                          n_all=n_all, out_row0=tok_p, dst=(h, hp, lgt))
            states_p.append(st_p)
            states_s.append(st_s)
        else:
            w_qkv = att_w_qkv[jj].astype(BF16)
            w_o = att_w_o[jj].astype(BF16)
            h, hp, lgt, kr, vr = _att_prompt(src_p, n_p, len_p, w_qkv, w_o, att_rel_bias[jj], lng, lnb,
                                             wr_hi, wr_lo, br, alpha, n_all)
            k_p.append(kr.reshape(n_p, -1, ATT_HEADS, dh))
            v_p.append(vr.reshape(n_p, -1, ATT_HEADS, dh))
            h, hp, lgt, kr, vr = _att_sample(src_s, row_s, n_s, len_s, cache_att_k[jj], cache_att_v[jj],
                                             w_qkv, w_o, att_rel_bias[jj], lng, lnb, wr_hi, wr_lo, br, alpha,
                                             nb=nb_s, n_all=n_all, out_row0=tok_p, dst=(h, hp, lgt))
            k_s.append(kr.reshape(n_s, len_s, ATT_HEADS, dh))
            v_s.append(vr.reshape(n_s, len_s, ATT_HEADS, dh))
        yg, meta_w = _moe(hp, lgt, i, moe_w_gate, moe_w_up, moe_w_down)
        last = i == depth - 1
        out = _combine(yg, meta_w, h, pp, ps, i, ln2_g[i].reshape(1, d_model), ln2_b[i].reshape(1, d_model),
                       ple_w_proj[i].astype(BF16), ple_w_gate[i].astype(BF16), alpha, split_out=last)
        if last:
            y_prompt = out[0].reshape(n_p, len_p, d_model)
            y_sample = out[1].reshape(n_s, len_s, d_model)
        else:
            x_all = out

    return (y_prompt, y_sample, jnp.stack(states_p), jnp.stack(states_s),
            jnp.stack(k_p), jnp.stack(v_p), jnp.stack(k_s), jnp.stack(v_s))
```

```python
import functools
import math

import numpy as np
import jax
import jax.numpy as jnp
from jax import lax
from jax.experimental import pallas as pl
from jax.experimental.pallas import tpu as pltpu
from jax.experimental.pallas import tpu_sc as plsc

CHUNK = 64
PAST_LEN = 2048
RET_HEADS = 4
ROPE_BASE = 10000.0
ATT_HEADS = 16
BAND_CHUNKS = 8
REL_CLIP = 256
N_GROUPS = 4
EXP_PER_GROUP = 8
N_EXPERTS = N_GROUPS * EXP_PER_GROUP
LN_EPS = 1e-5
NEG_INF = -1e30

LANES = 128
SUBLANES = 8
SC_WORKERS = 32
SC_LANES = 16
SC_ROWS_PER_CHUNK = 32
VMEM_LIMIT = 56 * 1024 * 1024

RET_BLOCK = 256
ATT_BLOCK = 4 * CHUNK
TOKEN_TILE = 512
GEMM_TILE = 512
MOE_PARTS = 2

F32 = jnp.float32
BF16 = jnp.bfloat16
U32 = jnp.uint32
HI_MASK = 0xFFFF0000


def _dot(a, b):
    return jnp.dot(a, b, preferred_element_type=F32)


def _dot_nt(a, b):
    return lax.dot_general(a, b, (((1,), (1,)), ((), ())), preferred_element_type=F32)


def _dot_tn(a, b):
    return lax.dot_general(a, b, (((0,), (0,)), ((), ())), preferred_element_type=F32)


def _layer_norm(x, g, b):
    mu = jnp.mean(x, axis=-1, keepdims=True)
    xc = x - mu
    var = jnp.mean(xc * xc, axis=-1, keepdims=True)
    return xc * lax.rsqrt(var + LN_EPS) * g + b


def _pack_rows(x):
    half = x.shape[1] // 2
    lo = lax.bitcast_convert_type(x[:, :half].astype(BF16).astype(F32), U32) >> 16
    hi = lax.bitcast_convert_type(x[:, half:].astype(BF16).astype(F32), U32) & U32(HI_MASK)
    return lo | hi


def _unpack_rows(p):
    lo = lax.bitcast_convert_type(p << 16, F32)
    hi = lax.bitcast_convert_type(p & U32(HI_MASK), F32)
    return jnp.concatenate([lo, hi], axis=1)


def _router_logits(h, wr_hi_ref, wr_lo_ref, br_ref):
    h_hi = h.astype(BF16)
    h_lo = (h - h_hi.astype(F32)).astype(BF16)
    w_hi = wr_hi_ref[...]
    return _dot(h_hi, w_hi) + _dot(h_lo, w_hi) + _dot(h_hi, wr_lo_ref[...]) + br_ref[...]


def _finish_tokens(x, mix, alpha, lng_ref, lnb_ref, wrh_ref, wrl_ref, br_ref, h_ref, hp_ref, lgt_ref):
    hh = _layer_norm(alpha * x + mix, lng_ref[...], lnb_ref[...])
    h_ref[...] = hh
    hp_ref[...] = _pack_rows(hh)
    lgt_ref[...] = _router_logits(hh, wrh_ref, wrl_ref, br_ref)


def _zero_tokens(h_ref, hp_ref, lgt_ref):
    h_ref[...] = jnp.zeros(h_ref.shape, h_ref.dtype)
    hp_ref[...] = jnp.zeros(hp_ref.shape, hp_ref.dtype)
    lgt_ref[...] = jnp.zeros(lgt_ref.shape, lgt_ref.dtype)


def _const_spec(shape):
    nd = len(shape)
    return pl.BlockSpec(shape, lambda *_: (0,) * nd, pipeline_mode=pl.Buffered(1))


def _token_out_shapes(n_all, d_model):
    return (
        jax.ShapeDtypeStruct((n_all, d_model), F32),
        jax.ShapeDtypeStruct((n_all, d_model // 2), U32),
        jax.ShapeDtypeStruct((n_all, LANES), F32),
    )


def _token_out_specs(rows, d_model, row_map):
    return (
        pl.BlockSpec((rows, d_model), row_map),
        pl.BlockSpec((rows, d_model // 2), row_map),
        pl.BlockSpec((rows, LANES), row_map),
    )


def _alias_dst(args, in_specs, dst):
    if dst is None:
        return {}
    aliases = {}
    for k, arr in enumerate(dst):
        aliases[len(args)] = k
        args.append(arr)
        in_specs.append(pl.BlockSpec(memory_space=pl.ANY))
    return aliases


def _ret_log_gamma():
    h = np.arange(RET_HEADS, dtype=np.float32)
    return np.log(np.float32(1.0) - np.float32(2.0) ** (np.float32(-5.0) - h)).astype(np.float32)


def _ret_kernel(*refs, nb, blk, nblk, n_steps, n_fill, has_state, n_alias, alpha):
    step = pl.program_id(0)
    h_ref, hp_ref, lgt_ref = refs[12 + int(has_state) + n_alias:][:3]

    @pl.when(step < n_steps)
    def _():
        _ret_step(*refs, first_block=step % nblk == 0, nb=nb, blk=blk, has_state=has_state,
                  n_alias=n_alias, alpha=alpha)

    if n_fill:
        @pl.when(step >= n_steps)
        def _():
            _zero_tokens(h_ref, hp_ref, lgt_ref)


def _ret_step(*refs, first_block, nb, blk, has_state, n_alias, alpha):
    (x_ref, cos_ref, sin_ref, dmask_ref, win_ref, wo_ref, gn_ref, lng_ref, lnb_ref,
     wrh_ref, wrl_ref, br_ref) = refs[:12]
    s_in_ref = refs[12] if has_state else None
    h_ref, hp_ref, lgt_ref, s_out_ref, gated_ref = refs[12 + int(has_state) + n_alias:]
    heads = RET_HEADS
    d_model = x_ref.shape[1]
    dk = d_model // heads
    dv = 2 * d_model // heads
    hk, hv = heads * dk, heads * dv
    half = dk // 2
    lg = _ret_log_gamma()

    x = x_ref[...]
    xb = x.astype(BF16)
    q_all = _dot(xb, win_ref[:, 0:hk])
    k_all = _dot(xb, win_ref[:, hk:2 * hk])
    v_all = _dot(xb, win_ref[:, 2 * hk:2 * hk + hv])
    g_all = _dot(xb, win_ref[:, 2 * hk + hv:2 * hk + 2 * hv])
    cos = cos_ref[...]
    sin = sin_ref[...]
    rowf = lax.broadcasted_iota(jnp.int32, (blk, 1), 0).astype(F32)

    def rot(t):
        t1, t2 = t[:, :half], t[:, half:]
        return jnp.concatenate([t1 * cos - t2 * sin, t1 * sin + t2 * cos], axis=1)

    if not has_state:
        @pl.when(first_block)
        def _():
            s_out_ref[...] = jnp.zeros(s_out_ref.shape, F32)

    s_prev_ref = s_in_ref if has_state else s_out_ref
    for s in range(nb):
        r0 = s * blk
        for h in range(heads):
            lgh = float(lg[h])
            q = rot(q_all[r0:r0 + blk, h * dk:(h + 1) * dk])
            k = rot(k_all[r0:r0 + blk, h * dk:(h + 1) * dk]) * (dk ** -0.5)
            v = v_all[r0:r0 + blk, h * dv:(h + 1) * dv]
            g = g_all[r0:r0 + blk, h * dv:(h + 1) * dv]
            vb = v.astype(BF16)
            scores = _dot_nt(q.astype(BF16), k.astype(BF16)) * dmask_ref[h]
            inner = _dot(scores.astype(BF16), vb)
            s_prev = s_prev_ref[s, h]
            q_dec = q * jnp.exp(lgh * (rowf + 1.0))
            cross = _dot(q_dec.astype(BF16), s_prev.astype(BF16))
            k_dec = k * jnp.exp(lgh * (float(blk - 1) - rowf))
            s_out_ref[s, h] = math.exp(lgh * blk) * s_prev + _dot_tn(k_dec.astype(BF16), vb)
            o = inner + cross
            mu = jnp.mean(o, axis=-1, keepdims=True)
            oc = o - mu
            var = jnp.mean(oc * oc, axis=-1, keepdims=True)
            on = oc * lax.rsqrt(var + LN_EPS) * gn_ref[:, h * dv:(h + 1) * dv]
            gated_ref[r0:r0 + blk, h * dv:(h + 1) * dv] = (jax.nn.silu(g) * on).astype(BF16)

    mix = _dot(gated_ref[...], wo_ref[...])
    _finish_tokens(x, mix, alpha, lng_ref, lnb_ref, wrh_ref, wrl_ref, br_ref, h_ref, hp_ref, lgt_ref)


def _ret_mixer(x2d, in_row0, n_seq, seq_len, pos0, state_in, w_in, w_o, gn_g, ln_g, ln_b,
               wr_hi, wr_lo, br, alpha, nb, blk, n_all, out_row0, dst):
    d_model = x2d.shape[1]
    heads = RET_HEADS
    dk, dv = d_model // heads, 2 * d_model // heads
    half = dk // 2
    nblk = seq_len // blk
    has_state = state_in is not None
    rows = nb * blk
    assert seq_len % blk == 0 and n_seq % nb == 0
    assert (not has_state) or nblk == 1
    assert nb == 1 or nblk == 1
    assert in_row0 % rows == 0 and out_row0 % rows == 0
    in_b0, out_b0 = in_row0 // rows, out_row0 // rows

    pos = (pos0 + jnp.arange(seq_len, dtype=jnp.int32)).astype(F32)
    inv_freq = ROPE_BASE ** (-jnp.arange(half, dtype=F32) / half)
    ang = pos[:, None] * inv_freq[None, :]
    cos, sin = jnp.cos(ang), jnp.sin(ang)
    lg = jnp.asarray(_ret_log_gamma())
    ii = jnp.arange(blk, dtype=F32)
    diff = ii[:, None] - ii[None, :]
    dmask = jnp.where(diff >= 0, jnp.exp(lg[:, None, None] * jnp.maximum(diff, 0.0)), 0.0)

    n_steps = (n_seq // nb) * nblk
    n_fill = 0 if dst is not None else (n_all - n_seq * seq_len) // rows
    assert dst is not None or (out_row0 == 0 and n_fill * rows == n_all - n_seq * seq_len)

    def work(t):
        return jnp.minimum(t, n_steps - 1)

    in_specs = [
        pl.BlockSpec((rows, d_model), lambda t: (in_b0 + work(t), 0)),
        pl.BlockSpec((blk, half), lambda t: (work(t) % nblk, 0)),
        pl.BlockSpec((blk, half), lambda t: (work(t) % nblk, 0)),
        _const_spec(dmask.shape),
        _const_spec(w_in.shape),
        _const_spec(w_o.shape),
        _const_spec(gn_g.shape),
        _const_spec(ln_g.shape),
        _const_spec(ln_b.shape),
        _const_spec(wr_hi.shape),
        _const_spec(wr_lo.shape),
        _const_spec(br.shape),
    ]
    args = [x2d, cos, sin, dmask, w_in, w_o, gn_g, ln_g, ln_b, wr_hi, wr_lo, br]
    state_spec = pl.BlockSpec((nb, heads, dk, dv), lambda t: (work(t) // nblk, 0, 0, 0))
    if has_state:
        in_specs.append(state_spec)
        args.append(state_in)
    aliases = _alias_dst(args, in_specs, dst)
    out_shape = _token_out_shapes(n_all, d_model) + (jax.ShapeDtypeStruct((n_seq, heads, dk, dv), F32),)
    out_specs = _token_out_specs(rows, d_model, lambda t: (out_b0 + t, 0)) + (state_spec,)
    return pl.pallas_call(
        functools.partial(_ret_kernel, nb=nb, blk=blk, nblk=nblk, n_steps=n_steps, n_fill=n_fill,
                          has_state=has_state, n_alias=len(aliases), alpha=alpha),
        grid=(n_steps + n_fill,),
        in_specs=in_specs,
        out_specs=out_specs,
        out_shape=out_shape,
        input_output_aliases=aliases,
        scratch_shapes=[pltpu.VMEM((rows, heads * dv), BF16)],
        compiler_params=pltpu.CompilerParams(
            dimension_semantics=("arbitrary",), vmem_limit_bytes=VMEM_LIMIT),
        name="ret_mixer_state" if has_state else "ret_mixer",
    )(*args)


_RING = 3


def _att_prompt_kernel(*refs, blk, nblk, n_steps, n_fill, alpha):
    step = pl.program_id(0)
    h_ref, hp_ref, lgt_ref = refs[9:12]

    @pl.when(step < n_steps)
    def _():
        _att_prompt_step(*refs, i=step % nblk, blk=blk, alpha=alpha)

    if n_fill:
        @pl.when(step >= n_steps)
        def _():
            _zero_tokens(h_ref, hp_ref, lgt_ref)


def _att_prompt_step(x_ref, wqkv_ref, wo_ref, bias_ref, lng_ref, lnb_ref, wrh_ref, wrl_ref, br_ref,
                     h_ref, hp_ref, lgt_ref, krow_ref, vrow_ref, kring, vring, o_scr, *, i, blk, alpha):
    d_model = x_ref.shape[1]
    dh = d_model // ATT_HEADS
    x = x_ref[...]
    xb = x.astype(BF16)
    q = _dot(xb, wqkv_ref[:, 0:d_model]) * (dh ** -0.5)
    k = _dot(xb, wqkv_ref[:, d_model:2 * d_model])
    v = _dot(xb, wqkv_ref[:, 2 * d_model:3 * d_model])
    krow_ref[...] = k
    vrow_ref[...] = v

    @pl.when(i == 0)
    def _():
        kring[...] = jnp.zeros(kring.shape, BF16)
        vring[...] = jnp.zeros(vring.shape, BF16)

    kring[i % _RING] = k.astype(BF16)
    vring[i % _RING] = jnp.transpose(v).astype(BF16)
    qb = q.astype(BF16)
    lane = lax.broadcasted_iota(jnp.int32, (1, LANES), 1)
    slots = [(i + _RING - d) % _RING for d in range(_RING)]

    heads_per_group = LANES // dh

    def scores(hd):
        c0 = (hd // heads_per_group) * LANES
        sub = hd % heads_per_group
        in_head = (lane >= sub * dh) & (lane < (sub + 1) * dh)
        q_pair = qb[:, c0:c0 + LANES]
        qm = jnp.where(in_head, q_pair, jnp.zeros_like(q_pair))
        s_list = []
        for d in range(_RING):
            slab = d if d == 0 else jnp.where(i >= d, d, _RING)
            s_list.append(_dot_nt(kring[slots[d], :, c0:c0 + LANES], qm) + bias_ref[slab, hd])
        return s_list

    def probs(s_list):
        m = s_list[0].max(axis=0, keepdims=True)
        for d in range(1, _RING):
            m = jnp.maximum(m, s_list[d].max(axis=0, keepdims=True))
        p_list = [jnp.exp(sc - m) for sc in s_list]
        l = p_list[0].sum(axis=0, keepdims=True)
        for d in range(1, _RING):
            l = l + p_list[d].sum(axis=0, keepdims=True)
        return [p.astype(BF16) for p in p_list], 1.0 / l

    def values(hd, p_list, inv_l):
        o = _dot(vring[slots[0], hd * dh:(hd + 1) * dh, :], p_list[0])
        for d in range(1, _RING):
            o = o + _dot(vring[slots[d], hd * dh:(hd + 1) * dh, :], p_list[d])
        o_scr[hd * dh:(hd + 1) * dh, :] = (o * inv_l).astype(BF16)

    s_next = scores(0)
    pending = None
    for hd in range(ATT_HEADS):
        s_cur = s_next
        if hd + 1 < ATT_HEADS:
            s_next = scores(hd + 1)
        p_list, inv_l = probs(s_cur)
        if pending is not None:
            values(*pending)
        pending = (hd, p_list, inv_l)
    values(*pending)

    mix = _dot_tn(o_scr[...], wo_ref[...])
    _finish_tokens(x, mix, alpha, lng_ref, lnb_ref, wrh_ref, wrl_ref, br_ref, h_ref, hp_ref, lgt_ref)


def _rel_bias(rel_table, n_rows, n_cols, offset, sign):
    heads = rel_table.shape[0]
    period = n_rows + n_cols
    m = jnp.arange(period)
    c_minus_r = jnp.where(m < n_cols, m, m - period)
    w = rel_table[:, jnp.clip(offset - sign * c_minus_r, -REL_CLIP, REL_CLIP) + REL_CLIP]
    flat = jnp.broadcast_to(w[:, None, :], (heads, n_rows, period)).reshape(heads, n_rows * period)
    return flat[:, :n_rows * (period - 1)].reshape(heads, n_rows, period - 1)[:, :, :n_cols]


def _att_prompt_bias(rel_table, blk):
    i = jnp.arange(blk)
    out = []
    for d in range(_RING):
        b = _rel_bias(rel_table, blk, blk, d * blk, -1)
        cd = (i[None, :] // CHUNK) - (i[:, None] // CHUNK) + d * (blk // CHUNK)
        ok = (cd >= 0) & (cd <= BAND_CHUNKS)
        out.append(jnp.where(ok[None], b, NEG_INF))
    out.append(jnp.full_like(out[0], NEG_INF))
    return jnp.stack(out).astype(F32)


def _att_prompt(x2d, n_seq, seq_len, w_qkv, w_o, rel_table, ln_g, ln_b, wr_hi, wr_lo, br, alpha, n_all):
    d_model = x2d.shape[1]
    blk = ATT_BLOCK
    nblk = seq_len // blk
    keep = min(BAND_CHUNKS * CHUNK, seq_len)
    assert seq_len % blk == 0 and keep % blk == 0
    assert (_RING - 1) * blk >= BAND_CHUNKS * CHUNK
    kb = keep // blk
    bias = _att_prompt_bias(rel_table, blk)
    n_steps = n_seq * nblk
    n_fill = (n_all - n_seq * seq_len) // blk
    assert n_fill * blk == n_all - n_seq * seq_len

    def work(t):
        return jnp.minimum(t, n_steps - 1)

    row_spec = pl.BlockSpec(
        (None, blk, d_model), lambda t: (work(t) // nblk, jnp.maximum(work(t) % nblk - (nblk - kb), 0), 0))
    rows_out = jax.ShapeDtypeStruct((n_seq, keep, d_model), F32)
    return pl.pallas_call(
        functools.partial(_att_prompt_kernel, blk=blk, nblk=nblk, n_steps=n_steps, n_fill=n_fill, alpha=alpha),
        grid=(n_steps + n_fill,),
        in_specs=[
            pl.BlockSpec((blk, d_model), lambda t: (work(t), 0)),
            _const_spec(w_qkv.shape),
            _const_spec(w_o.shape),
            _const_spec(bias.shape),
            _const_spec(ln_g.shape),
            _const_spec(ln_b.shape),
            _const_spec(wr_hi.shape),
            _const_spec(wr_lo.shape),
            _const_spec(br.shape),
        ],
        out_specs=_token_out_specs(blk, d_model, lambda t: (t, 0)) + (row_spec, row_spec),
        out_shape=_token_out_shapes(n_all, d_model) + (rows_out, rows_out),
        scratch_shapes=[
            pltpu.VMEM((_RING, blk, d_model), BF16),
            pltpu.VMEM((_RING, d_model, blk), BF16),
            pltpu.VMEM((d_model, blk), BF16),
        ],
        compiler_params=pltpu.CompilerParams(
            dimension_semantics=("arbitrary",), vmem_limit_bytes=VMEM_LIMIT),
        name="att_mixer",
    )(x2d, w_qkv, w_o, bias, ln_g, ln_b, wr_hi, wr_lo, br)


def _att_sample_kernel(*refs, nb, blk, alpha):
    (x_ref, kc_ref, vc_ref, wqkv_ref, wo_ref, bias_c_ref, bias_n_ref, lng_ref, lnb_ref,
     wrh_ref, wrl_ref, br_ref) = refs[:12]
    h_ref, hp_ref, lgt_ref, krow_ref, vrow_ref, o_scr = refs[-6:]
    d_model = x_ref.shape[1]
    heads = ATT_HEADS
    dh = d_model // heads
    x = x_ref[...]
    xb = x.astype(BF16)
    q = _dot(xb, wqkv_ref[:, 0:d_model]) * (dh ** -0.5)
    k = _dot(xb, wqkv_ref[:, d_model:2 * d_model])
    v = _dot(xb, wqkv_ref[:, 2 * d_model:3 * d_model])
    krow_ref[...] = k.reshape(nb, blk, d_model)
    vrow_ref[...] = v.reshape(nb, blk, d_model)
    lane_head = lax.broadcasted_iota(jnp.int32, (heads, 1, d_model), 2) // dh
    head_id = lax.broadcasted_iota(jnp.int32, (heads, 1, d_model), 0)
    head_mask = (lane_head == head_id).astype(F32)

    for s in range(nb):
        r0 = s * blk
        qs = q[r0:r0 + blk]
        q_bd = (qs[None, :, :] * head_mask).reshape(heads * blk, d_model).astype(BF16)
        kn = k[r0:r0 + blk].astype(BF16)
        vn = v[r0:r0 + blk].astype(BF16)
        s_c = _dot_nt(q_bd, kc_ref[s].astype(BF16)) + bias_c_ref[...]
        s_n = _dot_nt(q_bd, kn) + bias_n_ref[...]
        m = jnp.maximum(s_c.max(axis=-1, keepdims=True), s_n.max(axis=-1, keepdims=True))
        p_c = jnp.exp(s_c - m)
        p_n = jnp.exp(s_n - m)
        l = p_c.sum(axis=-1, keepdims=True) + p_n.sum(axis=-1, keepdims=True)
        o_full = _dot(p_c.astype(BF16), vc_ref[s].astype(BF16)) + _dot(p_n.astype(BF16), vn)
        o_full = o_full * (1.0 / l)
        o = (o_full.reshape(heads, blk, d_model) * head_mask).sum(axis=0)
        o_scr[r0:r0 + blk, :] = o.astype(BF16)

    mix = _dot(o_scr[...], wo_ref[...])
    _finish_tokens(x, mix, alpha, lng_ref, lnb_ref, wrh_ref, wrl_ref, br_ref, h_ref, hp_ref, lgt_ref)


def _att_sample(x2d, in_row0, n_seq, seq_len, k_cache, v_cache, w_qkv, w_o, rel_table, ln_g, ln_b,
                wr_hi, wr_lo, br, alpha, nb, n_all, out_row0, dst):
    d_model = x2d.shape[1]
    heads = ATT_HEADS
    blk = seq_len
    n_cache = k_cache.shape[1]
    rows = nb * blk
    assert n_seq % nb == 0 and in_row0 % rows == 0 and out_row0 % rows == 0
    in_b0, out_b0 = in_row0 // rows, out_row0 // rows
    bias_c = _rel_bias(rel_table, blk, n_cache, n_cache, 1).reshape(heads * blk, n_cache)
    bias_n = _rel_bias(rel_table, blk, blk, 0, 1).reshape(heads * blk, blk)
    kc = k_cache.reshape(n_seq, n_cache, d_model)
    vc = v_cache.reshape(n_seq, n_cache, d_model)
    cache_spec = pl.BlockSpec((nb, n_cache, d_model), lambda g: (g, 0, 0))
    row_spec = pl.BlockSpec((nb, blk, d_model), lambda g: (g, 0, 0))
    rows_out = jax.ShapeDtypeStruct((n_seq, blk, d_model), F32)
    in_specs = [
        pl.BlockSpec((rows, d_model), lambda g: (in_b0 + g, 0)),
        cache_spec,
        cache_spec,
        _const_spec(w_qkv.shape),
        _const_spec(w_o.shape),
        _const_spec(bias_c.shape),
        _const_spec(bias_n.shape),
        _const_spec(ln_g.shape),
        _const_spec(ln_b.shape),
        _const_spec(wr_hi.shape),
        _const_spec(wr_lo.shape),
        _const_spec(br.shape),
    ]
    args = [x2d, kc, vc, w_qkv, w_o, bias_c, bias_n, ln_g, ln_b, wr_hi, wr_lo, br]
    aliases = _alias_dst(args, in_specs, dst)
    return pl.pallas_call(
        functools.partial(_att_sample_kernel, nb=nb, blk=blk, alpha=alpha),
        grid=(n_seq // nb,),
        in_specs=in_specs,
        out_specs=_token_out_specs(rows, d_model, lambda g: (out_b0 + g, 0)) + (row_spec, row_spec),
        out_shape=_token_out_shapes(n_all, d_model) + (rows_out, rows_out),
        input_output_aliases=aliases,
        scratch_shapes=[pltpu.VMEM((rows, d_model), BF16)],
        compiler_params=pltpu.CompilerParams(
            dimension_semantics=("arbitrary",), vmem_limit_bytes=VMEM_LIMIT),
        name="att_mixer_cache",
    )(*args)


ROUTE_ROWS = 40


def _route_kernel(lgt_ref, meta_t_ref, meta_w_ref, count_ref, carry_ref, tri_ref):
    j = pl.program_id(0)
    tm = lgt_ref.shape[0]

    @pl.when(j == 0)
    def _():
        carry_ref[...] = jnp.zeros(carry_ref.shape, F32)
        r = lax.broadcasted_iota(jnp.int32, (tm, tm), 0)
        c = lax.broadcasted_iota(jnp.int32, (tm, tm), 1)
        tri_ref[...] = (r < c).astype(BF16)

    a = jnp.transpose(lgt_ref[...])[0:ROUTE_ROWS, :]
    row = lax.broadcasted_iota(jnp.int32, (ROUTE_ROWS, tm), 0)
    big = jnp.int32(LANES)
    glog = jnp.where(row < N_GROUPS, a, NEG_INF)
    gmax = glog.max(axis=0, keepdims=True)
    gsel = jnp.where(glog == gmax, row, big).min(axis=0, keepdims=True)
    gp = 1.0 / jnp.exp(glog - gmax).sum(axis=0, keepdims=True)
    first = N_GROUPS + gsel * EXP_PER_GROUP
    in_grp = (row >= first) & (row < first + EXP_PER_GROUP)
    elog = jnp.where(in_grp, a, NEG_INF)
    v1 = elog.max(axis=0, keepdims=True)
    l1 = jnp.where(elog == v1, row, big).min(axis=0, keepdims=True)
    elog2 = jnp.where(row == l1, NEG_INF, elog)
    v2 = elog2.max(axis=0, keepdims=True)
    l2 = jnp.where(elog2 == v2, row, big).min(axis=0, keepdims=True)
    e2 = jnp.exp(v2 - v1)
    w1 = gp / (1.0 + e2)
    w2 = gp * e2 / (1.0 + e2)

    is1 = row == l1
    is2 = row == l2
    oh = (is1 | is2).astype(F32)
    before = _dot(oh.astype(BF16), tri_ref[...]) + carry_ref[:, 0:1]
    rank1 = jnp.where(is1, before, 0.0).sum(axis=0, keepdims=True)
    rank2 = jnp.where(is2, before, 0.0).sum(axis=0, keepdims=True)
    carry_ref[...] = carry_ref[...] + oh.sum(axis=1, keepdims=True)
    count_ref[...] = carry_ref[...].astype(jnp.int32)

    pad = jnp.zeros((SUBLANES - 4, tm), jnp.int32)
    meta_t_ref[...] = jnp.concatenate(
        [l1 - N_GROUPS, l2 - N_GROUPS, rank1.astype(jnp.int32), rank2.astype(jnp.int32), pad], axis=0)
    wt = jnp.concatenate([w1, w2, jnp.zeros((LANES - 2, tm), F32)], axis=0)
    meta_w_ref[...] = jnp.transpose(wt)


def _route(logits):
    n_tok = logits.shape[0]
    tm = TOKEN_TILE
    assert n_tok % tm == 0
    tile = pl.BlockSpec((tm, LANES), lambda j: (j, 0))
    return pl.pallas_call(
        _route_kernel,
        grid=(n_tok // tm,),
        in_specs=[tile],
        out_specs=(pl.BlockSpec((SUBLANES, tm), lambda j: (0, j)), tile,
                   pl.BlockSpec((ROUTE_ROWS, LANES), lambda j: (0, 0))),
        out_shape=(
            jax.ShapeDtypeStruct((SUBLANES, n_tok), jnp.int32),
            jax.ShapeDtypeStruct((n_tok, LANES), F32),
            jax.ShapeDtypeStruct((ROUTE_ROWS, LANES), jnp.int32),
        ),
        scratch_shapes=[pltpu.VMEM((ROUTE_ROWS, LANES), F32), pltpu.VMEM((tm, tm), BF16)],
        compiler_params=pltpu.CompilerParams(dimension_semantics=("arbitrary",)),
        name="moe_route",
    )(logits)


def _sc_gather_loop(table_hbm, idx_v, out_hbm, rows_v, sems, base, n_chunk, r):
    def gather(c, slot):
        off = pl.multiple_of(c * r, r)
        return pltpu.make_async_copy(table_hbm.at[idx_v.at[pl.ds(off, r)]], rows_v.at[slot], sems.at[slot])

    def finish(c, slot):
        gather(c, slot).wait()
        pltpu.sync_copy(rows_v.at[slot], out_hbm.at[pl.ds(base + pl.multiple_of(c * r, r), r)])

    gather(0, 0).start()
    if n_chunk > 1:
        gather(1, 1).start()

    def body(pair, carry):
        c = 2 * pair
        for slot in range(2):
            finish(c + slot, slot)

            @pl.when(c + slot + 2 < n_chunk)
            def _():
                gather(c + slot + 2, slot).start()
        return carry

    lax.fori_loop(0, n_chunk // 2, body, 0)
    if n_chunk % 2:
        finish(n_chunk - 1, 0)


def _gather_rows(table, idx):
    m = idx.shape[0]
    width = table.shape[1]
    r = SC_ROWS_PER_CHUNK
    assert m % (SC_WORKERS * r) == 0
    per_w = m // SC_WORKERS
    mesh = plsc.VectorSubcoreMesh(core_axis_name="c", subcore_axis_name="s")

    @functools.partial(
        pl.kernel,
        mesh=mesh,
        out_type=jax.ShapeDtypeStruct((m, width), table.dtype),
        scratch_types=[
            pltpu.VMEM((per_w,), jnp.int32),
            pltpu.VMEM((2, r, width), table.dtype),
            pltpu.SemaphoreType.DMA((2,)),
        ],
    )
    def gather(table_hbm, idx_hbm, out_hbm, idx_v, rows_v, sem):
        wid = lax.axis_index("s") * 2 + lax.axis_index("c")
        base = wid * per_w
        pltpu.sync_copy(idx_hbm.at[pl.ds(base, per_w)], idx_v)
        _sc_gather_loop(table_hbm, idx_v, out_hbm, rows_v, sem, base, per_w // r, r)

    return gather(table, idx)


def _dispatch_rows(table, pos, row0, n_rows):
    n_tok, width = table.shape
    n_pairs = pos.shape[0]
    r = SC_ROWS_PER_CHUNK
    lanes = SC_LANES
    assert n_rows % (SC_WORKERS * r) == 0
    per_w = n_rows // SC_WORKERS
    n_stage = 16
    stage = n_pairs // n_stage
    assert stage * n_stage == n_pairs and stage % lanes == 0 and per_w % lanes == 0
    assert row0 + n_rows < 3 * n_tok
    mesh = plsc.VectorSubcoreMesh(core_axis_name="c", subcore_axis_name="s")

    @functools.partial(
        pl.kernel,
        mesh=mesh,
        out_type=jax.ShapeDtypeStruct((n_rows, width), table.dtype),
        scratch_types=[
            pltpu.VMEM((per_w,), jnp.int32),
            pltpu.VMEM((stage,), jnp.int32),
            pltpu.VMEM((2, r, width), table.dtype),
            pltpu.SemaphoreType.DMA((2,)),
        ],
        compiler_params=pltpu.CompilerParams(needs_layout_passes=False),
    )
    def dispatch(table_hbm, pos_hbm, out_hbm, src_v, pos_v, rows_v, sem):
        wid = lax.axis_index("s") * 2 + lax.axis_index("c")
        out_base = wid * per_w
        base = row0 + out_base
        lane = lax.iota(jnp.int32, lanes)

        def wrap(t):
            t = jnp.where(t >= n_tok, t - n_tok, t)
            return jnp.where(t >= n_tok, t - n_tok, t)

        def init(i, carry):
            off = pl.multiple_of(i * lanes, lanes)
            src_v[pl.ds(off, lanes)] = wrap(base + off + lane)
            return carry

        lax.fori_loop(0, per_w // lanes, init, 0)

        def scan_stage(sidx, carry):
            pair0 = sidx * stage
            pltpu.sync_copy(pos_hbm.at[pl.ds(pl.multiple_of(pair0, 8), stage)], pos_v)

            def scan(i, c2):
                off = pl.multiple_of(i * lanes, lanes)
                local = pos_v[pl.ds(off, lanes)] - base
                mine = (local >= 0) & (local < per_w)
                plsc.store_scatter(src_v, [jnp.where(mine, local, 0)], wrap(pair0 + off + lane), mask=mine)
                return c2

            lax.fori_loop(0, stage // lanes, scan, 0)
            return carry

        lax.fori_loop(0, n_stage, scan_stage, 0)
        _sc_gather_loop(table_hbm, src_v, out_hbm, rows_v, sem, out_base, per_w // r, r)

    return dispatch(table, pos)


def _gemm_kernel(tile_expert_ref, n_used_ref, xs_ref, wg_ref, wu_ref, wd_ref, *rest, tile0):
    ys_ref, wg_b, wu_b, wd_b = rest[-4:]
    j = pl.program_id(0)
    tile = tile0 + j

    @pl.when(tile < n_used_ref[0])
    def _():
        new_expert = (j == 0) | (tile_expert_ref[tile] != tile_expert_ref[jnp.maximum(tile - 1, 0)])

        @pl.when(new_expert)
        def _():
            wg_b[...] = wg_ref[...].astype(BF16)
            wu_b[...] = wu_ref[...].astype(BF16)
            wd_b[...] = wd_ref[...].astype(BF16)

        x = _unpack_rows(xs_ref[...]).astype(BF16)
        g = _dot(x, wg_b[...])
        u = _dot(x, wu_b[...])
        hmid = (jax.nn.silu(g) * u).astype(BF16)
        ys_ref[...] = _pack_rows(_dot(hmid, wd_b[...]))


def _expert_gemm(xs, tile_expert, n_used, layer, w_gate, w_up, w_down, tile0, n_rows_all, ys):
    n_rows, half = xs.shape
    d_model, d_exp = w_gate.shape[2], w_gate.shape[3]
    tm = GEMM_TILE
    n_tiles = n_rows // tm

    def last_used(j, nu):
        return jnp.minimum(tile0 + j, nu[0] - 1)

    def in_map(j, te, nu):
        return (jnp.maximum(last_used(j, nu) - tile0, 0), 0)

    def out_map(j, te, nu):
        return (jnp.maximum(last_used(j, nu), tile0), 0)

    def w_map(j, te, nu):
        return (layer, te[tile0 + j], 0, 0)

    in_specs = [
        pl.BlockSpec((tm, half), in_map),
        pl.BlockSpec((None, None, d_model, d_exp), w_map),
        pl.BlockSpec((None, None, d_model, d_exp), w_map),
        pl.BlockSpec((None, None, d_exp, d_model), w_map),
    ]
    args = [tile_expert, n_used, xs, w_gate, w_up, w_down]
    aliases = {}
    if ys is not None:
        aliases[len(args)] = 0
        args.append(ys)
        in_specs.append(pl.BlockSpec(memory_space=pl.ANY))
    grid_spec = pltpu.PrefetchScalarGridSpec(
        num_scalar_prefetch=2,
        grid=(n_tiles,),
        in_specs=in_specs,
        out_specs=pl.BlockSpec((tm, half), out_map),
        scratch_shapes=[
            pltpu.VMEM((d_model, d_exp), BF16),
            pltpu.VMEM((d_model, d_exp), BF16),
            pltpu.VMEM((d_exp, d_model), BF16),
        ],
    )
    return pl.pallas_call(
        functools.partial(_gemm_kernel, tile0=tile0),
        grid_spec=grid_spec,
        out_shape=jax.ShapeDtypeStruct((n_rows_all, half), U32),
        input_output_aliases=aliases,
        compiler_params=pltpu.CompilerParams(
            dimension_semantics=("arbitrary",), vmem_limit_bytes=VMEM_LIMIT),
        name="moe_gemm",
    )(*args)


def _combine_kernel(y0_ref, y1_ref, mw_ref, h_ref, pp_ref, ps_ref, lng_ref, lnb_ref, wproj_ref, wgate_ref,
                    *out_refs, n_prompt_tiles, alpha):
    j = pl.program_id(0)
    mw = mw_ref[...]
    ffn = mw[:, 0:1] * _unpack_rows(y0_ref[...]) + mw[:, 1:2] * _unpack_rows(y1_ref[...])
    h2 = _layer_norm(alpha * h_ref[...] + ffn, lng_ref[...], lnb_ref[...])
    p = jnp.where(j < n_prompt_tiles, pp_ref[...], ps_ref[...]).astype(BF16)
    gate = jax.nn.sigmoid(_dot(h2.astype(BF16), wgate_ref[...]))
    out = h2 + gate * _dot(p, wproj_ref[...])
    if len(out_refs) == 1:
        out_refs[0][...] = out
    else:
        @pl.when(j < n_prompt_tiles)
        def _():
            out_refs[0][...] = out

        @pl.when(j >= n_prompt_tiles)
        def _():
            out_refs[1][...] = out


def _combine(yg, meta_w, h, p_prompt, p_sample, layer, ln_g, ln_b, w_proj, w_gate, alpha, split_out):
    n_tok, d_model = h.shape
    tm = TOKEN_TILE
    n_t = n_tok // tm
    n_tp = p_prompt.shape[1] // tm
    n_ts = p_sample.shape[1] // tm
    assert n_tp * tm == p_prompt.shape[1] and n_ts * tm == p_sample.shape[1] and n_tp + n_ts == n_t
    ple = p_prompt.shape[2]
    tile = pl.BlockSpec((tm, d_model), lambda j: (j, 0))
    if split_out:
        out_specs = (pl.BlockSpec((tm, d_model), lambda j: (jnp.minimum(j, n_tp - 1), 0)),
                     pl.BlockSpec((tm, d_model), lambda j: (jnp.maximum(j - n_tp, 0), 0)))
        out_shape = (jax.ShapeDtypeStruct((n_tp * tm, d_model), F32),
                     jax.ShapeDtypeStruct((n_ts * tm, d_model), F32))
    else:
        out_specs = tile
        out_shape = jax.ShapeDtypeStruct((n_tok, d_model), F32)
    return pl.pallas_call(
        functools.partial(_combine_kernel, n_prompt_tiles=n_tp, alpha=alpha),
        grid=(n_t,),
        in_specs=[
            pl.BlockSpec((tm, d_model // 2), lambda j: (j, 0)),
            pl.BlockSpec((tm, d_model // 2), lambda j: (j + n_t, 0)),
            pl.BlockSpec((tm, LANES), lambda j: (j, 0)),
            tile,
            pl.BlockSpec((None, tm, ple), lambda j: (layer, jnp.minimum(j, n_tp - 1), 0)),
            pl.BlockSpec((None, tm, ple), lambda j: (layer, jnp.maximum(j - n_tp, 0), 0)),
            _const_spec(ln_g.shape),
            _const_spec(ln_b.shape),
            _const_spec(w_proj.shape),
            _const_spec(w_gate.shape),
        ],
        out_specs=out_specs,
        out_shape=out_shape,
        compiler_params=pltpu.CompilerParams(
            dimension_semantics=("arbitrary",), vmem_limit_bytes=VMEM_LIMIT),
        name="moe_combine",
    )(yg, yg, meta_w, h, p_prompt, p_sample, ln_g, ln_b, w_proj, w_gate)


def _router_weights(w_grp, b_grp, w_exp, b_exp):
    d_model = w_grp.shape[0]
    w = jnp.concatenate([w_grp, jnp.transpose(w_exp, (1, 0, 2)).reshape(d_model, N_EXPERTS)], axis=1)
    w = jnp.pad(w, ((0, 0), (0, LANES - w.shape[1])))
    b = jnp.concatenate([b_grp, b_exp.reshape(N_EXPERTS)])
    b = jnp.pad(b, (0, LANES - b.shape[0])).reshape(1, LANES)
    w_hi = w.astype(BF16)
    w_lo = (w - w_hi.astype(F32)).astype(BF16)
    return w_hi, w_lo, b


def _moe(h_packed, logits, layer, w_gate, w_up, w_down):
    n_tok = h_packed.shape[0]
    tm = GEMM_TILE
    meta_t, meta_w, counts = _route(logits)
    counts = counts[N_GROUPS:N_GROUPS + N_EXPERTS, 0]
    tiles_per_expert = (counts + tm - 1) // tm
    tile_end = jnp.cumsum(tiles_per_expert)
    row_start = (tile_end - tiles_per_expert) * tm
    eid = meta_t[0:2]
    rank = meta_t[2:4]
    experts = jnp.arange(N_EXPERTS, dtype=jnp.int32)
    start = jnp.sum(jnp.where(eid[:, :, None] == experts, row_start, 0), axis=-1)
    pos = (start + rank).reshape(-1).astype(jnp.int32)
    part_quant = MOE_PARTS * SC_WORKERS * SC_ROWS_PER_CHUNK * (tm // math.gcd(tm, SC_WORKERS * SC_ROWS_PER_CHUNK))
    n_tiles = -(-(2 * n_tok) // tm) + N_EXPERTS
    n_rows = -(-(n_tiles * tm) // part_quant) * part_quant
    n_tiles = n_rows // tm
    n_used = tile_end[-1:].astype(jnp.int32)
    tile_ids = jnp.minimum(jnp.arange(n_tiles, dtype=jnp.int32), n_used[0] - 1)
    tile_expert = jnp.sum(tile_end[None, :] <= tile_ids[:, None], axis=1).astype(jnp.int32)
    part_rows = n_rows // MOE_PARTS
    xs_parts = [_dispatch_rows(h_packed, pos, part * part_rows, part_rows) for part in range(MOE_PARTS)]
    ys = None
    for part, xs in enumerate(xs_parts):
        ys = _expert_gemm(xs, tile_expert, n_used, layer, w_gate, w_up, w_down,
                          part * (part_rows // tm), n_rows, ys)
    return _gather_rows(ys, pos), meta_w


def kernel(x_prompt, x_sample, p_prompt, p_sample, state_ret, cache_att_k, cache_att_v, ret_w_in, ret_gn_g,
           ret_w_o, att_w_qkv, att_rel_bias, att_w_o, ln1_g, ln1_b, ln2_g, ln2_b, moe_w_grp, moe_b_grp,
           moe_w_exp, moe_b_exp, moe_w_gate, moe_w_up, moe_w_down, ple_w_proj, ple_w_gate):
    n_p, len_p, d_model = x_prompt.shape
    n_s, len_s, _ = x_sample.shape
    depth = ln1_g.shape[0]
    alpha = float((2 * depth) ** 0.25)
    tok_p, tok_s = n_p * len_p, n_s * len_s
    n_all = tok_p + tok_s
    dh = d_model // ATT_HEADS
    pp = p_prompt.reshape(depth, tok_p, -1)
    ps = p_sample.reshape(depth, tok_s, -1)
    nb_s = 2

    x_all = None
    y_prompt = y_sample = None
    states_p, states_s, k_p, v_p, k_s, v_s = [], [], [], [], [], []
    for i in range(depth):
        jj = i // 2
        wr_hi, wr_lo, br = _router_weights(moe_w_grp[i], moe_b_grp[i], moe_w_exp[i], moe_b_exp[i])
        lng, lnb = ln1_g[i].reshape(1, d_model), ln1_b[i].reshape(1, d_model)
        if x_all is None:
            src_p, src_s, row_s = x_prompt.reshape(tok_p, d_model), x_sample.reshape(tok_s, d_model), 0
        else:
            src_p, src_s, row_s = x_all, x_all, tok_p
        if i % 2 == 0:
            w_in = ret_w_in[jj].astype(BF16)
            w_o = ret_w_o[jj].astype(BF16)
            gn = ret_gn_g[jj].reshape(1, -1)
            h, hp, lgt, st_p = _ret_mixer(src_p, 0, n_p, len_p, 0, None, w_in, w_o, gn, lng, lnb,
                                          wr_hi, wr_lo, br, alpha, nb=1, blk=min(RET_BLOCK, len_p),
                                          n_all=n_all, out_row0=0, dst=None)
            h, hp, lgt, st_s = _ret_mixer(src_s, row_s, n_s, len_s, PAST_LEN, state_ret[jj], w_in, w_o, gn,
                                          lng, lnb, wr_hi, wr_lo, br, alpha, nb=nb_s, blk=len_s,
                                          n_all=n_all, out_row0=tok_p, dst=(h, hp, lgt))
            states_p.append(st_p)
            states_s.append(st_s)
        else:
            w_qkv = att_w_qkv[jj].astype(BF16)
            w_o = att_w_o[jj].astype(BF16)
            h, hp, lgt, kr, vr = _att_prompt(src_p, n_p, len_p, w_qkv, w_o, att_rel_bias[jj], lng, lnb,
                                             wr_hi, wr_lo, br, alpha, n_all)
            k_p.append(kr.reshape(n_p, -1, ATT_HEADS, dh))
            v_p.append(vr.reshape(n_p, -1, ATT_HEADS, dh))
            h, hp, lgt, kr, vr = _att_sample(src_s, row_s, n_s, len_s, cache_att_k[jj], cache_att_v[jj],
                                             w_qkv, w_o, att_rel_bias[jj], lng, lnb, wr_hi, wr_lo, br, alpha,
                                             nb=nb_s, n_all=n_all, out_row0=tok_p, dst=(h, hp, lgt))
            k_s.append(kr.reshape(n_s, len_s, ATT_HEADS, dh))
            v_s.append(vr.reshape(n_s, len_s, ATT_HEADS, dh))
        yg, meta_w = _moe(hp, lgt, i, moe_w_gate, moe_w_up, moe_w_down)
        last = i == depth - 1
        out = _combine(yg, meta_w, h, pp, ps, i, ln2_g[i].reshape(1, d_model), ln2_b[i].reshape(1, d_model),
                       ple_w_proj[i].astype(BF16), ple_w_gate[i].astype(BF16), alpha, split_out=last)
        if last:
            y_prompt = out[0].reshape(n_p, len_p, d_model)
            y_sample = out[1].reshape(n_s, len_s, d_model)
        else:
            x_all = out

    return (y_prompt, y_sample, jnp.stack(states_p), jnp.stack(states_s),
            jnp.stack(k_p), jnp.stack(v_p), jnp.stack(k_s), jnp.stack(v_s))
```

```python
import functools
import math

import numpy as np
import jax
import jax.numpy as jnp
from jax import lax
from jax.experimental import pallas as pl
from jax.experimental.pallas import tpu as pltpu
from jax.experimental.pallas import tpu_sc as plsc

CHUNK = 64
PAST_LEN = 2048
RET_HEADS = 4
ROPE_BASE = 10000.0
ATT_HEADS = 16
BAND_CHUNKS = 8
REL_CLIP = 256
N_GROUPS = 4
EXP_PER_GROUP = 8
N_EXPERTS = N_GROUPS * EXP_PER_GROUP
LN_EPS = 1e-5
NEG_INF = -1e30

LANES = 128
SUBLANES = 8
BF16_SUBLANES = 16
SC_WORKERS = 32
SC_LANES = 16
SC_ROWS_PER_CHUNK = 32
VMEM_LIMIT = 56 * 1024 * 1024

RET_BLOCK = 256
RET_CHAIN = 2
ATT_BLOCK = 4 * CHUNK
ATT_CHAIN = 2
TOKEN_TILE = 512
GEMM_TILE = 512
MOE_PARTS = 2

F32 = jnp.float32
BF16 = jnp.bfloat16
U32 = jnp.uint32
HI_MASK = 0xFFFF0000
LOG2E = math.log2(math.e)


def _dot(a, b):
    return jnp.dot(a, b, preferred_element_type=F32)


def _dot_nt(a, b):
    return lax.dot_general(a, b, (((1,), (1,)), ((), ())), preferred_element_type=F32)


def _dot_tn(a, b):
    return lax.dot_general(a, b, (((0,), (0,)), ((), ())), preferred_element_type=F32)


def _layer_norm(x, g, b):
    mu = jnp.mean(x, axis=-1, keepdims=True)
    xc = x - mu
    var = jnp.mean(xc * xc, axis=-1, keepdims=True)
    return xc * lax.rsqrt(var + LN_EPS) * g + b


def _pack_rows(x):
    half = x.shape[1] // 2
    lo = lax.bitcast_convert_type(x[:, :half].astype(BF16).astype(F32), U32) >> 16
    hi = lax.bitcast_convert_type(x[:, half:].astype(BF16).astype(F32), U32) & U32(HI_MASK)
    return lo | hi


def _unpack_rows(p):
    lo = lax.bitcast_convert_type(p << 16, F32)
    hi = lax.bitcast_convert_type(p & U32(HI_MASK), F32)
    return jnp.concatenate([lo, hi], axis=1)


def _router_logits(h, wr_hi_ref, wr_lo_ref, br_ref):
    h_hi = h.astype(BF16)
    h_lo = (h - h_hi.astype(F32)).astype(BF16)
    w_hi = wr_hi_ref[...]
    return _dot(h_hi, w_hi) + _dot(h_lo, w_hi) + _dot(h_hi, wr_lo_ref[...]) + br_ref[...]


def _finish_tokens(x, mix, alpha, lng_ref, lnb_ref, wrh_ref, wrl_ref, br_ref, h_ref, hp_ref, lgt_ref):
    hh = _layer_norm(alpha * x + mix, lng_ref[...], lnb_ref[...])
    h_ref[...] = hh
    hp_ref[...] = _pack_rows(hh)
    lgt_ref[...] = _router_logits(hh, wrh_ref, wrl_ref, br_ref)


def _zero_tokens(h_ref, hp_ref, lgt_ref):
    h_ref[...] = jnp.zeros(h_ref.shape, h_ref.dtype)
    hp_ref[...] = jnp.zeros(hp_ref.shape, hp_ref.dtype)
    lgt_ref[...] = jnp.zeros(lgt_ref.shape, lgt_ref.dtype)


def _const_spec(shape):
    nd = len(shape)
    return pl.BlockSpec(shape, lambda *_: (0,) * nd, pipeline_mode=pl.Buffered(1))


def _token_out_shapes(n_all, d_model):
    return (
        jax.ShapeDtypeStruct((n_all, d_model), F32),
        jax.ShapeDtypeStruct((n_all, d_model // 2), U32),
        jax.ShapeDtypeStruct((n_all, LANES), F32),
    )


def _token_out_specs(rows, d_model, row_map):
    return (
        pl.BlockSpec((rows, d_model), row_map),
        pl.BlockSpec((rows, d_model // 2), row_map),
        pl.BlockSpec((rows, LANES), row_map),
    )


def _alias_dst(args, in_specs, dst):
    if dst is None:
        return {}
    aliases = {}
    for k, arr in enumerate(dst):
        aliases[len(args)] = k
        args.append(arr)
        in_specs.append(pl.BlockSpec(memory_space=pl.ANY))
    return aliases


def _ret_log_gamma():
    h = np.arange(RET_HEADS, dtype=np.float32)
    return np.log(np.float32(1.0) - np.float32(2.0) ** (np.float32(-5.0) - h)).astype(np.float32)


def _ret_kernel(*refs, nb, blk, chain, nblk, n_steps, n_fill, has_state, n_alias, alpha):
    step = pl.program_id(0)
    h_ref, hp_ref, lgt_ref = refs[12 + int(has_state) + n_alias:][:3]

    @pl.when(step < n_steps)
    def _():
        _ret_step(*refs, first_block=step % nblk == 0, nb=nb, blk=blk, chain=chain, has_state=has_state,
                  n_alias=n_alias, alpha=alpha)

    if n_fill:
        @pl.when(step >= n_steps)
        def _():
            _zero_tokens(h_ref, hp_ref, lgt_ref)


def _ret_step(*refs, first_block, nb, blk, chain, has_state, n_alias, alpha):
    (x_ref, cos_ref, sin_ref, dmask_ref, win_ref, wo_ref, gn_ref, lng_ref, lnb_ref,
     wrh_ref, wrl_ref, br_ref) = refs[:12]
    s_in_ref = refs[12] if has_state else None
    h_ref, hp_ref, lgt_ref, s_out_ref, gated_ref = refs[12 + int(has_state) + n_alias:]
    heads = RET_HEADS
    d_model = x_ref.shape[1]
    dk = d_model // heads
    dv = 2 * d_model // heads
    hk, hv = heads * dk, heads * dv
    half = dk // 2
    lg = _ret_log_gamma()

    n_sub = nb * chain
    proj_rows = blk if chain > 1 else n_sub * blk
    rowf = lax.broadcasted_iota(jnp.int32, (blk, 1), 0).astype(F32)

    def proj(group):
        xb = x_ref[group * proj_rows:(group + 1) * proj_rows, :].astype(BF16)
        return (_dot(xb, win_ref[:, 0:hk]), _dot(xb, win_ref[:, hk:2 * hk]),
                _dot(xb, win_ref[:, 2 * hk:2 * hk + hv]), _dot(xb, win_ref[:, 2 * hk + hv:2 * hk + 2 * hv]))

    if not has_state:
        @pl.when(first_block)
        def _():
            s_out_ref[...] = jnp.zeros(s_out_ref.shape, F32)

    s_prev_ref = s_in_ref if has_state else s_out_ref

    def head(j, h, projected):
        q_all, k_all, v_all, g_all = projected
        s = j // chain
        c0 = (j % chain) * blk
        r0 = (j * blk) % proj_rows
        cos = cos_ref[c0:c0 + blk, :]
        sin = sin_ref[c0:c0 + blk, :]

        def rot(t):
            t1, t2 = t[:, :half], t[:, half:]
            return jnp.concatenate([t1 * cos - t2 * sin, t1 * sin + t2 * cos], axis=1)

        lgh = float(lg[h])
        q = rot(q_all[r0:r0 + blk, h * dk:(h + 1) * dk])
        k = rot(k_all[r0:r0 + blk, h * dk:(h + 1) * dk]) * (dk ** -0.5)
        v = v_all[r0:r0 + blk, h * dv:(h + 1) * dv]
        g = g_all[r0:r0 + blk, h * dv:(h + 1) * dv]
        vb = v.astype(BF16)
        scores = _dot_nt(q.astype(BF16), k.astype(BF16)) * dmask_ref[h]
        inner = _dot(scores.astype(BF16), vb)
        s_prev = s_prev_ref[s, h]
        q_dec = q * jnp.exp(lgh * (rowf + 1.0))
        cross = _dot(q_dec.astype(BF16), s_prev.astype(BF16))
        k_dec = k * jnp.exp(lgh * (float(blk - 1) - rowf))
        s_out_ref[s, h] = math.exp(lgh * blk) * s_prev + _dot_tn(k_dec.astype(BF16), vb)
        o = inner + cross
        mu = jnp.mean(o, axis=-1, keepdims=True)
        oc = o - mu
        var = jnp.mean(oc * oc, axis=-1, keepdims=True)
        on = oc * lax.rsqrt(var + LN_EPS) * gn_ref[:, h * dv:(h + 1) * dv]
        gated_ref[j * blk:(j + 1) * blk, h * dv:(h + 1) * dv] = (jax.nn.silu(g) * on).astype(BF16)

    def tail(group):
        rows = slice(group * proj_rows, (group + 1) * proj_rows)
        mix = _dot(gated_ref[rows, :], wo_ref[...])
        _finish_tokens(x_ref[rows, :], mix, alpha, lng_ref, lnb_ref, wrh_ref, wrl_ref, br_ref,
                       h_ref.at[rows, :], hp_ref.at[rows, :], lgt_ref.at[rows, :])

    n_groups = (n_sub * blk) // proj_rows
    subs_per_group = proj_rows // blk
    projected = proj(0)
    for group in range(n_groups):
        nxt = None
        for jj in range(subs_per_group):
            for h in range(heads):
                head(group * subs_per_group + jj, h, projected)
                if jj == 0 and h == 0 and group + 1 < n_groups:
                    nxt = proj(group + 1)
                if jj == 0 and h == 1 and group > 0:
                    tail(group - 1)
        projected = nxt
    tail(n_groups - 1)


def _ret_mixer(x2d, in_row0, n_seq, seq_len, pos0, state_in, w_in, w_o, gn_g, ln_g, ln_b,
               wr_hi, wr_lo, br, alpha, nb, blk, chain, n_all, out_row0, dst):
    d_model = x2d.shape[1]
    heads = RET_HEADS
    dk, dv = d_model // heads, 2 * d_model // heads
    half = dk // 2
    nblk = seq_len // (blk * chain)
    has_state = state_in is not None
    rows = nb * blk * chain
    assert seq_len % (blk * chain) == 0 and n_seq % nb == 0
    assert (not has_state) or nblk == 1
    assert nb == 1 or (nblk == 1 and chain == 1)
    assert in_row0 % rows == 0 and out_row0 % rows == 0
    in_b0, out_b0 = in_row0 // rows, out_row0 // rows

    pos = (pos0 + jnp.arange(seq_len, dtype=jnp.int32)).astype(F32)
    inv_freq = ROPE_BASE ** (-jnp.arange(half, dtype=F32) / half)
    ang = pos[:, None] * inv_freq[None, :]
    cos, sin = jnp.cos(ang), jnp.sin(ang)
    lg = jnp.asarray(_ret_log_gamma())
    ii = jnp.arange(blk, dtype=F32)
    diff = ii[:, None] - ii[None, :]
    dmask = jnp.where(diff >= 0, jnp.exp(lg[:, None, None] * jnp.maximum(diff, 0.0)), 0.0)

    n_steps = (n_seq // nb) * nblk
    n_fill = 0 if dst is not None else (n_all - n_seq * seq_len) // rows
    assert dst is not None or (out_row0 == 0 and n_fill * rows == n_all - n_seq * seq_len)

    def work(t):
        return jnp.minimum(t, n_steps - 1)

    in_specs = [
        pl.BlockSpec((rows, d_model), lambda t: (in_b0 + work(t), 0)),
        pl.BlockSpec((blk * chain, half), lambda t: (work(t) % nblk, 0)),
        pl.BlockSpec((blk * chain, half), lambda t: (work(t) % nblk, 0)),
        _const_spec(dmask.shape),
        _const_spec(w_in.shape),
        _const_spec(w_o.shape),
        _const_spec(gn_g.shape),
        _const_spec(ln_g.shape),
        _const_spec(ln_b.shape),
        _const_spec(wr_hi.shape),
        _const_spec(wr_lo.shape),
        _const_spec(br.shape),
    ]
    args = [x2d, cos, sin, dmask, w_in, w_o, gn_g, ln_g, ln_b, wr_hi, wr_lo, br]
    state_spec = pl.BlockSpec((nb, heads, dk, dv), lambda t: (work(t) // nblk, 0, 0, 0))
    if has_state:
        in_specs.append(state_spec)
        args.append(state_in)
    aliases = _alias_dst(args, in_specs, dst)
    out_shape = _token_out_shapes(n_all, d_model) + (jax.ShapeDtypeStruct((n_seq, heads, dk, dv), F32),)
    out_specs = _token_out_specs(rows, d_model, lambda t: (out_b0 + t, 0)) + (state_spec,)
    return pl.pallas_call(
        functools.partial(_ret_kernel, nb=nb, blk=blk, chain=chain, nblk=nblk, n_steps=n_steps, n_fill=n_fill,
                          has_state=has_state, n_alias=len(aliases), alpha=alpha),
        grid=(n_steps + n_fill,),
        in_specs=in_specs,
        out_specs=out_specs,
        out_shape=out_shape,
        input_output_aliases=aliases,
        scratch_shapes=[pltpu.VMEM((rows, heads * dv), BF16)],
        compiler_params=pltpu.CompilerParams(
            dimension_semantics=("arbitrary",), vmem_limit_bytes=VMEM_LIMIT),
        name="ret_mixer_state" if has_state else "ret_mixer",
    )(*args)


_RING = 3


def _att_prompt_kernel(*refs, blk, chain, nblk, n_steps, n_fill, alpha):
    step = pl.program_id(0)
    h_ref, hp_ref, lgt_ref = refs[9:12]

    @pl.when(step < n_steps)
    def _():
        _att_prompt_step(*refs, i0=(step % nblk) * chain, blk=blk, chain=chain, alpha=alpha)

    if n_fill:
        @pl.when(step >= n_steps)
        def _():
            _zero_tokens(h_ref, hp_ref, lgt_ref)


def _att_prompt_step(x_ref, wqkv_ref, wo_ref, bias_ref, lng_ref, lnb_ref, wrh_ref, wrl_ref, br_ref,
                     h_ref, hp_ref, lgt_ref, krow_ref, vrow_ref, kring, vring, o_scr, *, i0, blk, chain, alpha):
    d_model = x_ref.shape[1]
    dh = d_model // ATT_HEADS
    heads_per_group = LANES // dh
    lane = lax.broadcasted_iota(jnp.int32, (1, LANES), 1)

    @pl.when(i0 == 0)
    def _():
        kring[...] = jnp.zeros(kring.shape, BF16)
        vring[...] = jnp.zeros(vring.shape, BF16)
        vring[:, :, dh:, :] = jnp.ones((_RING, ATT_HEADS, vring.shape[2] - dh, blk), BF16)

    def project(j):
        xb = x_ref[j * blk:(j + 1) * blk, :].astype(BF16)
        q = _dot(xb, wqkv_ref[:, 0:d_model]) * (dh ** -0.5 * LOG2E)
        k = _dot(xb, wqkv_ref[:, d_model:2 * d_model])
        v = _dot(xb, wqkv_ref[:, 2 * d_model:3 * d_model])
        return q, k, v

    def to_ring(j, k, v):
        slot = (i0 + j) % _RING
        krow_ref[j * blk:(j + 1) * blk, :] = k
        vrow_ref[j * blk:(j + 1) * blk, :] = v
        kring[slot] = k.astype(BF16)
        v_t = jnp.transpose(v).astype(BF16)
        for hd in range(ATT_HEADS):
            vring[slot, hd, 0:dh, :] = v_t[hd * dh:(hd + 1) * dh, :]

    def attend(j, q, between):
        i = i0 + j
        qb = q.astype(BF16)
        behind = [(i + _RING - k) % _RING for k in range(_RING)]
        slabs = [jnp.where(i >= behind[k], behind[k], _RING) for k in range(_RING)]

        def scores(hd):
            c0 = (hd // heads_per_group) * LANES
            sub = hd % heads_per_group
            in_head = (lane >= sub * dh) & (lane < (sub + 1) * dh)
            q_pair = qb[:, c0:c0 + LANES]
            qm = jnp.where(in_head, q_pair, jnp.zeros_like(q_pair))
            s_all = _dot_nt(kring[:, :, c0:c0 + LANES].reshape(_RING * blk, LANES), qm)
            return [s_all[k * blk:(k + 1) * blk] + bias_ref[slabs[k], hd] for k in range(_RING)]

        def probs(s_list):
            m = s_list[0].max(axis=0, keepdims=True)
            for k in range(1, _RING):
                m = jnp.maximum(m, s_list[k].max(axis=0, keepdims=True))
            return [jnp.exp2(sc - m).astype(BF16) for sc in s_list]

        def values(hd, p_list):
            o = _dot(vring[0, hd], p_list[0])
            for k in range(1, _RING):
                o = o + _dot(vring[k, hd], p_list[k])
            o_scr[j, hd * dh:(hd + 1) * dh, :] = (o[0:dh] * (1.0 / o[dh:dh + 1])).astype(BF16)

        s_next = scores(0)
        pending = None
        for hd in range(ATT_HEADS):
            s_cur = s_next
            if hd + 1 < ATT_HEADS:
                s_next = scores(hd + 1)
            p_list = probs(s_cur)
            if pending is not None:
                values(*pending)
            pending = (hd, p_list)
            if hd in between:
                between[hd]()
        values(*pending)

    def finish(j):
        rows = slice(j * blk, (j + 1) * blk)
        mix = _dot_tn(o_scr[j], wo_ref[...])
        _finish_tokens(x_ref[rows, :], mix, alpha, lng_ref, lnb_ref, wrh_ref, wrl_ref, br_ref,
                       h_ref.at[rows, :], hp_ref.at[rows, :], lgt_ref.at[rows, :])

    qkv = project(0)
    for j in range(chain):
        to_ring(j, qkv[1], qkv[2])
        nxt = []
        between = {}
        if j + 1 < chain:
            between[0] = lambda j=j: nxt.append(project(j + 1))
        attend(j, qkv[0], between)
        finish(j)
        qkv = nxt[0] if nxt else None


def _rel_bias(rel_table, n_rows, n_cols, offset, sign):
    heads = rel_table.shape[0]
    period = n_rows + n_cols
    m = jnp.arange(period)
    c_minus_r = jnp.where(m < n_cols, m, m - period)
    w = rel_table[:, jnp.clip(offset - sign * c_minus_r, -REL_CLIP, REL_CLIP) + REL_CLIP]
    flat = jnp.broadcast_to(w[:, None, :], (heads, n_rows, period)).reshape(heads, n_rows * period)
    return flat[:, :n_rows * (period - 1)].reshape(heads, n_rows, period - 1)[:, :, :n_cols]


def _att_prompt_bias(rel_table, blk):
    i = jnp.arange(blk)
    out = []
    for d in range(_RING):
        b = _rel_bias(rel_table, blk, blk, d * blk, -1)
        cd = (i[None, :] // CHUNK) - (i[:, None] // CHUNK) + d * (blk // CHUNK)
        ok = (cd >= 0) & (cd <= BAND_CHUNKS)
        out.append(jnp.where(ok[None], b, NEG_INF))
    out.append(jnp.full_like(out[0], NEG_INF))
    return jnp.stack(out).astype(F32)


def _att_prompt(x2d, n_seq, seq_len, w_qkv, w_o, rel_table, ln_g, ln_b, wr_hi, wr_lo, br, alpha, n_all):
    d_model = x2d.shape[1]
    blk = ATT_BLOCK
    chain = ATT_CHAIN
    rows = blk * chain
    nblk = seq_len // rows
    keep = min(BAND_CHUNKS * CHUNK, seq_len)
    assert seq_len % rows == 0 and keep % rows == 0
    assert (_RING - 1) * blk >= BAND_CHUNKS * CHUNK
    kb = keep // rows
    bias = _att_prompt_bias(rel_table, blk) * LOG2E
    n_steps = n_seq * nblk
    n_fill = (n_all - n_seq * seq_len) // rows
    assert n_fill * rows == n_all - n_seq * seq_len

    def work(t):
        return jnp.minimum(t, n_steps - 1)

    row_spec = pl.BlockSpec(
        (None, rows, d_model), lambda t: (work(t) // nblk, jnp.maximum(work(t) % nblk - (nblk - kb), 0), 0))
    rows_out = jax.ShapeDtypeStruct((n_seq, keep, d_model), F32)
    return pl.pallas_call(
        functools.partial(_att_prompt_kernel, blk=blk, chain=chain, nblk=nblk, n_steps=n_steps, n_fill=n_fill,
                          alpha=alpha),
        grid=(n_steps + n_fill,),
        in_specs=[
            pl.BlockSpec((rows, d_model), lambda t: (work(t), 0)),
            _const_spec(w_qkv.shape),
            _const_spec(w_o.shape),
            _const_spec(bias.shape),
            _const_spec(ln_g.shape),
            _const_spec(ln_b.shape),
            _const_spec(wr_hi.shape),
            _const_spec(wr_lo.shape),
            _const_spec(br.shape),
        ],
        out_specs=_token_out_specs(rows, d_model, lambda t: (t, 0)) + (row_spec, row_spec),
        out_shape=_token_out_shapes(n_all, d_model) + (rows_out, rows_out),
        scratch_shapes=[
            pltpu.VMEM((_RING, blk, d_model), BF16),
            pltpu.VMEM((_RING, ATT_HEADS, d_model // ATT_HEADS + BF16_SUBLANES, blk), BF16),
            pltpu.VMEM((chain, d_model, blk), BF16),
        ],
        compiler_params=pltpu.CompilerParams(
            dimension_semantics=("arbitrary",), vmem_limit_bytes=VMEM_LIMIT),
        name="att_mixer",
    )(x2d, w_qkv, w_o, bias, ln_g, ln_b, wr_hi, wr_lo, br)


def _att_sample_kernel(*refs, nb, blk, alpha):
    (x_ref, kc_ref, vc_ref, wqkv_ref, wo_ref, bias_c_ref, bias_n_ref, lng_ref, lnb_ref,
     wrh_ref, wrl_ref, br_ref) = refs[:12]
    h_ref, hp_ref, lgt_ref, krow_ref, vrow_ref, o_scr = refs[-6:]
    d_model = x_ref.shape[1]
    heads = ATT_HEADS
    dh = d_model // heads
    x = x_ref[...]
    xb = x.astype(BF16)
    q = _dot(xb, wqkv_ref[:, 0:d_model]) * (dh ** -0.5)
    k = _dot(xb, wqkv_ref[:, d_model:2 * d_model])
    v = _dot(xb, wqkv_ref[:, 2 * d_model:3 * d_model])
    krow_ref[...] = k.reshape(nb, blk, d_model)
    vrow_ref[...] = v.reshape(nb, blk, d_model)
    lane_head = lax.broadcasted_iota(jnp.int32, (heads, 1, d_model), 2) // dh
    head_id = lax.broadcasted_iota(jnp.int32, (heads, 1, d_model), 0)
    head_mask = (lane_head == head_id).astype(F32)

    for s in range(nb):
        r0 = s * blk
        qs = q[r0:r0 + blk]
        q_bd = (qs[None, :, :] * head_mask).reshape(heads * blk, d_model).astype(BF16)
        kn = k[r0:r0 + blk].astype(BF16)
        vn = v[r0:r0 + blk].astype(BF16)
        s_c = _dot_nt(q_bd, kc_ref[s].astype(BF16)) + bias_c_ref[...]
        s_n = _dot_nt(q_bd, kn) + bias_n_ref[...]
        m = jnp.maximum(s_c.max(axis=-1, keepdims=True), s_n.max(axis=-1, keepdims=True))
        p_c = jnp.exp(s_c - m)
        p_n = jnp.exp(s_n - m)
        l = p_c.sum(axis=-1, keepdims=True) + p_n.sum(axis=-1, keepdims=True)
        o_full = _dot(p_c.astype(BF16), vc_ref[s].astype(BF16)) + _dot(p_n.astype(BF16), vn)
        o_full = o_full * (1.0 / l)
        o = (o_full.reshape(heads, blk, d_model) * head_mask).sum(axis=0)
        o_scr[r0:r0 + blk, :] = o.astype(BF16)

    mix = _dot(o_scr[...], wo_ref[...])
    _finish_tokens(x, mix, alpha, lng_ref, lnb_ref, wrh_ref, wrl_ref, br_ref, h_ref, hp_ref, lgt_ref)


def _att_sample(x2d, in_row0, n_seq, seq_len, k_cache, v_cache, w_qkv, w_o, rel_table, ln_g, ln_b,
                wr_hi, wr_lo, br, alpha, nb, n_all, out_row0, dst):
    d_model = x2d.shape[1]
    heads = ATT_HEADS
    blk = seq_len
    n_cache = k_cache.shape[1]
    rows = nb * blk
    assert n_seq % nb == 0 and in_row0 % rows == 0 and out_row0 % rows == 0
    in_b0, out_b0 = in_row0 // rows, out_row0 // rows
    bias_c = _rel_bias(rel_table, blk, n_cache, n_cache, 1).reshape(heads * blk, n_cache)
    bias_n = _rel_bias(rel_table, blk, blk, 0, 1).reshape(heads * blk, blk)
    kc = k_cache.reshape(n_seq, n_cache, d_model)
    vc = v_cache.reshape(n_seq, n_cache, d_model)
    cache_spec = pl.BlockSpec((nb, n_cache, d_model), lambda g: (g, 0, 0))
    row_spec = pl.BlockSpec((nb, blk, d_model), lambda g: (g, 0, 0))
    rows_out = jax.ShapeDtypeStruct((n_seq, blk, d_model), F32)
    in_specs = [
        pl.BlockSpec((rows, d_model), lambda g: (in_b0 + g, 0)),
        cache_spec,
        cache_spec,
        _const_spec(w_qkv.shape),
        _const_spec(w_o.shape),
        _const_spec(bias_c.shape),
        _const_spec(bias_n.shape),
        _const_spec(ln_g.shape),
        _const_spec(ln_b.shape),
        _const_spec(wr_hi.shape),
        _const_spec(wr_lo.shape),
        _const_spec(br.shape),
    ]
    args = [x2d, kc, vc, w_qkv, w_o, bias_c, bias_n, ln_g, ln_b, wr_hi, wr_lo, br]
    aliases = _alias_dst(args, in_specs, dst)
    return pl.pallas_call(
        functools.partial(_att_sample_kernel, nb=nb, blk=blk, alpha=alpha),
        grid=(n_seq // nb,),
        in_specs=in_specs,
        out_specs=_token_out_specs(rows, d_model, lambda g: (out_b0 + g, 0)) + (row_spec, row_spec),
        out_shape=_token_out_shapes(n_all, d_model) + (rows_out, rows_out),
        input_output_aliases=aliases,
        scratch_shapes=[pltpu.VMEM((rows, d_model), BF16)],
        compiler_params=pltpu.CompilerParams(
            dimension_semantics=("arbitrary",), vmem_limit_bytes=VMEM_LIMIT),
        name="att_mixer_cache",
    )(*args)


ROUTE_ROWS = 40


def _route_kernel(lgt_ref, meta_t_ref, meta_w_ref, count_ref, carry_ref, tri_ref):
    j = pl.program_id(0)
    tm = lgt_ref.shape[0]

    @pl.when(j == 0)
    def _():
        carry_ref[...] = jnp.zeros(carry_ref.shape, F32)
        r = lax.broadcasted_iota(jnp.int32, (tm, tm), 0)
        c = lax.broadcasted_iota(jnp.int32, (tm, tm), 1)
        tri_ref[...] = (r < c).astype(BF16)

    a = jnp.transpose(lgt_ref[...])[0:ROUTE_ROWS, :]
    row = lax.broadcasted_iota(jnp.int32, (ROUTE_ROWS, tm), 0)
    big = jnp.int32(LANES)
    glog = jnp.where(row < N_GROUPS, a, NEG_INF)
    gmax = glog.max(axis=0, keepdims=True)
    gsel = jnp.where(glog == gmax, row, big).min(axis=0, keepdims=True)
    gp = 1.0 / jnp.exp(glog - gmax).sum(axis=0, keepdims=True)
    first = N_GROUPS + gsel * EXP_PER_GROUP
    in_grp = (row >= first) & (row < first + EXP_PER_GROUP)
    elog = jnp.where(in_grp, a, NEG_INF)
    v1 = elog.max(axis=0, keepdims=True)
    l1 = jnp.where(elog == v1, row, big).min(axis=0, keepdims=True)
    elog2 = jnp.where(row == l1, NEG_INF, elog)
    v2 = elog2.max(axis=0, keepdims=True)
    l2 = jnp.where(elog2 == v2, row, big).min(axis=0, keepdims=True)
    e2 = jnp.exp(v2 - v1)
    w1 = gp / (1.0 + e2)
    w2 = gp * e2 / (1.0 + e2)

    is1 = row == l1
    is2 = row == l2
    oh = (is1 | is2).astype(F32)
    before = _dot(oh.astype(BF16), tri_ref[...]) + carry_ref[:, 0:1]
    rank1 = jnp.where(is1, before, 0.0).sum(axis=0, keepdims=True)
    rank2 = jnp.where(is2, before, 0.0).sum(axis=0, keepdims=True)
    carry_ref[...] = carry_ref[...] + oh.sum(axis=1, keepdims=True)
    count_ref[...] = carry_ref[...].astype(jnp.int32)

    pad = jnp.zeros((SUBLANES - 4, tm), jnp.int32)
    meta_t_ref[...] = jnp.concatenate(
        [l1 - N_GROUPS, l2 - N_GROUPS, rank1.astype(jnp.int32), rank2.astype(jnp.int32), pad], axis=0)
    wt = jnp.concatenate([w1, w2, jnp.zeros((LANES - 2, tm), F32)], axis=0)
    meta_w_ref[...] = jnp.transpose(wt)


def _route(logits):
    n_tok = logits.shape[0]
    tm = TOKEN_TILE
    assert n_tok % tm == 0
    tile = pl.BlockSpec((tm, LANES), lambda j: (j, 0))
    return pl.pallas_call(
        _route_kernel,
        grid=(n_tok // tm,),
        in_specs=[tile],
        out_specs=(pl.BlockSpec((SUBLANES, tm), lambda j: (0, j)), tile,
                   pl.BlockSpec((ROUTE_ROWS, LANES), lambda j: (0, 0))),
        out_shape=(
            jax.ShapeDtypeStruct((SUBLANES, n_tok), jnp.int32),
            jax.ShapeDtypeStruct((n_tok, LANES), F32),
            jax.ShapeDtypeStruct((ROUTE_ROWS, LANES), jnp.int32),
        ),
        scratch_shapes=[pltpu.VMEM((ROUTE_ROWS, LANES), F32), pltpu.VMEM((tm, tm), BF16)],
        compiler_params=pltpu.CompilerParams(dimension_semantics=("arbitrary",)),
        name="moe_route",
    )(logits)


def _sc_gather_loop(table_hbm, idx_v, out_hbm, rows_v, sems, base, n_chunk, r):
    def gather(c, slot):
        off = pl.multiple_of(c * r, r)
        return pltpu.make_async_copy(table_hbm.at[idx_v.at[pl.ds(off, r)]], rows_v.at[slot], sems.at[slot])

    def finish(c, slot):
        gather(c, slot).wait()
        pltpu.sync_copy(rows_v.at[slot], out_hbm.at[pl.ds(base + pl.multiple_of(c * r, r), r)])

    gather(0, 0).start()
    if n_chunk > 1:
        gather(1, 1).start()

    def body(pair, carry):
        c = 2 * pair
        for slot in range(2):
            finish(c + slot, slot)

            @pl.when(c + slot + 2 < n_chunk)
            def _():
                gather(c + slot + 2, slot).start()
        return carry

    lax.fori_loop(0, n_chunk // 2, body, 0)
    if n_chunk % 2:
        finish(n_chunk - 1, 0)


def _gather_rows(table, idx):
    m = idx.shape[0]
    width = table.shape[1]
    r = SC_ROWS_PER_CHUNK
    assert m % (SC_WORKERS * r) == 0
    per_w = m // SC_WORKERS
    mesh = plsc.VectorSubcoreMesh(core_axis_name="c", subcore_axis_name="s")

    @functools.partial(
        pl.kernel,
        mesh=mesh,
        out_type=jax.ShapeDtypeStruct((m, width), table.dtype),
        scratch_types=[
            pltpu.VMEM((per_w,), jnp.int32),
            pltpu.VMEM((2, r, width), table.dtype),
            pltpu.SemaphoreType.DMA((2,)),
        ],
    )
    def gather(table_hbm, idx_hbm, out_hbm, idx_v, rows_v, sem):
        wid = lax.axis_index("s") * 2 + lax.axis_index("c")
        base = wid * per_w
        pltpu.sync_copy(idx_hbm.at[pl.ds(base, per_w)], idx_v)
        _sc_gather_loop(table_hbm, idx_v, out_hbm, rows_v, sem, base, per_w // r, r)

    return gather(table, idx)


def _dispatch_rows(table, pos, row0, n_rows):
    n_tok, width = table.shape
    n_pairs = pos.shape[0]
    r = SC_ROWS_PER_CHUNK
    lanes = SC_LANES
    assert n_rows % (SC_WORKERS * r) == 0
    per_w = n_rows // SC_WORKERS
    n_stage = 16
    stage = n_pairs // n_stage
    assert stage * n_stage == n_pairs and stage % lanes == 0 and per_w % lanes == 0
    assert row0 + n_rows < 3 * n_tok
    mesh = plsc.VectorSubcoreMesh(core_axis_name="c", subcore_axis_name="s")

    @functools.partial(
        pl.kernel,
        mesh=mesh,
        out_type=jax.ShapeDtypeStruct((n_rows, width), table.dtype),
        scratch_types=[
            pltpu.VMEM((per_w,), jnp.int32),
            pltpu.VMEM((stage,), jnp.int32),
            pltpu.VMEM((2, r, width), table.dtype),
            pltpu.SemaphoreType.DMA((2,)),
        ],
        compiler_params=pltpu.CompilerParams(needs_layout_passes=False),
    )
    def dispatch(table_hbm, pos_hbm, out_hbm, src_v, pos_v, rows_v, sem):
        wid = lax.axis_index("s") * 2 + lax.axis_index("c")
        out_base = wid * per_w
        base = row0 + out_base
        lane = lax.iota(jnp.int32, lanes)

        def wrap(t):
            t = jnp.where(t >= n_tok, t - n_tok, t)
            return jnp.where(t >= n_tok, t - n_tok, t)

        def init(i, carry):
            off = pl.multiple_of(i * lanes, lanes)
            src_v[pl.ds(off, lanes)] = wrap(base + off + lane)
            return carry

        lax.fori_loop(0, per_w // lanes, init, 0)

        def scan_stage(sidx, carry):
            pair0 = sidx * stage
            pltpu.sync_copy(pos_hbm.at[pl.ds(pl.multiple_of(pair0, 8), stage)], pos_v)

            def scan(i, c2):
                off = pl.multiple_of(i * lanes, lanes)
                local = pos_v[pl.ds(off, lanes)] - base
                mine = (local >= 0) & (local < per_w)
                plsc.store_scatter(src_v, [jnp.where(mine, local, 0)], wrap(pair0 + off + lane), mask=mine)
                return c2

            lax.fori_loop(0, stage // lanes, scan, 0)
            return carry

        lax.fori_loop(0, n_stage, scan_stage, 0)
        _sc_gather_loop(table_hbm, src_v, out_hbm, rows_v, sem, out_base, per_w // r, r)

    return dispatch(table, pos)


def _gemm_kernel(tile_expert_ref, n_used_ref, xs_ref, wg_ref, wu_ref, wd_ref, *rest, tile0):
    ys_ref, wg_b, wu_b, wd_b = rest[-4:]
    j = pl.program_id(0)
    tile = tile0 + j

    @pl.when(tile < n_used_ref[0])
    def _():
        new_expert = (j == 0) | (tile_expert_ref[tile] != tile_expert_ref[jnp.maximum(tile - 1, 0)])

        @pl.when(new_expert)
        def _():
            wg_b[...] = wg_ref[...].astype(BF16)
            wu_b[...] = wu_ref[...].astype(BF16)
            wd_b[...] = wd_ref[...].astype(BF16)

        x = _unpack_rows(xs_ref[...]).astype(BF16)
        g = _dot(x, wg_b[...])
        u = _dot(x, wu_b[...])
        hmid = (jax.nn.silu(g) * u).astype(BF16)
        ys_ref[...] = _pack_rows(_dot(hmid, wd_b[...]))


def _expert_gemm(xs, tile_expert, n_used, layer, w_gate, w_up, w_down, tile0, n_rows_all, ys):
    n_rows, half = xs.shape
    d_model, d_exp = w_gate.shape[2], w_gate.shape[3]
    tm = GEMM_TILE
    n_tiles = n_rows // tm

    def last_used(j, nu):
        return jnp.minimum(tile0 + j, nu[0] - 1)

    def in_map(j, te, nu):
        return (jnp.maximum(last_used(j, nu) - tile0, 0), 0)

    def out_map(j, te, nu):
        return (jnp.maximum(last_used(j, nu), tile0), 0)

    def w_map(j, te, nu):
        return (layer, te[tile0 + j], 0, 0)

    in_specs = [
        pl.BlockSpec((tm, half), in_map),
        pl.BlockSpec((None, None, d_model, d_exp), w_map),
        pl.BlockSpec((None, None, d_model, d_exp), w_map),
        pl.BlockSpec((None, None, d_exp, d_model), w_map),
    ]
    args = [tile_expert, n_used, xs, w_gate, w_up, w_down]
    aliases = {}
    if ys is not None:
        aliases[len(args)] = 0
        args.append(ys)
        in_specs.append(pl.BlockSpec(memory_space=pl.ANY))
    grid_spec = pltpu.PrefetchScalarGridSpec(
        num_scalar_prefetch=2,
        grid=(n_tiles,),
        in_specs=in_specs,
        out_specs=pl.BlockSpec((tm, half), out_map),
        scratch_shapes=[
            pltpu.VMEM((d_model, d_exp), BF16),
            pltpu.VMEM((d_model, d_exp), BF16),
            pltpu.VMEM((d_exp, d_model), BF16),
        ],
    )
    return pl.pallas_call(
        functools.partial(_gemm_kernel, tile0=tile0),
        grid_spec=grid_spec,
        out_shape=jax.ShapeDtypeStruct((n_rows_all, half), U32),
        input_output_aliases=aliases,
        compiler_params=pltpu.CompilerParams(
            dimension_semantics=("arbitrary",), vmem_limit_bytes=VMEM_LIMIT),
        name="moe_gemm",
    )(*args)


def _combine_kernel(y0_ref, y1_ref, mw_ref, h_ref, pp_ref, ps_ref, lng_ref, lnb_ref, wproj_ref, wgate_ref,
                    *out_refs, n_prompt_tiles, alpha):
    j = pl.program_id(0)
    mw = mw_ref[...]
    ffn = mw[:, 0:1] * _unpack_rows(y0_ref[...]) + mw[:, 1:2] * _unpack_rows(y1_ref[...])
    h2 = _layer_norm(alpha * h_ref[...] + ffn, lng_ref[...], lnb_ref[...])
    p = jnp.where(j < n_prompt_tiles, pp_ref[...], ps_ref[...]).astype(BF16)
    gate = jax.nn.sigmoid(_dot(h2.astype(BF16), wgate_ref[...]))
    out = h2 + gate * _dot(p, wproj_ref[...])
    if len(out_refs) == 1:
        out_refs[0][...] = out
    else:
        @pl.when(j < n_prompt_tiles)
        def _():
            out_refs[0][...] = out

        @pl.when(j >= n_prompt_tiles)
        def _():
            out_refs[1][...] = out


def _combine(yg, meta_w, h, p_prompt, p_sample, layer, ln_g, ln_b, w_proj, w_gate, alpha, split_out):
    n_tok, d_model = h.shape
    tm = TOKEN_TILE
    n_t = n_tok // tm
    n_tp = p_prompt.shape[1] // tm
    n_ts = p_sample.shape[1] // tm
    assert n_tp * tm == p_prompt.shape[1] and n_ts * tm == p_sample.shape[1] and n_tp + n_ts == n_t
    ple = p_prompt.shape[2]
    tile = pl.BlockSpec((tm, d_model), lambda j: (j, 0))
    if split_out:
        out_specs = (pl.BlockSpec((tm, d_model), lambda j: (jnp.minimum(j, n_tp - 1), 0)),
                     pl.BlockSpec((tm, d_model), lambda j: (jnp.maximum(j - n_tp, 0), 0)))
        out_shape = (jax.ShapeDtypeStruct((n_tp * tm, d_model), F32),
                     jax.ShapeDtypeStruct((n_ts * tm, d_model), F32))
    else:
        out_specs = tile
        out_shape = jax.ShapeDtypeStruct((n_tok, d_model), F32)
    return pl.pallas_call(
        functools.partial(_combine_kernel, n_prompt_tiles=n_tp, alpha=alpha),
        grid=(n_t,),
        in_specs=[
            pl.BlockSpec((tm, d_model // 2), lambda j: (j, 0)),
            pl.BlockSpec((tm, d_model // 2), lambda j: (j + n_t, 0)),
            pl.BlockSpec((tm, LANES), lambda j: (j, 0)),
            tile,
            pl.BlockSpec((None, tm, ple), lambda j: (layer, jnp.minimum(j, n_tp - 1), 0)),
            pl.BlockSpec((None, tm, ple), lambda j: (layer, jnp.maximum(j - n_tp, 0), 0)),
            _const_spec(ln_g.shape),
            _const_spec(ln_b.shape),
            _const_spec(w_proj.shape),
            _const_spec(w_gate.shape),
        ],
        out_specs=out_specs,
        out_shape=out_shape,
        compiler_params=pltpu.CompilerParams(
            dimension_semantics=("arbitrary",), vmem_limit_bytes=VMEM_LIMIT),
        name="moe_combine",
    )(yg, yg, meta_w, h, p_prompt, p_sample, ln_g, ln_b, w_proj, w_gate)


def _router_weights(w_grp, b_grp, w_exp, b_exp):
    d_model = w_grp.shape[0]
    w = jnp.concatenate([w_grp, jnp.transpose(w_exp, (1, 0, 2)).reshape(d_model, N_EXPERTS)], axis=1)
    w = jnp.pad(w, ((0, 0), (0, LANES - w.shape[1])))
    b = jnp.concatenate([b_grp, b_exp.reshape(N_EXPERTS)])
    b = jnp.pad(b, (0, LANES - b.shape[0])).reshape(1, LANES)
    w_hi = w.astype(BF16)
    w_lo = (w - w_hi.astype(F32)).astype(BF16)
    return w_hi, w_lo, b


def _moe(h_packed, logits, layer, w_gate, w_up, w_down):
    n_tok = h_packed.shape[0]
    tm = GEMM_TILE
    meta_t, meta_w, counts = _route(logits)
    counts = counts[N_GROUPS:N_GROUPS + N_EXPERTS, 0]
    tiles_per_expert = (counts + tm - 1) // tm
    tile_end = jnp.cumsum(tiles_per_expert)
    row_start = (tile_end - tiles_per_expert) * tm
    eid = meta_t[0:2]
    rank = meta_t[2:4]
    experts = jnp.arange(N_EXPERTS, dtype=jnp.int32)
    start = jnp.sum(jnp.where(eid[:, :, None] == experts, row_start, 0), axis=-1)
    pos = (start + rank).reshape(-1).astype(jnp.int32)
    part_quant = MOE_PARTS * SC_WORKERS * SC_ROWS_PER_CHUNK * (tm // math.gcd(tm, SC_WORKERS * SC_ROWS_PER_CHUNK))
    n_tiles = -(-(2 * n_tok) // tm) + N_EXPERTS
    n_rows = -(-(n_tiles * tm) // part_quant) * part_quant
    n_tiles = n_rows // tm
    n_used = tile_end[-1:].astype(jnp.int32)
    tile_ids = jnp.minimum(jnp.arange(n_tiles, dtype=jnp.int32), n_used[0] - 1)
    tile_expert = jnp.sum(tile_end[None, :] <= tile_ids[:, None], axis=1).astype(jnp.int32)
    part_rows = n_rows // MOE_PARTS
    xs_parts = [_dispatch_rows(h_packed, pos, part * part_rows, part_rows) for part in range(MOE_PARTS)]
    ys = None
    for part, xs in enumerate(xs_parts):
        ys = _expert_gemm(xs, tile_expert, n_used, layer, w_gate, w_up, w_down,
                          part * (part_rows // tm), n_rows, ys)
    return _gather_rows(ys, pos), meta_w


def kernel(x_prompt, x_sample, p_prompt, p_sample, state_ret, cache_att_k, cache_att_v, ret_w_in, ret_gn_g,
           ret_w_o, att_w_qkv, att_rel_bias, att_w_o, ln1_g, ln1_b, ln2_g, ln2_b, moe_w_grp, moe_b_grp,
           moe_w_exp, moe_b_exp, moe_w_gate, moe_w_up, moe_w_down, ple_w_proj, ple_w_gate):
    n_p, len_p, d_model = x_prompt.shape
    n_s, len_s, _ = x_sample.shape
    depth = ln1_g.shape[0]
    alpha = float((2 * depth) ** 0.25)
    tok_p, tok_s = n_p * len_p, n_s * len_s
    n_all = tok_p + tok_s
    dh = d_model // ATT_HEADS
    pp = p_prompt.reshape(depth, tok_p, -1)
    ps = p_sample.reshape(depth, tok_s, -1)
    nb_s = 4

    x_all = None
    y_prompt = y_sample = None
    states_p, states_s, k_p, v_p, k_s, v_s = [], [], [], [], [], []
    for i in range(depth):
        jj = i // 2
        wr_hi, wr_lo, br = _router_weights(moe_w_grp[i], moe_b_grp[i], moe_w_exp[i], moe_b_exp[i])
        lng, lnb = ln1_g[i].reshape(1, d_model), ln1_b[i].reshape(1, d_model)
        if x_all is None:
            src_p, src_s, row_s = x_prompt.reshape(tok_p, d_model), x_sample.reshape(tok_s, d_model), 0
        else:
            src_p, src_s, row_s = x_all, x_all, tok_p
        if i % 2 == 0:
            w_in = ret_w_in[jj].astype(BF16)
            w_o = ret_w_o[jj].astype(BF16)
            gn = ret_gn_g[jj].reshape(1, -1)
            h, hp, lgt, st_p = _ret_mixer(src_p, 0, n_p, len_p, 0, None, w_in, w_o, gn, lng, lnb,
                                          wr_hi, wr_lo, br, alpha, nb=1, blk=min(RET_BLOCK, len_p), chain=RET_CHAIN,
                                          n_all=n_all, out_row0=0, dst=None)
            h, hp, lgt, st_s = _ret_mixer(src_s, row_s, n_s, len_s, PAST_LEN, state_ret[jj], w_in, w_o, gn,
                                          lng, lnb, wr_hi, wr_lo, br, alpha, nb=nb_s, blk=len_s, chain=1,
                                          n_all=n_all, out_row0=tok_p, dst=(h, hp, lgt))
            states_p.append(st_p)
            states_s.append(st_s)
        else:
            w_qkv = att_w_qkv[jj].astype(BF16)
            w_o = att_w_o[jj].astype(BF16)
            h, hp, lgt, kr, vr = _att_prompt(src_p, n_p, len_p, w_qkv, w_o, att_rel_bias[jj], lng, lnb,
                                             wr_hi, wr_lo, br, alpha, n_all)
            k_p.append(kr.reshape(n_p, -1, ATT_HEADS, dh))
            v_p.append(vr.reshape(n_p, -1, ATT_HEADS, dh))
            h, hp, lgt, kr, vr = _att_sample(src_s, row_s, n_s, len_s, cache_att_k[jj], cache_att_v[jj],
                                             w_qkv, w_o, att_rel_bias[jj], lng, lnb, wr_hi, wr_lo, br, alpha,
                                             nb=nb_s, n_all=n_all, out_row0=tok_p, dst=(h, hp, lgt))
            k_s.append(kr.reshape(n_s, len_s, ATT_HEADS, dh))
            v_s.append(vr.reshape(n_s, len_s, ATT_HEADS, dh))
        yg, meta_w = _moe(hp, lgt, i, moe_w_gate, moe_w_up, moe_w_down)
        last = i == depth - 1
        out = _combine(yg, meta_w, h, pp, ps, i, ln2_g[i].reshape(1, d_model), ln2_b[i].reshape(1, d_model),
                       ple_w_proj[i].astype(BF16), ple_w_gate[i].astype(BF16), alpha, split_out=last)
        if last:
            y_prompt = out[0].reshape(n_p, len_p, d_model)
            y_sample = out[1].reshape(n_s, len_s, d_model)
        else:
            x_all = out

    return (y_prompt, y_sample, jnp.stack(states_p), jnp.stack(states_s),
            jnp.stack(k_p), jnp.stack(v_p), jnp.stack(k_s), jnp.stack(v_s))
```

```python
import functools
import math

import numpy as np
import jax
import jax.numpy as jnp
from jax import lax
from jax.experimental import pallas as pl
from jax.experimental.pallas import tpu as pltpu
from jax.experimental.pallas import tpu_sc as plsc

CHUNK = 64
PAST_LEN = 2048
RET_HEADS = 4
ROPE_BASE = 10000.0
ATT_HEADS = 16
BAND_CHUNKS = 8
REL_CLIP = 256
N_GROUPS = 4
EXP_PER_GROUP = 8
N_EXPERTS = N_GROUPS * EXP_PER_GROUP
LN_EPS = 1e-5
NEG_INF = -1e30

LANES = 128
SUBLANES = 8
BF16_SUBLANES = 16
SC_WORKERS = 32
SC_LANES = 16
SC_ROWS_PER_CHUNK = 32
VMEM_LIMIT = 56 * 1024 * 1024

RET_BLOCK = 256
RET_CHAIN = 2
ATT_BLOCK = 4 * CHUNK
ATT_CHAIN = 2
TOKEN_TILE = 512
GEMM_TILE = 512
MOE_PARTS = 3

F32 = jnp.float32
BF16 = jnp.bfloat16
U32 = jnp.uint32
HI_MASK = 0xFFFF0000
LOG2E = math.log2(math.e)


def _dot(a, b):
    return jnp.dot(a, b, preferred_element_type=F32)


def _dot_nt(a, b):
    return lax.dot_general(a, b, (((1,), (1,)), ((), ())), preferred_element_type=F32)


def _dot_tn(a, b):
    return lax.dot_general(a, b, (((0,), (0,)), ((), ())), preferred_element_type=F32)


def _layer_norm(x, g, b):
    mu = jnp.mean(x, axis=-1, keepdims=True)
    xc = x - mu
    var = jnp.mean(xc * xc, axis=-1, keepdims=True)
    return xc * lax.rsqrt(var + LN_EPS) * g + b


def _pack_rows(x):
    half = x.shape[1] // 2
    lo = lax.bitcast_convert_type(x[:, :half].astype(BF16).astype(F32), U32) >> 16
    hi = lax.bitcast_convert_type(x[:, half:].astype(BF16).astype(F32), U32) & U32(HI_MASK)
    return lo | hi


def _unpack_rows(p):
    lo = lax.bitcast_convert_type(p << 16, F32)
    hi = lax.bitcast_convert_type(p & U32(HI_MASK), F32)
    return jnp.concatenate([lo, hi], axis=1)


def _router_logits(h, wr_hi_ref, wr_lo_ref, br_ref):
    h_hi = h.astype(BF16)
    h_lo = (h - h_hi.astype(F32)).astype(BF16)
    w_hi = wr_hi_ref[...]
    return _dot(h_hi, w_hi) + _dot(h_lo, w_hi) + _dot(h_hi, wr_lo_ref[...]) + br_ref[...]


def _finish_tokens(x, mix, alpha, lng_ref, lnb_ref, wrh_ref, wrl_ref, br_ref, h_ref, hp_ref, lgt_ref):
    hh = _layer_norm(alpha * x + mix, lng_ref[...], lnb_ref[...])
    h_ref[...] = hh
    hp_ref[...] = _pack_rows(hh)
    lgt_ref[...] = _router_logits(hh, wrh_ref, wrl_ref, br_ref)


def _zero_tokens(h_ref, hp_ref, lgt_ref):
    h_ref[...] = jnp.zeros(h_ref.shape, h_ref.dtype)
    hp_ref[...] = jnp.zeros(hp_ref.shape, hp_ref.dtype)
    lgt_ref[...] = jnp.zeros(lgt_ref.shape, lgt_ref.dtype)


def _const_spec(shape):
    nd = len(shape)
    return pl.BlockSpec(shape, lambda *_: (0,) * nd, pipeline_mode=pl.Buffered(1))


def _token_out_shapes(n_all, d_model):
    return (
        jax.ShapeDtypeStruct((n_all, d_model), F32),
        jax.ShapeDtypeStruct((n_all, d_model // 2), U32),
        jax.ShapeDtypeStruct((n_all, LANES), F32),
    )


def _token_out_specs(rows, d_model, row_map):
    return (
        pl.BlockSpec((rows, d_model), row_map),
        pl.BlockSpec((rows, d_model // 2), row_map),
        pl.BlockSpec((rows, LANES), row_map),
    )


def _alias_dst(args, in_specs, dst):
    if dst is None:
        return {}
    aliases = {}
    for k, arr in enumerate(dst):
        aliases[len(args)] = k
        args.append(arr)
        in_specs.append(pl.BlockSpec(memory_space=pl.ANY))
    return aliases


def _ret_log_gamma():
    h = np.arange(RET_HEADS, dtype=np.float32)
    return np.log(np.float32(1.0) - np.float32(2.0) ** (np.float32(-5.0) - h)).astype(np.float32)


def _ret_kernel(*refs, nb, blk, chain, nblk, n_steps, n_fill, has_state, n_alias, alpha):
    step = pl.program_id(0)
    h_ref, hp_ref, lgt_ref = refs[12 + int(has_state) + n_alias:][:3]

    @pl.when(step < n_steps)
    def _():
        _ret_step(*refs, first_block=step % nblk == 0, nb=nb, blk=blk, chain=chain, has_state=has_state,
                  n_alias=n_alias, alpha=alpha)

    if n_fill:
        @pl.when(step >= n_steps)
        def _():
            _zero_tokens(h_ref, hp_ref, lgt_ref)


def _ret_step(*refs, first_block, nb, blk, chain, has_state, n_alias, alpha):
    (x_ref, cos_ref, sin_ref, dmask_ref, win_ref, wo_ref, gn_ref, lng_ref, lnb_ref,
     wrh_ref, wrl_ref, br_ref) = refs[:12]
    s_in_ref = refs[12] if has_state else None
    h_ref, hp_ref, lgt_ref, s_out_ref, gated_ref = refs[12 + int(has_state) + n_alias:]
    heads = RET_HEADS
    d_model = x_ref.shape[1]
    dk = d_model // heads
    dv = 2 * d_model // heads
    hk, hv = heads * dk, heads * dv
    half = dk // 2
    lg = _ret_log_gamma()

    n_sub = nb * chain
    proj_rows = blk if chain > 1 else n_sub * blk
    rowf = lax.broadcasted_iota(jnp.int32, (blk, 1), 0).astype(F32)

    def proj(group):
        xb = x_ref[group * proj_rows:(group + 1) * proj_rows, :].astype(BF16)
        return (_dot(xb, win_ref[:, 0:hk]), _dot(xb, win_ref[:, hk:2 * hk]),
                _dot(xb, win_ref[:, 2 * hk:2 * hk + hv]), _dot(xb, win_ref[:, 2 * hk + hv:2 * hk + 2 * hv]))

    if not has_state:
        @pl.when(first_block)
        def _():
            s_out_ref[...] = jnp.zeros(s_out_ref.shape, F32)

    s_prev_ref = s_in_ref if has_state else s_out_ref

    def head(j, h, projected):
        q_all, k_all, v_all, g_all = projected
        s = j // chain
        c0 = (j % chain) * blk
        r0 = (j * blk) % proj_rows
        cos = cos_ref[c0:c0 + blk, :]
        sin = sin_ref[c0:c0 + blk, :]

        def rot(t):
            t1, t2 = t[:, :half], t[:, half:]
            return jnp.concatenate([t1 * cos - t2 * sin, t1 * sin + t2 * cos], axis=1)

        lgh = float(lg[h])
        q = rot(q_all[r0:r0 + blk, h * dk:(h + 1) * dk])
        k = rot(k_all[r0:r0 + blk, h * dk:(h + 1) * dk]) * (dk ** -0.5)
        v = v_all[r0:r0 + blk, h * dv:(h + 1) * dv]
        g = g_all[r0:r0 + blk, h * dv:(h + 1) * dv]
        vb = v.astype(BF16)
        scores = _dot_nt(q.astype(BF16), k.astype(BF16)) * dmask_ref[h]
        inner = _dot(scores.astype(BF16), vb)
        s_prev = s_prev_ref[s, h]
        q_dec = q * jnp.exp(lgh * (rowf + 1.0))
        cross = _dot(q_dec.astype(BF16), s_prev.astype(BF16))
        k_dec = k * jnp.exp(lgh * (float(blk - 1) - rowf))
        s_out_ref[s, h] = math.exp(lgh * blk) * s_prev + _dot_tn(k_dec.astype(BF16), vb)
        o = inner + cross
        mu = jnp.mean(o, axis=-1, keepdims=True)
        oc = o - mu
        var = jnp.mean(oc * oc, axis=-1, keepdims=True)
        on = oc * lax.rsqrt(var + LN_EPS) * gn_ref[:, h * dv:(h + 1) * dv]
        gated_ref[j * blk:(j + 1) * blk, h * dv:(h + 1) * dv] = (jax.nn.silu(g) * on).astype(BF16)

    def tail(group):
        rows = slice(group * proj_rows, (group + 1) * proj_rows)
        mix = _dot(gated_ref[rows, :], wo_ref[...])
        _finish_tokens(x_ref[rows, :], mix, alpha, lng_ref, lnb_ref, wrh_ref, wrl_ref, br_ref,
                       h_ref.at[rows, :], hp_ref.at[rows, :], lgt_ref.at[rows, :])

    n_groups = (n_sub * blk) // proj_rows
    subs_per_group = proj_rows // blk
    projected = proj(0)
    for group in range(n_groups):
        nxt = None
        for jj in range(subs_per_group):
            for h in range(heads):
                head(group * subs_per_group + jj, h, projected)
                if jj == 0 and h == 0 and group + 1 < n_groups:
                    nxt = proj(group + 1)
                if jj == 0 and h == 1 and group > 0:
                    tail(group - 1)
        projected = nxt
    tail(n_groups - 1)


def _ret_mixer(x2d, in_row0, n_seq, seq_len, pos0, state_in, w_in, w_o, gn_g, ln_g, ln_b,
               wr_hi, wr_lo, br, alpha, nb, blk, chain, n_all, out_row0, dst):
    d_model = x2d.shape[1]
    heads = RET_HEADS
    dk, dv = d_model // heads, 2 * d_model // heads
    half = dk // 2
    nblk = seq_len // (blk * chain)
    has_state = state_in is not None
    rows = nb * blk * chain
    assert seq_len % (blk * chain) == 0 and n_seq % nb == 0
    assert (not has_state) or nblk == 1
    assert nb == 1 or (nblk == 1 and chain == 1)
    assert in_row0 % rows == 0 and out_row0 % rows == 0
    in_b0, out_b0 = in_row0 // rows, out_row0 // rows

    pos = (pos0 + jnp.arange(seq_len, dtype=jnp.int32)).astype(F32)
    inv_freq = ROPE_BASE ** (-jnp.arange(half, dtype=F32) / half)
    ang = pos[:, None] * inv_freq[None, :]
    cos, sin = jnp.cos(ang), jnp.sin(ang)
    lg = jnp.asarray(_ret_log_gamma())
    ii = jnp.arange(blk, dtype=F32)
    diff = ii[:, None] - ii[None, :]
    dmask = jnp.where(diff >= 0, jnp.exp(lg[:, None, None] * jnp.maximum(diff, 0.0)), 0.0)

    n_steps = (n_seq // nb) * nblk
    n_fill = 0 if dst is not None else (n_all - n_seq * seq_len) // rows
    assert dst is not None or (out_row0 == 0 and n_fill * rows == n_all - n_seq * seq_len)

    def work(t):
        return jnp.minimum(t, n_steps - 1)

    in_specs = [
        pl.BlockSpec((rows, d_model), lambda t: (in_b0 + work(t), 0)),
        pl.BlockSpec((blk * chain, half), lambda t: (work(t) % nblk, 0)),
        pl.BlockSpec((blk * chain, half), lambda t: (work(t) % nblk, 0)),
        _const_spec(dmask.shape),
        _const_spec(w_in.shape),
        _const_spec(w_o.shape),
        _const_spec(gn_g.shape),
        _const_spec(ln_g.shape),
        _const_spec(ln_b.shape),
        _const_spec(wr_hi.shape),
        _const_spec(wr_lo.shape),
        _const_spec(br.shape),
    ]
    args = [x2d, cos, sin, dmask, w_in, w_o, gn_g, ln_g, ln_b, wr_hi, wr_lo, br]
    state_spec = pl.BlockSpec((nb, heads, dk, dv), lambda t: (work(t) // nblk, 0, 0, 0))
    if has_state:
        in_specs.append(state_spec)
        args.append(state_in)
    aliases = _alias_dst(args, in_specs, dst)
    out_shape = _token_out_shapes(n_all, d_model) + (jax.ShapeDtypeStruct((n_seq, heads, dk, dv), F32),)
    out_specs = _token_out_specs(rows, d_model, lambda t: (out_b0 + t, 0)) + (state_spec,)
    return pl.pallas_call(
        functools.partial(_ret_kernel, nb=nb, blk=blk, chain=chain, nblk=nblk, n_steps=n_steps, n_fill=n_fill,
                          has_state=has_state, n_alias=len(aliases), alpha=alpha),
        grid=(n_steps + n_fill,),
        in_specs=in_specs,
        out_specs=out_specs,
        out_shape=out_shape,
        input_output_aliases=aliases,
        scratch_shapes=[pltpu.VMEM((rows, heads * dv), BF16)],
        compiler_params=pltpu.CompilerParams(
            dimension_semantics=("arbitrary",), vmem_limit_bytes=VMEM_LIMIT),
        name="ret_mixer_state" if has_state else "ret_mixer",
    )(*args)


_RING = 3


def _att_prompt_kernel(*refs, blk, chain, nblk, n_steps, n_fill, alpha):
    step = pl.program_id(0)
    h_ref, hp_ref, lgt_ref = refs[9:12]

    @pl.when(step < n_steps)
    def _():
        _att_prompt_step(*refs, i0=(step % nblk) * chain, blk=blk, chain=chain, alpha=alpha)

    if n_fill:
        @pl.when(step >= n_steps)
        def _():
            _zero_tokens(h_ref, hp_ref, lgt_ref)


def _att_prompt_step(x_ref, wqkv_ref, wo_ref, bias_ref, lng_ref, lnb_ref, wrh_ref, wrl_ref, br_ref,
                     h_ref, hp_ref, lgt_ref, krow_ref, vrow_ref, kring, vring, o_scr, *, i0, blk, chain, alpha):
    d_model = x_ref.shape[1]
    dh = d_model // ATT_HEADS
    heads_per_group = LANES // dh
    lane = lax.broadcasted_iota(jnp.int32, (1, LANES), 1)

    @pl.when(i0 == 0)
    def _():
        kring[...] = jnp.zeros(kring.shape, BF16)
        vring[...] = jnp.zeros(vring.shape, BF16)
        vring[:, :, dh:, :] = jnp.ones((_RING, ATT_HEADS, vring.shape[2] - dh, blk), BF16)

    def project(j):
        xb = x_ref[j * blk:(j + 1) * blk, :].astype(BF16)
        q = _dot(xb, wqkv_ref[:, 0:d_model]) * (dh ** -0.5 * LOG2E)
        k = _dot(xb, wqkv_ref[:, d_model:2 * d_model])
        v = _dot(xb, wqkv_ref[:, 2 * d_model:3 * d_model])
        return q, k, v

    def to_ring(j, k, v):
        slot = (i0 + j) % _RING
        krow_ref[j * blk:(j + 1) * blk, :] = k
        vrow_ref[j * blk:(j + 1) * blk, :] = v
        kring[slot] = k.astype(BF16)
        v_t = jnp.transpose(v).astype(BF16)
        for hd in range(ATT_HEADS):
            vring[slot, hd, 0:dh, :] = v_t[hd * dh:(hd + 1) * dh, :]

    def attend(j, q, between):
        i = i0 + j
        qb = q.astype(BF16)
        behind = [(i + _RING - k) % _RING for k in range(_RING)]
        slabs = [jnp.where(i >= behind[k], behind[k], _RING) for k in range(_RING)]

        def scores(hd):
            c0 = (hd // heads_per_group) * LANES
            sub = hd % heads_per_group
            in_head = (lane >= sub * dh) & (lane < (sub + 1) * dh)
            q_pair = qb[:, c0:c0 + LANES]
            qm = jnp.where(in_head, q_pair, jnp.zeros_like(q_pair))
            s_all = _dot_nt(kring[:, :, c0:c0 + LANES].reshape(_RING * blk, LANES), qm)
            return [s_all[k * blk:(k + 1) * blk] + bias_ref[slabs[k], hd] for k in range(_RING)]

        def probs(s_list):
            m = s_list[0].max(axis=0, keepdims=True)
            for k in range(1, _RING):
                m = jnp.maximum(m, s_list[k].max(axis=0, keepdims=True))
            return [jnp.exp2(sc - m).astype(BF16) for sc in s_list]

        def values(hd, p_list):
            o = _dot(vring[0, hd], p_list[0])
            for k in range(1, _RING):
                o = o + _dot(vring[k, hd], p_list[k])
            o_scr[j, hd * dh:(hd + 1) * dh, :] = (o[0:dh] * (1.0 / o[dh:dh + 1])).astype(BF16)

        s_next = scores(0)
        pending = None
        for hd in range(ATT_HEADS):
            s_cur = s_next
            if hd + 1 < ATT_HEADS:
                s_next = scores(hd + 1)
            p_list = probs(s_cur)
            if pending is not None:
                values(*pending)
            pending = (hd, p_list)
            if hd in between:
                between[hd]()
        values(*pending)

    def finish(j):
        rows = slice(j * blk, (j + 1) * blk)
        mix = _dot_tn(o_scr[j], wo_ref[...])
        _finish_tokens(x_ref[rows, :], mix, alpha, lng_ref, lnb_ref, wrh_ref, wrl_ref, br_ref,
                       h_ref.at[rows, :], hp_ref.at[rows, :], lgt_ref.at[rows, :])

    qkv = project(0)
    for j in range(chain):
        to_ring(j, qkv[1], qkv[2])
        nxt = []
        between = {}
        if j + 1 < chain:
            between[0] = lambda j=j: nxt.append(project(j + 1))
        attend(j, qkv[0], between)
        finish(j)
        qkv = nxt[0] if nxt else None


def _rel_bias(rel_table, n_rows, n_cols, offset, sign):
    heads = rel_table.shape[0]
    period = n_rows + n_cols
    m = jnp.arange(period)
    c_minus_r = jnp.where(m < n_cols, m, m - period)
    w = rel_table[:, jnp.clip(offset - sign * c_minus_r, -REL_CLIP, REL_CLIP) + REL_CLIP]
    flat = jnp.broadcast_to(w[:, None, :], (heads, n_rows, period)).reshape(heads, n_rows * period)
    return flat[:, :n_rows * (period - 1)].reshape(heads, n_rows, period - 1)[:, :, :n_cols]


def _att_prompt_bias(rel_table, blk):
    i = jnp.arange(blk)
    out = []
    for d in range(_RING):
        b = _rel_bias(rel_table, blk, blk, d * blk, -1)
        cd = (i[None, :] // CHUNK) - (i[:, None] // CHUNK) + d * (blk // CHUNK)
        ok = (cd >= 0) & (cd <= BAND_CHUNKS)
        out.append(jnp.where(ok[None], b, NEG_INF))
    out.append(jnp.full_like(out[0], NEG_INF))
    return jnp.stack(out).astype(F32)


def _att_prompt(x2d, n_seq, seq_len, w_qkv, w_o, rel_table, ln_g, ln_b, wr_hi, wr_lo, br, alpha, n_all):
    d_model = x2d.shape[1]
    blk = ATT_BLOCK
    chain = ATT_CHAIN
    rows = blk * chain
    nblk = seq_len // rows
    keep = min(BAND_CHUNKS * CHUNK, seq_len)
    assert seq_len % rows == 0 and keep % rows == 0
    assert (_RING - 1) * blk >= BAND_CHUNKS * CHUNK
    kb = keep // rows
    bias = _att_prompt_bias(rel_table, blk) * LOG2E
    n_steps = n_seq * nblk
    n_fill = (n_all - n_seq * seq_len) // rows
    assert n_fill * rows == n_all - n_seq * seq_len

    def work(t):
        return jnp.minimum(t, n_steps - 1)

    row_spec = pl.BlockSpec(
        (None, rows, d_model), lambda t: (work(t) // nblk, jnp.maximum(work(t) % nblk - (nblk - kb), 0), 0))
    rows_out = jax.ShapeDtypeStruct((n_seq, keep, d_model), F32)
    return pl.pallas_call(
        functools.partial(_att_prompt_kernel, blk=blk, chain=chain, nblk=nblk, n_steps=n_steps, n_fill=n_fill,
                          alpha=alpha),
        grid=(n_steps + n_fill,),
        in_specs=[
            pl.BlockSpec((rows, d_model), lambda t: (work(t), 0)),
            _const_spec(w_qkv.shape),
            _const_spec(w_o.shape),
            _const_spec(bias.shape),
            _const_spec(ln_g.shape),
            _const_spec(ln_b.shape),
            _const_spec(wr_hi.shape),
            _const_spec(wr_lo.shape),
            _const_spec(br.shape),
        ],
        out_specs=_token_out_specs(rows, d_model, lambda t: (t, 0)) + (row_spec, row_spec),
        out_shape=_token_out_shapes(n_all, d_model) + (rows_out, rows_out),
        scratch_shapes=[
            pltpu.VMEM((_RING, blk, d_model), BF16),
            pltpu.VMEM((_RING, ATT_HEADS, d_model // ATT_HEADS + BF16_SUBLANES, blk), BF16),
            pltpu.VMEM((chain, d_model, blk), BF16),
        ],
        compiler_params=pltpu.CompilerParams(
            dimension_semantics=("arbitrary",), vmem_limit_bytes=VMEM_LIMIT),
        name="att_mixer",
    )(x2d, w_qkv, w_o, bias, ln_g, ln_b, wr_hi, wr_lo, br)


def _att_sample_kernel(*refs, nb, blk, alpha):
    (x_ref, kc_ref, vc_ref, wqkv_ref, wo_ref, bias_c_ref, bias_n_ref, lng_ref, lnb_ref,
     wrh_ref, wrl_ref, br_ref) = refs[:12]
    h_ref, hp_ref, lgt_ref, krow_ref, vrow_ref, o_scr = refs[-6:]
    d_model = x_ref.shape[1]
    heads = ATT_HEADS
    dh = d_model // heads
    x = x_ref[...]
    xb = x.astype(BF16)
    q = _dot(xb, wqkv_ref[:, 0:d_model]) * (dh ** -0.5)
    k = _dot(xb, wqkv_ref[:, d_model:2 * d_model])
    v = _dot(xb, wqkv_ref[:, 2 * d_model:3 * d_model])
    krow_ref[...] = k.reshape(nb, blk, d_model)
    vrow_ref[...] = v.reshape(nb, blk, d_model)
    lane_head = lax.broadcasted_iota(jnp.int32, (heads, 1, d_model), 2) // dh
    head_id = lax.broadcasted_iota(jnp.int32, (heads, 1, d_model), 0)
    head_mask = (lane_head == head_id).astype(F32)

    for s in range(nb):
        r0 = s * blk
        qs = q[r0:r0 + blk]
        q_bd = (qs[None, :, :] * head_mask).reshape(heads * blk, d_model).astype(BF16)
        kn = k[r0:r0 + blk].astype(BF16)
        vn = v[r0:r0 + blk].astype(BF16)
        s_c = _dot_nt(q_bd, kc_ref[s].astype(BF16)) + bias_c_ref[...]
        s_n = _dot_nt(q_bd, kn) + bias_n_ref[...]
        m = jnp.maximum(s_c.max(axis=-1, keepdims=True), s_n.max(axis=-1, keepdims=True))
        p_c = jnp.exp(s_c - m)
        p_n = jnp.exp(s_n - m)
        l = p_c.sum(axis=-1, keepdims=True) + p_n.sum(axis=-1, keepdims=True)
        o_full = _dot(p_c.astype(BF16), vc_ref[s].astype(BF16)) + _dot(p_n.astype(BF16), vn)
        o_full = o_full * (1.0 / l)
        o = (o_full.reshape(heads, blk, d_model) * head_mask).sum(axis=0)
        o_scr[r0:r0 + blk, :] = o.astype(BF16)

    mix = _dot(o_scr[...], wo_ref[...])
    _finish_tokens(x, mix, alpha, lng_ref, lnb_ref, wrh_ref, wrl_ref, br_ref, h_ref, hp_ref, lgt_ref)


def _att_sample(x2d, in_row0, n_seq, seq_len, k_cache, v_cache, w_qkv, w_o, rel_table, ln_g, ln_b,
                wr_hi, wr_lo, br, alpha, nb, n_all, out_row0, dst):
    d_model = x2d.shape[1]
    heads = ATT_HEADS
    blk = seq_len
    n_cache = k_cache.shape[1]
    rows = nb * blk
    assert n_seq % nb == 0 and in_row0 % rows == 0 and out_row0 % rows == 0
    in_b0, out_b0 = in_row0 // rows, out_row0 // rows
    bias_c = _rel_bias(rel_table, blk, n_cache, n_cache, 1).reshape(heads * blk, n_cache)
    bias_n = _rel_bias(rel_table, blk, blk, 0, 1).reshape(heads * blk, blk)
    kc = k_cache.reshape(n_seq, n_cache, d_model)
    vc = v_cache.reshape(n_seq, n_cache, d_model)
    cache_spec = pl.BlockSpec((nb, n_cache, d_model), lambda g: (g, 0, 0))
    row_spec = pl.BlockSpec((nb, blk, d_model), lambda g: (g, 0, 0))
    rows_out = jax.ShapeDtypeStruct((n_seq, blk, d_model), F32)
    in_specs = [
        pl.BlockSpec((rows, d_model), lambda g: (in_b0 + g, 0)),
        cache_spec,
        cache_spec,
        _const_spec(w_qkv.shape),
        _const_spec(w_o.shape),
        _const_spec(bias_c.shape),
        _const_spec(bias_n.shape),
        _const_spec(ln_g.shape),
        _const_spec(ln_b.shape),
        _const_spec(wr_hi.shape),
        _const_spec(wr_lo.shape),
        _const_spec(br.shape),
    ]
    args = [x2d, kc, vc, w_qkv, w_o, bias_c, bias_n, ln_g, ln_b, wr_hi, wr_lo, br]
    aliases = _alias_dst(args, in_specs, dst)
    return pl.pallas_call(
        functools.partial(_att_sample_kernel, nb=nb, blk=blk, alpha=alpha),
        grid=(n_seq // nb,),
        in_specs=in_specs,
        out_specs=_token_out_specs(rows, d_model, lambda g: (out_b0 + g, 0)) + (row_spec, row_spec),
        out_shape=_token_out_shapes(n_all, d_model) + (rows_out, rows_out),
        input_output_aliases=aliases,
        scratch_shapes=[pltpu.VMEM((rows, d_model), BF16)],
        compiler_params=pltpu.CompilerParams(
            dimension_semantics=("arbitrary",), vmem_limit_bytes=VMEM_LIMIT),
        name="att_mixer_cache",
    )(*args)


ROUTE_ROWS = 40


def _route_kernel(lgt_ref, meta_t_ref, meta_w_ref, count_ref, carry_ref, tri_ref):
    j = pl.program_id(0)
    tm = lgt_ref.shape[0]

    @pl.when(j == 0)
    def _():
        carry_ref[...] = jnp.zeros(carry_ref.shape, F32)
        r = lax.broadcasted_iota(jnp.int32, (tm, tm), 0)
        c = lax.broadcasted_iota(jnp.int32, (tm, tm), 1)
        tri_ref[...] = (r < c).astype(BF16)

    a = jnp.transpose(lgt_ref[...])[0:ROUTE_ROWS, :]
    row = lax.broadcasted_iota(jnp.int32, (ROUTE_ROWS, tm), 0)
    big = jnp.int32(LANES)
    glog = jnp.where(row < N_GROUPS, a, NEG_INF)
    gmax = glog.max(axis=0, keepdims=True)
    gsel = jnp.where(glog == gmax, row, big).min(axis=0, keepdims=True)
    gp = 1.0 / jnp.exp(glog - gmax).sum(axis=0, keepdims=True)
    first = N_GROUPS + gsel * EXP_PER_GROUP
    in_grp = (row >= first) & (row < first + EXP_PER_GROUP)
    elog = jnp.where(in_grp, a, NEG_INF)
    v1 = elog.max(axis=0, keepdims=True)
    l1 = jnp.where(elog == v1, row, big).min(axis=0, keepdims=True)
    elog2 = jnp.where(row == l1, NEG_INF, elog)
    v2 = elog2.max(axis=0, keepdims=True)
    l2 = jnp.where(elog2 == v2, row, big).min(axis=0, keepdims=True)
    e2 = jnp.exp(v2 - v1)
    w1 = gp / (1.0 + e2)
    w2 = gp * e2 / (1.0 + e2)

    is1 = row == l1
    is2 = row == l2
    oh = (is1 | is2).astype(F32)
    before = _dot(oh.astype(BF16), tri_ref[...]) + carry_ref[:, 0:1]
    rank1 = jnp.where(is1, before, 0.0).sum(axis=0, keepdims=True)
    rank2 = jnp.where(is2, before, 0.0).sum(axis=0, keepdims=True)
    carry_ref[...] = carry_ref[...] + oh.sum(axis=1, keepdims=True)
    count_ref[...] = carry_ref[...].astype(jnp.int32)

    pad = jnp.zeros((SUBLANES - 4, tm), jnp.int32)
    meta_t_ref[...] = jnp.concatenate(
        [l1 - N_GROUPS, l2 - N_GROUPS, rank1.astype(jnp.int32), rank2.astype(jnp.int32), pad], axis=0)
    wt = jnp.concatenate([w1, w2, jnp.zeros((LANES - 2, tm), F32)], axis=0)
    meta_w_ref[...] = jnp.transpose(wt)


def _route(logits):
    n_tok = logits.shape[0]
    tm = TOKEN_TILE
    assert n_tok % tm == 0
    tile = pl.BlockSpec((tm, LANES), lambda j: (j, 0))
    return pl.pallas_call(
        _route_kernel,
        grid=(n_tok // tm,),
        in_specs=[tile],
        out_specs=(pl.BlockSpec((SUBLANES, tm), lambda j: (0, j)), tile,
                   pl.BlockSpec((ROUTE_ROWS, LANES), lambda j: (0, 0))),
        out_shape=(
            jax.ShapeDtypeStruct((SUBLANES, n_tok), jnp.int32),
            jax.ShapeDtypeStruct((n_tok, LANES), F32),
            jax.ShapeDtypeStruct((ROUTE_ROWS, LANES), jnp.int32),
        ),
        scratch_shapes=[pltpu.VMEM((ROUTE_ROWS, LANES), F32), pltpu.VMEM((tm, tm), BF16)],
        compiler_params=pltpu.CompilerParams(dimension_semantics=("arbitrary",)),
        name="moe_route",
    )(logits)


def _sc_gather_loop(table_hbm, idx_v, out_hbm, rows_v, sems, base, n_chunk, r):
    def gather(c, slot):
        off = pl.multiple_of(c * r, r)
        return pltpu.make_async_copy(table_hbm.at[idx_v.at[pl.ds(off, r)]], rows_v.at[slot], sems.at[slot])

    def finish(c, slot):
        gather(c, slot).wait()
        pltpu.sync_copy(rows_v.at[slot], out_hbm.at[pl.ds(base + pl.multiple_of(c * r, r), r)])

    gather(0, 0).start()
    if n_chunk > 1:
        gather(1, 1).start()

    def body(pair, carry):
        c = 2 * pair
        for slot in range(2):
            finish(c + slot, slot)

            @pl.when(c + slot + 2 < n_chunk)
            def _():
                gather(c + slot + 2, slot).start()
        return carry

    lax.fori_loop(0, n_chunk // 2, body, 0)
    if n_chunk % 2:
        finish(n_chunk - 1, 0)


def _gather_rows(table, idx):
    m = idx.shape[0]
    width = table.shape[1]
    r = SC_ROWS_PER_CHUNK
    assert m % (SC_WORKERS * r) == 0
    per_w = m // SC_WORKERS
    mesh = plsc.VectorSubcoreMesh(core_axis_name="c", subcore_axis_name="s")

    @functools.partial(
        pl.kernel,
        mesh=mesh,
        out_type=jax.ShapeDtypeStruct((m, width), table.dtype),
        scratch_types=[
            pltpu.VMEM((per_w,), jnp.int32),
            pltpu.VMEM((2, r, width), table.dtype),
            pltpu.SemaphoreType.DMA((2,)),
        ],
    )
    def gather(table_hbm, idx_hbm, out_hbm, idx_v, rows_v, sem):
        wid = lax.axis_index("s") * 2 + lax.axis_index("c")
        base = wid * per_w
        pltpu.sync_copy(idx_hbm.at[pl.ds(base, per_w)], idx_v)
        _sc_gather_loop(table_hbm, idx_v, out_hbm, rows_v, sem, base, per_w // r, r)

    return gather(table, idx)


def _dispatch_rows(table, pos, row0, n_rows):
    n_tok, width = table.shape
    n_pairs = pos.shape[0]
    r = SC_ROWS_PER_CHUNK
    lanes = SC_LANES
    assert n_rows % (SC_WORKERS * r) == 0
    per_w = n_rows // SC_WORKERS
    n_stage = 16
    stage = n_pairs // n_stage
    assert stage * n_stage == n_pairs and stage % lanes == 0 and per_w % lanes == 0
    assert row0 + n_rows < 3 * n_tok
    mesh = plsc.VectorSubcoreMesh(core_axis_name="c", subcore_axis_name="s")

    @functools.partial(
        pl.kernel,
        mesh=mesh,
        out_type=jax.ShapeDtypeStruct((n_rows, width), table.dtype),
        scratch_types=[
            pltpu.VMEM((per_w,), jnp.int32),
            pltpu.VMEM((stage,), jnp.int32),
            pltpu.VMEM((2, r, width), table.dtype),
            pltpu.SemaphoreType.DMA((2,)),
        ],
        compiler_params=pltpu.CompilerParams(needs_layout_passes=False),
    )
    def dispatch(table_hbm, pos_hbm, out_hbm, src_v, pos_v, rows_v, sem):
        wid = lax.axis_index("s") * 2 + lax.axis_index("c")
        out_base = wid * per_w
        base = row0 + out_base
        lane = lax.iota(jnp.int32, lanes)

        def wrap(t):
            t = jnp.where(t >= n_tok, t - n_tok, t)
            return jnp.where(t >= n_tok, t - n_tok, t)

        def init(i, carry):
            off = pl.multiple_of(i * lanes, lanes)
            src_v[pl.ds(off, lanes)] = wrap(base + off + lane)
            return carry

        lax.fori_loop(0, per_w // lanes, init, 0)

        def scan_stage(sidx, carry):
            pair0 = sidx * stage
            pltpu.sync_copy(pos_hbm.at[pl.ds(pl.multiple_of(pair0, 8), stage)], pos_v)

            @plsc.parallel_loop(0, stage // lanes, unroll=4)
            def _(i):
                off = pl.multiple_of(i * lanes, lanes)
                local = pos_v[pl.ds(off, lanes)] - base
                mine = (local >= 0) & (local < per_w)
                plsc.store_scatter(src_v, [jnp.where(mine, local, 0)], wrap(pair0 + off + lane), mask=mine)

            return carry

        lax.fori_loop(0, n_stage, scan_stage, 0)
        _sc_gather_loop(table_hbm, src_v, out_hbm, rows_v, sem, out_base, per_w // r, r)

    return dispatch(table, pos)


def _gemm_kernel(tile_expert_ref, n_used_ref, xs_ref, wg_ref, wu_ref, wd_ref, *rest, tile0):
    ys_ref, wg_b, wu_b, wd_b = rest[-4:]
    j = pl.program_id(0)
    tile = tile0 + j

    @pl.when(tile < n_used_ref[0])
    def _():
        new_expert = (j == 0) | (tile_expert_ref[tile] != tile_expert_ref[jnp.maximum(tile - 1, 0)])

        @pl.when(new_expert)
        def _():
            wg_b[...] = wg_ref[...].astype(BF16)
            wu_b[...] = wu_ref[...].astype(BF16)
            wd_b[...] = wd_ref[...].astype(BF16)

        x = _unpack_rows(xs_ref[...]).astype(BF16)
        g = _dot(x, wg_b[...])
        u = _dot(x, wu_b[...])
        hmid = (jax.nn.silu(g) * u).astype(BF16)
        ys_ref[...] = _pack_rows(_dot(hmid, wd_b[...]))


def _expert_gemm(xs, tile_expert, n_used, layer, w_gate, w_up, w_down, tile0, n_rows_all, ys):
    n_rows, half = xs.shape
    d_model, d_exp = w_gate.shape[2], w_gate.shape[3]
    tm = GEMM_TILE
    n_tiles = n_rows // tm

    def last_used(j, nu):
        return jnp.minimum(tile0 + j, nu[0] - 1)

    def in_map(j, te, nu):
        return (jnp.maximum(last_used(j, nu) - tile0, 0), 0)

    def out_map(j, te, nu):
        return (jnp.maximum(last_used(j, nu), tile0), 0)

    def w_map(j, te, nu):
        return (layer, te[tile0 + j], 0, 0)

    in_specs = [
        pl.BlockSpec((tm, half), in_map),
        pl.BlockSpec((None, None, d_model, d_exp), w_map),
        pl.BlockSpec((None, None, d_model, d_exp), w_map),
        pl.BlockSpec((None, None, d_exp, d_model), w_map),
    ]
    args = [tile_expert, n_used, xs, w_gate, w_up, w_down]
    aliases = {}
    if ys is not None:
        aliases[len(args)] = 0
        args.append(ys)
        in_specs.append(pl.BlockSpec(memory_space=pl.ANY))
    grid_spec = pltpu.PrefetchScalarGridSpec(
        num_scalar_prefetch=2,
        grid=(n_tiles,),
        in_specs=in_specs,
        out_specs=pl.BlockSpec((tm, half), out_map),
        scratch_shapes=[
            pltpu.VMEM((d_model, d_exp), BF16),
            pltpu.VMEM((d_model, d_exp), BF16),
            pltpu.VMEM((d_exp, d_model), BF16),
        ],
    )
    return pl.pallas_call(
        functools.partial(_gemm_kernel, tile0=tile0),
        grid_spec=grid_spec,
        out_shape=jax.ShapeDtypeStruct((n_rows_all, half), U32),
        input_output_aliases=aliases,
        compiler_params=pltpu.CompilerParams(
            dimension_semantics=("arbitrary",), vmem_limit_bytes=VMEM_LIMIT),
        name="moe_gemm",
    )(*args)


def _combine_kernel(y0_ref, y1_ref, mw_ref, h_ref, pp_ref, ps_ref, lng_ref, lnb_ref, wproj_ref, wgate_ref,
                    *out_refs, n_prompt_tiles, alpha):
    j = pl.program_id(0)
    tm = h_ref.shape[0]
    half = tm // 2

    def normed(rows):
        mw = mw_ref[rows, :]
        ffn = mw[:, 0:1] * _unpack_rows(y0_ref[rows, :]) + mw[:, 1:2] * _unpack_rows(y1_ref[rows, :])
        return _layer_norm(alpha * h_ref[rows, :] + ffn, lng_ref[...], lnb_ref[...])

    def gated(rows, h2):
        p = jnp.where(j < n_prompt_tiles, pp_ref[rows, :], ps_ref[rows, :]).astype(BF16)
        gate = jax.nn.sigmoid(_dot(h2.astype(BF16), wgate_ref[...]))
        return h2 + gate * _dot(p, wproj_ref[...])

    rows_a, rows_b = slice(0, half), slice(half, tm)
    h2_a = normed(rows_a)
    h2_b = normed(rows_b)
    out = jnp.concatenate([gated(rows_a, h2_a), gated(rows_b, h2_b)], axis=0)
    if len(out_refs) == 1:
        out_refs[0][...] = out
    else:
        @pl.when(j < n_prompt_tiles)
        def _():
            out_refs[0][...] = out

        @pl.when(j >= n_prompt_tiles)
        def _():
            out_refs[1][...] = out


def _combine(yg, meta_w, h, p_prompt, p_sample, layer, ln_g, ln_b, w_proj, w_gate, alpha, split_out):
    n_tok, d_model = h.shape
    tm = TOKEN_TILE
    n_t = n_tok // tm
    n_tp = p_prompt.shape[1] // tm
    n_ts = p_sample.shape[1] // tm
    assert n_tp * tm == p_prompt.shape[1] and n_ts * tm == p_sample.shape[1] and n_tp + n_ts == n_t
    ple = p_prompt.shape[2]
    tile = pl.BlockSpec((tm, d_model), lambda j: (j, 0))
    if split_out:
        out_specs = (pl.BlockSpec((tm, d_model), lambda j: (jnp.minimum(j, n_tp - 1), 0)),
                     pl.BlockSpec((tm, d_model), lambda j: (jnp.maximum(j - n_tp, 0), 0)))
        out_shape = (jax.ShapeDtypeStruct((n_tp * tm, d_model), F32),
                     jax.ShapeDtypeStruct((n_ts * tm, d_model), F32))
    else:
        out_specs = tile
        out_shape = jax.ShapeDtypeStruct((n_tok, d_model), F32)
    return pl.pallas_call(
        functools.partial(_combine_kernel, n_prompt_tiles=n_tp, alpha=alpha),
        grid=(n_t,),
        in_specs=[
            pl.BlockSpec((tm, d_model // 2), lambda j: (j, 0)),
            pl.BlockSpec((tm, d_model // 2), lambda j: (j + n_t, 0)),
            pl.BlockSpec((tm, LANES), lambda j: (j, 0)),
            tile,
            pl.BlockSpec((None, tm, ple), lambda j: (layer, jnp.minimum(j, n_tp - 1), 0)),
            pl.BlockSpec((None, tm, ple), lambda j: (layer, jnp.maximum(j - n_tp, 0), 0)),
            _const_spec(ln_g.shape),
            _const_spec(ln_b.shape),
            _const_spec(w_proj.shape),
            _const_spec(w_gate.shape),
        ],
        out_specs=out_specs,
        out_shape=out_shape,
        compiler_params=pltpu.CompilerParams(
            dimension_semantics=("arbitrary",), vmem_limit_bytes=VMEM_LIMIT),
        name="moe_combine",
    )(yg, yg, meta_w, h, p_prompt, p_sample, ln_g, ln_b, w_proj, w_gate)


def _router_weights(w_grp, b_grp, w_exp, b_exp):
    d_model = w_grp.shape[0]
    w = jnp.concatenate([w_grp, jnp.transpose(w_exp, (1, 0, 2)).reshape(d_model, N_EXPERTS)], axis=1)
    w = jnp.pad(w, ((0, 0), (0, LANES - w.shape[1])))
    b = jnp.concatenate([b_grp, b_exp.reshape(N_EXPERTS)])
    b = jnp.pad(b, (0, LANES - b.shape[0])).reshape(1, LANES)
    w_hi = w.astype(BF16)
    w_lo = (w - w_hi.astype(F32)).astype(BF16)
    return w_hi, w_lo, b


def _moe(h_packed, logits, layer, w_gate, w_up, w_down):
    n_tok = h_packed.shape[0]
    tm = GEMM_TILE
    meta_t, meta_w, counts = _route(logits)
    counts = counts[N_GROUPS:N_GROUPS + N_EXPERTS, 0]
    tiles_per_expert = (counts + tm - 1) // tm
    tile_end = jnp.cumsum(tiles_per_expert)
    row_start = (tile_end - tiles_per_expert) * tm
    eid = meta_t[0:2]
    rank = meta_t[2:4]
    experts = jnp.arange(N_EXPERTS, dtype=jnp.int32)
    start = jnp.sum(jnp.where(eid[:, :, None] == experts, row_start, 0), axis=-1)
    pos = (start + rank).reshape(-1).astype(jnp.int32)
    part_quant = MOE_PARTS * SC_WORKERS * SC_ROWS_PER_CHUNK * (tm // math.gcd(tm, SC_WORKERS * SC_ROWS_PER_CHUNK))
    n_tiles = -(-(2 * n_tok) // tm) + N_EXPERTS
    n_rows = -(-(n_tiles * tm) // part_quant) * part_quant
    n_tiles = n_rows // tm
    n_used = tile_end[-1:].astype(jnp.int32)
    tile_ids = jnp.minimum(jnp.arange(n_tiles, dtype=jnp.int32), n_used[0] - 1)
    tile_expert = jnp.sum(tile_end[None, :] <= tile_ids[:, None], axis=1).astype(jnp.int32)
    part_rows = n_rows // MOE_PARTS
    xs_parts = [_dispatch_rows(h_packed, pos, part * part_rows, part_rows) for part in range(MOE_PARTS)]
    ys = None
    for part, xs in enumerate(xs_parts):
        ys = _expert_gemm(xs, tile_expert, n_used, layer, w_gate, w_up, w_down,
                          part * (part_rows // tm), n_rows, ys)
    return _gather_rows(ys, pos), meta_w


def kernel(x_prompt, x_sample, p_prompt, p_sample, state_ret, cache_att_k, cache_att_v, ret_w_in, ret_gn_g,
           ret_w_o, att_w_qkv, att_rel_bias, att_w_o, ln1_g, ln1_b, ln2_g, ln2_b, moe_w_grp, moe_b_grp,
           moe_w_exp, moe_b_exp, moe_w_gate, moe_w_up, moe_w_down, ple_w_proj, ple_w_gate):
    n_p, len_p, d_model = x_prompt.shape
    n_s, len_s, _ = x_sample.shape
    depth = ln1_g.shape[0]
    alpha = float((2 * depth) ** 0.25)
    tok_p, tok_s = n_p * len_p, n_s * len_s
    n_all = tok_p + tok_s
    dh = d_model // ATT_HEADS
    pp = p_prompt.reshape(depth, tok_p, -1)
    ps = p_sample.reshape(depth, tok_s, -1)
    nb_s = 4

    x_all = None
    y_prompt = y_sample = None
    states_p, states_s, k_p, v_p, k_s, v_s = [], [], [], [], [], []
    for i in range(depth):
        jj = i // 2
        wr_hi, wr_lo, br = _router_weights(moe_w_grp[i], moe_b_grp[i], moe_w_exp[i], moe_b_exp[i])
        lng, lnb = ln1_g[i].reshape(1, d_model), ln1_b[i].reshape(1, d_model)
        if x_all is None:
            src_p, src_s, row_s = x_prompt.reshape(tok_p, d_model), x_sample.reshape(tok_s, d_model), 0
        else:
            src_p, src_s, row_s = x_all, x_all, tok_p
        if i % 2 == 0:
            w_in = ret_w_in[jj].astype(BF16)
            w_o = ret_w_o[jj].astype(BF16)
            gn = ret_gn_g[jj].reshape(1, -1)
            h, hp, lgt, st_p = _ret_mixer(src_p, 0, n_p, len_p, 0, None, w_in, w_o, gn, lng, lnb,
                                          wr_hi, wr_lo, br, alpha, nb=1, blk=min(RET_BLOCK, len_p), chain=RET_CHAIN,
                                          n_all=n_all, out_row0=0, dst=None)
            h, hp, lgt, st_s = _ret_mixer(src_s, row_s, n_s, len_s, PAST_LEN, state_ret[jj], w_in, w_o, gn,
                                          lng, lnb, wr_hi, wr_lo, br, alpha, nb=nb_s, blk=len_s, chain=1,
                                          n_all=n_all, out_row0=tok_p, dst=(h, hp, lgt))
            states_p.append(st_p)
            states_s.append(st_s)
        else:
            w_qkv = att_w_qkv[jj].astype(BF16)
            w_o = att_w_o[jj].astype(BF16)
            h, hp, lgt, kr, vr = _att_prompt(src_p, n_p, len_p, w_qkv, w_o, att_rel_bias[jj], lng, lnb,
                                             wr_hi, wr_lo, br, alpha, n_all)
            k_p.append(kr.reshape(n_p, -1, ATT_HEADS, dh))
            v_p.append(vr.reshape(n_p, -1, ATT_HEADS, dh))
            h, hp, lgt, kr, vr = _att_sample(src_s, row_s, n_s, len_s, cache_att_k[jj], cache_att_v[jj],
                                             w_qkv, w_o, att_rel_bias[jj], lng, lnb, wr_hi, wr_lo, br, alpha,
                                             nb=nb_s, n_all=n_all, out_row0=tok_p, dst=(h, hp, lgt))
            k_s.append(kr.reshape(n_s, len_s, ATT_HEADS, dh))
            v_s.append(vr.reshape(n_s, len_s, ATT_HEADS, dh))
        yg, meta_w = _moe(hp, lgt, i, moe_w_gate, moe_w_up, moe_w_down)
        last = i == depth - 1
        out = _combine(yg, meta_w, h, pp, ps, i, ln2_g[i].reshape(1, d_model), ln2_b[i].reshape(1, d_model),
                       ple_w_proj[i].astype(BF16), ple_w_gate[i].astype(BF16), alpha, split_out=last)
        if last:
            y_prompt = out[0].reshape(n_p, len_p, d_model)
            y_sample = out[1].reshape(n_s, len_s, d_model)
        else:
            x_all = out

    return (y_prompt, y_sample, jnp.stack(states_p), jnp.stack(states_s),
            jnp.stack(k_p), jnp.stack(v_p), jnp.stack(k_s), jnp.stack(v_s))
```

```python
import functools
import math

import numpy as np
import jax
import jax.numpy as jnp
from jax import lax
from jax.experimental import pallas as pl
from jax.experimental.pallas import tpu as pltpu
from jax.experimental.pallas import tpu_sc as plsc

CHUNK = 64
PAST_LEN = 2048
RET_HEADS = 4
ROPE_BASE = 10000.0
ATT_HEADS = 16
BAND_CHUNKS = 8
REL_CLIP = 256
N_GROUPS = 4
EXP_PER_GROUP = 8
N_EXPERTS = N_GROUPS * EXP_PER_GROUP
LN_EPS = 1e-5
NEG_INF = -1e30

LANES = 128
SUBLANES = 8
BF16_SUBLANES = 16
SC_WORKERS = 32
SC_LANES = 16
SC_ROWS_PER_CHUNK = 32
VMEM_LIMIT = 58 * 1024 * 1024

RET_BLOCK = 256
RET_CHAIN = 2
ATT_BLOCK = 4 * CHUNK
ATT_CHAIN = 2
TOKEN_TILE = 512
GEMM_TILE = 512
MOE_PARTS = 3

F32 = jnp.float32
BF16 = jnp.bfloat16
U32 = jnp.uint32
HI_MASK = 0xFFFF0000
LOG2E = math.log2(math.e)


def _dot(a, b):
    return jnp.dot(a, b, preferred_element_type=F32)


def _dot_nt(a, b):
    return lax.dot_general(a, b, (((1,), (1,)), ((), ())), preferred_element_type=F32)


def _dot_tn(a, b):
    return lax.dot_general(a, b, (((0,), (0,)), ((), ())), preferred_element_type=F32)


def _layer_norm(x, g, b):
    mu = jnp.mean(x, axis=-1, keepdims=True)
    xc = x - mu
    var = jnp.mean(xc * xc, axis=-1, keepdims=True)
    return xc * lax.rsqrt(var + LN_EPS) * g + b


def _pack_rows(x):
    half = x.shape[1] // 2
    lo = lax.bitcast_convert_type(x[:, :half].astype(BF16).astype(F32), U32) >> 16
    hi = lax.bitcast_convert_type(x[:, half:].astype(BF16).astype(F32), U32) & U32(HI_MASK)
    return lo | hi


def _unpack_rows(p):
    lo = lax.bitcast_convert_type(p << 16, F32)
    hi = lax.bitcast_convert_type(p & U32(HI_MASK), F32)
    return jnp.concatenate([lo, hi], axis=1)


def _router_logits(h, wr_hi_ref, wr_lo_ref, br_ref):
    h_hi = h.astype(BF16)
    h_lo = (h - h_hi.astype(F32)).astype(BF16)
    w_hi = wr_hi_ref[...]
    return _dot(h_hi, w_hi) + _dot(h_lo, w_hi) + _dot(h_hi, wr_lo_ref[...]) + br_ref[...]


def _finish_tokens(x, mix, alpha, lng_ref, lnb_ref, wrh_ref, wrl_ref, br_ref, h_ref, hp_ref, lgt_ref):
    hh = _layer_norm(alpha * x + mix, lng_ref[...], lnb_ref[...])
    h_ref[...] = hh
    hp_ref[...] = _pack_rows(hh)
    lgt_ref[...] = _router_logits(hh, wrh_ref, wrl_ref, br_ref)


def _zero_tokens(h_ref, hp_ref, lgt_ref):
    h_ref[...] = jnp.zeros(h_ref.shape, h_ref.dtype)
    hp_ref[...] = jnp.zeros(hp_ref.shape, hp_ref.dtype)
    lgt_ref[...] = jnp.zeros(lgt_ref.shape, lgt_ref.dtype)


def _const_spec(shape):
    nd = len(shape)
    return pl.BlockSpec(shape, lambda *_: (0,) * nd, pipeline_mode=pl.Buffered(1))


def _token_out_shapes(n_all, d_model):
    return (
        jax.ShapeDtypeStruct((n_all, d_model), F32),
        jax.ShapeDtypeStruct((n_all, d_model // 2), U32),
        jax.ShapeDtypeStruct((n_all, LANES), F32),
    )


def _token_out_specs(rows, d_model, row_map):
    return (
        pl.BlockSpec((rows, d_model), row_map),
        pl.BlockSpec((rows, d_model // 2), row_map),
        pl.BlockSpec((rows, LANES), row_map),
    )


def _alias_dst(args, in_specs, dst):
    if dst is None:
        return {}
    aliases = {}
    for k, arr in enumerate(dst):
        aliases[len(args)] = k
        args.append(arr)
        in_specs.append(pl.BlockSpec(memory_space=pl.ANY))
    return aliases


def _cast_plumbing(args, in_specs, cast, n_steps, work):
    if cast is None:
        return (), ()
    layer, weights = cast
    shapes, specs = [], []
    for w in weights:
        depth, n_exp, rows, cols = w.shape
        assert (n_exp * rows) % n_steps == 0
        slab = (n_exp * rows) // n_steps
        args.append(w.reshape(depth, n_exp * rows, cols))
        in_specs.append(pl.BlockSpec((None, slab, cols), lambda t: (layer, work(t), 0)))
        shapes.append(jax.ShapeDtypeStruct((n_exp * rows, cols), BF16))
        specs.append(pl.BlockSpec((slab, cols), lambda t: (work(t), 0)))
    return tuple(shapes), tuple(specs)


def _cast_slabs(in_refs, out_refs):
    for src, dst in zip(in_refs, out_refs):
        dst[...] = src[...].astype(BF16)


def _ret_log_gamma():
    h = np.arange(RET_HEADS, dtype=np.float32)
    return np.log(np.float32(1.0) - np.float32(2.0) ** (np.float32(-5.0) - h)).astype(np.float32)


def _ret_kernel(*refs, nb, blk, chain, nblk, n_steps, n_fill, has_state, n_alias, n_cast, alpha):
    step = pl.program_id(0)
    n_in = 12 + int(has_state) + n_alias + n_cast
    h_ref, hp_ref, lgt_ref = refs[n_in:n_in + 3]
    core = refs[:12 + int(has_state)] + refs[n_in:n_in + 4] + refs[-1:]

    @pl.when(step < n_steps)
    def _():
        _cast_slabs(refs[n_in - n_cast:n_in], refs[n_in + 4:n_in + 4 + n_cast])
        _ret_step(*core, first_block=step % nblk == 0, nb=nb, blk=blk, chain=chain, has_state=has_state,
                  alpha=alpha)

    if n_fill:
        @pl.when(step >= n_steps)
        def _():
            _zero_tokens(h_ref, hp_ref, lgt_ref)


def _ret_step(*refs, first_block, nb, blk, chain, has_state, alpha):
    (x_ref, cos_ref, sin_ref, dmask_ref, win_ref, wo_ref, gn_ref, lng_ref, lnb_ref,
     wrh_ref, wrl_ref, br_ref) = refs[:12]
    s_in_ref = refs[12] if has_state else None
    h_ref, hp_ref, lgt_ref, s_out_ref, gated_ref = refs[12 + int(has_state):]
    heads = RET_HEADS
    d_model = x_ref.shape[1]
    dk = d_model // heads
    dv = 2 * d_model // heads
    hk, hv = heads * dk, heads * dv
    half = dk // 2
    lg = _ret_log_gamma()

    n_sub = nb * chain
    proj_rows = blk if chain > 1 else n_sub * blk
    rowf = lax.broadcasted_iota(jnp.int32, (blk, 1), 0).astype(F32)

    def proj(group):
        xb = x_ref[group * proj_rows:(group + 1) * proj_rows, :].astype(BF16)
        return (_dot(xb, win_ref[:, 0:hk]), _dot(xb, win_ref[:, hk:2 * hk]),
                _dot(xb, win_ref[:, 2 * hk:2 * hk + hv]), _dot(xb, win_ref[:, 2 * hk + hv:2 * hk + 2 * hv]))

    if not has_state:
        @pl.when(first_block)
        def _():
            s_out_ref[...] = jnp.zeros(s_out_ref.shape, F32)

    s_prev_ref = s_in_ref if has_state else s_out_ref

    def head(j, h, projected):
        q_all, k_all, v_all, g_all = projected
        s = j // chain
        c0 = (j % chain) * blk
        r0 = (j * blk) % proj_rows
        cos = cos_ref[c0:c0 + blk, :]
        sin = sin_ref[c0:c0 + blk, :]

        def rot(t):
            t1, t2 = t[:, :half], t[:, half:]
            return jnp.concatenate([t1 * cos - t2 * sin, t1 * sin + t2 * cos], axis=1)

        lgh = float(lg[h])
        q = rot(q_all[r0:r0 + blk, h * dk:(h + 1) * dk])
        k = rot(k_all[r0:r0 + blk, h * dk:(h + 1) * dk]) * (dk ** -0.5)
        v = v_all[r0:r0 + blk, h * dv:(h + 1) * dv]
        g = g_all[r0:r0 + blk, h * dv:(h + 1) * dv]
        vb = v.astype(BF16)
        scores = _dot_nt(q.astype(BF16), k.astype(BF16)) * dmask_ref[h]
        inner = _dot(scores.astype(BF16), vb)
        s_prev = s_prev_ref[s, h]
        q_dec = q * jnp.exp(lgh * (rowf + 1.0))
        cross = _dot(q_dec.astype(BF16), s_prev.astype(BF16))
        k_dec = k * jnp.exp(lgh * (float(blk - 1) - rowf))
        s_out_ref[s, h] = math.exp(lgh * blk) * s_prev + _dot_tn(k_dec.astype(BF16), vb)
        o = inner + cross
        mu = jnp.mean(o, axis=-1, keepdims=True)
        oc = o - mu
        var = jnp.mean(oc * oc, axis=-1, keepdims=True)
        on = oc * lax.rsqrt(var + LN_EPS) * gn_ref[:, h * dv:(h + 1) * dv]
        gated_ref[j * blk:(j + 1) * blk, h * dv:(h + 1) * dv] = (jax.nn.silu(g) * on).astype(BF16)

    def tail(group):
        rows = slice(group * proj_rows, (group + 1) * proj_rows)
        mix = _dot(gated_ref[rows, :], wo_ref[...])
        _finish_tokens(x_ref[rows, :], mix, alpha, lng_ref, lnb_ref, wrh_ref, wrl_ref, br_ref,
                       h_ref.at[rows, :], hp_ref.at[rows, :], lgt_ref.at[rows, :])

    n_groups = (n_sub * blk) // proj_rows
    subs_per_group = proj_rows // blk
    projected = proj(0)
    for group in range(n_groups):
        nxt = None
        for jj in range(subs_per_group):
            for h in range(heads):
                head(group * subs_per_group + jj, h, projected)
                if jj == 0 and h == 0 and group + 1 < n_groups:
                    nxt = proj(group + 1)
                if jj == 0 and h == 1 and group > 0:
                    tail(group - 1)
        projected = nxt
    tail(n_groups - 1)


def _ret_mixer(x2d, in_row0, n_seq, seq_len, pos0, state_in, w_in, w_o, gn_g, ln_g, ln_b,
               wr_hi, wr_lo, br, alpha, nb, blk, chain, n_all, out_row0, dst, cast=None):
    d_model = x2d.shape[1]
    heads = RET_HEADS
    dk, dv = d_model // heads, 2 * d_model // heads
    half = dk // 2
    nblk = seq_len // (blk * chain)
    has_state = state_in is not None
    rows = nb * blk * chain
    assert seq_len % (blk * chain) == 0 and n_seq % nb == 0
    assert (not has_state) or nblk == 1
    assert nb == 1 or (nblk == 1 and chain == 1)
    assert in_row0 % rows == 0 and out_row0 % rows == 0
    in_b0, out_b0 = in_row0 // rows, out_row0 // rows

    pos = (pos0 + jnp.arange(seq_len, dtype=jnp.int32)).astype(F32)
    inv_freq = ROPE_BASE ** (-jnp.arange(half, dtype=F32) / half)
    ang = pos[:, None] * inv_freq[None, :]
    cos, sin = jnp.cos(ang), jnp.sin(ang)
    lg = jnp.asarray(_ret_log_gamma())
    ii = jnp.arange(blk, dtype=F32)
    diff = ii[:, None] - ii[None, :]
    dmask = jnp.where(diff >= 0, jnp.exp(lg[:, None, None] * jnp.maximum(diff, 0.0)), 0.0)

    n_steps = (n_seq // nb) * nblk
    n_fill = 0 if dst is not None else (n_all - n_seq * seq_len) // rows
    assert dst is not None or (out_row0 == 0 and n_fill * rows == n_all - n_seq * seq_len)

    def work(t):
        return jnp.minimum(t, n_steps - 1)

    in_specs = [
        pl.BlockSpec((rows, d_model), lambda t: (in_b0 + work(t), 0)),
        pl.BlockSpec((blk * chain, half), lambda t: (work(t) % nblk, 0)),
        pl.BlockSpec((blk * chain, half), lambda t: (work(t) % nblk, 0)),
        _const_spec(dmask.shape),
        _const_spec(w_in.shape),
        _const_spec(w_o.shape),
        _const_spec(gn_g.shape),
        _const_spec(ln_g.shape),
        _const_spec(ln_b.shape),
        _const_spec(wr_hi.shape),
        _const_spec(wr_lo.shape),
        _const_spec(br.shape),
    ]
    args = [x2d, cos, sin, dmask, w_in, w_o, gn_g, ln_g, ln_b, wr_hi, wr_lo, br]
    state_spec = pl.BlockSpec((nb, heads, dk, dv), lambda t: (work(t) // nblk, 0, 0, 0))
    if has_state:
        in_specs.append(state_spec)
        args.append(state_in)
    aliases = _alias_dst(args, in_specs, dst)
    cast_shapes, cast_specs = _cast_plumbing(args, in_specs, cast, n_steps, work)
    out_shape = (_token_out_shapes(n_all, d_model) + (jax.ShapeDtypeStruct((n_seq, heads, dk, dv), F32),)
                 + cast_shapes)
    out_specs = _token_out_specs(rows, d_model, lambda t: (out_b0 + t, 0)) + (state_spec,) + cast_specs
    return pl.pallas_call(
        functools.partial(_ret_kernel, nb=nb, blk=blk, chain=chain, nblk=nblk, n_steps=n_steps, n_fill=n_fill,
                          has_state=has_state, n_alias=len(aliases), n_cast=len(cast_shapes), alpha=alpha),
        grid=(n_steps + n_fill,),
        in_specs=in_specs,
        out_specs=out_specs,
        out_shape=out_shape,
        input_output_aliases=aliases,
        scratch_shapes=[pltpu.VMEM((rows, heads * dv), BF16)],
        compiler_params=pltpu.CompilerParams(
            dimension_semantics=("arbitrary",), vmem_limit_bytes=VMEM_LIMIT),
        name="ret_mixer_state" if has_state else "ret_mixer",
    )(*args)


_RING = 3


def _att_prompt_kernel(*refs, blk, chain, nblk, n_steps, n_fill, n_cast, alpha):
    step = pl.program_id(0)
    n_in = 9 + n_cast
    h_ref, hp_ref, lgt_ref = refs[n_in:n_in + 3]
    core = refs[:9] + refs[n_in:n_in + 5] + refs[-3:]

    @pl.when(step < n_steps)
    def _():
        _cast_slabs(refs[9:n_in], refs[n_in + 5:n_in + 5 + n_cast])
        _att_prompt_step(*core, i0=(step % nblk) * chain, blk=blk, chain=chain, alpha=alpha)

    if n_fill:
        @pl.when(step >= n_steps)
        def _():
            _zero_tokens(h_ref, hp_ref, lgt_ref)


def _att_prompt_step(x_ref, wqkv_ref, wo_ref, bias_ref, lng_ref, lnb_ref, wrh_ref, wrl_ref, br_ref,
                     h_ref, hp_ref, lgt_ref, krow_ref, vrow_ref, kring, vring, o_scr, *, i0, blk, chain, alpha):
    d_model = x_ref.shape[1]
    dh = d_model // ATT_HEADS
    heads_per_group = LANES // dh
    lane = lax.broadcasted_iota(jnp.int32, (1, LANES), 1)

    @pl.when(i0 == 0)
    def _():
        kring[...] = jnp.zeros(kring.shape, BF16)
        vring[...] = jnp.zeros(vring.shape, BF16)
        vring[:, :, dh:, :] = jnp.ones((_RING, ATT_HEADS, vring.shape[2] - dh, blk), BF16)

    def project(j):
        xb = x_ref[j * blk:(j + 1) * blk, :].astype(BF16)
        q = _dot(xb, wqkv_ref[:, 0:d_model]) * (dh ** -0.5 * LOG2E)
        k = _dot(xb, wqkv_ref[:, d_model:2 * d_model])
        v = _dot(xb, wqkv_ref[:, 2 * d_model:3 * d_model])
        return q, k, v

    def to_ring(j, k, v):
        slot = (i0 + j) % _RING
        krow_ref[j * blk:(j + 1) * blk, :] = k
        vrow_ref[j * blk:(j + 1) * blk, :] = v
        kring[slot] = k.astype(BF16)
        v_t = jnp.transpose(v).astype(BF16)
        for hd in range(ATT_HEADS):
            vring[slot, hd, 0:dh, :] = v_t[hd * dh:(hd + 1) * dh, :]

    def attend(j, q, between):
        i = i0 + j
        qb = q.astype(BF16)
        behind = [(i + _RING - k) % _RING for k in range(_RING)]
        slabs = [jnp.where(i >= behind[k], behind[k], _RING) for k in range(_RING)]

        def scores(hd):
            c0 = (hd // heads_per_group) * LANES
            sub = hd % heads_per_group
            in_head = (lane >= sub * dh) & (lane < (sub + 1) * dh)
            q_pair = qb[:, c0:c0 + LANES]
            qm = jnp.where(in_head, q_pair, jnp.zeros_like(q_pair))
            s_all = _dot_nt(kring[:, :, c0:c0 + LANES].reshape(_RING * blk, LANES), qm)
            return [s_all[k * blk:(k + 1) * blk] + bias_ref[slabs[k], hd] for k in range(_RING)]

        def probs(s_list):
            m = s_list[0].max(axis=0, keepdims=True)
            for k in range(1, _RING):
                m = jnp.maximum(m, s_list[k].max(axis=0, keepdims=True))
            return [jnp.exp2(sc - m).astype(BF16) for sc in s_list]

        def values(hd, p_list):
            o = _dot(vring[0, hd], p_list[0])
            for k in range(1, _RING):
                o = o + _dot(vring[k, hd], p_list[k])
            o_scr[j, hd * dh:(hd + 1) * dh, :] = (o[0:dh] * (1.0 / o[dh:dh + 1])).astype(BF16)

        s_next = scores(0)
        pending = None
        for hd in range(ATT_HEADS):
            s_cur = s_next
            if hd + 1 < ATT_HEADS:
                s_next = scores(hd + 1)
            p_list = probs(s_cur)
            if pending is not None:
                values(*pending)
            pending = (hd, p_list)
            if hd in between:
                between[hd]()
        values(*pending)

    def finish(j):
        rows = slice(j * blk, (j + 1) * blk)
        mix = _dot_tn(o_scr[j], wo_ref[...])
        _finish_tokens(x_ref[rows, :], mix, alpha, lng_ref, lnb_ref, wrh_ref, wrl_ref, br_ref,
                       h_ref.at[rows, :], hp_ref.at[rows, :], lgt_ref.at[rows, :])

    qkv = project(0)
    for j in range(chain):
        to_ring(j, qkv[1], qkv[2])
        nxt = []
        between = {}
        if j + 1 < chain:
            between[0] = lambda j=j: nxt.append(project(j + 1))
        attend(j, qkv[0], between)
        finish(j)
        qkv = nxt[0] if nxt else None


def _rel_bias(rel_table, n_rows, n_cols, offset, sign):
    heads = rel_table.shape[0]
    period = n_rows + n_cols
    m = jnp.arange(period)
    c_minus_r = jnp.where(m < n_cols, m, m - period)
    w = rel_table[:, jnp.clip(offset - sign * c_minus_r, -REL_CLIP, REL_CLIP) + REL_CLIP]
    flat = jnp.broadcast_to(w[:, None, :], (heads, n_rows, period)).reshape(heads, n_rows * period)
    return flat[:, :n_rows * (period - 1)].reshape(heads, n_rows, period - 1)[:, :, :n_cols]


def _att_prompt_bias(rel_table, blk):
    i = jnp.arange(blk)
    out = []
    for d in range(_RING):
        b = _rel_bias(rel_table, blk, blk, d * blk, -1)
        cd = (i[None, :] // CHUNK) - (i[:, None] // CHUNK) + d * (blk // CHUNK)
        ok = (cd >= 0) & (cd <= BAND_CHUNKS)
        out.append(jnp.where(ok[None], b, NEG_INF))
    out.append(jnp.full_like(out[0], NEG_INF))
    return jnp.stack(out).astype(F32)


def _att_prompt(x2d, n_seq, seq_len, w_qkv, w_o, rel_table, ln_g, ln_b, wr_hi, wr_lo, br, alpha, n_all,
                cast=None):
    d_model = x2d.shape[1]
    blk = ATT_BLOCK
    chain = ATT_CHAIN
    rows = blk * chain
    nblk = seq_len // rows
    keep = min(BAND_CHUNKS * CHUNK, seq_len)
    assert seq_len % rows == 0 and keep % rows == 0
    assert (_RING - 1) * blk >= BAND_CHUNKS * CHUNK
    kb = keep // rows
    bias = (_att_prompt_bias(rel_table, blk) * LOG2E).astype(BF16)
    n_steps = n_seq * nblk
    n_fill = (n_all - n_seq * seq_len) // rows
    assert n_fill * rows == n_all - n_seq * seq_len

    def work(t):
        return jnp.minimum(t, n_steps - 1)

    row_spec = pl.BlockSpec(
        (None, rows, d_model), lambda t: (work(t) // nblk, jnp.maximum(work(t) % nblk - (nblk - kb), 0), 0),
        pipeline_mode=pl.Buffered(1))
    rows_out = jax.ShapeDtypeStruct((n_seq, keep, d_model), F32)
    in_specs = [
        pl.BlockSpec((rows, d_model), lambda t: (work(t), 0)),
        _const_spec(w_qkv.shape),
        _const_spec(w_o.shape),
        _const_spec(bias.shape),
        _const_spec(ln_g.shape),
        _const_spec(ln_b.shape),
        _const_spec(wr_hi.shape),
        _const_spec(wr_lo.shape),
        _const_spec(br.shape),
    ]
    args = [x2d, w_qkv, w_o, bias, ln_g, ln_b, wr_hi, wr_lo, br]
    cast_shapes, cast_specs = _cast_plumbing(args, in_specs, cast, n_steps, work)
    return pl.pallas_call(
        functools.partial(_att_prompt_kernel, blk=blk, chain=chain, nblk=nblk, n_steps=n_steps, n_fill=n_fill,
                          n_cast=len(cast_shapes), alpha=alpha),
        grid=(n_steps + n_fill,),
        in_specs=in_specs,
        out_specs=_token_out_specs(rows, d_model, lambda t: (t, 0)) + (row_spec, row_spec) + cast_specs,
        out_shape=_token_out_shapes(n_all, d_model) + (rows_out, rows_out) + cast_shapes,
        scratch_shapes=[
            pltpu.VMEM((_RING, blk, d_model), BF16),
            pltpu.VMEM((_RING, ATT_HEADS, d_model // ATT_HEADS + BF16_SUBLANES, blk), BF16),
            pltpu.VMEM((chain, d_model, blk), BF16),
        ],
        compiler_params=pltpu.CompilerParams(
            dimension_semantics=("arbitrary",), vmem_limit_bytes=VMEM_LIMIT),
        name="att_mixer",
    )(*args)


def _att_sample_kernel(*refs, nb, blk, alpha):
    (x_ref, kc_ref, vc_ref, wqkv_ref, wo_ref, bias_c_ref, bias_n_ref, lng_ref, lnb_ref,
     wrh_ref, wrl_ref, br_ref) = refs[:12]
    h_ref, hp_ref, lgt_ref, krow_ref, vrow_ref, o_scr = refs[-6:]
    d_model = x_ref.shape[1]
    heads = ATT_HEADS
    dh = d_model // heads
    x = x_ref[...]
    xb = x.astype(BF16)
    q = _dot(xb, wqkv_ref[:, 0:d_model]) * (dh ** -0.5)
    k = _dot(xb, wqkv_ref[:, d_model:2 * d_model])
    v = _dot(xb, wqkv_ref[:, 2 * d_model:3 * d_model])
    krow_ref[...] = k.reshape(nb, blk, d_model)
    vrow_ref[...] = v.reshape(nb, blk, d_model)
    lane_head = lax.broadcasted_iota(jnp.int32, (heads, 1, d_model), 2) // dh
    head_id = lax.broadcasted_iota(jnp.int32, (heads, 1, d_model), 0)
    head_mask = (lane_head == head_id).astype(F32)

    for s in range(nb):
        r0 = s * blk
        qs = q[r0:r0 + blk]
        q_bd = (qs[None, :, :] * head_mask).reshape(heads * blk, d_model).astype(BF16)
        kn = k[r0:r0 + blk].astype(BF16)
        vn = v[r0:r0 + blk].astype(BF16)
        s_c = _dot_nt(q_bd, kc_ref[s].astype(BF16)) + bias_c_ref[...]
        s_n = _dot_nt(q_bd, kn) + bias_n_ref[...]
        m = jnp.maximum(s_c.max(axis=-1, keepdims=True), s_n.max(axis=-1, keepdims=True))
        p_c = jnp.exp(s_c - m)
        p_n = jnp.exp(s_n - m)
        l = p_c.sum(axis=-1, keepdims=True) + p_n.sum(axis=-1, keepdims=True)
        o_full = _dot(p_c.astype(BF16), vc_ref[s].astype(BF16)) + _dot(p_n.astype(BF16), vn)
        o_full = o_full * (1.0 / l)
        o = (o_full.reshape(heads, blk, d_model) * head_mask).sum(axis=0)
        o_scr[r0:r0 + blk, :] = o.astype(BF16)

    mix = _dot(o_scr[...], wo_ref[...])
    _finish_tokens(x, mix, alpha, lng_ref, lnb_ref, wrh_ref, wrl_ref, br_ref, h_ref, hp_ref, lgt_ref)


def _att_sample(x2d, in_row0, n_seq, seq_len, k_cache, v_cache, w_qkv, w_o, rel_table, ln_g, ln_b,
                wr_hi, wr_lo, br, alpha, nb, n_all, out_row0, dst):
    d_model = x2d.shape[1]
    heads = ATT_HEADS
    blk = seq_len
    n_cache = k_cache.shape[1]
    rows = nb * blk
    assert n_seq % nb == 0 and in_row0 % rows == 0 and out_row0 % rows == 0
    in_b0, out_b0 = in_row0 // rows, out_row0 // rows
    bias_c = _rel_bias(rel_table, blk, n_cache, n_cache, 1).reshape(heads * blk, n_cache)
    bias_n = _rel_bias(rel_table, blk, blk, 0, 1).reshape(heads * blk, blk)
    kc = k_cache.reshape(n_seq, n_cache, d_model)
    vc = v_cache.reshape(n_seq, n_cache, d_model)
    cache_spec = pl.BlockSpec((nb, n_cache, d_model), lambda g: (g, 0, 0))
    row_spec = pl.BlockSpec((nb, blk, d_model), lambda g: (g, 0, 0))
    rows_out = jax.ShapeDtypeStruct((n_seq, blk, d_model), F32)
    in_specs = [
        pl.BlockSpec((rows, d_model), lambda g: (in_b0 + g, 0)),
        cache_spec,
        cache_spec,
        _const_spec(w_qkv.shape),
        _const_spec(w_o.shape),
        _const_spec(bias_c.shape),
        _const_spec(bias_n.shape),
        _const_spec(ln_g.shape),
        _const_spec(ln_b.shape),
        _const_spec(wr_hi.shape),
        _const_spec(wr_lo.shape),
        _const_spec(br.shape),
    ]
    args = [x2d, kc, vc, w_qkv, w_o, bias_c, bias_n, ln_g, ln_b, wr_hi, wr_lo, br]
    aliases = _alias_dst(args, in_specs, dst)
    return pl.pallas_call(
        functools.partial(_att_sample_kernel, nb=nb, blk=blk, alpha=alpha),
        grid=(n_seq // nb,),
        in_specs=in_specs,
        out_specs=_token_out_specs(rows, d_model, lambda g: (out_b0 + g, 0)) + (row_spec, row_spec),
        out_shape=_token_out_shapes(n_all, d_model) + (rows_out, rows_out),
        input_output_aliases=aliases,
        scratch_shapes=[pltpu.VMEM((rows, d_model), BF16)],
        compiler_params=pltpu.CompilerParams(
            dimension_semantics=("arbitrary",), vmem_limit_bytes=VMEM_LIMIT),
        name="att_mixer_cache",
    )(*args)


ROUTE_ROWS = 40


def _route_kernel(lgt_ref, meta_t_ref, meta_w_ref, count_ref, carry_ref, tri_ref):
    j = pl.program_id(0)
    tm = lgt_ref.shape[0]

    @pl.when(j == 0)
    def _():
        carry_ref[...] = jnp.zeros(carry_ref.shape, F32)
        r = lax.broadcasted_iota(jnp.int32, (tm, tm), 0)
        c = lax.broadcasted_iota(jnp.int32, (tm, tm), 1)
        tri_ref[...] = (r < c).astype(BF16)

    a = jnp.transpose(lgt_ref[...])[0:ROUTE_ROWS, :]
    row = lax.broadcasted_iota(jnp.int32, (ROUTE_ROWS, tm), 0)
    big = jnp.int32(LANES)
    glog = jnp.where(row < N_GROUPS, a, NEG_INF)
    gmax = glog.max(axis=0, keepdims=True)
    gsel = jnp.where(glog == gmax, row, big).min(axis=0, keepdims=True)
    gp = 1.0 / jnp.exp(glog - gmax).sum(axis=0, keepdims=True)
    first = N_GROUPS + gsel * EXP_PER_GROUP
    in_grp = (row >= first) & (row < first + EXP_PER_GROUP)
    elog = jnp.where(in_grp, a, NEG_INF)
    v1 = elog.max(axis=0, keepdims=True)
    l1 = jnp.where(elog == v1, row, big).min(axis=0, keepdims=True)
    elog2 = jnp.where(row == l1, NEG_INF, elog)
    v2 = elog2.max(axis=0, keepdims=True)
    l2 = jnp.where(elog2 == v2, row, big).min(axis=0, keepdims=True)
    e2 = jnp.exp(v2 - v1)
    w1 = gp / (1.0 + e2)
    w2 = gp * e2 / (1.0 + e2)

    is1 = row == l1
    is2 = row == l2
    oh = (is1 | is2).astype(F32)
    before = _dot(oh.astype(BF16), tri_ref[...]) + carry_ref[:, 0:1]
    rank1 = jnp.where(is1, before, 0.0).sum(axis=0, keepdims=True)
    rank2 = jnp.where(is2, before, 0.0).sum(axis=0, keepdims=True)
    carry_ref[...] = carry_ref[...] + oh.sum(axis=1, keepdims=True)
    count_ref[...] = carry_ref[...].astype(jnp.int32)

    pad = jnp.zeros((SUBLANES - 4, tm), jnp.int32)
    meta_t_ref[...] = jnp.concatenate(
        [l1 - N_GROUPS, l2 - N_GROUPS, rank1.astype(jnp.int32), rank2.astype(jnp.int32), pad], axis=0)
    wt = jnp.concatenate([w1, w2, jnp.zeros((LANES - 2, tm), F32)], axis=0)
    meta_w_ref[...] = jnp.transpose(wt)


def _route(logits):
    n_tok = logits.shape[0]
    tm = TOKEN_TILE
    assert n_tok % tm == 0
    tile = pl.BlockSpec((tm, LANES), lambda j: (j, 0))
    return pl.pallas_call(
        _route_kernel,
        grid=(n_tok // tm,),
        in_specs=[tile],
        out_specs=(pl.BlockSpec((SUBLANES, tm), lambda j: (0, j)), tile,
                   pl.BlockSpec((ROUTE_ROWS, LANES), lambda j: (0, 0))),
        out_shape=(
            jax.ShapeDtypeStruct((SUBLANES, n_tok), jnp.int32),
            jax.ShapeDtypeStruct((n_tok, LANES), F32),
            jax.ShapeDtypeStruct((ROUTE_ROWS, LANES), jnp.int32),
        ),
        scratch_shapes=[pltpu.VMEM((ROUTE_ROWS, LANES), F32), pltpu.VMEM((tm, tm), BF16)],
        compiler_params=pltpu.CompilerParams(dimension_semantics=("arbitrary",)),
        name="moe_route",
    )(logits)


def _sc_gather_loop(table_hbm, idx_v, out_hbm, rows_v, sems, base, n_chunk, r):
    def gather(c, slot):
        off = pl.multiple_of(c * r, r)
        return pltpu.make_async_copy(table_hbm.at[idx_v.at[pl.ds(off, r)]], rows_v.at[slot], sems.at[slot])

    def finish(c, slot):
        gather(c, slot).wait()
        pltpu.sync_copy(rows_v.at[slot], out_hbm.at[pl.ds(base + pl.multiple_of(c * r, r), r)])

    gather(0, 0).start()
    if n_chunk > 1:
        gather(1, 1).start()

    def body(pair, carry):
        c = 2 * pair
        for slot in range(2):
            finish(c + slot, slot)

            @pl.when(c + slot + 2 < n_chunk)
            def _():
                gather(c + slot + 2, slot).start()
        return carry

    lax.fori_loop(0, n_chunk // 2, body, 0)
    if n_chunk % 2:
        finish(n_chunk - 1, 0)


def _gather_rows(table, idx):
    m = idx.shape[0]
    width = table.shape[1]
    r = SC_ROWS_PER_CHUNK
    assert m % (SC_WORKERS * r) == 0
    per_w = m // SC_WORKERS
    mesh = plsc.VectorSubcoreMesh(core_axis_name="c", subcore_axis_name="s")

    @functools.partial(
        pl.kernel,
        mesh=mesh,
        out_type=jax.ShapeDtypeStruct((m, width), table.dtype),
        scratch_types=[
            pltpu.VMEM((per_w,), jnp.int32),
            pltpu.VMEM((2, r, width), table.dtype),
            pltpu.SemaphoreType.DMA((2,)),
        ],
    )
    def gather(table_hbm, idx_hbm, out_hbm, idx_v, rows_v, sem):
        wid = lax.axis_index("s") * 2 + lax.axis_index("c")
        base = wid * per_w
        pltpu.sync_copy(idx_hbm.at[pl.ds(base, per_w)], idx_v)
        _sc_gather_loop(table_hbm, idx_v, out_hbm, rows_v, sem, base, per_w // r, r)

    return gather(table, idx)


def _dispatch_rows(table, pos, row0, n_rows):
    n_tok, width = table.shape
    n_pairs = pos.shape[0]
    r = SC_ROWS_PER_CHUNK
    lanes = SC_LANES
    assert n_rows % (SC_WORKERS * r) == 0
    per_w = n_rows // SC_WORKERS
    n_stage = 16
    stage = n_pairs // n_stage
    assert stage * n_stage == n_pairs and stage % lanes == 0 and per_w % lanes == 0
    assert row0 + n_rows < 3 * n_tok
    mesh = plsc.VectorSubcoreMesh(core_axis_name="c", subcore_axis_name="s")

    @functools.partial(
        pl.kernel,
        mesh=mesh,
        out_type=jax.ShapeDtypeStruct((n_rows, width), table.dtype),
        scratch_types=[
            pltpu.VMEM((per_w,), jnp.int32),
            pltpu.VMEM((stage,), jnp.int32),
            pltpu.VMEM((2, r, width), table.dtype),
            pltpu.SemaphoreType.DMA((2,)),
        ],
        compiler_params=pltpu.CompilerParams(needs_layout_passes=False),
    )
    def dispatch(table_hbm, pos_hbm, out_hbm, src_v, pos_v, rows_v, sem):
        wid = lax.axis_index("s") * 2 + lax.axis_index("c")
        out_base = wid * per_w
        base = row0 + out_base
        lane = lax.iota(jnp.int32, lanes)

        def wrap(t):
            t = jnp.where(t >= n_tok, t - n_tok, t)
            return jnp.where(t >= n_tok, t - n_tok, t)

        def init(i, carry):
            off = pl.multiple_of(i * lanes, lanes)
            src_v[pl.ds(off, lanes)] = wrap(base + off + lane)
            return carry

        lax.fori_loop(0, per_w // lanes, init, 0)

        def scan_stage(sidx, carry):
            pair0 = sidx * stage
            pltpu.sync_copy(pos_hbm.at[pl.ds(pl.multiple_of(pair0, 8), stage)], pos_v)

            @plsc.parallel_loop(0, stage // lanes, unroll=4)
            def _(i):
                off = pl.multiple_of(i * lanes, lanes)
                local = pos_v[pl.ds(off, lanes)] - base
                mine = (local >= 0) & (local < per_w)
                plsc.store_scatter(src_v, [jnp.where(mine, local, 0)], wrap(pair0 + off + lane), mask=mine)

            return carry

        lax.fori_loop(0, n_stage, scan_stage, 0)
        _sc_gather_loop(table_hbm, src_v, out_hbm, rows_v, sem, out_base, per_w // r, r)

    return dispatch(table, pos)


def _gemm_kernel(tile_expert_ref, n_used_ref, xs_ref, wg_ref, wu_ref, wd_ref, *rest, tile0):
    ys_ref = rest[-1]
    tile = tile0 + pl.program_id(0)

    @pl.when(tile < n_used_ref[0])
    def _():
        x = _unpack_rows(xs_ref[...]).astype(BF16)
        g = _dot(x, wg_ref[...])
        u = _dot(x, wu_ref[...])
        hmid = (jax.nn.silu(g) * u).astype(BF16)
        ys_ref[...] = _pack_rows(_dot(hmid, wd_ref[...]))


def _expert_gemm(xs, tile_expert, n_used, w_gate, w_up, w_down, tile0, n_rows_all, ys):
    n_rows, half = xs.shape
    d_model, d_exp = w_gate.shape[1], w_gate.shape[2]
    tm = GEMM_TILE
    n_tiles = n_rows // tm

    def last_used(j, nu):
        return jnp.minimum(tile0 + j, nu[0] - 1)

    def in_map(j, te, nu):
        return (jnp.maximum(last_used(j, nu) - tile0, 0), 0)

    def out_map(j, te, nu):
        return (jnp.maximum(last_used(j, nu), tile0), 0)

    def w_map(j, te, nu):
        return (te[tile0 + j], 0, 0)

    in_specs = [
        pl.BlockSpec((tm, half), in_map),
        pl.BlockSpec((None, d_model, d_exp), w_map),
        pl.BlockSpec((None, d_model, d_exp), w_map),
        pl.BlockSpec((None, d_exp, d_model), w_map),
    ]
    args = [tile_expert, n_used, xs, w_gate, w_up, w_down]
    aliases = {}
    if ys is not None:
        aliases[len(args)] = 0
        args.append(ys)
        in_specs.append(pl.BlockSpec(memory_space=pl.ANY))
    grid_spec = pltpu.PrefetchScalarGridSpec(
        num_scalar_prefetch=2,
        grid=(n_tiles,),
        in_specs=in_specs,
        out_specs=pl.BlockSpec((tm, half), out_map),
    )
    return pl.pallas_call(
        functools.partial(_gemm_kernel, tile0=tile0),
        grid_spec=grid_spec,
        out_shape=jax.ShapeDtypeStruct((n_rows_all, half), U32),
        input_output_aliases=aliases,
        compiler_params=pltpu.CompilerParams(
            dimension_semantics=("arbitrary",), vmem_limit_bytes=VMEM_LIMIT),
        name="moe_gemm",
    )(*args)


def _combine_kernel(y0_ref, y1_ref, mw_ref, h_ref, pp_ref, ps_ref, lng_ref, lnb_ref, wproj_ref, wgate_ref,
                    *out_refs, n_prompt_tiles, alpha):
    j = pl.program_id(0)
    tm = h_ref.shape[0]
    half = tm // 2

    def normed(rows):
        mw = mw_ref[rows, :]
        ffn = mw[:, 0:1] * _unpack_rows(y0_ref[rows, :]) + mw[:, 1:2] * _unpack_rows(y1_ref[rows, :])
        return _layer_norm(alpha * h_ref[rows, :] + ffn, lng_ref[...], lnb_ref[...])

    def gated(rows, h2):
        p = jnp.where(j < n_prompt_tiles, pp_ref[rows, :], ps_ref[rows, :]).astype(BF16)
        gate = jax.nn.sigmoid(_dot(h2.astype(BF16), wgate_ref[...]))
        return h2 + gate * _dot(p, wproj_ref[...])

    rows_a, rows_b = slice(0, half), slice(half, tm)
    h2_a = normed(rows_a)
    h2_b = normed(rows_b)
    out = jnp.concatenate([gated(rows_a, h2_a), gated(rows_b, h2_b)], axis=0)
    if len(out_refs) == 1:
        out_refs[0][...] = out
    else:
        @pl.when(j < n_prompt_tiles)
        def _():
            out_refs[0][...] = out

        @pl.when(j >= n_prompt_tiles)
        def _():
            out_refs[1][...] = out


def _combine(yg, meta_w, h, p_prompt, p_sample, layer, ln_g, ln_b, w_proj, w_gate, alpha, split_out):
    n_tok, d_model = h.shape
    tm = TOKEN_TILE
    n_t = n_tok // tm
    n_tp = p_prompt.shape[1] // tm
    n_ts = p_sample.shape[1] // tm
    assert n_tp * tm == p_prompt.shape[1] and n_ts * tm == p_sample.shape[1] and n_tp + n_ts == n_t
    ple = p_prompt.shape[2]
    tile = pl.BlockSpec((tm, d_model), lambda j: (j, 0))
    if split_out:
        out_specs = (pl.BlockSpec((tm, d_model), lambda j: (jnp.minimum(j, n_tp - 1), 0)),
                     pl.BlockSpec((tm, d_model), lambda j: (jnp.maximum(j - n_tp, 0), 0)))
        out_shape = (jax.ShapeDtypeStruct((n_tp * tm, d_model), F32),
                     jax.ShapeDtypeStruct((n_ts * tm, d_model), F32))
    else:
        out_specs = tile
        out_shape = jax.ShapeDtypeStruct((n_tok, d_model), F32)
    return pl.pallas_call(
        functools.partial(_combine_kernel, n_prompt_tiles=n_tp, alpha=alpha),
        grid=(n_t,),
        in_specs=[
            pl.BlockSpec((tm, d_model // 2), lambda j: (j, 0)),
            pl.BlockSpec((tm, d_model // 2), lambda j: (j + n_t, 0)),
            pl.BlockSpec((tm, LANES), lambda j: (j, 0)),
            tile,
            pl.BlockSpec((None, tm, ple), lambda j: (layer, jnp.minimum(j, n_tp - 1), 0)),
            pl.BlockSpec((None, tm, ple), lambda j: (layer, jnp.maximum(j - n_tp, 0), 0)),
            _const_spec(ln_g.shape),
            _const_spec(ln_b.shape),
            _const_spec(w_proj.shape),
            _const_spec(w_gate.shape),
        ],
        out_specs=out_specs,
        out_shape=out_shape,
        compiler_params=pltpu.CompilerParams(
            dimension_semantics=("arbitrary",), vmem_limit_bytes=VMEM_LIMIT),
        name="moe_combine",
    )(yg, yg, meta_w, h, p_prompt, p_sample, ln_g, ln_b, w_proj, w_gate)


def _router_weights(w_grp, b_grp, w_exp, b_exp):
    d_model = w_grp.shape[0]
    w = jnp.concatenate([w_grp, jnp.transpose(w_exp, (1, 0, 2)).reshape(d_model, N_EXPERTS)], axis=1)
    w = jnp.pad(w, ((0, 0), (0, LANES - w.shape[1])))
    b = jnp.concatenate([b_grp, b_exp.reshape(N_EXPERTS)])
    b = jnp.pad(b, (0, LANES - b.shape[0])).reshape(1, LANES)
    w_hi = w.astype(BF16)
    w_lo = (w - w_hi.astype(F32)).astype(BF16)
    return w_hi, w_lo, b


def _moe(h_packed, logits, w_gate, w_up, w_down):
    n_tok = h_packed.shape[0]
    tm = GEMM_TILE
    meta_t, meta_w, counts = _route(logits)
    counts = counts[N_GROUPS:N_GROUPS + N_EXPERTS, 0]
    tiles_per_expert = (counts + tm - 1) // tm
    tile_end = jnp.cumsum(tiles_per_expert)
    row_start = (tile_end - tiles_per_expert) * tm
    eid = meta_t[0:2]
    rank = meta_t[2:4]
    experts = jnp.arange(N_EXPERTS, dtype=jnp.int32)
    start = jnp.sum(jnp.where(eid[:, :, None] == experts, row_start, 0), axis=-1)
    pos = (start + rank).reshape(-1).astype(jnp.int32)
    part_quant = MOE_PARTS * SC_WORKERS * SC_ROWS_PER_CHUNK * (tm // math.gcd(tm, SC_WORKERS * SC_ROWS_PER_CHUNK))
    n_tiles = -(-(2 * n_tok) // tm) + N_EXPERTS
    n_rows = -(-(n_tiles * tm) // part_quant) * part_quant
    n_tiles = n_rows // tm
    n_used = tile_end[-1:].astype(jnp.int32)
    tile_ids = jnp.minimum(jnp.arange(n_tiles, dtype=jnp.int32), n_used[0] - 1)
    tile_expert = jnp.sum(tile_end[None, :] <= tile_ids[:, None], axis=1).astype(jnp.int32)
    part_rows = n_rows // MOE_PARTS
    xs_parts = [_dispatch_rows(h_packed, pos, part * part_rows, part_rows) for part in range(MOE_PARTS)]
    ys = None
    for part, xs in enumerate(xs_parts):
        ys = _expert_gemm(xs, tile_expert, n_used, w_gate, w_up, w_down,
                          part * (part_rows // tm), n_rows, ys)
    return _gather_rows(ys, pos), meta_w


def kernel(x_prompt, x_sample, p_prompt, p_sample, state_ret, cache_att_k, cache_att_v, ret_w_in, ret_gn_g,
           ret_w_o, att_w_qkv, att_rel_bias, att_w_o, ln1_g, ln1_b, ln2_g, ln2_b, moe_w_grp, moe_b_grp,
           moe_w_exp, moe_b_exp, moe_w_gate, moe_w_up, moe_w_down, ple_w_proj, ple_w_gate):
    n_p, len_p, d_model = x_prompt.shape
    n_s, len_s, _ = x_sample.shape
    depth = ln1_g.shape[0]
    alpha = float((2 * depth) ** 0.25)
    tok_p, tok_s = n_p * len_p, n_s * len_s
    n_all = tok_p + tok_s
    dh = d_model // ATT_HEADS
    pp = p_prompt.reshape(depth, tok_p, -1)
    ps = p_sample.reshape(depth, tok_s, -1)
    nb_s = 4

    x_all = None
    y_prompt = y_sample = None
    states_p, states_s, k_p, v_p, k_s, v_s = [], [], [], [], [], []
    for i in range(depth):
        jj = i // 2
        wr_hi, wr_lo, br = _router_weights(moe_w_grp[i], moe_b_grp[i], moe_w_exp[i], moe_b_exp[i])
        lng, lnb = ln1_g[i].reshape(1, d_model), ln1_b[i].reshape(1, d_model)
        moe_cast = (i, [moe_w_gate, moe_w_up, moe_w_down])
        if x_all is None:
            src_p, src_s, row_s = x_prompt.reshape(tok_p, d_model), x_sample.reshape(tok_s, d_model), 0
        else:
            src_p, src_s, row_s = x_all, x_all, tok_p
        if i % 2 == 0:
            w_in = ret_w_in[jj].astype(BF16)
            w_o = ret_w_o[jj].astype(BF16)
            gn = ret_gn_g[jj].reshape(1, -1)
            h, hp, lgt, st_p, *w_moe = _ret_mixer(
                src_p, 0, n_p, len_p, 0, None, w_in, w_o, gn, lng, lnb, wr_hi, wr_lo, br, alpha, nb=1,
                blk=min(RET_BLOCK, len_p), chain=RET_CHAIN, n_all=n_all, out_row0=0, dst=None, cast=moe_cast)
            h, hp, lgt, st_s = _ret_mixer(src_s, row_s, n_s, len_s, PAST_LEN, state_ret[jj], w_in, w_o, gn,
                                          lng, lnb, wr_hi, wr_lo, br, alpha, nb=nb_s, blk=len_s, chain=1,
                                          n_all=n_all, out_row0=tok_p, dst=(h, hp, lgt))
            states_p.append(st_p)
            states_s.append(st_s)
        else:
            w_qkv = att_w_qkv[jj].astype(BF16)
            w_o = att_w_o[jj].astype(BF16)
            h, hp, lgt, kr, vr, *w_moe = _att_prompt(src_p, n_p, len_p, w_qkv, w_o, att_rel_bias[jj], lng, lnb,
                                                     wr_hi, wr_lo, br, alpha, n_all, cast=moe_cast)
            k_p.append(kr.reshape(n_p, -1, ATT_HEADS, dh))
            v_p.append(vr.reshape(n_p, -1, ATT_HEADS, dh))
            h, hp, lgt, kr, vr = _att_sample(src_s, row_s, n_s, len_s, cache_att_k[jj], cache_att_v[jj],
                                             w_qkv, w_o, att_rel_bias[jj], lng, lnb, wr_hi, wr_lo, br, alpha,
                                             nb=nb_s, n_all=n_all, out_row0=tok_p, dst=(h, hp, lgt))
            k_s.append(kr.reshape(n_s, len_s, ATT_HEADS, dh))
            v_s.append(vr.reshape(n_s, len_s, ATT_HEADS, dh))
        n_exp = moe_w_gate.shape[1]
        w_moe = [w.reshape(n_exp, -1, w.shape[1]) for w in w_moe]
        yg, meta_w = _moe(hp, lgt, *w_moe)
        last = i == depth - 1
        out = _combine(yg, meta_w, h, pp, ps, i, ln2_g[i].reshape(1, d_model), ln2_b[i].reshape(1, d_model),
                       ple_w_proj[i].astype(BF16), ple_w_gate[i].astype(BF16), alpha, split_out=last)
        if last:
            y_prompt = out[0].reshape(n_p, len_p, d_model)
            y_sample = out[1].reshape(n_s, len_s, d_model)
        else:
            x_all = out

    return (y_prompt, y_sample, jnp.stack(states_p), jnp.stack(states_s),
            jnp.stack(k_p), jnp.stack(v_p), jnp.stack(k_s), jnp.stack(v_s))
```

```python
import functools
import math

import numpy as np
import jax
import jax.numpy as jnp
from jax import lax
from jax.experimental import pallas as pl
from jax.experimental.pallas import tpu as pltpu
from jax.experimental.pallas import tpu_sc as plsc

CHUNK = 64
PAST_LEN = 2048
RET_HEADS = 4
ROPE_BASE = 10000.0
ATT_HEADS = 16
BAND_CHUNKS = 8
REL_CLIP = 256
N_GROUPS = 4
EXP_PER_GROUP = 8
N_EXPERTS = N_GROUPS * EXP_PER_GROUP
LN_EPS = 1e-5
NEG_INF = -1e30

LANES = 128
SUBLANES = 8
BF16_SUBLANES = 16
SC_WORKERS = 32
SC_LANES = 16
SC_ROWS_PER_CHUNK = 32
VMEM_LIMIT = 58 * 1024 * 1024

RET_BLOCK = 256
RET_CHAIN = 2
ATT_BLOCK = 4 * CHUNK
ATT_CHAIN = 2
TOKEN_TILE = 512
GEMM_TILE = 512
MOE_PARTS = 3

F32 = jnp.float32
BF16 = jnp.bfloat16
U32 = jnp.uint32
HI_MASK = 0xFFFF0000
LOG2E = math.log2(math.e)


def _dot(a, b):
    return jnp.dot(a, b, preferred_element_type=F32)


def _dot_nt(a, b):
    return lax.dot_general(a, b, (((1,), (1,)), ((), ())), preferred_element_type=F32)


def _dot_tn(a, b):
    return lax.dot_general(a, b, (((0,), (0,)), ((), ())), preferred_element_type=F32)


def _layer_norm(x, g, b):
    mu = jnp.mean(x, axis=-1, keepdims=True)
    xc = x - mu
    var = jnp.mean(xc * xc, axis=-1, keepdims=True)
    return xc * lax.rsqrt(var + LN_EPS) * g + b


def _pack_rows(x):
    half = x.shape[1] // 2
    lo = lax.bitcast_convert_type(x[:, :half].astype(BF16).astype(F32), U32) >> 16
    hi = lax.bitcast_convert_type(x[:, half:].astype(BF16).astype(F32), U32) & U32(HI_MASK)
    return lo | hi


def _unpack_rows(p):
    lo = lax.bitcast_convert_type(p << 16, F32)
    hi = lax.bitcast_convert_type(p & U32(HI_MASK), F32)
    return jnp.concatenate([lo, hi], axis=1)


def _router_logits(h, wr_hi_ref, wr_lo_ref, br_ref):
    h_hi = h.astype(BF16)
    h_lo = (h - h_hi.astype(F32)).astype(BF16)
    w_hi = wr_hi_ref[...]
    return _dot(h_hi, w_hi) + _dot(h_lo, w_hi) + _dot(h_hi, wr_lo_ref[...]) + br_ref[...]


def _finish_tokens(x, mix, alpha, lng_ref, lnb_ref, wrh_ref, wrl_ref, br_ref, h_ref, hp_ref, lgt_ref):
    hh = _layer_norm(alpha * x + mix, lng_ref[...], lnb_ref[...])
    h_ref[...] = hh
    hp_ref[...] = _pack_rows(hh)
    lgt_ref[...] = _router_logits(hh, wrh_ref, wrl_ref, br_ref)


def _zero_tokens(h_ref, hp_ref, lgt_ref):
    h_ref[...] = jnp.zeros(h_ref.shape, h_ref.dtype)
    hp_ref[...] = jnp.zeros(hp_ref.shape, hp_ref.dtype)
    lgt_ref[...] = jnp.zeros(lgt_ref.shape, lgt_ref.dtype)


def _const_spec(shape):
    nd = len(shape)
    return pl.BlockSpec(shape, lambda *_: (0,) * nd, pipeline_mode=pl.Buffered(1))


def _token_out_shapes(n_all, d_model):
    return (
        jax.ShapeDtypeStruct((n_all, d_model), F32),
        jax.ShapeDtypeStruct((n_all, d_model // 2), U32),
        jax.ShapeDtypeStruct((n_all, LANES), F32),
    )


def _token_out_specs(rows, d_model, row_map):
    return (
        pl.BlockSpec((rows, d_model), row_map),
        pl.BlockSpec((rows, d_model // 2), row_map),
        pl.BlockSpec((rows, LANES), row_map),
    )


def _alias_dst(args, in_specs, dst):
    if dst is None:
        return {}
    aliases = {}
    for k, arr in enumerate(dst):
        aliases[len(args)] = k
        args.append(arr)
        in_specs.append(pl.BlockSpec(memory_space=pl.ANY))
    return aliases


def _cast_plumbing(args, in_specs, cast, n_steps, work):
    if cast is None:
        return (), ()
    layer, weights = cast
    shapes, specs = [], []
    for w in weights:
        depth, n_exp, rows, cols = w.shape
        assert (n_exp * rows) % n_steps == 0
        slab = (n_exp * rows) // n_steps
        args.append(w.reshape(depth, n_exp * rows, cols))
        in_specs.append(pl.BlockSpec((None, slab, cols), lambda t: (layer, work(t), 0)))
        shapes.append(jax.ShapeDtypeStruct((n_exp * rows, cols), BF16))
        specs.append(pl.BlockSpec((slab, cols), lambda t: (work(t), 0)))
    return tuple(shapes), tuple(specs)


def _cast_slabs(in_refs, out_refs):
    for src, dst in zip(in_refs, out_refs):
        dst[...] = src[...].astype(BF16)


def _ret_log_gamma():
    h = np.arange(RET_HEADS, dtype=np.float32)
    return np.log(np.float32(1.0) - np.float32(2.0) ** (np.float32(-5.0) - h)).astype(np.float32)


def _ret_kernel(*refs, nb, blk, chain, nblk, n_steps, n_fill, has_state, n_alias, n_cast, alpha):
    step = pl.program_id(0)
    n_in = 12 + int(has_state) + n_alias + n_cast
    h_ref, hp_ref, lgt_ref = refs[n_in:n_in + 3]
    core = refs[:12 + int(has_state)] + refs[n_in:n_in + 4] + refs[-1:]

    @pl.when(step < n_steps)
    def _():
        _cast_slabs(refs[n_in - n_cast:n_in], refs[n_in + 4:n_in + 4 + n_cast])
        _ret_step(*core, first_block=step % nblk == 0, nb=nb, blk=blk, chain=chain, has_state=has_state,
                  alpha=alpha)

    if n_fill:
        @pl.when(step >= n_steps)
        def _():
            _zero_tokens(h_ref, hp_ref, lgt_ref)


def _ret_step(*refs, first_block, nb, blk, chain, has_state, alpha):
    (x_ref, cos_ref, sin_ref, dmask_ref, win_ref, wo_ref, gn_ref, lng_ref, lnb_ref,
     wrh_ref, wrl_ref, br_ref) = refs[:12]
    s_in_ref = refs[12] if has_state else None
    h_ref, hp_ref, lgt_ref, s_out_ref, gated_ref = refs[12 + int(has_state):]
    heads = RET_HEADS
    d_model = x_ref.shape[1]
    dk = d_model // heads
    dv = 2 * d_model // heads
    hk, hv = heads * dk, heads * dv
    half = dk // 2
    lg = _ret_log_gamma()

    n_sub = nb * chain
    proj_rows = blk if chain > 1 else n_sub * blk
    rowf = lax.broadcasted_iota(jnp.int32, (blk, 1), 0).astype(F32)

    def proj(group):
        xb = x_ref[group * proj_rows:(group + 1) * proj_rows, :].astype(BF16)
        return (_dot(xb, win_ref[:, 0:hk]), _dot(xb, win_ref[:, hk:2 * hk]),
                _dot(xb, win_ref[:, 2 * hk:2 * hk + hv]), _dot(xb, win_ref[:, 2 * hk + hv:2 * hk + 2 * hv]))

    if not has_state:
        @pl.when(first_block)
        def _():
            s_out_ref[...] = jnp.zeros(s_out_ref.shape, F32)

    s_prev_ref = s_in_ref if has_state else s_out_ref

    def head(j, h, projected):
        q_all, k_all, v_all, g_all = projected
        s = j // chain
        c0 = (j % chain) * blk
        r0 = (j * blk) % proj_rows
        cos = cos_ref[c0:c0 + blk, :]
        sin = sin_ref[c0:c0 + blk, :]

        def rot(t):
            t1, t2 = t[:, :half], t[:, half:]
            return jnp.concatenate([t1 * cos - t2 * sin, t1 * sin + t2 * cos], axis=1)

        lgh = float(lg[h])
        q = rot(q_all[r0:r0 + blk, h * dk:(h + 1) * dk])
        k = rot(k_all[r0:r0 + blk, h * dk:(h + 1) * dk]) * (dk ** -0.5)
        v = v_all[r0:r0 + blk, h * dv:(h + 1) * dv]
        g = g_all[r0:r0 + blk, h * dv:(h + 1) * dv]
        vb = v.astype(BF16)
        scores = _dot_nt(q.astype(BF16), k.astype(BF16)) * dmask_ref[h]
        inner = _dot(scores.astype(BF16), vb)
        s_prev = s_prev_ref[s, h]
        q_dec = q * jnp.exp(lgh * (rowf + 1.0))
        cross = _dot(q_dec.astype(BF16), s_prev.astype(BF16))
        k_dec = k * jnp.exp(lgh * (float(blk - 1) - rowf))
        s_out_ref[s, h] = math.exp(lgh * blk) * s_prev + _dot_tn(k_dec.astype(BF16), vb)
        o = inner + cross
        mu = jnp.mean(o, axis=-1, keepdims=True)
        oc = o - mu
        var = jnp.mean(oc * oc, axis=-1, keepdims=True)
        on = oc * lax.rsqrt(var + LN_EPS) * gn_ref[:, h * dv:(h + 1) * dv]
        gated_ref[j * blk:(j + 1) * blk, h * dv:(h + 1) * dv] = (jax.nn.silu(g) * on).astype(BF16)

    def tail(group):
        rows = slice(group * proj_rows, (group + 1) * proj_rows)
        mix = _dot(gated_ref[rows, :], wo_ref[...])
        _finish_tokens(x_ref[rows, :], mix, alpha, lng_ref, lnb_ref, wrh_ref, wrl_ref, br_ref,
                       h_ref.at[rows, :], hp_ref.at[rows, :], lgt_ref.at[rows, :])

    n_groups = (n_sub * blk) // proj_rows
    subs_per_group = proj_rows // blk
    projected = proj(0)
    for group in range(n_groups):
        nxt = None
        for jj in range(subs_per_group):
            for h in range(heads):
                head(group * subs_per_group + jj, h, projected)
                if jj == 0 and h == 0 and group + 1 < n_groups:
                    nxt = proj(group + 1)
                if jj == 0 and h == 1 and group > 0:
                    tail(group - 1)
        projected = nxt
    tail(n_groups - 1)


def _ret_mixer(x2d, in_row0, n_seq, seq_len, pos0, state_in, w_in, w_o, gn_g, ln_g, ln_b,
               wr_hi, wr_lo, br, alpha, nb, blk, chain, n_all, out_row0, dst, cast=None):
    d_model = x2d.shape[1]
    heads = RET_HEADS
    dk, dv = d_model // heads, 2 * d_model // heads
    half = dk // 2
    nblk = seq_len // (blk * chain)
    has_state = state_in is not None
    rows = nb * blk * chain
    assert seq_len % (blk * chain) == 0 and n_seq % nb == 0
    assert (not has_state) or nblk == 1
    assert nb == 1 or (nblk == 1 and chain == 1)
    assert in_row0 % rows == 0 and out_row0 % rows == 0
    in_b0, out_b0 = in_row0 // rows, out_row0 // rows

    pos = (pos0 + jnp.arange(seq_len, dtype=jnp.int32)).astype(F32)
    inv_freq = ROPE_BASE ** (-jnp.arange(half, dtype=F32) / half)
    ang = pos[:, None] * inv_freq[None, :]
    cos, sin = jnp.cos(ang), jnp.sin(ang)
    lg = jnp.asarray(_ret_log_gamma())
    ii = jnp.arange(blk, dtype=F32)
    diff = ii[:, None] - ii[None, :]
    dmask = jnp.where(diff >= 0, jnp.exp(lg[:, None, None] * jnp.maximum(diff, 0.0)), 0.0)

    n_steps = (n_seq // nb) * nblk
    n_fill = 0 if dst is not None else (n_all - n_seq * seq_len) // rows
    assert dst is not None or (out_row0 == 0 and n_fill * rows == n_all - n_seq * seq_len)

    def work(t):
        return jnp.minimum(t, n_steps - 1)

    in_specs = [
        pl.BlockSpec((rows, d_model), lambda t: (in_b0 + work(t), 0)),
        pl.BlockSpec((blk * chain, half), lambda t: (work(t) % nblk, 0)),
        pl.BlockSpec((blk * chain, half), lambda t: (work(t) % nblk, 0)),
        _const_spec(dmask.shape),
        _const_spec(w_in.shape),
        _const_spec(w_o.shape),
        _const_spec(gn_g.shape),
        _const_spec(ln_g.shape),
        _const_spec(ln_b.shape),
        _const_spec(wr_hi.shape),
        _const_spec(wr_lo.shape),
        _const_spec(br.shape),
    ]
    args = [x2d, cos, sin, dmask, w_in, w_o, gn_g, ln_g, ln_b, wr_hi, wr_lo, br]
    state_spec = pl.BlockSpec((nb, heads, dk, dv), lambda t: (work(t) // nblk, 0, 0, 0))
    if has_state:
        in_specs.append(state_spec)
        args.append(state_in)
    aliases = _alias_dst(args, in_specs, dst)
    cast_shapes, cast_specs = _cast_plumbing(args, in_specs, cast, n_steps, work)
    out_shape = (_token_out_shapes(n_all, d_model) + (jax.ShapeDtypeStruct((n_seq, heads, dk, dv), F32),)
                 + cast_shapes)
    out_specs = _token_out_specs(rows, d_model, lambda t: (out_b0 + t, 0)) + (state_spec,) + cast_specs
    return pl.pallas_call(
        functools.partial(_ret_kernel, nb=nb, blk=blk, chain=chain, nblk=nblk, n_steps=n_steps, n_fill=n_fill,
                          has_state=has_state, n_alias=len(aliases), n_cast=len(cast_shapes), alpha=alpha),
        grid=(n_steps + n_fill,),
        in_specs=in_specs,
        out_specs=out_specs,
        out_shape=out_shape,
        input_output_aliases=aliases,
        scratch_shapes=[pltpu.VMEM((rows, heads * dv), BF16)],
        compiler_params=pltpu.CompilerParams(
            dimension_semantics=("arbitrary",), vmem_limit_bytes=VMEM_LIMIT),
        name="ret_mixer_state" if has_state else "ret_mixer",
    )(*args)


_RING = 3


def _att_prompt_kernel(*refs, blk, chain, nblk, n_steps, n_fill, n_cast, alpha):
    step = pl.program_id(0)
    n_in = 9 + n_cast
    h_ref, hp_ref, lgt_ref = refs[n_in:n_in + 3]
    core = refs[:9] + refs[n_in:n_in + 5] + refs[-3:]

    @pl.when(step < n_steps)
    def _():
        _cast_slabs(refs[9:n_in], refs[n_in + 5:n_in + 5 + n_cast])
        _att_prompt_step(*core, i0=(step % nblk) * chain, blk=blk, chain=chain, alpha=alpha)

    if n_fill:
        @pl.when(step >= n_steps)
        def _():
            _zero_tokens(h_ref, hp_ref, lgt_ref)


def _att_prompt_step(x_ref, wqkv_ref, wo_ref, bias_ref, lng_ref, lnb_ref, wrh_ref, wrl_ref, br_ref,
                     h_ref, hp_ref, lgt_ref, krow_ref, vrow_ref, kring, vring, o_scr, *, i0, blk, chain, alpha):
    d_model = x_ref.shape[1]
    dh = d_model // ATT_HEADS
    heads_per_group = LANES // dh
    lane = lax.broadcasted_iota(jnp.int32, (1, LANES), 1)

    @pl.when(i0 == 0)
    def _():
        kring[...] = jnp.zeros(kring.shape, BF16)
        vring[...] = jnp.zeros(vring.shape, BF16)
        vring[:, :, dh:, :] = jnp.ones((_RING, ATT_HEADS, vring.shape[2] - dh, blk), BF16)

    def project(j):
        xb = x_ref[j * blk:(j + 1) * blk, :].astype(BF16)
        q = _dot(xb, wqkv_ref[:, 0:d_model]) * (dh ** -0.5 * LOG2E)
        k = _dot(xb, wqkv_ref[:, d_model:2 * d_model])
        v = _dot(xb, wqkv_ref[:, 2 * d_model:3 * d_model])
        return q, k, v

    def to_ring(j, k, v):
        slot = (i0 + j) % _RING
        krow_ref[j * blk:(j + 1) * blk, :] = k
        vrow_ref[j * blk:(j + 1) * blk, :] = v
        kring[slot] = k.astype(BF16)
        v_t = jnp.transpose(v).astype(BF16)
        for hd in range(ATT_HEADS):
            vring[slot, hd, 0:dh, :] = v_t[hd * dh:(hd + 1) * dh, :]

    def attend(j, q, between):
        i = i0 + j
        qb = q.astype(BF16)
        behind = [(i + _RING - k) % _RING for k in range(_RING)]
        slabs = [jnp.where(i >= behind[k], behind[k], _RING) for k in range(_RING)]

        def scores(hd):
            c0 = (hd // heads_per_group) * LANES
            sub = hd % heads_per_group
            in_head = (lane >= sub * dh) & (lane < (sub + 1) * dh)
            q_pair = qb[:, c0:c0 + LANES]
            qm = jnp.where(in_head, q_pair, jnp.zeros_like(q_pair))
            s_all = _dot_nt(kring[:, :, c0:c0 + LANES].reshape(_RING * blk, LANES), qm)
            return [s_all[k * blk:(k + 1) * blk] + bias_ref[slabs[k], hd] for k in range(_RING)]

        def probs(s_list):
            m = s_list[0].max(axis=0, keepdims=True)
            for k in range(1, _RING):
                m = jnp.maximum(m, s_list[k].max(axis=0, keepdims=True))
            return [jnp.exp2(sc - m).astype(BF16) for sc in s_list]

        def values(hd, p_list):
            o = _dot(vring[0, hd], p_list[0])
            for k in range(1, _RING):
                o = o + _dot(vring[k, hd], p_list[k])
            o_scr[j, hd * dh:(hd + 1) * dh, :] = (o[0:dh] * (1.0 / o[dh:dh + 1])).astype(BF16)

        s_next = scores(0)
        pending = None
        for hd in range(ATT_HEADS):
            s_cur = s_next
            if hd + 1 < ATT_HEADS:
                s_next = scores(hd + 1)
            p_list = probs(s_cur)
            if pending is not None:
                values(*pending)
            pending = (hd, p_list)
            if hd in between:
                between[hd]()
        values(*pending)

    def finish(j):
        rows = slice(j * blk, (j + 1) * blk)
        mix = _dot_tn(o_scr[j], wo_ref[...])
        _finish_tokens(x_ref[rows, :], mix, alpha, lng_ref, lnb_ref, wrh_ref, wrl_ref, br_ref,
                       h_ref.at[rows, :], hp_ref.at[rows, :], lgt_ref.at[rows, :])

    qkv = project(0)
    for j in range(chain):
        to_ring(j, qkv[1], qkv[2])
        nxt = []
        between = {}
        if j + 1 < chain:
            between[0] = lambda j=j: nxt.append(project(j + 1))
        attend(j, qkv[0], between)
        finish(j)
        qkv = nxt[0] if nxt else None


def _rel_bias(rel_table, n_rows, n_cols, offset, sign):
    heads = rel_table.shape[0]
    period = n_rows + n_cols
    m = jnp.arange(period)
    c_minus_r = jnp.where(m < n_cols, m, m - period)
    w = rel_table[:, jnp.clip(offset - sign * c_minus_r, -REL_CLIP, REL_CLIP) + REL_CLIP]
    flat = jnp.broadcast_to(w[:, None, :], (heads, n_rows, period)).reshape(heads, n_rows * period)
    return flat[:, :n_rows * (period - 1)].reshape(heads, n_rows, period - 1)[:, :, :n_cols]


def _att_prompt_bias(rel_table, blk):
    i = jnp.arange(blk)
    out = []
    for d in range(_RING):
        b = _rel_bias(rel_table, blk, blk, d * blk, -1)
        cd = (i[None, :] // CHUNK) - (i[:, None] // CHUNK) + d * (blk // CHUNK)
        ok = (cd >= 0) & (cd <= BAND_CHUNKS)
        out.append(jnp.where(ok[None], b, NEG_INF))
    out.append(jnp.full_like(out[0], NEG_INF))
    return jnp.stack(out).astype(F32)


def _att_prompt(x2d, n_seq, seq_len, w_qkv, w_o, rel_table, ln_g, ln_b, wr_hi, wr_lo, br, alpha, n_all,
                cast=None):
    d_model = x2d.shape[1]
    blk = ATT_BLOCK
    chain = ATT_CHAIN
    rows = blk * chain
    nblk = seq_len // rows
    keep = min(BAND_CHUNKS * CHUNK, seq_len)
    assert seq_len % rows == 0 and keep % rows == 0
    assert (_RING - 1) * blk >= BAND_CHUNKS * CHUNK
    kb = keep // rows
    bias = (_att_prompt_bias(rel_table, blk) * LOG2E).astype(BF16)
    n_steps = n_seq * nblk
    n_fill = (n_all - n_seq * seq_len) // rows
    assert n_fill * rows == n_all - n_seq * seq_len

    def work(t):
        return jnp.minimum(t, n_steps - 1)

    row_spec = pl.BlockSpec(
        (None, rows, d_model), lambda t: (work(t) // nblk, jnp.maximum(work(t) % nblk - (nblk - kb), 0), 0),
        pipeline_mode=pl.Buffered(1))
    rows_out = jax.ShapeDtypeStruct((n_seq, keep, d_model), F32)
    in_specs = [
        pl.BlockSpec((rows, d_model), lambda t: (work(t), 0)),
        _const_spec(w_qkv.shape),
        _const_spec(w_o.shape),
        _const_spec(bias.shape),
        _const_spec(ln_g.shape),
        _const_spec(ln_b.shape),
        _const_spec(wr_hi.shape),
        _const_spec(wr_lo.shape),
        _const_spec(br.shape),
    ]
    args = [x2d, w_qkv, w_o, bias, ln_g, ln_b, wr_hi, wr_lo, br]
    cast_shapes, cast_specs = _cast_plumbing(args, in_specs, cast, n_steps, work)
    return pl.pallas_call(
        functools.partial(_att_prompt_kernel, blk=blk, chain=chain, nblk=nblk, n_steps=n_steps, n_fill=n_fill,
                          n_cast=len(cast_shapes), alpha=alpha),
        grid=(n_steps + n_fill,),
        in_specs=in_specs,
        out_specs=_token_out_specs(rows, d_model, lambda t: (t, 0)) + (row_spec, row_spec) + cast_specs,
        out_shape=_token_out_shapes(n_all, d_model) + (rows_out, rows_out) + cast_shapes,
        scratch_shapes=[
            pltpu.VMEM((_RING, blk, d_model), BF16),
            pltpu.VMEM((_RING, ATT_HEADS, d_model // ATT_HEADS + BF16_SUBLANES, blk), BF16),
            pltpu.VMEM((chain, d_model, blk), BF16),
        ],
        compiler_params=pltpu.CompilerParams(
            dimension_semantics=("arbitrary",), vmem_limit_bytes=VMEM_LIMIT),
        name="att_mixer",
    )(*args)


def _att_sample_kernel(*refs, nb, blk, alpha):
    (x_ref, kc_ref, vc_ref, wqkv_ref, wo_ref, bias_c_ref, bias_n_ref, lng_ref, lnb_ref,
     wrh_ref, wrl_ref, br_ref) = refs[:12]
    h_ref, hp_ref, lgt_ref, krow_ref, vrow_ref, o_scr, kd_scr, vd_scr = refs[-8:]
    d_model = x_ref.shape[1]
    heads = ATT_HEADS
    dh = d_model // heads
    n_cache = kd_scr.shape[0]

    def densify(cache_ref, s, dense_ref):
        per_group = LANES // dh
        for grp in range(heads // per_group):
            parts = [cache_ref[s, pl.ds(grp * per_group + sub, n_cache, stride=heads), :] for sub in range(per_group)]
            dense_ref[:, grp * LANES:(grp + 1) * LANES] = jnp.concatenate(parts, axis=1).astype(BF16)

    x = x_ref[...]
    xb = x.astype(BF16)
    q = _dot(xb, wqkv_ref[:, 0:d_model]) * (dh ** -0.5)
    k = _dot(xb, wqkv_ref[:, d_model:2 * d_model])
    v = _dot(xb, wqkv_ref[:, 2 * d_model:3 * d_model])
    krow_ref[...] = k.reshape(nb, blk, d_model)
    vrow_ref[...] = v.reshape(nb, blk, d_model)
    lane_head = lax.broadcasted_iota(jnp.int32, (heads, 1, d_model), 2) // dh
    head_id = lax.broadcasted_iota(jnp.int32, (heads, 1, d_model), 0)
    head_mask = (lane_head == head_id).astype(F32)

    for s in range(nb):
        r0 = s * blk
        qs = q[r0:r0 + blk]
        q_bd = (qs[None, :, :] * head_mask).reshape(heads * blk, d_model).astype(BF16)
        kn = k[r0:r0 + blk].astype(BF16)
        vn = v[r0:r0 + blk].astype(BF16)
        densify(kc_ref, s, kd_scr)
        densify(vc_ref, s, vd_scr)
        s_c = _dot_nt(q_bd, kd_scr[...]) + bias_c_ref[...]
        s_n = _dot_nt(q_bd, kn) + bias_n_ref[...]
        m = jnp.maximum(s_c.max(axis=-1, keepdims=True), s_n.max(axis=-1, keepdims=True))
        p_c = jnp.exp(s_c - m)
        p_n = jnp.exp(s_n - m)
        l = p_c.sum(axis=-1, keepdims=True) + p_n.sum(axis=-1, keepdims=True)
        o_full = _dot(p_c.astype(BF16), vd_scr[...]) + _dot(p_n.astype(BF16), vn)
        o_full = o_full * (1.0 / l)
        o = (o_full.reshape(heads, blk, d_model) * head_mask).sum(axis=0)
        o_scr[r0:r0 + blk, :] = o.astype(BF16)

    mix = _dot(o_scr[...], wo_ref[...])
    _finish_tokens(x, mix, alpha, lng_ref, lnb_ref, wrh_ref, wrl_ref, br_ref, h_ref, hp_ref, lgt_ref)


def _att_sample(x2d, in_row0, n_seq, seq_len, k_cache, v_cache, w_qkv, w_o, rel_table, ln_g, ln_b,
                wr_hi, wr_lo, br, alpha, nb, n_all, out_row0, dst):
    d_model = x2d.shape[1]
    heads = ATT_HEADS
    blk = seq_len
    n_cache = k_cache.shape[1]
    rows = nb * blk
    assert n_seq % nb == 0 and in_row0 % rows == 0 and out_row0 % rows == 0
    in_b0, out_b0 = in_row0 // rows, out_row0 // rows
    bias_c = _rel_bias(rel_table, blk, n_cache, n_cache, 1).reshape(heads * blk, n_cache)
    bias_n = _rel_bias(rel_table, blk, blk, 0, 1).reshape(heads * blk, blk)
    dh = d_model // heads
    kc = k_cache.reshape(n_seq, n_cache * heads, dh)
    vc = v_cache.reshape(n_seq, n_cache * heads, dh)
    cache_spec = pl.BlockSpec((nb, n_cache * heads, dh), lambda g: (g, 0, 0))
    row_spec = pl.BlockSpec((nb, blk, d_model), lambda g: (g, 0, 0))
    rows_out = jax.ShapeDtypeStruct((n_seq, blk, d_model), F32)
    in_specs = [
        pl.BlockSpec((rows, d_model), lambda g: (in_b0 + g, 0)),
        cache_spec,
        cache_spec,
        _const_spec(w_qkv.shape),
        _const_spec(w_o.shape),
        _const_spec(bias_c.shape),
        _const_spec(bias_n.shape),
        _const_spec(ln_g.shape),
        _const_spec(ln_b.shape),
        _const_spec(wr_hi.shape),
        _const_spec(wr_lo.shape),
        _const_spec(br.shape),
    ]
    args = [x2d, kc, vc, w_qkv, w_o, bias_c, bias_n, ln_g, ln_b, wr_hi, wr_lo, br]
    aliases = _alias_dst(args, in_specs, dst)
    return pl.pallas_call(
        functools.partial(_att_sample_kernel, nb=nb, blk=blk, alpha=alpha),
        grid=(n_seq // nb,),
        in_specs=in_specs,
        out_specs=_token_out_specs(rows, d_model, lambda g: (out_b0 + g, 0)) + (row_spec, row_spec),
        out_shape=_token_out_shapes(n_all, d_model) + (rows_out, rows_out),
        input_output_aliases=aliases,
        scratch_shapes=[
            pltpu.VMEM((rows, d_model), BF16),
            pltpu.VMEM((n_cache, d_model), BF16),
            pltpu.VMEM((n_cache, d_model), BF16),
        ],
        compiler_params=pltpu.CompilerParams(
            dimension_semantics=("arbitrary",), vmem_limit_bytes=VMEM_LIMIT),
        name="att_mixer_cache",
    )(*args)


ROUTE_ROWS = 40


def _route_kernel(lgt_ref, meta_t_ref, meta_w_ref, count_ref, carry_ref, tri_ref):
    j = pl.program_id(0)
    tm = lgt_ref.shape[0]

    @pl.when(j == 0)
    def _():
        carry_ref[...] = jnp.zeros(carry_ref.shape, F32)
        r = lax.broadcasted_iota(jnp.int32, (tm, tm), 0)
        c = lax.broadcasted_iota(jnp.int32, (tm, tm), 1)
        tri_ref[...] = (r < c).astype(BF16)

    a = jnp.transpose(lgt_ref[...])[0:ROUTE_ROWS, :]
    row = lax.broadcasted_iota(jnp.int32, (ROUTE_ROWS, tm), 0)
    big = jnp.int32(LANES)
    glog = jnp.where(row < N_GROUPS, a, NEG_INF)
    gmax = glog.max(axis=0, keepdims=True)
    gsel = jnp.where(glog == gmax, row, big).min(axis=0, keepdims=True)
    gp = 1.0 / jnp.exp(glog - gmax).sum(axis=0, keepdims=True)
    first = N_GROUPS + gsel * EXP_PER_GROUP
    in_grp = (row >= first) & (row < first + EXP_PER_GROUP)
    elog = jnp.where(in_grp, a, NEG_INF)
    v1 = elog.max(axis=0, keepdims=True)
    l1 = jnp.where(elog == v1, row, big).min(axis=0, keepdims=True)
    elog2 = jnp.where(row == l1, NEG_INF, elog)
    v2 = elog2.max(axis=0, keepdims=True)
    l2 = jnp.where(elog2 == v2, row, big).min(axis=0, keepdims=True)
    e2 = jnp.exp(v2 - v1)
    w1 = gp / (1.0 + e2)
    w2 = gp * e2 / (1.0 + e2)

    is1 = row == l1
    is2 = row == l2
    oh = (is1 | is2).astype(F32)
    before = _dot(oh.astype(BF16), tri_ref[...]) + carry_ref[:, 0:1]
    rank1 = jnp.where(is1, before, 0.0).sum(axis=0, keepdims=True)
    rank2 = jnp.where(is2, before, 0.0).sum(axis=0, keepdims=True)
    carry_ref[...] = carry_ref[...] + oh.sum(axis=1, keepdims=True)
    count_ref[...] = carry_ref[...].astype(jnp.int32)

    pad = jnp.zeros((SUBLANES - 4, tm), jnp.int32)
    meta_t_ref[...] = jnp.concatenate(
        [l1 - N_GROUPS, l2 - N_GROUPS, rank1.astype(jnp.int32), rank2.astype(jnp.int32), pad], axis=0)
    wt = jnp.concatenate([w1, w2, jnp.zeros((LANES - 2, tm), F32)], axis=0)
    meta_w_ref[...] = jnp.transpose(wt)


def _route(logits):
    n_tok = logits.shape[0]
    tm = TOKEN_TILE
    assert n_tok % tm == 0
    tile = pl.BlockSpec((tm, LANES), lambda j: (j, 0))
    return pl.pallas_call(
        _route_kernel,
        grid=(n_tok // tm,),
        in_specs=[tile],
        out_specs=(pl.BlockSpec((SUBLANES, tm), lambda j: (0, j)), tile,
                   pl.BlockSpec((ROUTE_ROWS, LANES), lambda j: (0, 0))),
        out_shape=(
            jax.ShapeDtypeStruct((SUBLANES, n_tok), jnp.int32),
            jax.ShapeDtypeStruct((n_tok, LANES), F32),
            jax.ShapeDtypeStruct((ROUTE_ROWS, LANES), jnp.int32),
        ),
        scratch_shapes=[pltpu.VMEM((ROUTE_ROWS, LANES), F32), pltpu.VMEM((tm, tm), BF16)],
        compiler_params=pltpu.CompilerParams(dimension_semantics=("arbitrary",)),
        name="moe_route",
    )(logits)


def _sc_gather_loop(table_hbm, idx_v, out_hbm, rows_v, sems, base, n_chunk, r):
    def gather(c, slot):
        off = pl.multiple_of(c * r, r)
        return pltpu.make_async_copy(table_hbm.at[idx_v.at[pl.ds(off, r)]], rows_v.at[slot], sems.at[slot])

    def finish(c, slot):
        gather(c, slot).wait()
        pltpu.sync_copy(rows_v.at[slot], out_hbm.at[pl.ds(base + pl.multiple_of(c * r, r), r)])

    gather(0, 0).start()
    if n_chunk > 1:
        gather(1, 1).start()

    def body(pair, carry):
        c = 2 * pair
        for slot in range(2):
            finish(c + slot, slot)

            @pl.when(c + slot + 2 < n_chunk)
            def _():
                gather(c + slot + 2, slot).start()
        return carry

    lax.fori_loop(0, n_chunk // 2, body, 0)
    if n_chunk % 2:
        finish(n_chunk - 1, 0)


def _gather_rows(table, idx):
    m = idx.shape[0]
    width = table.shape[1]
    r = SC_ROWS_PER_CHUNK
    assert m % (SC_WORKERS * r) == 0
    per_w = m // SC_WORKERS
    mesh = plsc.VectorSubcoreMesh(core_axis_name="c", subcore_axis_name="s")

    @functools.partial(
        pl.kernel,
        mesh=mesh,
        out_type=jax.ShapeDtypeStruct((m, width), table.dtype),
        scratch_types=[
            pltpu.VMEM((per_w,), jnp.int32),
            pltpu.VMEM((2, r, width), table.dtype),
            pltpu.SemaphoreType.DMA((2,)),
        ],
    )
    def gather(table_hbm, idx_hbm, out_hbm, idx_v, rows_v, sem):
        wid = lax.axis_index("s") * 2 + lax.axis_index("c")
        base = wid * per_w
        pltpu.sync_copy(idx_hbm.at[pl.ds(base, per_w)], idx_v)
        _sc_gather_loop(table_hbm, idx_v, out_hbm, rows_v, sem, base, per_w // r, r)

    return gather(table, idx)


def _dispatch_rows(table, pos, row0, n_rows):
    n_tok, width = table.shape
    n_pairs = pos.shape[0]
    r = SC_ROWS_PER_CHUNK
    lanes = SC_LANES
    assert n_rows % (SC_WORKERS * r) == 0
    per_w = n_rows // SC_WORKERS
    n_stage = 16
    stage = n_pairs // n_stage
    assert stage * n_stage == n_pairs and stage % lanes == 0 and per_w % lanes == 0
    assert row0 + n_rows < 3 * n_tok
    mesh = plsc.VectorSubcoreMesh(core_axis_name="c", subcore_axis_name="s")

    @functools.partial(
        pl.kernel,
        mesh=mesh,
        out_type=jax.ShapeDtypeStruct((n_rows, width), table.dtype),
        scratch_types=[
            pltpu.VMEM((per_w,), jnp.int32),
            pltpu.VMEM((stage,), jnp.int32),
            pltpu.VMEM((2, r, width), table.dtype),
            pltpu.SemaphoreType.DMA((2,)),
        ],
        compiler_params=pltpu.CompilerParams(needs_layout_passes=False),
    )
    def dispatch(table_hbm, pos_hbm, out_hbm, src_v, pos_v, rows_v, sem):
        wid = lax.axis_index("s") * 2 + lax.axis_index("c")
        out_base = wid * per_w
        base = row0 + out_base
        lane = lax.iota(jnp.int32, lanes)

        def wrap(t):
            t = jnp.where(t >= n_tok, t - n_tok, t)
            return jnp.where(t >= n_tok, t - n_tok, t)

        def init(i, carry):
            off = pl.multiple_of(i * lanes, lanes)
            src_v[pl.ds(off, lanes)] = wrap(base + off + lane)
            return carry

        lax.fori_loop(0, per_w // lanes, init, 0)

        def scan_stage(sidx, carry):
            pair0 = sidx * stage
            pltpu.sync_copy(pos_hbm.at[pl.ds(pl.multiple_of(pair0, 8), stage)], pos_v)

            @plsc.parallel_loop(0, stage // lanes, unroll=4)
            def _(i):
                off = pl.multiple_of(i * lanes, lanes)
                local = pos_v[pl.ds(off, lanes)] - base
                mine = (local >= 0) & (local < per_w)
                plsc.store_scatter(src_v, [jnp.where(mine, local, 0)], wrap(pair0 + off + lane), mask=mine)

            return carry

        lax.fori_loop(0, n_stage, scan_stage, 0)
        _sc_gather_loop(table_hbm, src_v, out_hbm, rows_v, sem, out_base, per_w // r, r)

    return dispatch(table, pos)


def _gemm_kernel(tile_expert_ref, n_used_ref, xs_ref, wg_ref, wu_ref, wd_ref, *rest, tile0):
    ys_ref = rest[-1]
    tile = tile0 + pl.program_id(0)

    @pl.when(tile < n_used_ref[0])
    def _():
        x = _unpack_rows(xs_ref[...]).astype(BF16)
        g = _dot(x, wg_ref[...])
        u = _dot(x, wu_ref[...])
        hmid = (jax.nn.silu(g) * u).astype(BF16)
        ys_ref[...] = _pack_rows(_dot(hmid, wd_ref[...]))


def _expert_gemm(xs, tile_expert, n_used, w_gate, w_up, w_down, tile0, n_rows_all, ys):
    n_rows, half = xs.shape
    d_model, d_exp = w_gate.shape[1], w_gate.shape[2]
    tm = GEMM_TILE
    n_tiles = n_rows // tm

    def last_used(j, nu):
        return jnp.minimum(tile0 + j, nu[0] - 1)

    def in_map(j, te, nu):
        return (jnp.maximum(last_used(j, nu) - tile0, 0), 0)

    def out_map(j, te, nu):
        return (jnp.maximum(last_used(j, nu), tile0), 0)

    def w_map(j, te, nu):
        return (te[tile0 + j], 0, 0)

    in_specs = [
        pl.BlockSpec((tm, half), in_map),
        pl.BlockSpec((None, d_model, d_exp), w_map),
        pl.BlockSpec((None, d_model, d_exp), w_map),
        pl.BlockSpec((None, d_exp, d_model), w_map),
    ]
    args = [tile_expert, n_used, xs, w_gate, w_up, w_down]
    aliases = {}
    if ys is not None:
        aliases[len(args)] = 0
        args.append(ys)
        in_specs.append(pl.BlockSpec(memory_space=pl.ANY))
    grid_spec = pltpu.PrefetchScalarGridSpec(
        num_scalar_prefetch=2,
        grid=(n_tiles,),
        in_specs=in_specs,
        out_specs=pl.BlockSpec((tm, half), out_map),
    )
    return pl.pallas_call(
        functools.partial(_gemm_kernel, tile0=tile0),
        grid_spec=grid_spec,
        out_shape=jax.ShapeDtypeStruct((n_rows_all, half), U32),
        input_output_aliases=aliases,
        compiler_params=pltpu.CompilerParams(
            dimension_semantics=("arbitrary",), vmem_limit_bytes=VMEM_LIMIT),
        name="moe_gemm",
    )(*args)


def _combine_kernel(y0_ref, y1_ref, mw_ref, h_ref, pp_ref, ps_ref, lng_ref, lnb_ref, wproj_ref, wgate_ref,
                    *out_refs, n_prompt_tiles, alpha):
    j = pl.program_id(0)
    tm = h_ref.shape[0]
    half = tm // 2

    def normed(rows):
        mw = mw_ref[rows, :]
        ffn = mw[:, 0:1] * _unpack_rows(y0_ref[rows, :]) + mw[:, 1:2] * _unpack_rows(y1_ref[rows, :])
        return _layer_norm(alpha * h_ref[rows, :] + ffn, lng_ref[...], lnb_ref[...])

    def gated(rows, h2):
        p = jnp.where(j < n_prompt_tiles, pp_ref[rows, :], ps_ref[rows, :]).astype(BF16)
        gate = jax.nn.sigmoid(_dot(h2.astype(BF16), wgate_ref[...]))
        return h2 + gate * _dot(p, wproj_ref[...])

    rows_a, rows_b = slice(0, half), slice(half, tm)
    h2_a = normed(rows_a)
    h2_b = normed(rows_b)
    out = jnp.concatenate([gated(rows_a, h2_a), gated(rows_b, h2_b)], axis=0)
    if len(out_refs) == 1:
        out_refs[0][...] = out
    else:
        @pl.when(j < n_prompt_tiles)
        def _():
            out_refs[0][...] = out

        @pl.when(j >= n_prompt_tiles)
        def _():
            out_refs[1][...] = out


def _combine(yg, meta_w, h, p_prompt, p_sample, layer, ln_g, ln_b, w_proj, w_gate, alpha, split_out):
    n_tok, d_model = h.shape
    tm = TOKEN_TILE
    n_t = n_tok // tm
    n_tp = p_prompt.shape[1] // tm
    n_ts = p_sample.shape[1] // tm
    assert n_tp * tm == p_prompt.shape[1] and n_ts * tm == p_sample.shape[1] and n_tp + n_ts == n_t
    ple = p_prompt.shape[2]
    tile = pl.BlockSpec((tm, d_model), lambda j: (j, 0))
    if split_out:
        out_specs = (pl.BlockSpec((tm, d_model), lambda j: (jnp.minimum(j, n_tp - 1), 0)),
                     pl.BlockSpec((tm, d_model), lambda j: (jnp.maximum(j - n_tp, 0), 0)))
        out_shape = (jax.ShapeDtypeStruct((n_tp * tm, d_model), F32),
                     jax.ShapeDtypeStruct((n_ts * tm, d_model), F32))
    else:
        out_specs = tile
        out_shape = jax.ShapeDtypeStruct((n_tok, d_model), F32)
    return pl.pallas_call(
        functools.partial(_combine_kernel, n_prompt_tiles=n_tp, alpha=alpha),
        grid=(n_t,),
        in_specs=[
            pl.BlockSpec((tm, d_model // 2), lambda j: (j, 0)),
            pl.BlockSpec((tm, d_model // 2), lambda j: (j + n_t, 0)),
            pl.BlockSpec((tm, LANES), lambda j: (j, 0)),
            tile,
            pl.BlockSpec((None, tm, ple), lambda j: (layer, jnp.minimum(j, n_tp - 1), 0)),
            pl.BlockSpec((None, tm, ple), lambda j: (layer, jnp.maximum(j - n_tp, 0), 0)),
            _const_spec(ln_g.shape),
            _const_spec(ln_b.shape),
            _const_spec(w_proj.shape),
            _const_spec(w_gate.shape),
        ],
        out_specs=out_specs,
        out_shape=out_shape,
        compiler_params=pltpu.CompilerParams(
            dimension_semantics=("arbitrary",), vmem_limit_bytes=VMEM_LIMIT),
        name="moe_combine",
    )(yg, yg, meta_w, h, p_prompt, p_sample, ln_g, ln_b, w_proj, w_gate)


def _router_weights(w_grp, b_grp, w_exp, b_exp):
    d_model = w_grp.shape[0]
    w = jnp.concatenate([w_grp, jnp.transpose(w_exp, (1, 0, 2)).reshape(d_model, N_EXPERTS)], axis=1)
    w = jnp.pad(w, ((0, 0), (0, LANES - w.shape[1])))
    b = jnp.concatenate([b_grp, b_exp.reshape(N_EXPERTS)])
    b = jnp.pad(b, (0, LANES - b.shape[0])).reshape(1, LANES)
    w_hi = w.astype(BF16)
    w_lo = (w - w_hi.astype(F32)).astype(BF16)
    return w_hi, w_lo, b


def _moe(h_packed, logits, w_gate, w_up, w_down):
    n_tok = h_packed.shape[0]
    tm = GEMM_TILE
    meta_t, meta_w, counts = _route(logits)
    counts = counts[N_GROUPS:N_GROUPS + N_EXPERTS, 0]
    tiles_per_expert = (counts + tm - 1) // tm
    tile_end = jnp.cumsum(tiles_per_expert)
    row_start = (tile_end - tiles_per_expert) * tm
    eid = meta_t[0:2]
    rank = meta_t[2:4]
    experts = jnp.arange(N_EXPERTS, dtype=jnp.int32)
    start = jnp.sum(jnp.where(eid[:, :, None] == experts, row_start, 0), axis=-1)
    pos = (start + rank).reshape(-1).astype(jnp.int32)
    part_quant = MOE_PARTS * SC_WORKERS * SC_ROWS_PER_CHUNK * (tm // math.gcd(tm, SC_WORKERS * SC_ROWS_PER_CHUNK))
    n_tiles = -(-(2 * n_tok) // tm) + N_EXPERTS
    n_rows = -(-(n_tiles * tm) // part_quant) * part_quant
    n_tiles = n_rows // tm
    n_used = tile_end[-1:].astype(jnp.int32)
    tile_ids = jnp.minimum(jnp.arange(n_tiles, dtype=jnp.int32), n_used[0] - 1)
    tile_expert = jnp.sum(tile_end[None, :] <= tile_ids[:, None], axis=1).astype(jnp.int32)
    part_rows = n_rows // MOE_PARTS
    xs_parts = [_dispatch_rows(h_packed, pos, part * part_rows, part_rows) for part in range(MOE_PARTS)]
    ys = None
    for part, xs in enumerate(xs_parts):
        ys = _expert_gemm(xs, tile_expert, n_used, w_gate, w_up, w_down,
                          part * (part_rows // tm), n_rows, ys)
    return _gather_rows(ys, pos), meta_w


def kernel(x_prompt, x_sample, p_prompt, p_sample, state_ret, cache_att_k, cache_att_v, ret_w_in, ret_gn_g,
           ret_w_o, att_w_qkv, att_rel_bias, att_w_o, ln1_g, ln1_b, ln2_g, ln2_b, moe_w_grp, moe_b_grp,
           moe_w_exp, moe_b_exp, moe_w_gate, moe_w_up, moe_w_down, ple_w_proj, ple_w_gate):
    n_p, len_p, d_model = x_prompt.shape
    n_s, len_s, _ = x_sample.shape
    depth = ln1_g.shape[0]
    alpha = float((2 * depth) ** 0.25)
    tok_p, tok_s = n_p * len_p, n_s * len_s
    n_all = tok_p + tok_s
    dh = d_model // ATT_HEADS
    pp = p_prompt.reshape(depth, tok_p, -1)
    ps = p_sample.reshape(depth, tok_s, -1)
    nb_s = 4
    nb_cache = 2

    x_all = None
    y_prompt = y_sample = None
    states_p, states_s, k_p, v_p, k_s, v_s = [], [], [], [], [], []
    for i in range(depth):
        jj = i // 2
        wr_hi, wr_lo, br = _router_weights(moe_w_grp[i], moe_b_grp[i], moe_w_exp[i], moe_b_exp[i])
        lng, lnb = ln1_g[i].reshape(1, d_model), ln1_b[i].reshape(1, d_model)
        moe_cast = (i, [moe_w_gate, moe_w_up, moe_w_down])
        if x_all is None:
            src_p, src_s, row_s = x_prompt.reshape(tok_p, d_model), x_sample.reshape(tok_s, d_model), 0
        else:
            src_p, src_s, row_s = x_all, x_all, tok_p
        if i % 2 == 0:
            w_in = ret_w_in[jj].astype(BF16)
            w_o = ret_w_o[jj].astype(BF16)
            gn = ret_gn_g[jj].reshape(1, -1)
            h, hp, lgt, st_p, *w_moe = _ret_mixer(
                src_p, 0, n_p, len_p, 0, None, w_in, w_o, gn, lng, lnb, wr_hi, wr_lo, br, alpha, nb=1,
                blk=min(RET_BLOCK, len_p), chain=RET_CHAIN, n_all=n_all, out_row0=0, dst=None, cast=moe_cast)
            h, hp, lgt, st_s = _ret_mixer(src_s, row_s, n_s, len_s, PAST_LEN, state_ret[jj], w_in, w_o, gn,
                                          lng, lnb, wr_hi, wr_lo, br, alpha, nb=nb_s, blk=len_s, chain=1,
                                          n_all=n_all, out_row0=tok_p, dst=(h, hp, lgt))
            states_p.append(st_p)
            states_s.append(st_s)
        else:
            w_qkv = att_w_qkv[jj].astype(BF16)
            w_o = att_w_o[jj].astype(BF16)
            h, hp, lgt, kr, vr, *w_moe = _att_prompt(src_p, n_p, len_p, w_qkv, w_o, att_rel_bias[jj], lng, lnb,
                                                     wr_hi, wr_lo, br, alpha, n_all, cast=moe_cast)
            k_p.append(kr.reshape(n_p, -1, ATT_HEADS, dh))
            v_p.append(vr.reshape(n_p, -1, ATT_HEADS, dh))
            h, hp, lgt, kr, vr = _att_sample(src_s, row_s, n_s, len_s, cache_att_k[jj], cache_att_v[jj],
                                             w_qkv, w_o, att_rel_bias[jj], lng, lnb, wr_hi, wr_lo, br, alpha,
                                             nb=nb_cache, n_all=n_all, out_row0=tok_p, dst=(h, hp, lgt))
            k_s.append(kr.reshape(n_s, len_s, ATT_HEADS, dh))
            v_s.append(vr.reshape(n_s, len_s, ATT_HEADS, dh))
        n_exp = moe_w_gate.shape[1]
        w_moe = [w.reshape(n_exp, -1, w.shape[1]) for w in w_moe]
        yg, meta_w = _moe(hp, lgt, *w_moe)
        last = i == depth - 1
        out = _combine(yg, meta_w, h, pp, ps, i, ln2_g[i].reshape(1, d_model), ln2_b[i].reshape(1, d_model),
                       ple_w_proj[i].astype(BF16), ple_w_gate[i].astype(BF16), alpha, split_out=last)
        if last:
            y_prompt = out[0].reshape(n_p, len_p, d_model)
            y_sample = out[1].reshape(n_s, len_s, d_model)
        else:
            x_all = out

    return (y_prompt, y_sample, jnp.stack(states_p), jnp.stack(states_s),
            jnp.stack(k_p), jnp.stack(v_p), jnp.stack(k_s), jnp.stack(v_s))
```

```python
import functools
import math

import numpy as np
import jax
import jax.numpy as jnp
from jax import lax
from jax.experimental import pallas as pl
from jax.experimental.pallas import tpu as pltpu
from jax.experimental.pallas import tpu_sc as plsc

CHUNK = 64
PAST_LEN = 2048
RET_HEADS = 4
ROPE_BASE = 10000.0
ATT_HEADS = 16
BAND_CHUNKS = 8
REL_CLIP = 256
N_GROUPS = 4
EXP_PER_GROUP = 8
N_EXPERTS = N_GROUPS * EXP_PER_GROUP
LN_EPS = 1e-5
NEG_INF = -1e30

LANES = 128
SUBLANES = 8
BF16_SUBLANES = 16
SC_WORKERS = 32
SC_LANES = 16
SC_ROWS_PER_CHUNK = 32
VMEM_LIMIT = 58 * 1024 * 1024

RET_BLOCK = 256
RET_CHAIN = 2
ATT_BLOCK = 4 * CHUNK
ATT_CHAIN = 2
TOKEN_TILE = 512
GEMM_TILE = 512
MOE_PARTS = 3

F32 = jnp.float32
BF16 = jnp.bfloat16
U32 = jnp.uint32
HI_MASK = 0xFFFF0000
LOG2E = math.log2(math.e)


def _dot(a, b):
    return jnp.dot(a, b, preferred_element_type=F32)


def _dot_nt(a, b):
    return lax.dot_general(a, b, (((1,), (1,)), ((), ())), preferred_element_type=F32)


def _dot_tn(a, b):
    return lax.dot_general(a, b, (((0,), (0,)), ((), ())), preferred_element_type=F32)


def _layer_norm(x, g, b):
    mu = jnp.mean(x, axis=-1, keepdims=True)
    xc = x - mu
    var = jnp.mean(xc * xc, axis=-1, keepdims=True)
    return xc * lax.rsqrt(var + LN_EPS) * g + b


def _pack_rows(x):
    half = x.shape[1] // 2
    lo = lax.bitcast_convert_type(x[:, :half].astype(BF16).astype(F32), U32) >> 16
    hi = lax.bitcast_convert_type(x[:, half:].astype(BF16).astype(F32), U32) & U32(HI_MASK)
    return lo | hi


def _unpack_rows(p):
    lo = lax.bitcast_convert_type(p << 16, F32)
    hi = lax.bitcast_convert_type(p & U32(HI_MASK), F32)
    return jnp.concatenate([lo, hi], axis=1)


def _router_logits(h, wr_hi_ref, wr_lo_ref, br_ref):
    h_hi = h.astype(BF16)
    h_lo = (h - h_hi.astype(F32)).astype(BF16)
    w_hi = wr_hi_ref[...]
    return _dot(h_hi, w_hi) + _dot(h_lo, w_hi) + _dot(h_hi, wr_lo_ref[...]) + br_ref[...]


def _finish_tokens(x, mix, alpha, lng_ref, lnb_ref, wrh_ref, wrl_ref, br_ref, h_ref, hp_ref, lgt_ref):
    hh = _layer_norm(alpha * x + mix, lng_ref[...], lnb_ref[...])
    h_ref[...] = hh
    hp_ref[...] = _pack_rows(hh)
    lgt_ref[...] = _router_logits(hh, wrh_ref, wrl_ref, br_ref)


def _zero_tokens(h_ref, hp_ref, lgt_ref):
    h_ref[...] = jnp.zeros(h_ref.shape, h_ref.dtype)
    hp_ref[...] = jnp.zeros(hp_ref.shape, hp_ref.dtype)
    lgt_ref[...] = jnp.zeros(lgt_ref.shape, lgt_ref.dtype)


def _const_spec(shape):
    nd = len(shape)
    return pl.BlockSpec(shape, lambda *_: (0,) * nd, pipeline_mode=pl.Buffered(1))


def _token_out_shapes(n_all, d_model):
    return (
        jax.ShapeDtypeStruct((n_all, d_model), F32),
        jax.ShapeDtypeStruct((n_all, d_model // 2), U32),
        jax.ShapeDtypeStruct((n_all, LANES), F32),
    )


def _token_out_specs(rows, d_model, row_map):
    return (
        pl.BlockSpec((rows, d_model), row_map),
        pl.BlockSpec((rows, d_model // 2), row_map),
        pl.BlockSpec((rows, LANES), row_map),
    )


def _alias_dst(args, in_specs, dst):
    if dst is None:
        return {}
    aliases = {}
    for k, arr in enumerate(dst):
        aliases[len(args)] = k
        args.append(arr)
        in_specs.append(pl.BlockSpec(memory_space=pl.ANY))
    return aliases


def _cast_plumbing(args, in_specs, cast, n_steps, work):
    if cast is None:
        return (), ()
    layer, weights = cast
    shapes, specs = [], []
    for w in weights:
        depth, n_exp, rows, cols = w.shape
        assert (n_exp * rows) % n_steps == 0
        slab = (n_exp * rows) // n_steps
        args.append(w.reshape(depth, n_exp * rows, cols))
        in_specs.append(pl.BlockSpec((None, slab, cols), lambda t: (layer, work(t), 0)))
        shapes.append(jax.ShapeDtypeStruct((n_exp * rows, cols), BF16))
        specs.append(pl.BlockSpec((slab, cols), lambda t: (work(t), 0)))
    return tuple(shapes), tuple(specs)


def _cast_slabs(in_refs, out_refs):
    for src, dst in zip(in_refs, out_refs):
        dst[...] = src[...].astype(BF16)


def _ret_log_gamma():
    h = np.arange(RET_HEADS, dtype=np.float32)
    return np.log(np.float32(1.0) - np.float32(2.0) ** (np.float32(-5.0) - h)).astype(np.float32)


def _ret_kernel(*refs, nb, blk, chain, nblk, n_steps, n_fill, has_state, n_alias, n_cast, alpha):
    step = pl.program_id(0)
    n_in = 12 + int(has_state) + n_alias + n_cast
    h_ref, hp_ref, lgt_ref = refs[n_in:n_in + 3]
    core = refs[:12 + int(has_state)] + refs[n_in:n_in + 4] + refs[-1:]

    @pl.when(step < n_steps)
    def _():
        _cast_slabs(refs[n_in - n_cast:n_in], refs[n_in + 4:n_in + 4 + n_cast])
        _ret_step(*core, first_block=step % nblk == 0, nb=nb, blk=blk, chain=chain, has_state=has_state,
                  alpha=alpha)

    if n_fill:
        @pl.when(step >= n_steps)
        def _():
            _zero_tokens(h_ref, hp_ref, lgt_ref)


def _ret_step(*refs, first_block, nb, blk, chain, has_state, alpha):
    (x_ref, cos_ref, sin_ref, dmask_ref, win_ref, wo_ref, gn_ref, lng_ref, lnb_ref,
     wrh_ref, wrl_ref, br_ref) = refs[:12]
    s_in_ref = refs[12] if has_state else None
    h_ref, hp_ref, lgt_ref, s_out_ref, gated_ref = refs[12 + int(has_state):]
    heads = RET_HEADS
    d_model = x_ref.shape[1]
    dk = d_model // heads
    dv = 2 * d_model // heads
    hk, hv = heads * dk, heads * dv
    half = dk // 2
    lg = _ret_log_gamma()

    n_sub = nb * chain
    proj_rows = blk if chain > 1 else n_sub * blk
    rowf = lax.broadcasted_iota(jnp.int32, (blk, 1), 0).astype(F32)

    def proj(group):
        xb = x_ref[group * proj_rows:(group + 1) * proj_rows, :].astype(BF16)
        return (_dot(xb, win_ref[:, 0:hk]), _dot(xb, win_ref[:, hk:2 * hk]),
                _dot(xb, win_ref[:, 2 * hk:2 * hk + hv]), _dot(xb, win_ref[:, 2 * hk + hv:2 * hk + 2 * hv]))

    if not has_state:
        @pl.when(first_block)
        def _():
            s_out_ref[...] = jnp.zeros(s_out_ref.shape, F32)

    s_prev_ref = s_in_ref if has_state else s_out_ref

    def head(j, h, projected):
        q_all, k_all, v_all, g_all = projected
        s = j // chain
        c0 = (j % chain) * blk
        r0 = (j * blk) % proj_rows
        cos = cos_ref[c0:c0 + blk, :]
        sin = sin_ref[c0:c0 + blk, :]

        def rot(t):
            t1, t2 = t[:, :half], t[:, half:]
            return jnp.concatenate([t1 * cos - t2 * sin, t1 * sin + t2 * cos], axis=1)

        lgh = float(lg[h])
        q = rot(q_all[r0:r0 + blk, h * dk:(h + 1) * dk])
        k = rot(k_all[r0:r0 + blk, h * dk:(h + 1) * dk]) * (dk ** -0.5)
        v = v_all[r0:r0 + blk, h * dv:(h + 1) * dv]
        g = g_all[r0:r0 + blk, h * dv:(h + 1) * dv]
        vb = v.astype(BF16)
        scores = _dot_nt(q.astype(BF16), k.astype(BF16)) * dmask_ref[h]
        inner = _dot(scores.astype(BF16), vb)
        s_prev = s_prev_ref[s, h]
        q_dec = q * jnp.exp(lgh * (rowf + 1.0))
        cross = _dot(q_dec.astype(BF16), s_prev.astype(BF16))
        k_dec = k * jnp.exp(lgh * (float(blk - 1) - rowf))
        s_out_ref[s, h] = math.exp(lgh * blk) * s_prev + _dot_tn(k_dec.astype(BF16), vb)
        o = inner + cross
        mu = jnp.mean(o, axis=-1, keepdims=True)
        oc = o - mu
        var = jnp.mean(oc * oc, axis=-1, keepdims=True)
        on = oc * lax.rsqrt(var + LN_EPS) * gn_ref[:, h * dv:(h + 1) * dv]
        gated_ref[j * blk:(j + 1) * blk, h * dv:(h + 1) * dv] = (jax.nn.silu(g) * on).astype(BF16)

    def tail(group):
        rows = slice(group * proj_rows, (group + 1) * proj_rows)
        mix = _dot(gated_ref[rows, :], wo_ref[...])
        _finish_tokens(x_ref[rows, :], mix, alpha, lng_ref, lnb_ref, wrh_ref, wrl_ref, br_ref,
                       h_ref.at[rows, :], hp_ref.at[rows, :], lgt_ref.at[rows, :])

    n_groups = (n_sub * blk) // proj_rows
    subs_per_group = proj_rows // blk
    projected = proj(0)
    for group in range(n_groups):
        nxt = None
        for jj in range(subs_per_group):
            for h in range(heads):
                head(group * subs_per_group + jj, h, projected)
                if jj == 0 and h == 0 and group + 1 < n_groups:
                    nxt = proj(group + 1)
                if jj == 0 and h == 1 and group > 0:
                    tail(group - 1)
        projected = nxt
    tail(n_groups - 1)


def _ret_mixer(x2d, in_row0, n_seq, seq_len, pos0, state_in, w_in, w_o, gn_g, ln_g, ln_b,
               wr_hi, wr_lo, br, alpha, nb, blk, chain, n_all, out_row0, dst, cast=None):
    d_model = x2d.shape[1]
    heads = RET_HEADS
    dk, dv = d_model // heads, 2 * d_model // heads
    half = dk // 2
    nblk = seq_len // (blk * chain)
    has_state = state_in is not None
    rows = nb * blk * chain
    assert seq_len % (blk * chain) == 0 and n_seq % nb == 0
    assert (not has_state) or nblk == 1
    assert nb == 1 or (nblk == 1 and chain == 1)
    assert in_row0 % rows == 0 and out_row0 % rows == 0
    in_b0, out_b0 = in_row0 // rows, out_row0 // rows

    pos = (pos0 + jnp.arange(seq_len, dtype=jnp.int32)).astype(F32)
    inv_freq = ROPE_BASE ** (-jnp.arange(half, dtype=F32) / half)
    ang = pos[:, None] * inv_freq[None, :]
    cos, sin = jnp.cos(ang), jnp.sin(ang)
    lg = jnp.asarray(_ret_log_gamma())
    ii = jnp.arange(blk, dtype=F32)
    diff = ii[:, None] - ii[None, :]
    dmask = jnp.where(diff >= 0, jnp.exp(lg[:, None, None] * jnp.maximum(diff, 0.0)), 0.0)

    n_steps = (n_seq // nb) * nblk
    n_fill = 0 if dst is not None else (n_all - n_seq * seq_len) // rows
    assert dst is not None or (out_row0 == 0 and n_fill * rows == n_all - n_seq * seq_len)

    def work(t):
        return jnp.minimum(t, n_steps - 1)

    in_specs = [
        pl.BlockSpec((rows, d_model), lambda t: (in_b0 + work(t), 0)),
        pl.BlockSpec((blk * chain, half), lambda t: (work(t) % nblk, 0)),
        pl.BlockSpec((blk * chain, half), lambda t: (work(t) % nblk, 0)),
        _const_spec(dmask.shape),
        _const_spec(w_in.shape),
        _const_spec(w_o.shape),
        _const_spec(gn_g.shape),
        _const_spec(ln_g.shape),
        _const_spec(ln_b.shape),
        _const_spec(wr_hi.shape),
        _const_spec(wr_lo.shape),
        _const_spec(br.shape),
    ]
    args = [x2d, cos, sin, dmask, w_in, w_o, gn_g, ln_g, ln_b, wr_hi, wr_lo, br]
    state_spec = pl.BlockSpec((nb, heads, dk, dv), lambda t: (work(t) // nblk, 0, 0, 0))
    if has_state:
        in_specs.append(state_spec)
        args.append(state_in)
    aliases = _alias_dst(args, in_specs, dst)
    cast_shapes, cast_specs = _cast_plumbing(args, in_specs, cast, n_steps, work)
    out_shape = (_token_out_shapes(n_all, d_model) + (jax.ShapeDtypeStruct((n_seq, heads, dk, dv), F32),)
                 + cast_shapes)
    out_specs = _token_out_specs(rows, d_model, lambda t: (out_b0 + t, 0)) + (state_spec,) + cast_specs
    return pl.pallas_call(
        functools.partial(_ret_kernel, nb=nb, blk=blk, chain=chain, nblk=nblk, n_steps=n_steps, n_fill=n_fill,
                          has_state=has_state, n_alias=len(aliases), n_cast=len(cast_shapes), alpha=alpha),
        grid=(n_steps + n_fill,),
        in_specs=in_specs,
        out_specs=out_specs,
        out_shape=out_shape,
        input_output_aliases=aliases,
        scratch_shapes=[pltpu.VMEM((rows, heads * dv), BF16)],
        compiler_params=pltpu.CompilerParams(
            dimension_semantics=("arbitrary",), vmem_limit_bytes=VMEM_LIMIT),
        name="ret_mixer_state" if has_state else "ret_mixer",
    )(*args)


_RING = 3


def _att_prompt_kernel(*refs, blk, chain, nblk, n_steps, n_fill, n_cast, alpha):
    step = pl.program_id(0)
    n_in = 9 + n_cast
    h_ref, hp_ref, lgt_ref = refs[n_in:n_in + 3]
    core = refs[:9] + refs[n_in:n_in + 5] + refs[-3:]

    @pl.when(step < n_steps)
    def _():
        _cast_slabs(refs[9:n_in], refs[n_in + 5:n_in + 5 + n_cast])
        _att_prompt_step(*core, i0=(step % nblk) * chain, blk=blk, chain=chain, alpha=alpha)

    if n_fill:
        @pl.when(step >= n_steps)
        def _():
            _zero_tokens(h_ref, hp_ref, lgt_ref)


def _att_prompt_step(x_ref, wqkv_ref, wo_ref, bias_ref, lng_ref, lnb_ref, wrh_ref, wrl_ref, br_ref,
                     h_ref, hp_ref, lgt_ref, krow_ref, vrow_ref, kring, vring, o_scr, *, i0, blk, chain, alpha):
    d_model = x_ref.shape[1]
    dh = d_model // ATT_HEADS
    heads_per_group = LANES // dh
    lane = lax.broadcasted_iota(jnp.int32, (1, LANES), 1)

    @pl.when(i0 == 0)
    def _():
        kring[...] = jnp.zeros(kring.shape, BF16)
        vring[...] = jnp.zeros(vring.shape, BF16)
        vring[:, :, dh:, :] = jnp.ones((_RING, ATT_HEADS, vring.shape[2] - dh, blk), BF16)

    def project(j):
        xb = x_ref[j * blk:(j + 1) * blk, :].astype(BF16)
        q = _dot(xb, wqkv_ref[:, 0:d_model]) * (dh ** -0.5 * LOG2E)
        k = _dot(xb, wqkv_ref[:, d_model:2 * d_model])
        v = _dot(xb, wqkv_ref[:, 2 * d_model:3 * d_model])
        return q, k, v

    def to_ring(j, k, v):
        slot = (i0 + j) % _RING
        krow_ref[j * blk:(j + 1) * blk, :] = k
        vrow_ref[j * blk:(j + 1) * blk, :] = v
        kring[slot] = k.astype(BF16)
        v_t = jnp.transpose(v).astype(BF16)
        for hd in range(ATT_HEADS):
            vring[slot, hd, 0:dh, :] = v_t[hd * dh:(hd + 1) * dh, :]

    def attend(j, q, between):
        i = i0 + j
        qb = q.astype(BF16)
        behind = [(i + _RING - k) % _RING for k in range(_RING)]
        slabs = [jnp.where(i >= behind[k], behind[k], _RING) for k in range(_RING)]

        def scores(hd):
            c0 = (hd // heads_per_group) * LANES
            sub = hd % heads_per_group
            in_head = (lane >= sub * dh) & (lane < (sub + 1) * dh)
            q_pair = qb[:, c0:c0 + LANES]
            qm = jnp.where(in_head, q_pair, jnp.zeros_like(q_pair))
            s_all = _dot_nt(kring[:, :, c0:c0 + LANES].reshape(_RING * blk, LANES), qm)
            return [s_all[k * blk:(k + 1) * blk] + bias_ref[slabs[k], hd] for k in range(_RING)]

        def probs(s_list):
            m = s_list[0].max(axis=0, keepdims=True)
            for k in range(1, _RING):
                m = jnp.maximum(m, s_list[k].max(axis=0, keepdims=True))
            return [jnp.exp2(sc - m).astype(BF16) for sc in s_list]

        def values(hd, p_list):
            o = _dot(vring[0, hd], p_list[0])
            for k in range(1, _RING):
                o = o + _dot(vring[k, hd], p_list[k])
            o_scr[j, hd * dh:(hd + 1) * dh, :] = (o[0:dh] * (1.0 / o[dh:dh + 1])).astype(BF16)

        s_next = scores(0)
        pending = None
        for hd in range(ATT_HEADS):
            s_cur = s_next
            if hd + 1 < ATT_HEADS:
                s_next = scores(hd + 1)
            p_list = probs(s_cur)
            if pending is not None:
                values(*pending)
            pending = (hd, p_list)
            if hd in between:
                between[hd]()
        values(*pending)

    def finish(j):
        rows = slice(j * blk, (j + 1) * blk)
        mix = _dot_tn(o_scr[j], wo_ref[...])
        _finish_tokens(x_ref[rows, :], mix, alpha, lng_ref, lnb_ref, wrh_ref, wrl_ref, br_ref,
                       h_ref.at[rows, :], hp_ref.at[rows, :], lgt_ref.at[rows, :])

    qkv = project(0)
    for j in range(chain):
        to_ring(j, qkv[1], qkv[2])
        nxt = []
        between = {}
        if j + 1 < chain:
            between[0] = lambda j=j: nxt.append(project(j + 1))
        attend(j, qkv[0], between)
        finish(j)
        qkv = nxt[0] if nxt else None


def _rel_bias(rel_table, n_rows, n_cols, offset, sign):
    heads = rel_table.shape[0]
    period = n_rows + n_cols
    m = jnp.arange(period)
    c_minus_r = jnp.where(m < n_cols, m, m - period)
    w = rel_table[:, jnp.clip(offset - sign * c_minus_r, -REL_CLIP, REL_CLIP) + REL_CLIP]
    flat = jnp.broadcast_to(w[:, None, :], (heads, n_rows, period)).reshape(heads, n_rows * period)
    return flat[:, :n_rows * (period - 1)].reshape(heads, n_rows, period - 1)[:, :, :n_cols]


def _att_prompt_bias(rel_table, blk):
    i = jnp.arange(blk)
    out = []
    for d in range(_RING):
        b = _rel_bias(rel_table, blk, blk, d * blk, -1)
        cd = (i[None, :] // CHUNK) - (i[:, None] // CHUNK) + d * (blk // CHUNK)
        ok = (cd >= 0) & (cd <= BAND_CHUNKS)
        out.append(jnp.where(ok[None], b, NEG_INF))
    out.append(jnp.full_like(out[0], NEG_INF))
    return jnp.stack(out).astype(F32)


def _att_prompt(x2d, n_seq, seq_len, w_qkv, w_o, rel_table, ln_g, ln_b, wr_hi, wr_lo, br, alpha, n_all,
                cast=None):
    d_model = x2d.shape[1]
    blk = ATT_BLOCK
    chain = ATT_CHAIN
    rows = blk * chain
    nblk = seq_len // rows
    keep = min(BAND_CHUNKS * CHUNK, seq_len)
    assert seq_len % rows == 0 and keep % rows == 0
    assert (_RING - 1) * blk >= BAND_CHUNKS * CHUNK
    kb = keep // rows
    bias = (_att_prompt_bias(rel_table, blk) * LOG2E).astype(BF16)
    n_steps = n_seq * nblk
    n_fill = (n_all - n_seq * seq_len) // rows
    assert n_fill * rows == n_all - n_seq * seq_len

    def work(t):
        return jnp.minimum(t, n_steps - 1)

    row_spec = pl.BlockSpec(
        (None, rows, d_model), lambda t: (work(t) // nblk, jnp.maximum(work(t) % nblk - (nblk - kb), 0), 0),
        pipeline_mode=pl.Buffered(1))
    rows_out = jax.ShapeDtypeStruct((n_seq, keep, d_model), F32)
    in_specs = [
        pl.BlockSpec((rows, d_model), lambda t: (work(t), 0)),
        _const_spec(w_qkv.shape),
        _const_spec(w_o.shape),
        _const_spec(bias.shape),
        _const_spec(ln_g.shape),
        _const_spec(ln_b.shape),
        _const_spec(wr_hi.shape),
        _const_spec(wr_lo.shape),
        _const_spec(br.shape),
    ]
    args = [x2d, w_qkv, w_o, bias, ln_g, ln_b, wr_hi, wr_lo, br]
    cast_shapes, cast_specs = _cast_plumbing(args, in_specs, cast, n_steps, work)
    return pl.pallas_call(
        functools.partial(_att_prompt_kernel, blk=blk, chain=chain, nblk=nblk, n_steps=n_steps, n_fill=n_fill,
                          n_cast=len(cast_shapes), alpha=alpha),
        grid=(n_steps + n_fill,),
        in_specs=in_specs,
        out_specs=_token_out_specs(rows, d_model, lambda t: (t, 0)) + (row_spec, row_spec) + cast_specs,
        out_shape=_token_out_shapes(n_all, d_model) + (rows_out, rows_out) + cast_shapes,
        scratch_shapes=[
            pltpu.VMEM((_RING, blk, d_model), BF16),
            pltpu.VMEM((_RING, ATT_HEADS, d_model // ATT_HEADS + BF16_SUBLANES, blk), BF16),
            pltpu.VMEM((chain, d_model, blk), BF16),
        ],
        compiler_params=pltpu.CompilerParams(
            dimension_semantics=("arbitrary",), vmem_limit_bytes=VMEM_LIMIT),
        name="att_mixer",
    )(*args)


def _att_sample_kernel(*refs, nb, blk, alpha):
    (x_ref, kc_ref, vc_ref, wqkv_ref, wo_ref, bias_c_ref, bias_n_ref, lng_ref, lnb_ref,
     wrh_ref, wrl_ref, br_ref) = refs[:12]
    h_ref, hp_ref, lgt_ref, krow_ref, vrow_ref, o_scr = refs[-6:]
    d_model = x_ref.shape[1]
    heads = ATT_HEADS
    dh = d_model // heads
    x = x_ref[...]
    xb = x.astype(BF16)
    q = _dot(xb, wqkv_ref[:, 0:d_model]) * (dh ** -0.5)
    k = _dot(xb, wqkv_ref[:, d_model:2 * d_model])
    v = _dot(xb, wqkv_ref[:, 2 * d_model:3 * d_model])
    krow_ref[...] = k.reshape(nb, blk, d_model)
    vrow_ref[...] = v.reshape(nb, blk, d_model)
    lane_head = lax.broadcasted_iota(jnp.int32, (heads, 1, d_model), 2) // dh
    head_id = lax.broadcasted_iota(jnp.int32, (heads, 1, d_model), 0)
    head_mask = (lane_head == head_id).astype(F32)

    for s in range(nb):
        r0 = s * blk
        qs = q[r0:r0 + blk]
        q_bd = (qs[None, :, :] * head_mask).reshape(heads * blk, d_model).astype(BF16)
        kn = k[r0:r0 + blk].astype(BF16)
        vn = v[r0:r0 + blk].astype(BF16)
        s_c = _dot(q_bd, kc_ref[s].astype(BF16)) + bias_c_ref[...]
        s_n = _dot_nt(q_bd, kn) + bias_n_ref[...]
        m = jnp.maximum(s_c.max(axis=-1, keepdims=True), s_n.max(axis=-1, keepdims=True))
        p_c = jnp.exp(s_c - m)
        p_n = jnp.exp(s_n - m)
        l = p_c.sum(axis=-1, keepdims=True) + p_n.sum(axis=-1, keepdims=True)
        o_full = _dot_nt(p_c.astype(BF16), vc_ref[s].astype(BF16)) + _dot(p_n.astype(BF16), vn)
        o_full = o_full * (1.0 / l)
        o = (o_full.reshape(heads, blk, d_model) * head_mask).sum(axis=0)
        o_scr[r0:r0 + blk, :] = o.astype(BF16)

    mix = _dot(o_scr[...], wo_ref[...])
    _finish_tokens(x, mix, alpha, lng_ref, lnb_ref, wrh_ref, wrl_ref, br_ref, h_ref, hp_ref, lgt_ref)


def _att_sample(x2d, in_row0, n_seq, seq_len, k_cache, v_cache, w_qkv, w_o, rel_table, ln_g, ln_b,
                wr_hi, wr_lo, br, alpha, nb, n_all, out_row0, dst):
    d_model = x2d.shape[1]
    heads = ATT_HEADS
    blk = seq_len
    n_cache = k_cache.shape[1]
    rows = nb * blk
    assert n_seq % nb == 0 and in_row0 % rows == 0 and out_row0 % rows == 0
    in_b0, out_b0 = in_row0 // rows, out_row0 // rows
    bias_c = _rel_bias(rel_table, blk, n_cache, n_cache, 1).reshape(heads * blk, n_cache)
    bias_n = _rel_bias(rel_table, blk, blk, 0, 1).reshape(heads * blk, blk)
    kc = jnp.transpose(k_cache, (0, 2, 3, 1)).reshape(n_seq, d_model, n_cache)
    vc = jnp.transpose(v_cache, (0, 2, 3, 1)).reshape(n_seq, d_model, n_cache)
    cache_spec = pl.BlockSpec((nb, d_model, n_cache), lambda g: (g, 0, 0))
    row_spec = pl.BlockSpec((nb, blk, d_model), lambda g: (g, 0, 0))
    rows_out = jax.ShapeDtypeStruct((n_seq, blk, d_model), F32)
    in_specs = [
        pl.BlockSpec((rows, d_model), lambda g: (in_b0 + g, 0)),
        cache_spec,
        cache_spec,
        _const_spec(w_qkv.shape),
        _const_spec(w_o.shape),
        _const_spec(bias_c.shape),
        _const_spec(bias_n.shape),
        _const_spec(ln_g.shape),
        _const_spec(ln_b.shape),
        _const_spec(wr_hi.shape),
        _const_spec(wr_lo.shape),
        _const_spec(br.shape),
    ]
    args = [x2d, kc, vc, w_qkv, w_o, bias_c, bias_n, ln_g, ln_b, wr_hi, wr_lo, br]
    aliases = _alias_dst(args, in_specs, dst)
    return pl.pallas_call(
        functools.partial(_att_sample_kernel, nb=nb, blk=blk, alpha=alpha),
        grid=(n_seq // nb,),
        in_specs=in_specs,
        out_specs=_token_out_specs(rows, d_model, lambda g: (out_b0 + g, 0)) + (row_spec, row_spec),
        out_shape=_token_out_shapes(n_all, d_model) + (rows_out, rows_out),
        input_output_aliases=aliases,
        scratch_shapes=[pltpu.VMEM((rows, d_model), BF16)],
        compiler_params=pltpu.CompilerParams(
            dimension_semantics=("arbitrary",), vmem_limit_bytes=VMEM_LIMIT),
        name="att_mixer_cache",
    )(*args)


ROUTE_ROWS = 40


def _route_kernel(lgt_ref, meta_t_ref, meta_w_ref, count_ref, carry_ref, tri_ref):
    j = pl.program_id(0)
    tm = lgt_ref.shape[0]

    @pl.when(j == 0)
    def _():
        carry_ref[...] = jnp.zeros(carry_ref.shape, F32)
        r = lax.broadcasted_iota(jnp.int32, (tm, tm), 0)
        c = lax.broadcasted_iota(jnp.int32, (tm, tm), 1)
        tri_ref[...] = (r < c).astype(BF16)

    a = jnp.transpose(lgt_ref[...])[0:ROUTE_ROWS, :]
    row = lax.broadcasted_iota(jnp.int32, (ROUTE_ROWS, tm), 0)
    big = jnp.int32(LANES)
    glog = jnp.where(row < N_GROUPS, a, NEG_INF)
    gmax = glog.max(axis=0, keepdims=True)
    gsel = jnp.where(glog == gmax, row, big).min(axis=0, keepdims=True)
    gp = 1.0 / jnp.exp(glog - gmax).sum(axis=0, keepdims=True)
    first = N_GROUPS + gsel * EXP_PER_GROUP
    in_grp = (row >= first) & (row < first + EXP_PER_GROUP)
    elog = jnp.where(in_grp, a, NEG_INF)
    v1 = elog.max(axis=0, keepdims=True)
    l1 = jnp.where(elog == v1, row, big).min(axis=0, keepdims=True)
    elog2 = jnp.where(row == l1, NEG_INF, elog)
    v2 = elog2.max(axis=0, keepdims=True)
    l2 = jnp.where(elog2 == v2, row, big).min(axis=0, keepdims=True)
    e2 = jnp.exp(v2 - v1)
    w1 = gp / (1.0 + e2)
    w2 = gp * e2 / (1.0 + e2)

    is1 = row == l1
    is2 = row == l2
    oh = (is1 | is2).astype(F32)
    before = _dot(oh.astype(BF16), tri_ref[...]) + carry_ref[:, 0:1]
    rank1 = jnp.where(is1, before, 0.0).sum(axis=0, keepdims=True)
    rank2 = jnp.where(is2, before, 0.0).sum(axis=0, keepdims=True)
    carry_ref[...] = carry_ref[...] + oh.sum(axis=1, keepdims=True)
    count_ref[...] = carry_ref[...].astype(jnp.int32)

    pad = jnp.zeros((SUBLANES - 4, tm), jnp.int32)
    meta_t_ref[...] = jnp.concatenate(
        [l1 - N_GROUPS, l2 - N_GROUPS, rank1.astype(jnp.int32), rank2.astype(jnp.int32), pad], axis=0)
    wt = jnp.concatenate([w1, w2, jnp.zeros((LANES - 2, tm), F32)], axis=0)
    meta_w_ref[...] = jnp.transpose(wt)


def _route(logits):
    n_tok = logits.shape[0]
    tm = TOKEN_TILE
    assert n_tok % tm == 0
    tile = pl.BlockSpec((tm, LANES), lambda j: (j, 0))
    return pl.pallas_call(
        _route_kernel,
        grid=(n_tok // tm,),
        in_specs=[tile],
        out_specs=(pl.BlockSpec((SUBLANES, tm), lambda j: (0, j)), tile,
                   pl.BlockSpec((ROUTE_ROWS, LANES), lambda j: (0, 0))),
        out_shape=(
            jax.ShapeDtypeStruct((SUBLANES, n_tok), jnp.int32),
            jax.ShapeDtypeStruct((n_tok, LANES), F32),
            jax.ShapeDtypeStruct((ROUTE_ROWS, LANES), jnp.int32),
        ),
        scratch_shapes=[pltpu.VMEM((ROUTE_ROWS, LANES), F32), pltpu.VMEM((tm, tm), BF16)],
        compiler_params=pltpu.CompilerParams(dimension_semantics=("arbitrary",)),
        name="moe_route",
    )(logits)


def _sc_gather_loop(table_hbm, idx_v, out_hbm, rows_v, sems, base, n_chunk, r):
    def gather(c, slot):
        off = pl.multiple_of(c * r, r)
        return pltpu.make_async_copy(table_hbm.at[idx_v.at[pl.ds(off, r)]], rows_v.at[slot], sems.at[slot])

    def finish(c, slot):
        gather(c, slot).wait()
        pltpu.sync_copy(rows_v.at[slot], out_hbm.at[pl.ds(base + pl.multiple_of(c * r, r), r)])

    gather(0, 0).start()
    if n_chunk > 1:
        gather(1, 1).start()

    def body(pair, carry):
        c = 2 * pair
        for slot in range(2):
            finish(c + slot, slot)

            @pl.when(c + slot + 2 < n_chunk)
            def _():
                gather(c + slot + 2, slot).start()
        return carry

    lax.fori_loop(0, n_chunk // 2, body, 0)
    if n_chunk % 2:
        finish(n_chunk - 1, 0)


def _gather_rows(table, idx):
    m = idx.shape[0]
    width = table.shape[1]
    r = SC_ROWS_PER_CHUNK
    assert m % (SC_WORKERS * r) == 0
    per_w = m // SC_WORKERS
    mesh = plsc.VectorSubcoreMesh(core_axis_name="c", subcore_axis_name="s")

    @functools.partial(
        pl.kernel,
        mesh=mesh,
        out_type=jax.ShapeDtypeStruct((m, width), table.dtype),
        scratch_types=[
            pltpu.VMEM((per_w,), jnp.int32),
            pltpu.VMEM((2, r, width), table.dtype),
            pltpu.SemaphoreType.DMA((2,)),
        ],
    )
    def gather(table_hbm, idx_hbm, out_hbm, idx_v, rows_v, sem):
        wid = lax.axis_index("s") * 2 + lax.axis_index("c")
        base = wid * per_w
        pltpu.sync_copy(idx_hbm.at[pl.ds(base, per_w)], idx_v)
        _sc_gather_loop(table_hbm, idx_v, out_hbm, rows_v, sem, base, per_w // r, r)

    return gather(table, idx)


def _dispatch_rows(table, pos, row0, n_rows):
    n_tok, width = table.shape
    n_pairs = pos.shape[0]
    r = SC_ROWS_PER_CHUNK
    lanes = SC_LANES
    assert n_rows % (SC_WORKERS * r) == 0
    per_w = n_rows // SC_WORKERS
    n_stage = 16
    stage = n_pairs // n_stage
    assert stage * n_stage == n_pairs and stage % lanes == 0 and per_w % lanes == 0
    assert row0 + n_rows < 3 * n_tok
    mesh = plsc.VectorSubcoreMesh(core_axis_name="c", subcore_axis_name="s")

    @functools.partial(
        pl.kernel,
        mesh=mesh,
        out_type=jax.ShapeDtypeStruct((n_rows, width), table.dtype),
        scratch_types=[
            pltpu.VMEM((per_w,), jnp.int32),
            pltpu.VMEM((stage,), jnp.int32),
            pltpu.VMEM((2, r, width), table.dtype),
            pltpu.SemaphoreType.DMA((2,)),
        ],
        compiler_params=pltpu.CompilerParams(needs_layout_passes=False),
    )
    def dispatch(table_hbm, pos_hbm, out_hbm, src_v, pos_v, rows_v, sem):
        wid = lax.axis_index("s") * 2 + lax.axis_index("c")
        out_base = wid * per_w
        base = row0 + out_base
        lane = lax.iota(jnp.int32, lanes)

        def wrap(t):
            t = jnp.where(t >= n_tok, t - n_tok, t)
            return jnp.where(t >= n_tok, t - n_tok, t)

        def init(i, carry):
            off = pl.multiple_of(i * lanes, lanes)
            src_v[pl.ds(off, lanes)] = wrap(base + off + lane)
            return carry

        lax.fori_loop(0, per_w // lanes, init, 0)

        def scan_stage(sidx, carry):
            pair0 = sidx * stage
            pltpu.sync_copy(pos_hbm.at[pl.ds(pl.multiple_of(pair0, 8), stage)], pos_v)

            @plsc.parallel_loop(0, stage // lanes, unroll=4)
            def _(i):
                off = pl.multiple_of(i * lanes, lanes)
                local = pos_v[pl.ds(off, lanes)] - base
                mine = (local >= 0) & (local < per_w)
                plsc.store_scatter(src_v, [jnp.where(mine, local, 0)], wrap(pair0 + off + lane), mask=mine)

            return carry

        lax.fori_loop(0, n_stage, scan_stage, 0)
        _sc_gather_loop(table_hbm, src_v, out_hbm, rows_v, sem, out_base, per_w // r, r)

    return dispatch(table, pos)


def _gemm_kernel(tile_expert_ref, n_used_ref, xs_ref, wg_ref, wu_ref, wd_ref, *rest, tile0):
    ys_ref = rest[-1]
    tile = tile0 + pl.program_id(0)

    @pl.when(tile < n_used_ref[0])
    def _():
        x = _unpack_rows(xs_ref[...]).astype(BF16)
        g = _dot(x, wg_ref[...])
        u = _dot(x, wu_ref[...])
        hmid = (jax.nn.silu(g) * u).astype(BF16)
        ys_ref[...] = _pack_rows(_dot(hmid, wd_ref[...]))


def _expert_gemm(xs, tile_expert, n_used, w_gate, w_up, w_down, tile0, n_rows_all, ys):
    n_rows, half = xs.shape
    d_model, d_exp = w_gate.shape[1], w_gate.shape[2]
    tm = GEMM_TILE
    n_tiles = n_rows // tm

    def last_used(j, nu):
        return jnp.minimum(tile0 + j, nu[0] - 1)

    def in_map(j, te, nu):
        return (jnp.maximum(last_used(j, nu) - tile0, 0), 0)

    def out_map(j, te, nu):
        return (jnp.maximum(last_used(j, nu), tile0), 0)

    def w_map(j, te, nu):
        return (te[tile0 + j], 0, 0)

    in_specs = [
        pl.BlockSpec((tm, half), in_map),
        pl.BlockSpec((None, d_model, d_exp), w_map),
        pl.BlockSpec((None, d_model, d_exp), w_map),
        pl.BlockSpec((None, d_exp, d_model), w_map),
    ]
    args = [tile_expert, n_used, xs, w_gate, w_up, w_down]
    aliases = {}
    if ys is not None:
        aliases[len(args)] = 0
        args.append(ys)
        in_specs.append(pl.BlockSpec(memory_space=pl.ANY))
    grid_spec = pltpu.PrefetchScalarGridSpec(
        num_scalar_prefetch=2,
        grid=(n_tiles,),
        in_specs=in_specs,
        out_specs=pl.BlockSpec((tm, half), out_map),
    )
    return pl.pallas_call(
        functools.partial(_gemm_kernel, tile0=tile0),
        grid_spec=grid_spec,
        out_shape=jax.ShapeDtypeStruct((n_rows_all, half), U32),
        input_output_aliases=aliases,
        compiler_params=pltpu.CompilerParams(
            dimension_semantics=("arbitrary",), vmem_limit_bytes=VMEM_LIMIT),
        name="moe_gemm",
    )(*args)


def _combine_kernel(y0_ref, y1_ref, mw_ref, h_ref, pp_ref, ps_ref, lng_ref, lnb_ref, wproj_ref, wgate_ref,
                    *out_refs, n_prompt_tiles, alpha):
    j = pl.program_id(0)
    tm = h_ref.shape[0]
    half = tm // 2

    def normed(rows):
        mw = mw_ref[rows, :]
        ffn = mw[:, 0:1] * _unpack_rows(y0_ref[rows, :]) + mw[:, 1:2] * _unpack_rows(y1_ref[rows, :])
        return _layer_norm(alpha * h_ref[rows, :] + ffn, lng_ref[...], lnb_ref[...])

    def gated(rows, h2):
        p = jnp.where(j < n_prompt_tiles, pp_ref[rows, :], ps_ref[rows, :]).astype(BF16)
        gate = jax.nn.sigmoid(_dot(h2.astype(BF16), wgate_ref[...]))
        return h2 + gate * _dot(p, wproj_ref[...])

    rows_a, rows_b = slice(0, half), slice(half, tm)
    h2_a = normed(rows_a)
    h2_b = normed(rows_b)
    out = jnp.concatenate([gated(rows_a, h2_a), gated(rows_b, h2_b)], axis=0)
    if len(out_refs) == 1:
        out_refs[0][...] = out
    else:
        @pl.when(j < n_prompt_tiles)
        def _():
            out_refs[0][...] = out

        @pl.when(j >= n_prompt_tiles)
        def _():
            out_refs[1][...] = out


def _combine(yg, meta_w, h, p_prompt, p_sample, layer, ln_g, ln_b, w_proj, w_gate, alpha, split_out):
    n_tok, d_model = h.shape
    tm = TOKEN_TILE
    n_t = n_tok // tm
    n_tp = p_prompt.shape[1] // tm
    n_ts = p_sample.shape[1] // tm
    assert n_tp * tm == p_prompt.shape[1] and n_ts * tm == p_sample.shape[1] and n_tp + n_ts == n_t
    ple = p_prompt.shape[2]
    tile = pl.BlockSpec((tm, d_model), lambda j: (j, 0))
    if split_out:
        out_specs = (pl.BlockSpec((tm, d_model), lambda j: (jnp.minimum(j, n_tp - 1), 0)),
                     pl.BlockSpec((tm, d_model), lambda j: (jnp.maximum(j - n_tp, 0), 0)))
        out_shape = (jax.ShapeDtypeStruct((n_tp * tm, d_model), F32),
                     jax.ShapeDtypeStruct((n_ts * tm, d_model), F32))
    else:
        out_specs = tile
        out_shape = jax.ShapeDtypeStruct((n_tok, d_model), F32)
    return pl.pallas_call(
        functools.partial(_combine_kernel, n_prompt_tiles=n_tp, alpha=alpha),
        grid=(n_t,),
        in_specs=[
            pl.BlockSpec((tm, d_model // 2), lambda j: (j, 0)),
            pl.BlockSpec((tm, d_model // 2), lambda j: (j + n_t, 0)),
            pl.BlockSpec((tm, LANES), lambda j: (j, 0)),
            tile,
            pl.BlockSpec((None, tm, ple), lambda j: (layer, jnp.minimum(j, n_tp - 1), 0)),
            pl.BlockSpec((None, tm, ple), lambda j: (layer, jnp.maximum(j - n_tp, 0), 0)),
            _const_spec(ln_g.shape),
            _const_spec(ln_b.shape),
            _const_spec(w_proj.shape),
            _const_spec(w_gate.shape),
        ],
        out_specs=out_specs,
        out_shape=out_shape,
        compiler_params=pltpu.CompilerParams(
            dimension_semantics=("arbitrary",), vmem_limit_bytes=VMEM_LIMIT),
        name="moe_combine",
    )(yg, yg, meta_w, h, p_prompt, p_sample, ln_g, ln_b, w_proj, w_gate)


def _router_weights(w_grp, b_grp, w_exp, b_exp):
    d_model = w_grp.shape[0]
    w = jnp.concatenate([w_grp, jnp.transpose(w_exp, (1, 0, 2)).reshape(d_model, N_EXPERTS)], axis=1)
    w = jnp.pad(w, ((0, 0), (0, LANES - w.shape[1])))
    b = jnp.concatenate([b_grp, b_exp.reshape(N_EXPERTS)])
    b = jnp.pad(b, (0, LANES - b.shape[0])).reshape(1, LANES)
    w_hi = w.astype(BF16)
    w_lo = (w - w_hi.astype(F32)).astype(BF16)
    return w_hi, w_lo, b


def _moe(h_packed, logits, w_gate, w_up, w_down):
    n_tok = h_packed.shape[0]
    tm = GEMM_TILE
    meta_t, meta_w, counts = _route(logits)
    counts = counts[N_GROUPS:N_GROUPS + N_EXPERTS, 0]
    tiles_per_expert = (counts + tm - 1) // tm
    tile_end = jnp.cumsum(tiles_per_expert)
    row_start = (tile_end - tiles_per_expert) * tm
    eid = meta_t[0:2]
    rank = meta_t[2:4]
    experts = jnp.arange(N_EXPERTS, dtype=jnp.int32)
    start = jnp.sum(jnp.where(eid[:, :, None] == experts, row_start, 0), axis=-1)
    pos = (start + rank).reshape(-1).astype(jnp.int32)
    part_quant = MOE_PARTS * SC_WORKERS * SC_ROWS_PER_CHUNK * (tm // math.gcd(tm, SC_WORKERS * SC_ROWS_PER_CHUNK))
    n_tiles = -(-(2 * n_tok) // tm) + N_EXPERTS
    n_rows = -(-(n_tiles * tm) // part_quant) * part_quant
    n_tiles = n_rows // tm
    n_used = tile_end[-1:].astype(jnp.int32)
    tile_ids = jnp.minimum(jnp.arange(n_tiles, dtype=jnp.int32), n_used[0] - 1)
    tile_expert = jnp.sum(tile_end[None, :] <= tile_ids[:, None], axis=1).astype(jnp.int32)
    part_rows = n_rows // MOE_PARTS
    xs_parts = [_dispatch_rows(h_packed, pos, part * part_rows, part_rows) for part in range(MOE_PARTS)]
    ys = None
    for part, xs in enumerate(xs_parts):
        ys = _expert_gemm(xs, tile_expert, n_used, w_gate, w_up, w_down,
                          part * (part_rows // tm), n_rows, ys)
    return _gather_rows(ys, pos), meta_w


def kernel(x_prompt, x_sample, p_prompt, p_sample, state_ret, cache_att_k, cache_att_v, ret_w_in, ret_gn_g,
           ret_w_o, att_w_qkv, att_rel_bias, att_w_o, ln1_g, ln1_b, ln2_g, ln2_b, moe_w_grp, moe_b_grp,
           moe_w_exp, moe_b_exp, moe_w_gate, moe_w_up, moe_w_down, ple_w_proj, ple_w_gate):
    n_p, len_p, d_model = x_prompt.shape
    n_s, len_s, _ = x_sample.shape
    depth = ln1_g.shape[0]
    alpha = float((2 * depth) ** 0.25)
    tok_p, tok_s = n_p * len_p, n_s * len_s
    n_all = tok_p + tok_s
    dh = d_model // ATT_HEADS
    pp = p_prompt.reshape(depth, tok_p, -1)
    ps = p_sample.reshape(depth, tok_s, -1)
    nb_s = 4

    x_all = None
    y_prompt = y_sample = None
    states_p, states_s, k_p, v_p, k_s, v_s = [], [], [], [], [], []
    for i in range(depth):
        jj = i // 2
        wr_hi, wr_lo, br = _router_weights(moe_w_grp[i], moe_b_grp[i], moe_w_exp[i], moe_b_exp[i])
        lng, lnb = ln1_g[i].reshape(1, d_model), ln1_b[i].reshape(1, d_model)
        moe_cast = (i, [moe_w_gate, moe_w_up, moe_w_down])
        if x_all is None:
            src_p, src_s, row_s = x_prompt.reshape(tok_p, d_model), x_sample.reshape(tok_s, d_model), 0
        else:
            src_p, src_s, row_s = x_all, x_all, tok_p
        if i % 2 == 0:
            w_in = ret_w_in[jj].astype(BF16)
            w_o = ret_w_o[jj].astype(BF16)
            gn = ret_gn_g[jj].reshape(1, -1)
            h, hp, lgt, st_p, *w_moe = _ret_mixer(
                src_p, 0, n_p, len_p, 0, None, w_in, w_o, gn, lng, lnb, wr_hi, wr_lo, br, alpha, nb=1,
                blk=min(RET_BLOCK, len_p), chain=RET_CHAIN, n_all=n_all, out_row0=0, dst=None, cast=moe_cast)
            h, hp, lgt, st_s = _ret_mixer(src_s, row_s, n_s, len_s, PAST_LEN, state_ret[jj], w_in, w_o, gn,
                                          lng, lnb, wr_hi, wr_lo, br, alpha, nb=nb_s, blk=len_s, chain=1,
                                          n_all=n_all, out_row0=tok_p, dst=(h, hp, lgt))
            states_p.append(st_p)
            states_s.append(st_s)
        else:
            w_qkv = att_w_qkv[jj].astype(BF16)
            w_o = att_w_o[jj].astype(BF16)
            h, hp, lgt, kr, vr, *w_moe = _att_prompt(src_p, n_p, len_p, w_qkv, w_o, att_rel_bias[jj], lng, lnb,
                                                     wr_hi, wr_lo, br, alpha, n_all, cast=moe_cast)
            k_p.append(kr.reshape(n_p, -1, ATT_HEADS, dh))
            v_p.append(vr.reshape(n_p, -1, ATT_HEADS, dh))
            h, hp, lgt, kr, vr = _att_sample(src_s, row_s, n_s, len_s, cache_att_k[jj], cache_att_v[jj],
                                             w_qkv, w_o, att_rel_bias[jj], lng, lnb, wr_hi, wr_lo, br, alpha,
                                             nb=nb_s, n_all=n_all, out_row0=tok_p, dst=(h, hp, lgt))
            k_s.append(kr.reshape(n_s, len_s, ATT_HEADS, dh))
            v_s.append(vr.reshape(n_s, len_s, ATT_HEADS, dh))
        n_exp = moe_w_gate.shape[1]
        w_moe = [w.reshape(n_exp, -1, w.shape[1]) for w in w_moe]
        yg, meta_w = _moe(hp, lgt, *w_moe)
        last = i == depth - 1
        out = _combine(yg, meta_w, h, pp, ps, i, ln2_g[i].reshape(1, d_model), ln2_b[i].reshape(1, d_model),
                       ple_w_proj[i].astype(BF16), ple_w_gate[i].astype(BF16), alpha, split_out=last)
        if last:
            y_prompt = out[0].reshape(n_p, len_p, d_model)
            y_sample = out[1].reshape(n_s, len_s, d_model)
        else:
            x_all = out

    return (y_prompt, y_sample, jnp.stack(states_p), jnp.stack(states_s),
            jnp.stack(k_p), jnp.stack(v_p), jnp.stack(k_s), jnp.stack(v_s))
```

```python
import functools
import math

import numpy as np
import jax
import jax.numpy as jnp
from jax import lax
from jax.experimental import pallas as pl
from jax.experimental.pallas import tpu as pltpu
from jax.experimental.pallas import tpu_sc as plsc

CHUNK = 64
PAST_LEN = 2048
RET_HEADS = 4
ROPE_BASE = 10000.0
ATT_HEADS = 16
BAND_CHUNKS = 8
REL_CLIP = 256
N_GROUPS = 4
EXP_PER_GROUP = 8
N_EXPERTS = N_GROUPS * EXP_PER_GROUP
LN_EPS = 1e-5
NEG_INF = -1e30

LANES = 128
SUBLANES = 8
BF16_SUBLANES = 16
SC_WORKERS = 32
SC_LANES = 16
SC_ROWS_PER_CHUNK = 32
VMEM_LIMIT = 58 * 1024 * 1024

RET_BLOCK = 256
RET_CHAIN = 2
ATT_BLOCK = 4 * CHUNK
ATT_CHAIN = 2
TOKEN_TILE = 512
GEMM_TILE = 512
MOE_PARTS = 3

F32 = jnp.float32
BF16 = jnp.bfloat16
U32 = jnp.uint32
HI_MASK = 0xFFFF0000
LOG2E = math.log2(math.e)


def _dot(a, b):
    return jnp.dot(a, b, preferred_element_type=F32)


def _dot_nt(a, b):
    return lax.dot_general(a, b, (((1,), (1,)), ((), ())), preferred_element_type=F32)


def _dot_tn(a, b):
    return lax.dot_general(a, b, (((0,), (0,)), ((), ())), preferred_element_type=F32)


def _layer_norm(x, g, b):
    mu = jnp.mean(x, axis=-1, keepdims=True)
    xc = x - mu
    var = jnp.mean(xc * xc, axis=-1, keepdims=True)
    return xc * lax.rsqrt(var + LN_EPS) * g + b


def _pack_rows(x):
    half = x.shape[1] // 2
    lo = lax.bitcast_convert_type(x[:, :half].astype(BF16).astype(F32), U32) >> 16
    hi = lax.bitcast_convert_type(x[:, half:].astype(BF16).astype(F32), U32) & U32(HI_MASK)
    return lo | hi


def _unpack_rows(p):
    lo = lax.bitcast_convert_type(p << 16, F32)
    hi = lax.bitcast_convert_type(p & U32(HI_MASK), F32)
    return jnp.concatenate([lo, hi], axis=1)


def _router_logits(h, wr_hi_ref, wr_lo_ref, br_ref):
    h_hi = h.astype(BF16)
    h_lo = (h - h_hi.astype(F32)).astype(BF16)
    w_hi = wr_hi_ref[...]
    return _dot(h_hi, w_hi) + _dot(h_lo, w_hi) + _dot(h_hi, wr_lo_ref[...]) + br_ref[...]


def _finish_tokens(x, mix, alpha, lng_ref, lnb_ref, wrh_ref, wrl_ref, br_ref, h_ref, hp_ref, lgt_ref):
    hh = _layer_norm(alpha * x + mix, lng_ref[...], lnb_ref[...])
    h_ref[...] = hh
    hp_ref[...] = _pack_rows(hh)
    lgt_ref[...] = _router_logits(hh, wrh_ref, wrl_ref, br_ref)


def _zero_tokens(h_ref, hp_ref, lgt_ref):
    h_ref[...] = jnp.zeros(h_ref.shape, h_ref.dtype)
    hp_ref[...] = jnp.zeros(hp_ref.shape, hp_ref.dtype)
    lgt_ref[...] = jnp.zeros(lgt_ref.shape, lgt_ref.dtype)


def _const_spec(shape):
    nd = len(shape)
    return pl.BlockSpec(shape, lambda *_: (0,) * nd, pipeline_mode=pl.Buffered(1))


def _token_out_shapes(n_all, d_model):
    return (
        jax.ShapeDtypeStruct((n_all, d_model), F32),
        jax.ShapeDtypeStruct((n_all, d_model // 2), U32),
        jax.ShapeDtypeStruct((n_all, LANES), F32),
    )


def _token_out_specs(rows, d_model, row_map):
    return (
        pl.BlockSpec((rows, d_model), row_map),
        pl.BlockSpec((rows, d_model // 2), row_map),
        pl.BlockSpec((rows, LANES), row_map),
    )


def _alias_dst(args, in_specs, dst):
    if dst is None:
        return {}
    aliases = {}
    for k, arr in enumerate(dst):
        aliases[len(args)] = k
        args.append(arr)
        in_specs.append(pl.BlockSpec(memory_space=pl.ANY))
    return aliases


def _cast_plumbing(args, in_specs, cast, n_steps, work):
    if cast is None:
        return (), ()
    layer, weights = cast
    shapes, specs = [], []
    for w in weights:
        depth, n_exp, rows, cols = w.shape
        assert (n_exp * rows) % n_steps == 0
        slab = (n_exp * rows) // n_steps
        args.append(w.reshape(depth, n_exp * rows, cols))
        in_specs.append(pl.BlockSpec((None, slab, cols), lambda t: (layer, work(t), 0)))
        shapes.append(jax.ShapeDtypeStruct((n_exp * rows, cols), BF16))
        specs.append(pl.BlockSpec((slab, cols), lambda t: (work(t), 0)))
    return tuple(shapes), tuple(specs)


def _cast_slabs(in_refs, out_refs):
    for src, dst in zip(in_refs, out_refs):
        dst[...] = src[...].astype(BF16)


def _ret_log_gamma():
    h = np.arange(RET_HEADS, dtype=np.float32)
    return np.log(np.float32(1.0) - np.float32(2.0) ** (np.float32(-5.0) - h)).astype(np.float32)


def _ret_kernel(*refs, nb, blk, chain, nblk, n_steps, n_fill, has_state, n_alias, n_cast, alpha):
    step = pl.program_id(0)
    n_in = 12 + int(has_state) + n_alias + n_cast
    h_ref, hp_ref, lgt_ref = refs[n_in:n_in + 3]
    core = refs[:12 + int(has_state)] + refs[n_in:n_in + 4] + refs[-1:]

    @pl.when(step < n_steps)
    def _():
        _cast_slabs(refs[n_in - n_cast:n_in], refs[n_in + 4:n_in + 4 + n_cast])
        _ret_step(*core, first_block=step % nblk == 0, nb=nb, blk=blk, chain=chain, has_state=has_state,
                  alpha=alpha)

    if n_fill:
        @pl.when(step >= n_steps)
        def _():
            _zero_tokens(h_ref, hp_ref, lgt_ref)


def _ret_step(*refs, first_block, nb, blk, chain, has_state, alpha):
    (x_ref, cos_ref, sin_ref, dmask_ref, win_ref, wo_ref, gn_ref, lng_ref, lnb_ref,
     wrh_ref, wrl_ref, br_ref) = refs[:12]
    s_in_ref = refs[12] if has_state else None
    h_ref, hp_ref, lgt_ref, s_out_ref, gated_ref = refs[12 + int(has_state):]
    heads = RET_HEADS
    d_model = x_ref.shape[1]
    dk = d_model // heads
    dv = 2 * d_model // heads
    hk, hv = heads * dk, heads * dv
    half = dk // 2
    lg = _ret_log_gamma()

    n_sub = nb * chain
    proj_rows = blk if chain > 1 else n_sub * blk
    rowf = lax.broadcasted_iota(jnp.int32, (blk, 1), 0).astype(F32)

    def proj(group):
        xb = x_ref[group * proj_rows:(group + 1) * proj_rows, :].astype(BF16)
        return (_dot(xb, win_ref[:, 0:hk]), _dot(xb, win_ref[:, hk:2 * hk]),
                _dot(xb, win_ref[:, 2 * hk:2 * hk + hv]), _dot(xb, win_ref[:, 2 * hk + hv:2 * hk + 2 * hv]))

    if not has_state:
        @pl.when(first_block)
        def _():
            s_out_ref[...] = jnp.zeros(s_out_ref.shape, F32)

    s_prev_ref = s_in_ref if has_state else s_out_ref

    def head(j, h, projected):
        q_all, k_all, v_all, g_all = projected
        s = j // chain
        c0 = (j % chain) * blk
        r0 = (j * blk) % proj_rows
        cos = cos_ref[c0:c0 + blk, :]
        sin = sin_ref[c0:c0 + blk, :]

        def rot(t):
            t1, t2 = t[:, :half], t[:, half:]
            return jnp.concatenate([t1 * cos - t2 * sin, t1 * sin + t2 * cos], axis=1)

        lgh = float(lg[h])
        q = rot(q_all[r0:r0 + blk, h * dk:(h + 1) * dk])
        k = rot(k_all[r0:r0 + blk, h * dk:(h + 1) * dk]) * (dk ** -0.5)
        v = v_all[r0:r0 + blk, h * dv:(h + 1) * dv]
        g = g_all[r0:r0 + blk, h * dv:(h + 1) * dv]
        vb = v.astype(BF16)
        scores = _dot_nt(q.astype(BF16), k.astype(BF16)) * dmask_ref[h]
        inner = _dot(scores.astype(BF16), vb)
        s_prev = s_prev_ref[s, h]
        q_dec = q * jnp.exp(lgh * (rowf + 1.0))
        cross = _dot(q_dec.astype(BF16), s_prev.astype(BF16))
        k_dec = k * jnp.exp(lgh * (float(blk - 1) - rowf))
        s_out_ref[s, h] = math.exp(lgh * blk) * s_prev + _dot_tn(k_dec.astype(BF16), vb)
        o = inner + cross
        mu = jnp.mean(o, axis=-1, keepdims=True)
        oc = o - mu
        var = jnp.mean(oc * oc, axis=-1, keepdims=True)
        on = oc * lax.rsqrt(var + LN_EPS) * gn_ref[:, h * dv:(h + 1) * dv]
        gated_ref[j * blk:(j + 1) * blk, h * dv:(h + 1) * dv] = (jax.nn.silu(g) * on).astype(BF16)

    def tail(group):
        rows = slice(group * proj_rows, (group + 1) * proj_rows)
        mix = _dot(gated_ref[rows, :], wo_ref[...])
        _finish_tokens(x_ref[rows, :], mix, alpha, lng_ref, lnb_ref, wrh_ref, wrl_ref, br_ref,
                       h_ref.at[rows, :], hp_ref.at[rows, :], lgt_ref.at[rows, :])

    n_groups = (n_sub * blk) // proj_rows
    subs_per_group = proj_rows // blk
    projected = proj(0)
    for group in range(n_groups):
        nxt = None
        for jj in range(subs_per_group):
            for h in range(heads):
                head(group * subs_per_group + jj, h, projected)
                if jj == 0 and h == 0 and group + 1 < n_groups:
                    nxt = proj(group + 1)
                if jj == 0 and h == 1 and group > 0:
                    tail(group - 1)
        projected = nxt
    tail(n_groups - 1)


def _ret_mixer(x2d, in_row0, n_seq, seq_len, pos0, state_in, w_in, w_o, gn_g, ln_g, ln_b,
               wr_hi, wr_lo, br, alpha, nb, blk, chain, n_all, out_row0, dst, cast=None):
    d_model = x2d.shape[1]
    heads = RET_HEADS
    dk, dv = d_model // heads, 2 * d_model // heads
    half = dk // 2
    nblk = seq_len // (blk * chain)
    has_state = state_in is not None
    rows = nb * blk * chain
    assert seq_len % (blk * chain) == 0 and n_seq % nb == 0
    assert (not has_state) or nblk == 1
    assert nb == 1 or (nblk == 1 and chain == 1)
    assert in_row0 % rows == 0 and out_row0 % rows == 0
    in_b0, out_b0 = in_row0 // rows, out_row0 // rows

    pos = (pos0 + jnp.arange(seq_len, dtype=jnp.int32)).astype(F32)
    inv_freq = ROPE_BASE ** (-jnp.arange(half, dtype=F32) / half)
    ang = pos[:, None] * inv_freq[None, :]
    cos, sin = jnp.cos(ang), jnp.sin(ang)
    lg = jnp.asarray(_ret_log_gamma())
    ii = jnp.arange(blk, dtype=F32)
    diff = ii[:, None] - ii[None, :]
    dmask = jnp.where(diff >= 0, jnp.exp(lg[:, None, None] * jnp.maximum(diff, 0.0)), 0.0)

    n_steps = (n_seq // nb) * nblk
    n_fill = 0 if dst is not None else (n_all - n_seq * seq_len) // rows
    assert dst is not None or (out_row0 == 0 and n_fill * rows == n_all - n_seq * seq_len)

    def work(t):
        return jnp.minimum(t, n_steps - 1)

    in_specs = [
        pl.BlockSpec((rows, d_model), lambda t: (in_b0 + work(t), 0)),
        pl.BlockSpec((blk * chain, half), lambda t: (work(t) % nblk, 0)),
        pl.BlockSpec((blk * chain, half), lambda t: (work(t) % nblk, 0)),
        _const_spec(dmask.shape),
        _const_spec(w_in.shape),
        _const_spec(w_o.shape),
        _const_spec(gn_g.shape),
        _const_spec(ln_g.shape),
        _const_spec(ln_b.shape),
        _const_spec(wr_hi.shape),
        _const_spec(wr_lo.shape),
        _const_spec(br.shape),
    ]
    args = [x2d, cos, sin, dmask, w_in, w_o, gn_g, ln_g, ln_b, wr_hi, wr_lo, br]
    state_spec = pl.BlockSpec((nb, heads, dk, dv), lambda t: (work(t) // nblk, 0, 0, 0))
    if has_state:
        in_specs.append(state_spec)
        args.append(state_in)
    aliases = _alias_dst(args, in_specs, dst)
    cast_shapes, cast_specs = _cast_plumbing(args, in_specs, cast, n_steps, work)
    out_shape = (_token_out_shapes(n_all, d_model) + (jax.ShapeDtypeStruct((n_seq, heads, dk, dv), F32),)
                 + cast_shapes)
    out_specs = _token_out_specs(rows, d_model, lambda t: (out_b0 + t, 0)) + (state_spec,) + cast_specs
    return pl.pallas_call(
        functools.partial(_ret_kernel, nb=nb, blk=blk, chain=chain, nblk=nblk, n_steps=n_steps, n_fill=n_fill,
                          has_state=has_state, n_alias=len(aliases), n_cast=len(cast_shapes), alpha=alpha),
        grid=(n_steps + n_fill,),
        in_specs=in_specs,
        out_specs=out_specs,
        out_shape=out_shape,
        input_output_aliases=aliases,
        scratch_shapes=[pltpu.VMEM((rows, heads * dv), BF16)],
        compiler_params=pltpu.CompilerParams(
            dimension_semantics=("arbitrary",), vmem_limit_bytes=VMEM_LIMIT),
        name="ret_mixer_state" if has_state else "ret_mixer",
    )(*args)


_RING = 3


def _att_prompt_kernel(*refs, blk, chain, nblk, n_steps, n_fill, n_cast, alpha):
    step = pl.program_id(0)
    n_in = 9 + n_cast
    h_ref, hp_ref, lgt_ref = refs[n_in:n_in + 3]
    core = refs[:9] + refs[n_in:n_in + 5] + refs[-3:]

    @pl.when(step < n_steps)
    def _():
        _cast_slabs(refs[9:n_in], refs[n_in + 5:n_in + 5 + n_cast])
        _att_prompt_step(*core, i0=(step % nblk) * chain, blk=blk, chain=chain, alpha=alpha)

    if n_fill:
        @pl.when(step >= n_steps)
        def _():
            _zero_tokens(h_ref, hp_ref, lgt_ref)


def _att_prompt_step(x_ref, wqkv_ref, wo_ref, bias_ref, lng_ref, lnb_ref, wrh_ref, wrl_ref, br_ref,
                     h_ref, hp_ref, lgt_ref, krow_ref, vrow_ref, kring, vring, o_scr, *, i0, blk, chain, alpha):
    d_model = x_ref.shape[1]
    dh = d_model // ATT_HEADS
    heads_per_group = LANES // dh
    lane = lax.broadcasted_iota(jnp.int32, (1, LANES), 1)

    @pl.when(i0 == 0)
    def _():
        kring[...] = jnp.zeros(kring.shape, BF16)
        vring[...] = jnp.zeros(vring.shape, BF16)
        vring[:, :, dh:, :] = jnp.ones((_RING, ATT_HEADS, vring.shape[2] - dh, blk), BF16)

    def project(j):
        xb = x_ref[j * blk:(j + 1) * blk, :].astype(BF16)
        q = _dot(xb, wqkv_ref[:, 0:d_model]) * (dh ** -0.5 * LOG2E)
        k = _dot(xb, wqkv_ref[:, d_model:2 * d_model])
        v = _dot(xb, wqkv_ref[:, 2 * d_model:3 * d_model])
        return q, k, v

    def to_ring(j, k, v):
        slot = (i0 + j) % _RING
        krow_ref[j * blk:(j + 1) * blk, :] = k
        vrow_ref[j * blk:(j + 1) * blk, :] = v
        kring[slot] = k.astype(BF16)
        v_t = jnp.transpose(v).astype(BF16)
        for hd in range(ATT_HEADS):
            vring[slot, hd, 0:dh, :] = v_t[hd * dh:(hd + 1) * dh, :]

    def attend(j, q, between):
        i = i0 + j
        qb = q.astype(BF16)
        behind = [(i + _RING - k) % _RING for k in range(_RING)]
        slabs = [jnp.where(i >= behind[k], behind[k], _RING) for k in range(_RING)]

        def scores(hd):
            c0 = (hd // heads_per_group) * LANES
            sub = hd % heads_per_group
            in_head = (lane >= sub * dh) & (lane < (sub + 1) * dh)
            q_pair = qb[:, c0:c0 + LANES]
            qm = jnp.where(in_head, q_pair, jnp.zeros_like(q_pair))
            s_all = _dot_nt(kring[:, :, c0:c0 + LANES].reshape(_RING * blk, LANES), qm)
            return [s_all[k * blk:(k + 1) * blk] + bias_ref[slabs[k], hd] for k in range(_RING)]

        def probs(s_list):
            m = s_list[0].max(axis=0, keepdims=True)
            for k in range(1, _RING):
                m = jnp.maximum(m, s_list[k].max(axis=0, keepdims=True))
            return [jnp.exp2(sc - m).astype(BF16) for sc in s_list]

        def values(hd, p_list):
            o = _dot(vring[0, hd], p_list[0])
            for k in range(1, _RING):
                o = o + _dot(vring[k, hd], p_list[k])
            o_scr[j, hd * dh:(hd + 1) * dh, :] = (o[0:dh] * (1.0 / o[dh:dh + 1])).astype(BF16)

        s_next = scores(0)
        pending = None
        for hd in range(ATT_HEADS):
            s_cur = s_next
            if hd + 1 < ATT_HEADS:
                s_next = scores(hd + 1)
            p_list = probs(s_cur)
            if pending is not None:
                values(*pending)
            pending = (hd, p_list)
            if hd in between:
                between[hd]()
        values(*pending)

    def finish(j):
        rows = slice(j * blk, (j + 1) * blk)
        mix = _dot_tn(o_scr[j], wo_ref[...])
        _finish_tokens(x_ref[rows, :], mix, alpha, lng_ref, lnb_ref, wrh_ref, wrl_ref, br_ref,
                       h_ref.at[rows, :], hp_ref.at[rows, :], lgt_ref.at[rows, :])

    qkv = project(0)
    for j in range(chain):
        to_ring(j, qkv[1], qkv[2])
        nxt = []
        between = {}
        if j + 1 < chain:
            between[0] = lambda j=j: nxt.append(project(j + 1))
        attend(j, qkv[0], between)
        finish(j)
        qkv = nxt[0] if nxt else None


def _rel_bias(rel_table, n_rows, n_cols, offset, sign):
    heads = rel_table.shape[0]
    period = n_rows + n_cols
    m = jnp.arange(period)
    c_minus_r = jnp.where(m < n_cols, m, m - period)
    w = rel_table[:, jnp.clip(offset - sign * c_minus_r, -REL_CLIP, REL_CLIP) + REL_CLIP]
    flat = jnp.broadcast_to(w[:, None, :], (heads, n_rows, period)).reshape(heads, n_rows * period)
    return flat[:, :n_rows * (period - 1)].reshape(heads, n_rows, period - 1)[:, :, :n_cols]


def _att_prompt_bias(rel_table, blk):
    i = jnp.arange(blk)
    out = []
    wide = _rel_bias(rel_table, blk, _RING * blk, 0, -1)
    for d in range(_RING):
        b = wide[:, :, d * blk:(d + 1) * blk]
        cd = (i[None, :] // CHUNK) - (i[:, None] // CHUNK) + d * (blk // CHUNK)
        ok = (cd >= 0) & (cd <= BAND_CHUNKS)
        out.append(jnp.where(ok[None], b, NEG_INF))
    out.append(jnp.full_like(out[0], NEG_INF))
    return jnp.stack(out).astype(F32)


def _att_prompt(x2d, n_seq, seq_len, w_qkv, w_o, rel_table, ln_g, ln_b, wr_hi, wr_lo, br, alpha, n_all,
                cast=None):
    d_model = x2d.shape[1]
    blk = ATT_BLOCK
    chain = ATT_CHAIN
    rows = blk * chain
    nblk = seq_len // rows
    keep = min(BAND_CHUNKS * CHUNK, seq_len)
    assert seq_len % rows == 0 and keep % rows == 0
    assert (_RING - 1) * blk >= BAND_CHUNKS * CHUNK
    kb = keep // rows
    bias = (_att_prompt_bias(rel_table, blk) * LOG2E).astype(BF16)
    n_steps = n_seq * nblk
    n_fill = (n_all - n_seq * seq_len) // rows
    assert n_fill * rows == n_all - n_seq * seq_len

    def work(t):
        return jnp.minimum(t, n_steps - 1)

    row_spec = pl.BlockSpec(
        (None, rows, d_model), lambda t: (work(t) // nblk, jnp.maximum(work(t) % nblk - (nblk - kb), 0), 0),
        pipeline_mode=pl.Buffered(1))
    rows_out = jax.ShapeDtypeStruct((n_seq, keep, d_model), F32)
    in_specs = [
        pl.BlockSpec((rows, d_model), lambda t: (work(t), 0)),
        _const_spec(w_qkv.shape),
        _const_spec(w_o.shape),
        _const_spec(bias.shape),
        _const_spec(ln_g.shape),
        _const_spec(ln_b.shape),
        _const_spec(wr_hi.shape),
        _const_spec(wr_lo.shape),
        _const_spec(br.shape),
    ]
    args = [x2d, w_qkv, w_o, bias, ln_g, ln_b, wr_hi, wr_lo, br]
    cast_shapes, cast_specs = _cast_plumbing(args, in_specs, cast, n_steps, work)
    return pl.pallas_call(
        functools.partial(_att_prompt_kernel, blk=blk, chain=chain, nblk=nblk, n_steps=n_steps, n_fill=n_fill,
                          n_cast=len(cast_shapes), alpha=alpha),
        grid=(n_steps + n_fill,),
        in_specs=in_specs,
        out_specs=_token_out_specs(rows, d_model, lambda t: (t, 0)) + (row_spec, row_spec) + cast_specs,
        out_shape=_token_out_shapes(n_all, d_model) + (rows_out, rows_out) + cast_shapes,
        scratch_shapes=[
            pltpu.VMEM((_RING, blk, d_model), BF16),
            pltpu.VMEM((_RING, ATT_HEADS, d_model // ATT_HEADS + BF16_SUBLANES, blk), BF16),
            pltpu.VMEM((chain, d_model, blk), BF16),
        ],
        compiler_params=pltpu.CompilerParams(
            dimension_semantics=("arbitrary",), vmem_limit_bytes=VMEM_LIMIT),
        name="att_mixer",
    )(*args)


def _att_sample_kernel(*refs, nb, blk, alpha):
    (x_ref, kc_ref, vc_ref, wqkv_ref, wo_ref, bias_c_ref, bias_n_ref, lng_ref, lnb_ref,
     wrh_ref, wrl_ref, br_ref) = refs[:12]
    h_ref, hp_ref, lgt_ref, krow_ref, vrow_ref, o_scr = refs[-6:]
    d_model = x_ref.shape[1]
    heads = ATT_HEADS
    dh = d_model // heads
    x = x_ref[...]
    xb = x.astype(BF16)
    q = _dot(xb, wqkv_ref[:, 0:d_model]) * (dh ** -0.5)
    k = _dot(xb, wqkv_ref[:, d_model:2 * d_model])
    v = _dot(xb, wqkv_ref[:, 2 * d_model:3 * d_model])
    krow_ref[...] = k.reshape(nb, blk, d_model)
    vrow_ref[...] = v.reshape(nb, blk, d_model)
    lane_head = lax.broadcasted_iota(jnp.int32, (heads, 1, d_model), 2) // dh
    head_id = lax.broadcasted_iota(jnp.int32, (heads, 1, d_model), 0)
    head_mask = (lane_head == head_id).astype(F32)

    for s in range(nb):
        r0 = s * blk
        qs = q[r0:r0 + blk]
        q_bd = (qs[None, :, :] * head_mask).reshape(heads * blk, d_model).astype(BF16)
        kn = k[r0:r0 + blk].astype(BF16)
        vn = v[r0:r0 + blk].astype(BF16)
        s_c = _dot(q_bd, kc_ref[s].astype(BF16)) + bias_c_ref[...]
        s_n = _dot_nt(q_bd, kn) + bias_n_ref[...]
        m = jnp.maximum(s_c.max(axis=-1, keepdims=True), s_n.max(axis=-1, keepdims=True))
        p_c = jnp.exp(s_c - m)
        p_n = jnp.exp(s_n - m)
        l = p_c.sum(axis=-1, keepdims=True) + p_n.sum(axis=-1, keepdims=True)
        o_full = _dot_nt(p_c.astype(BF16), vc_ref[s].astype(BF16)) + _dot(p_n.astype(BF16), vn)
        o_full = o_full * (1.0 / l)
        o = (o_full.reshape(heads, blk, d_model) * head_mask).sum(axis=0)
        o_scr[r0:r0 + blk, :] = o.astype(BF16)

    mix = _dot(o_scr[...], wo_ref[...])
    _finish_tokens(x, mix, alpha, lng_ref, lnb_ref, wrh_ref, wrl_ref, br_ref, h_ref, hp_ref, lgt_ref)


def _att_sample(x2d, in_row0, n_seq, seq_len, k_cache, v_cache, w_qkv, w_o, rel_table, ln_g, ln_b,
                wr_hi, wr_lo, br, alpha, nb, n_all, out_row0, dst):
    d_model = x2d.shape[1]
    heads = ATT_HEADS
    blk = seq_len
    n_cache = k_cache.shape[1]
    rows = nb * blk
    assert n_seq % nb == 0 and in_row0 % rows == 0 and out_row0 % rows == 0
    in_b0, out_b0 = in_row0 // rows, out_row0 // rows
    bias_c = _rel_bias(rel_table, blk, n_cache, n_cache, 1).reshape(heads * blk, n_cache)
    bias_n = _rel_bias(rel_table, blk, blk, 0, 1).reshape(heads * blk, blk)
    kc = jnp.transpose(k_cache, (0, 2, 3, 1)).reshape(n_seq, d_model, n_cache)
    vc = jnp.transpose(v_cache, (0, 2, 3, 1)).reshape(n_seq, d_model, n_cache)
    cache_spec = pl.BlockSpec((nb, d_model, n_cache), lambda g: (g, 0, 0))
    row_spec = pl.BlockSpec((nb, blk, d_model), lambda g: (g, 0, 0))
    rows_out = jax.ShapeDtypeStruct((n_seq, blk, d_model), F32)
    in_specs = [
        pl.BlockSpec((rows, d_model), lambda g: (in_b0 + g, 0)),
        cache_spec,
        cache_spec,
        _const_spec(w_qkv.shape),
        _const_spec(w_o.shape),
        _const_spec(bias_c.shape),
        _const_spec(bias_n.shape),
        _const_spec(ln_g.shape),
        _const_spec(ln_b.shape),
        _const_spec(wr_hi.shape),
        _const_spec(wr_lo.shape),
        _const_spec(br.shape),
    ]
    args = [x2d, kc, vc, w_qkv, w_o, bias_c, bias_n, ln_g, ln_b, wr_hi, wr_lo, br]
    aliases = _alias_dst(args, in_specs, dst)
    return pl.pallas_call(
        functools.partial(_att_sample_kernel, nb=nb, blk=blk, alpha=alpha),
        grid=(n_seq // nb,),
        in_specs=in_specs,
        out_specs=_token_out_specs(rows, d_model, lambda g: (out_b0 + g, 0)) + (row_spec, row_spec),
        out_shape=_token_out_shapes(n_all, d_model) + (rows_out, rows_out),
        input_output_aliases=aliases,
        scratch_shapes=[pltpu.VMEM((rows, d_model), BF16)],
        compiler_params=pltpu.CompilerParams(
            dimension_semantics=("arbitrary",), vmem_limit_bytes=VMEM_LIMIT),
        name="att_mixer_cache",
    )(*args)


ROUTE_ROWS = 40


def _route_kernel(lgt_ref, meta_t_ref, meta_w_ref, count_ref, carry_ref, tri_ref):
    j = pl.program_id(0)
    tm = lgt_ref.shape[0]

    @pl.when(j == 0)
    def _():
        carry_ref[...] = jnp.zeros(carry_ref.shape, F32)
        r = lax.broadcasted_iota(jnp.int32, (tm, tm), 0)
        c = lax.broadcasted_iota(jnp.int32, (tm, tm), 1)
        tri_ref[...] = (r < c).astype(BF16)

    a = jnp.transpose(lgt_ref[...])[0:ROUTE_ROWS, :]
    row = lax.broadcasted_iota(jnp.int32, (ROUTE_ROWS, tm), 0)
    big = jnp.int32(LANES)
    glog = jnp.where(row < N_GROUPS, a, NEG_INF)
    gmax = glog.max(axis=0, keepdims=True)
    gsel = jnp.where(glog == gmax, row, big).min(axis=0, keepdims=True)
    gp = 1.0 / jnp.exp(glog - gmax).sum(axis=0, keepdims=True)
    first = N_GROUPS + gsel * EXP_PER_GROUP
    in_grp = (row >= first) & (row < first + EXP_PER_GROUP)
    elog = jnp.where(in_grp, a, NEG_INF)
    v1 = elog.max(axis=0, keepdims=True)
    l1 = jnp.where(elog == v1, row, big).min(axis=0, keepdims=True)
    elog2 = jnp.where(row == l1, NEG_INF, elog)
    v2 = elog2.max(axis=0, keepdims=True)
    l2 = jnp.where(elog2 == v2, row, big).min(axis=0, keepdims=True)
    e2 = jnp.exp(v2 - v1)
    w1 = gp / (1.0 + e2)
    w2 = gp * e2 / (1.0 + e2)

    is1 = row == l1
    is2 = row == l2
    oh = (is1 | is2).astype(F32)
    before = _dot(oh.astype(BF16), tri_ref[...]) + carry_ref[:, 0:1]
    rank1 = jnp.where(is1, before, 0.0).sum(axis=0, keepdims=True)
    rank2 = jnp.where(is2, before, 0.0).sum(axis=0, keepdims=True)
    carry_ref[...] = carry_ref[...] + oh.sum(axis=1, keepdims=True)
    count_ref[...] = carry_ref[...].astype(jnp.int32)

    pad = jnp.zeros((SUBLANES - 4, tm), jnp.int32)
    meta_t_ref[...] = jnp.concatenate(
        [l1 - N_GROUPS, l2 - N_GROUPS, rank1.astype(jnp.int32), rank2.astype(jnp.int32), pad], axis=0)
    wt = jnp.concatenate([w1, w2, jnp.zeros((LANES - 2, tm), F32)], axis=0)
    meta_w_ref[...] = jnp.transpose(wt)


def _route(logits):
    n_tok = logits.shape[0]
    tm = TOKEN_TILE
    assert n_tok % tm == 0
    tile = pl.BlockSpec((tm, LANES), lambda j: (j, 0))
    return pl.pallas_call(
        _route_kernel,
        grid=(n_tok // tm,),
        in_specs=[tile],
        out_specs=(pl.BlockSpec((SUBLANES, tm), lambda j: (0, j)), tile,
                   pl.BlockSpec((ROUTE_ROWS, LANES), lambda j: (0, 0))),
        out_shape=(
            jax.ShapeDtypeStruct((SUBLANES, n_tok), jnp.int32),
            jax.ShapeDtypeStruct((n_tok, LANES), F32),
            jax.ShapeDtypeStruct((ROUTE_ROWS, LANES), jnp.int32),
        ),
        scratch_shapes=[pltpu.VMEM((ROUTE_ROWS, LANES), F32), pltpu.VMEM((tm, tm), BF16)],
        compiler_params=pltpu.CompilerParams(dimension_semantics=("arbitrary",)),
        name="moe_route",
    )(logits)


def _sc_gather_loop(table_hbm, idx_v, out_hbm, rows_v, sems, base, n_chunk, r):
    def gather(c, slot):
        off = pl.multiple_of(c * r, r)
        return pltpu.make_async_copy(table_hbm.at[idx_v.at[pl.ds(off, r)]], rows_v.at[slot], sems.at[slot])

    def finish(c, slot):
        gather(c, slot).wait()
        pltpu.sync_copy(rows_v.at[slot], out_hbm.at[pl.ds(base + pl.multiple_of(c * r, r), r)])

    gather(0, 0).start()
    if n_chunk > 1:
        gather(1, 1).start()

    def body(pair, carry):
        c = 2 * pair
        for slot in range(2):
            finish(c + slot, slot)

            @pl.when(c + slot + 2 < n_chunk)
            def _():
                gather(c + slot + 2, slot).start()
        return carry

    lax.fori_loop(0, n_chunk // 2, body, 0)
    if n_chunk % 2:
        finish(n_chunk - 1, 0)


def _gather_rows(table, idx):
    m = idx.shape[0]
    width = table.shape[1]
    r = SC_ROWS_PER_CHUNK
    assert m % (SC_WORKERS * r) == 0
    per_w = m // SC_WORKERS
    mesh = plsc.VectorSubcoreMesh(core_axis_name="c", subcore_axis_name="s")

    @functools.partial(
        pl.kernel,
        mesh=mesh,
        out_type=jax.ShapeDtypeStruct((m, width), table.dtype),
        scratch_types=[
            pltpu.VMEM((per_w,), jnp.int32),
            pltpu.VMEM((2, r, width), table.dtype),
            pltpu.SemaphoreType.DMA((2,)),
        ],
    )
    def gather(table_hbm, idx_hbm, out_hbm, idx_v, rows_v, sem):
        wid = lax.axis_index("s") * 2 + lax.axis_index("c")
        base = wid * per_w
        pltpu.sync_copy(idx_hbm.at[pl.ds(base, per_w)], idx_v)
        _sc_gather_loop(table_hbm, idx_v, out_hbm, rows_v, sem, base, per_w // r, r)

    return gather(table, idx)


def _dispatch_rows(table, pos, row0, n_rows):
    n_tok, width = table.shape
    n_pairs = pos.shape[0]
    r = SC_ROWS_PER_CHUNK
    lanes = SC_LANES
    assert n_rows % (SC_WORKERS * r) == 0
    per_w = n_rows // SC_WORKERS
    n_stage = 16
    stage = n_pairs // n_stage
    assert stage * n_stage == n_pairs and stage % lanes == 0 and per_w % lanes == 0
    assert row0 + n_rows < 3 * n_tok
    mesh = plsc.VectorSubcoreMesh(core_axis_name="c", subcore_axis_name="s")

    @functools.partial(
        pl.kernel,
        mesh=mesh,
        out_type=jax.ShapeDtypeStruct((n_rows, width), table.dtype),
        scratch_types=[
            pltpu.VMEM((per_w,), jnp.int32),
            pltpu.VMEM((stage,), jnp.int32),
            pltpu.VMEM((2, r, width), table.dtype),
            pltpu.SemaphoreType.DMA((2,)),
        ],
        compiler_params=pltpu.CompilerParams(needs_layout_passes=False),
    )
    def dispatch(table_hbm, pos_hbm, out_hbm, src_v, pos_v, rows_v, sem):
        wid = lax.axis_index("s") * 2 + lax.axis_index("c")
        out_base = wid * per_w
        base = row0 + out_base
        lane = lax.iota(jnp.int32, lanes)

        def wrap(t):
            t = jnp.where(t >= n_tok, t - n_tok, t)
            return jnp.where(t >= n_tok, t - n_tok, t)

        def init(i, carry):
            off = pl.multiple_of(i * lanes, lanes)
            src_v[pl.ds(off, lanes)] = wrap(base + off + lane)
            return carry

        lax.fori_loop(0, per_w // lanes, init, 0)

        def scan_stage(sidx, carry):
            pair0 = sidx * stage
            pltpu.sync_copy(pos_hbm.at[pl.ds(pl.multiple_of(pair0, 8), stage)], pos_v)

            @plsc.parallel_loop(0, stage // lanes, unroll=4)
            def _(i):
                off = pl.multiple_of(i * lanes, lanes)
                local = pos_v[pl.ds(off, lanes)] - base
                mine = (local >= 0) & (local < per_w)
                plsc.store_scatter(src_v, [jnp.where(mine, local, 0)], wrap(pair0 + off + lane), mask=mine)

            return carry

        lax.fori_loop(0, n_stage, scan_stage, 0)
        _sc_gather_loop(table_hbm, src_v, out_hbm, rows_v, sem, out_base, per_w // r, r)

    return dispatch(table, pos)


def _gemm_kernel(tile_expert_ref, n_used_ref, xs_ref, wg_ref, wu_ref, wd_ref, *rest, tile0):
    ys_ref = rest[-1]
    tile = tile0 + pl.program_id(0)

    @pl.when(tile < n_used_ref[0])
    def _():
        x = _unpack_rows(xs_ref[...]).astype(BF16)
        g = _dot(x, wg_ref[...])
        u = _dot(x, wu_ref[...])
        hmid = (jax.nn.silu(g) * u).astype(BF16)
        ys_ref[...] = _pack_rows(_dot(hmid, wd_ref[...]))


def _expert_gemm(xs, tile_expert, n_used, w_gate, w_up, w_down, tile0, n_rows_all, ys):
    n_rows, half = xs.shape
    d_model, d_exp = w_gate.shape[1], w_gate.shape[2]
    tm = GEMM_TILE
    n_tiles = n_rows // tm

    def last_used(j, nu):
        return jnp.minimum(tile0 + j, nu[0] - 1)

    def in_map(j, te, nu):
        return (jnp.maximum(last_used(j, nu) - tile0, 0), 0)

    def out_map(j, te, nu):
        return (jnp.maximum(last_used(j, nu), tile0), 0)

    def w_map(j, te, nu):
        return (te[tile0 + j], 0, 0)

    in_specs = [
        pl.BlockSpec((tm, half), in_map),
        pl.BlockSpec((None, d_model, d_exp), w_map),
        pl.BlockSpec((None, d_model, d_exp), w_map),
        pl.BlockSpec((None, d_exp, d_model), w_map),
    ]
    args = [tile_expert, n_used, xs, w_gate, w_up, w_down]
    aliases = {}
    if ys is not None:
        aliases[len(args)] = 0
        args.append(ys)
        in_specs.append(pl.BlockSpec(memory_space=pl.ANY))
    grid_spec = pltpu.PrefetchScalarGridSpec(
        num_scalar_prefetch=2,
        grid=(n_tiles,),
        in_specs=in_specs,
        out_specs=pl.BlockSpec((tm, half), out_map),
    )
    return pl.pallas_call(
        functools.partial(_gemm_kernel, tile0=tile0),
        grid_spec=grid_spec,
        out_shape=jax.ShapeDtypeStruct((n_rows_all, half), U32),
        input_output_aliases=aliases,
        compiler_params=pltpu.CompilerParams(
            dimension_semantics=("arbitrary",), vmem_limit_bytes=VMEM_LIMIT),
        name="moe_gemm",
    )(*args)


def _combine_kernel(y0_ref, y1_ref, mw_ref, h_ref, pp_ref, ps_ref, lng_ref, lnb_ref, wproj_ref, wgate_ref,
                    *out_refs, n_prompt_tiles, alpha):
    j = pl.program_id(0)
    tm = h_ref.shape[0]
    half = tm // 2

    def normed(rows):
        mw = mw_ref[rows, :]
        ffn = mw[:, 0:1] * _unpack_rows(y0_ref[rows, :]) + mw[:, 1:2] * _unpack_rows(y1_ref[rows, :])
        return _layer_norm(alpha * h_ref[rows, :] + ffn, lng_ref[...], lnb_ref[...])

    def gated(rows, h2):
        p = jnp.where(j < n_prompt_tiles, pp_ref[rows, :], ps_ref[rows, :]).astype(BF16)
        gate = jax.nn.sigmoid(_dot(h2.astype(BF16), wgate_ref[...]))
        return h2 + gate * _dot(p, wproj_ref[...])

    rows_a, rows_b = slice(0, half), slice(half, tm)
    h2_a = normed(rows_a)
    h2_b = normed(rows_b)
    out = jnp.concatenate([gated(rows_a, h2_a), gated(rows_b, h2_b)], axis=0)
    if len(out_refs) == 1:
        out_refs[0][...] = out
    else:
        @pl.when(j < n_prompt_tiles)
        def _():
            out_refs[0][...] = out

        @pl.when(j >= n_prompt_tiles)
        def _():
            out_refs[1][...] = out


def _combine(yg, meta_w, h, p_prompt, p_sample, layer, ln_g, ln_b, w_proj, w_gate, alpha, split_out):
    n_tok, d_model = h.shape
    tm = TOKEN_TILE
    n_t = n_tok // tm
    n_tp = p_prompt.shape[1] // tm
    n_ts = p_sample.shape[1] // tm
    assert n_tp * tm == p_prompt.shape[1] and n_ts * tm == p_sample.shape[1] and n_tp + n_ts == n_t
    ple = p_prompt.shape[2]
    tile = pl.BlockSpec((tm, d_model), lambda j: (j, 0))
    if split_out:
        out_specs = (pl.BlockSpec((tm, d_model), lambda j: (jnp.minimum(j, n_tp - 1), 0)),
                     pl.BlockSpec((tm, d_model), lambda j: (jnp.maximum(j - n_tp, 0), 0)))
        out_shape = (jax.ShapeDtypeStruct((n_tp * tm, d_model), F32),
                     jax.ShapeDtypeStruct((n_ts * tm, d_model), F32))
    else:
        out_specs = tile
        out_shape = jax.ShapeDtypeStruct((n_tok, d_model), F32)
    return pl.pallas_call(
        functools.partial(_combine_kernel, n_prompt_tiles=n_tp, alpha=alpha),
        grid=(n_t,),
        in_specs=[
            pl.BlockSpec((tm, d_model // 2), lambda j: (j, 0)),
            pl.BlockSpec((tm, d_model // 2), lambda j: (j + n_t, 0)),
            pl.BlockSpec((tm, LANES), lambda j: (j, 0)),
            tile,
            pl.BlockSpec((None, tm, ple), lambda j: (layer, jnp.minimum(j, n_tp - 1), 0)),
            pl.BlockSpec((None, tm, ple), lambda j: (layer, jnp.maximum(j - n_tp, 0), 0)),
            _const_spec(ln_g.shape),
            _const_spec(ln_b.shape),
            _const_spec(w_proj.shape),
            _const_spec(w_gate.shape),
        ],
        out_specs=out_specs,
        out_shape=out_shape,
        compiler_params=pltpu.CompilerParams(
            dimension_semantics=("arbitrary",), vmem_limit_bytes=VMEM_LIMIT),
        name="moe_combine",
    )(yg, yg, meta_w, h, p_prompt, p_sample, ln_g, ln_b, w_proj, w_gate)


def _router_weights(w_grp, b_grp, w_exp, b_exp):
    d_model = w_grp.shape[0]
    w = jnp.concatenate([w_grp, jnp.transpose(w_exp, (1, 0, 2)).reshape(d_model, N_EXPERTS)], axis=1)
    w = jnp.pad(w, ((0, 0), (0, LANES - w.shape[1])))
    b = jnp.concatenate([b_grp, b_exp.reshape(N_EXPERTS)])
    b = jnp.pad(b, (0, LANES - b.shape[0])).reshape(1, LANES)
    w_hi = w.astype(BF16)
    w_lo = (w - w_hi.astype(F32)).astype(BF16)
    return w_hi, w_lo, b


def _moe(h_packed, logits, w_gate, w_up, w_down):
    n_tok = h_packed.shape[0]
    tm = GEMM_TILE
    meta_t, meta_w, counts = _route(logits)
    counts = counts[N_GROUPS:N_GROUPS + N_EXPERTS, 0]
    tiles_per_expert = (counts + tm - 1) // tm
    tile_end = jnp.cumsum(tiles_per_expert)
    row_start = (tile_end - tiles_per_expert) * tm
    eid = meta_t[0:2]
    rank = meta_t[2:4]
    experts = jnp.arange(N_EXPERTS, dtype=jnp.int32)
    start = jnp.sum(jnp.where(eid[:, :, None] == experts, row_start, 0), axis=-1)
    pos = (start + rank).reshape(-1).astype(jnp.int32)
    part_quant = MOE_PARTS * SC_WORKERS * SC_ROWS_PER_CHUNK * (tm // math.gcd(tm, SC_WORKERS * SC_ROWS_PER_CHUNK))
    n_tiles = -(-(2 * n_tok) // tm) + N_EXPERTS
    n_rows = -(-(n_tiles * tm) // part_quant) * part_quant
    n_tiles = n_rows // tm
    n_used = tile_end[-1:].astype(jnp.int32)
    tile_ids = jnp.minimum(jnp.arange(n_tiles, dtype=jnp.int32), n_used[0] - 1)
    tile_expert = jnp.sum(tile_end[None, :] <= tile_ids[:, None], axis=1).astype(jnp.int32)
    part_rows = n_rows // MOE_PARTS
    xs_parts = [_dispatch_rows(h_packed, pos, part * part_rows, part_rows) for part in range(MOE_PARTS)]
    ys = None
    for part, xs in enumerate(xs_parts):
        ys = _expert_gemm(xs, tile_expert, n_used, w_gate, w_up, w_down,
                          part * (part_rows // tm), n_rows, ys)
    return _gather_rows(ys, pos), meta_w


def kernel(x_prompt, x_sample, p_prompt, p_sample, state_ret, cache_att_k, cache_att_v, ret_w_in, ret_gn_g,
           ret_w_o, att_w_qkv, att_rel_bias, att_w_o, ln1_g, ln1_b, ln2_g, ln2_b, moe_w_grp, moe_b_grp,
           moe_w_exp, moe_b_exp, moe_w_gate, moe_w_up, moe_w_down, ple_w_proj, ple_w_gate):
    n_p, len_p, d_model = x_prompt.shape
    n_s, len_s, _ = x_sample.shape
    depth = ln1_g.shape[0]
    alpha = float((2 * depth) ** 0.25)
    tok_p, tok_s = n_p * len_p, n_s * len_s
    n_all = tok_p + tok_s
    dh = d_model // ATT_HEADS
    pp = p_prompt.reshape(depth, tok_p, -1)
    ps = p_sample.reshape(depth, tok_s, -1)
    nb_s = 4

    x_all = None
    y_prompt = y_sample = None
    states_p, states_s, k_p, v_p, k_s, v_s = [], [], [], [], [], []
    for i in range(depth):
        jj = i // 2
        wr_hi, wr_lo, br = _router_weights(moe_w_grp[i], moe_b_grp[i], moe_w_exp[i], moe_b_exp[i])
        lng, lnb = ln1_g[i].reshape(1, d_model), ln1_b[i].reshape(1, d_model)
        moe_cast = (i, [moe_w_gate, moe_w_up, moe_w_down])
        if x_all is None:
            src_p, src_s, row_s = x_prompt.reshape(tok_p, d_model), x_sample.reshape(tok_s, d_model), 0
        else:
            src_p, src_s, row_s = x_all, x_all, tok_p
        if i % 2 == 0:
            w_in = ret_w_in[jj].astype(BF16)
            w_o = ret_w_o[jj].astype(BF16)
            gn = ret_gn_g[jj].reshape(1, -1)
            h, hp, lgt, st_p, *w_moe = _ret_mixer(
                src_p, 0, n_p, len_p, 0, None, w_in, w_o, gn, lng, lnb, wr_hi, wr_lo, br, alpha, nb=1,
                blk=min(RET_BLOCK, len_p), chain=RET_CHAIN, n_all=n_all, out_row0=0, dst=None, cast=moe_cast)
            h, hp, lgt, st_s = _ret_mixer(src_s, row_s, n_s, len_s, PAST_LEN, state_ret[jj], w_in, w_o, gn,
                                          lng, lnb, wr_hi, wr_lo, br, alpha, nb=nb_s, blk=len_s, chain=1,
                                          n_all=n_all, out_row0=tok_p, dst=(h, hp, lgt))
            states_p.append(st_p)
            states_s.append(st_s)
        else:
            w_qkv = att_w_qkv[jj].astype(BF16)
            w_o = att_w_o[jj].astype(BF16)
            h, hp, lgt, kr, vr, *w_moe = _att_prompt(src_p, n_p, len_p, w_qkv, w_o, att_rel_bias[jj], lng, lnb,
                                                     wr_hi, wr_lo, br, alpha, n_all, cast=moe_cast)
            k_p.append(kr.reshape(n_p, -1, ATT_HEADS, dh))
            v_p.append(vr.reshape(n_p, -1, ATT_HEADS, dh))
            h, hp, lgt, kr, vr = _att_sample(src_s, row_s, n_s, len_s, cache_att_k[jj], cache_att_v[jj],
                                             w_qkv, w_o, att_rel_bias[jj], lng, lnb, wr_hi, wr_lo, br, alpha,
                                             nb=nb_s, n_all=n_all, out_row0=tok_p, dst=(h, hp, lgt))
            k_s.append(kr.reshape(n_s, len_s, ATT_HEADS, dh))
            v_s.append(vr.reshape(n_s, len_s, ATT_HEADS, dh))
        n_exp = moe_w_gate.shape[1]
        w_moe = [w.reshape(n_exp, -1, w.shape[1]) for w in w_moe]
        yg, meta_w = _moe(hp, lgt, *w_moe)
        last = i == depth - 1
        out = _combine(yg, meta_w, h, pp, ps, i, ln2_g[i].reshape(1, d_model), ln2_b[i].reshape(1, d_model),
                       ple_w_proj[i].astype(BF16), ple_w_gate[i].astype(BF16), alpha, split_out=last)
        if last:
            y_prompt = out[0].reshape(n_p, len_p, d_model)
            y_sample = out[1].reshape(n_s, len_s, d_model)
        else:
            x_all = out

    return (y_prompt, y_sample, jnp.stack(states_p), jnp.stack(states_s),
            jnp.stack(k_p), jnp.stack(v_p), jnp.stack(k_s), jnp.stack(v_s))
```

```python
import functools
import math

import numpy as np
import jax
import jax.numpy as jnp
from jax import lax
from jax.experimental import pallas as pl
from jax.experimental.pallas import tpu as pltpu
from jax.experimental.pallas import tpu_sc as plsc

CHUNK = 64
PAST_LEN = 2048
RET_HEADS = 4
ROPE_BASE = 10000.0
ATT_HEADS = 16
BAND_CHUNKS = 8
REL_CLIP = 256
N_GROUPS = 4
EXP_PER_GROUP = 8
N_EXPERTS = N_GROUPS * EXP_PER_GROUP
LN_EPS = 1e-5
NEG_INF = -1e30

LANES = 128
SUBLANES = 8
BF16_SUBLANES = 16
SC_WORKERS = 32
SC_LANES = 16
SC_ROWS_PER_CHUNK = 32
VMEM_LIMIT = 58 * 1024 * 1024

RET_BLOCK = 256
RET_CHAIN = 2
ATT_BLOCK = 4 * CHUNK
ATT_CHAIN = 2
TOKEN_TILE = 512
GEMM_TILE = 512
MOE_PARTS = 4

F32 = jnp.float32
BF16 = jnp.bfloat16
U32 = jnp.uint32
HI_MASK = 0xFFFF0000
LOG2E = math.log2(math.e)


def _dot(a, b):
    return jnp.dot(a, b, preferred_element_type=F32)


def _dot_nt(a, b):
    return lax.dot_general(a, b, (((1,), (1,)), ((), ())), preferred_element_type=F32)


def _dot_tn(a, b):
    return lax.dot_general(a, b, (((0,), (0,)), ((), ())), preferred_element_type=F32)


def _layer_norm(x, g, b):
    mu = jnp.mean(x, axis=-1, keepdims=True)
    xc = x - mu
    var = jnp.mean(xc * xc, axis=-1, keepdims=True)
    return xc * lax.rsqrt(var + LN_EPS) * g + b


def _pack_rows(x):
    half = x.shape[1] // 2
    lo = lax.bitcast_convert_type(x[:, :half].astype(BF16).astype(F32), U32) >> 16
    hi = lax.bitcast_convert_type(x[:, half:].astype(BF16).astype(F32), U32) & U32(HI_MASK)
    return lo | hi


def _unpack_rows(p):
    lo = lax.bitcast_convert_type(p << 16, F32)
    hi = lax.bitcast_convert_type(p & U32(HI_MASK), F32)
    return jnp.concatenate([lo, hi], axis=1)


def _router_logits(h, wr_hi_ref, wr_lo_ref, br_ref):
    h_hi = h.astype(BF16)
    h_lo = (h - h_hi.astype(F32)).astype(BF16)
    w_hi = wr_hi_ref[...]
    return _dot(h_hi, w_hi) + _dot(h_lo, w_hi) + _dot(h_hi, wr_lo_ref[...]) + br_ref[...]


def _finish_tokens(x, mix, alpha, lng_ref, lnb_ref, wrh_ref, wrl_ref, br_ref, h_ref, hp_ref, lgt_ref):
    hh = _layer_norm(alpha * x + mix, lng_ref[...], lnb_ref[...])
    h_ref[...] = hh
    hp_ref[...] = _pack_rows(hh)
    lgt_ref[...] = _router_logits(hh, wrh_ref, wrl_ref, br_ref)


def _zero_tokens(h_ref, hp_ref, lgt_ref):
    h_ref[...] = jnp.zeros(h_ref.shape, h_ref.dtype)
    hp_ref[...] = jnp.zeros(hp_ref.shape, hp_ref.dtype)
    lgt_ref[...] = jnp.zeros(lgt_ref.shape, lgt_ref.dtype)


def _const_spec(shape):
    nd = len(shape)
    return pl.BlockSpec(shape, lambda *_: (0,) * nd, pipeline_mode=pl.Buffered(1))


def _token_out_shapes(n_all, d_model):
    return (
        jax.ShapeDtypeStruct((n_all, d_model), F32),
        jax.ShapeDtypeStruct((n_all, d_model // 2), U32),
        jax.ShapeDtypeStruct((n_all, LANES), F32),
    )


def _token_out_specs(rows, d_model, row_map):
    return (
        pl.BlockSpec((rows, d_model), row_map),
        pl.BlockSpec((rows, d_model // 2), row_map),
        pl.BlockSpec((rows, LANES), row_map),
    )


def _alias_dst(args, in_specs, dst):
    if dst is None:
        return {}
    aliases = {}
    for k, arr in enumerate(dst):
        aliases[len(args)] = k
        args.append(arr)
        in_specs.append(pl.BlockSpec(memory_space=pl.ANY))
    return aliases


def _cast_plumbing(args, in_specs, cast, n_steps, work):
    if cast is None:
        return (), ()
    layer, weights = cast
    shapes, specs = [], []
    for w in weights:
        depth, n_exp, rows, cols = w.shape
        assert (n_exp * rows) % n_steps == 0
        slab = (n_exp * rows) // n_steps
        args.append(w.reshape(depth, n_exp * rows, cols))
        in_specs.append(pl.BlockSpec((None, slab, cols), lambda t: (layer, work(t), 0)))
        shapes.append(jax.ShapeDtypeStruct((n_exp * rows, cols), BF16))
        specs.append(pl.BlockSpec((slab, cols), lambda t: (work(t), 0)))
    return tuple(shapes), tuple(specs)


def _cast_slabs(in_refs, out_refs):
    for src, dst in zip(in_refs, out_refs):
        dst[...] = src[...].astype(BF16)


def _ret_log_gamma():
    h = np.arange(RET_HEADS, dtype=np.float32)
    return np.log(np.float32(1.0) - np.float32(2.0) ** (np.float32(-5.0) - h)).astype(np.float32)


def _ret_kernel(*refs, nb, blk, chain, nblk, n_steps, n_fill, has_state, n_alias, n_cast, alpha):
    step = pl.program_id(0)
    n_in = 12 + int(has_state) + n_alias + n_cast
    h_ref, hp_ref, lgt_ref = refs[n_in:n_in + 3]
    core = refs[:12 + int(has_state)] + refs[n_in:n_in + 4] + refs[-1:]

    @pl.when(step < n_steps)
    def _():
        _cast_slabs(refs[n_in - n_cast:n_in], refs[n_in + 4:n_in + 4 + n_cast])
        _ret_step(*core, first_block=step % nblk == 0, nb=nb, blk=blk, chain=chain, has_state=has_state,
                  alpha=alpha)

    if n_fill:
        @pl.when(step >= n_steps)
        def _():
            _zero_tokens(h_ref, hp_ref, lgt_ref)


def _ret_step(*refs, first_block, nb, blk, chain, has_state, alpha):
    (x_ref, cos_ref, sin_ref, dmask_ref, win_ref, wo_ref, gn_ref, lng_ref, lnb_ref,
     wrh_ref, wrl_ref, br_ref) = refs[:12]
    s_in_ref = refs[12] if has_state else None
    h_ref, hp_ref, lgt_ref, s_out_ref, gated_ref = refs[12 + int(has_state):]
    heads = RET_HEADS
    d_model = x_ref.shape[1]
    dk = d_model // heads
    dv = 2 * d_model // heads
    hk, hv = heads * dk, heads * dv
    half = dk // 2
    lg = _ret_log_gamma()

    n_sub = nb * chain
    proj_rows = blk if chain > 1 else n_sub * blk
    rowf = lax.broadcasted_iota(jnp.int32, (blk, 1), 0).astype(F32)

    def proj(group):
        xb = x_ref[group * proj_rows:(group + 1) * proj_rows, :].astype(BF16)
        return (_dot(xb, win_ref[:, 0:hk]), _dot(xb, win_ref[:, hk:2 * hk]),
                _dot(xb, win_ref[:, 2 * hk:2 * hk + hv]), _dot(xb, win_ref[:, 2 * hk + hv:2 * hk + 2 * hv]))

    if not has_state:
        @pl.when(first_block)
        def _():
            s_out_ref[...] = jnp.zeros(s_out_ref.shape, F32)

    s_prev_ref = s_in_ref if has_state else s_out_ref

    def head(j, h, projected):
        q_all, k_all, v_all, g_all = projected
        s = j // chain
        c0 = (j % chain) * blk
        r0 = (j * blk) % proj_rows
        cos = cos_ref[c0:c0 + blk, :]
        sin = sin_ref[c0:c0 + blk, :]

        def rot(t):
            t1, t2 = t[:, :half], t[:, half:]
            return jnp.concatenate([t1 * cos - t2 * sin, t1 * sin + t2 * cos], axis=1)

        lgh = float(lg[h])
        q = rot(q_all[r0:r0 + blk, h * dk:(h + 1) * dk])
        k = rot(k_all[r0:r0 + blk, h * dk:(h + 1) * dk]) * (dk ** -0.5)
        v = v_all[r0:r0 + blk, h * dv:(h + 1) * dv]
        g = g_all[r0:r0 + blk, h * dv:(h + 1) * dv]
        vb = v.astype(BF16)
        scores = _dot_nt(q.astype(BF16), k.astype(BF16)) * dmask_ref[h]
        inner = _dot(scores.astype(BF16), vb)
        s_prev = s_prev_ref[s, h]
        q_dec = q * jnp.exp(lgh * (rowf + 1.0))
        cross = _dot(q_dec.astype(BF16), s_prev.astype(BF16))
        k_dec = k * jnp.exp(lgh * (float(blk - 1) - rowf))
        s_out_ref[s, h] = math.exp(lgh * blk) * s_prev + _dot_tn(k_dec.astype(BF16), vb)
        o = inner + cross
        mu = jnp.mean(o, axis=-1, keepdims=True)
        oc = o - mu
        var = jnp.mean(oc * oc, axis=-1, keepdims=True)
        on = oc * lax.rsqrt(var + LN_EPS) * gn_ref[:, h * dv:(h + 1) * dv]
        gated_ref[j * blk:(j + 1) * blk, h * dv:(h + 1) * dv] = (jax.nn.silu(g) * on).astype(BF16)

    def tail(group):
        rows = slice(group * proj_rows, (group + 1) * proj_rows)
        mix = _dot(gated_ref[rows, :], wo_ref[...])
        _finish_tokens(x_ref[rows, :], mix, alpha, lng_ref, lnb_ref, wrh_ref, wrl_ref, br_ref,
                       h_ref.at[rows, :], hp_ref.at[rows, :], lgt_ref.at[rows, :])

    n_groups = (n_sub * blk) // proj_rows
    subs_per_group = proj_rows // blk
    projected = proj(0)
    for group in range(n_groups):
        nxt = None
        for jj in range(subs_per_group):
            for h in range(heads):
                head(group * subs_per_group + jj, h, projected)
                if jj == 0 and h == 0 and group + 1 < n_groups:
                    nxt = proj(group + 1)
                if jj == 0 and h == 1 and group > 0:
                    tail(group - 1)
        projected = nxt
    tail(n_groups - 1)


def _ret_mixer(x2d, in_row0, n_seq, seq_len, pos0, state_in, w_in, w_o, gn_g, ln_g, ln_b,
               wr_hi, wr_lo, br, alpha, nb, blk, chain, n_all, out_row0, dst, cast=None):
    d_model = x2d.shape[1]
    heads = RET_HEADS
    dk, dv = d_model // heads, 2 * d_model // heads
    half = dk // 2
    nblk = seq_len // (blk * chain)
    has_state = state_in is not None
    rows = nb * blk * chain
    assert seq_len % (blk * chain) == 0 and n_seq % nb == 0
    assert (not has_state) or nblk == 1
    assert nb == 1 or (nblk == 1 and chain == 1)
    assert in_row0 % rows == 0 and out_row0 % rows == 0
    in_b0, out_b0 = in_row0 // rows, out_row0 // rows

    pos = (pos0 + jnp.arange(seq_len, dtype=jnp.int32)).astype(F32)
    inv_freq = ROPE_BASE ** (-jnp.arange(half, dtype=F32) / half)
    ang = pos[:, None] * inv_freq[None, :]
    cos, sin = jnp.cos(ang), jnp.sin(ang)
    lg = jnp.asarray(_ret_log_gamma())
    ii = jnp.arange(blk, dtype=F32)
    diff = ii[:, None] - ii[None, :]
    dmask = jnp.where(diff >= 0, jnp.exp(lg[:, None, None] * jnp.maximum(diff, 0.0)), 0.0)

    n_steps = (n_seq // nb) * nblk
    n_fill = 0 if dst is not None else (n_all - n_seq * seq_len) // rows
    assert dst is not None or (out_row0 == 0 and n_fill * rows == n_all - n_seq * seq_len)

    def work(t):
        return jnp.minimum(t, n_steps - 1)

    in_specs = [
        pl.BlockSpec((rows, d_model), lambda t: (in_b0 + work(t), 0)),
        pl.BlockSpec((blk * chain, half), lambda t: (work(t) % nblk, 0)),
        pl.BlockSpec((blk * chain, half), lambda t: (work(t) % nblk, 0)),
        _const_spec(dmask.shape),
        _const_spec(w_in.shape),
        _const_spec(w_o.shape),
        _const_spec(gn_g.shape),
        _const_spec(ln_g.shape),
        _const_spec(ln_b.shape),
        _const_spec(wr_hi.shape),
        _const_spec(wr_lo.shape),
        _const_spec(br.shape),
    ]
    args = [x2d, cos, sin, dmask, w_in, w_o, gn_g, ln_g, ln_b, wr_hi, wr_lo, br]
    state_spec = pl.BlockSpec((nb, heads, dk, dv), lambda t: (work(t) // nblk, 0, 0, 0))
    if has_state:
        in_specs.append(state_spec)
        args.append(state_in)
    aliases = _alias_dst(args, in_specs, dst)
    cast_shapes, cast_specs = _cast_plumbing(args, in_specs, cast, n_steps, work)
    out_shape = (_token_out_shapes(n_all, d_model) + (jax.ShapeDtypeStruct((n_seq, heads, dk, dv), F32),)
                 + cast_shapes)
    out_specs = _token_out_specs(rows, d_model, lambda t: (out_b0 + t, 0)) + (state_spec,) + cast_specs
    return pl.pallas_call(
        functools.partial(_ret_kernel, nb=nb, blk=blk, chain=chain, nblk=nblk, n_steps=n_steps, n_fill=n_fill,
                          has_state=has_state, n_alias=len(aliases), n_cast=len(cast_shapes), alpha=alpha),
        grid=(n_steps + n_fill,),
        in_specs=in_specs,
        out_specs=out_specs,
        out_shape=out_shape,
        input_output_aliases=aliases,
        scratch_shapes=[pltpu.VMEM((rows, heads * dv), BF16)],
        compiler_params=pltpu.CompilerParams(
            dimension_semantics=("arbitrary",), vmem_limit_bytes=VMEM_LIMIT),
        name="ret_mixer_state" if has_state else "ret_mixer",
    )(*args)


_RING = 3


def _att_prompt_kernel(*refs, blk, chain, nblk, n_steps, n_fill, n_cast, alpha):
    step = pl.program_id(0)
    n_in = 9 + n_cast
    h_ref, hp_ref, lgt_ref = refs[n_in:n_in + 3]
    core = refs[:9] + refs[n_in:n_in + 5] + refs[-3:]

    @pl.when(step < n_steps)
    def _():
        _cast_slabs(refs[9:n_in], refs[n_in + 5:n_in + 5 + n_cast])
        _att_prompt_step(*core, i0=(step % nblk) * chain, blk=blk, chain=chain, alpha=alpha)

    if n_fill:
        @pl.when(step >= n_steps)
        def _():
            _zero_tokens(h_ref, hp_ref, lgt_ref)


def _att_prompt_step(x_ref, wqkv_ref, wo_ref, bias_ref, lng_ref, lnb_ref, wrh_ref, wrl_ref, br_ref,
                     h_ref, hp_ref, lgt_ref, krow_ref, vrow_ref, kring, vring, o_scr, *, i0, blk, chain, alpha):
    d_model = x_ref.shape[1]
    dh = d_model // ATT_HEADS
    heads_per_group = LANES // dh
    lane = lax.broadcasted_iota(jnp.int32, (1, LANES), 1)

    @pl.when(i0 == 0)
    def _():
        kring[...] = jnp.zeros(kring.shape, BF16)
        vring[...] = jnp.zeros(vring.shape, BF16)
        vring[:, :, dh:, :] = jnp.ones((_RING, ATT_HEADS, vring.shape[2] - dh, blk), BF16)

    def project(j):
        xb = x_ref[j * blk:(j + 1) * blk, :].astype(BF16)
        q = _dot(xb, wqkv_ref[:, 0:d_model]) * (dh ** -0.5 * LOG2E)
        k = _dot(xb, wqkv_ref[:, d_model:2 * d_model])
        v = _dot(xb, wqkv_ref[:, 2 * d_model:3 * d_model])
        return q, k, v

    def to_ring(j, k, v):
        slot = (i0 + j) % _RING
        krow_ref[j * blk:(j + 1) * blk, :] = k
        vrow_ref[j * blk:(j + 1) * blk, :] = v
        kring[slot] = k.astype(BF16)
        v_t = jnp.transpose(v).astype(BF16)
        for hd in range(ATT_HEADS):
            vring[slot, hd, 0:dh, :] = v_t[hd * dh:(hd + 1) * dh, :]

    def attend(j, q, between):
        i = i0 + j
        qb = q.astype(BF16)
        behind = [(i + _RING - k) % _RING for k in range(_RING)]
        slabs = [jnp.where(i >= behind[k], behind[k], _RING) for k in range(_RING)]

        def scores(hd):
            c0 = (hd // heads_per_group) * LANES
            sub = hd % heads_per_group
            in_head = (lane >= sub * dh) & (lane < (sub + 1) * dh)
            q_pair = qb[:, c0:c0 + LANES]
            qm = jnp.where(in_head, q_pair, jnp.zeros_like(q_pair))
            s_all = _dot_nt(kring[:, :, c0:c0 + LANES].reshape(_RING * blk, LANES), qm)
            return [s_all[k * blk:(k + 1) * blk] + bias_ref[slabs[k], hd] for k in range(_RING)]

        def probs(s_list):
            m = s_list[0].max(axis=0, keepdims=True)
            for k in range(1, _RING):
                m = jnp.maximum(m, s_list[k].max(axis=0, keepdims=True))
            return [jnp.exp2(sc - m).astype(BF16) for sc in s_list]

        def values(hd, p_list):
            o = _dot(vring[0, hd], p_list[0])
            for k in range(1, _RING):
                o = o + _dot(vring[k, hd], p_list[k])
            o_scr[j, hd * dh:(hd + 1) * dh, :] = (o[0:dh] * (1.0 / o[dh:dh + 1])).astype(BF16)

        s_next = scores(0)
        pending = None
        for hd in range(ATT_HEADS):
            s_cur = s_next
            if hd + 1 < ATT_HEADS:
                s_next = scores(hd + 1)
            p_list = probs(s_cur)
            if pending is not None:
                values(*pending)
            pending = (hd, p_list)
            if hd in between:
                between[hd]()
        values(*pending)

    def finish(j):
        rows = slice(j * blk, (j + 1) * blk)
        mix = _dot_tn(o_scr[j], wo_ref[...])
        _finish_tokens(x_ref[rows, :], mix, alpha, lng_ref, lnb_ref, wrh_ref, wrl_ref, br_ref,
                       h_ref.at[rows, :], hp_ref.at[rows, :], lgt_ref.at[rows, :])

    qkv = project(0)
    for j in range(chain):
        to_ring(j, qkv[1], qkv[2])
        nxt = []
        between = {}
        if j + 1 < chain:
            between[0] = lambda j=j: nxt.append(project(j + 1))
        attend(j, qkv[0], between)
        finish(j)
        qkv = nxt[0] if nxt else None


def _rel_bias(rel_table, n_rows, n_cols, offset, sign):
    heads = rel_table.shape[0]
    period = n_rows + n_cols
    m = jnp.arange(period)
    c_minus_r = jnp.where(m < n_cols, m, m - period)
    w = rel_table[:, jnp.clip(offset - sign * c_minus_r, -REL_CLIP, REL_CLIP) + REL_CLIP]
    flat = jnp.broadcast_to(w[:, None, :], (heads, n_rows, period)).reshape(heads, n_rows * period)
    return flat[:, :n_rows * (period - 1)].reshape(heads, n_rows, period - 1)[:, :, :n_cols]


def _att_prompt_bias(rel_table, blk):
    i = jnp.arange(blk)
    out = []
    wide = _rel_bias(rel_table, blk, _RING * blk, 0, -1)
    for d in range(_RING):
        b = wide[:, :, d * blk:(d + 1) * blk]
        cd = (i[None, :] // CHUNK) - (i[:, None] // CHUNK) + d * (blk // CHUNK)
        ok = (cd >= 0) & (cd <= BAND_CHUNKS)
        out.append(jnp.where(ok[None], b, NEG_INF))
    out.append(jnp.full_like(out[0], NEG_INF))
    return jnp.stack(out).astype(F32)


def _att_prompt(x2d, n_seq, seq_len, w_qkv, w_o, rel_table, ln_g, ln_b, wr_hi, wr_lo, br, alpha, n_all,
                cast=None):
    d_model = x2d.shape[1]
    blk = ATT_BLOCK
    chain = ATT_CHAIN
    rows = blk * chain
    nblk = seq_len // rows
    keep = min(BAND_CHUNKS * CHUNK, seq_len)
    assert seq_len % rows == 0 and keep % rows == 0
    assert (_RING - 1) * blk >= BAND_CHUNKS * CHUNK
    kb = keep // rows
    bias = (_att_prompt_bias(rel_table, blk) * LOG2E).astype(BF16)
    n_steps = n_seq * nblk
    n_fill = (n_all - n_seq * seq_len) // rows
    assert n_fill * rows == n_all - n_seq * seq_len

    def work(t):
        return jnp.minimum(t, n_steps - 1)

    row_spec = pl.BlockSpec(
        (None, rows, d_model), lambda t: (work(t) // nblk, jnp.maximum(work(t) % nblk - (nblk - kb), 0), 0),
        pipeline_mode=pl.Buffered(1))
    rows_out = jax.ShapeDtypeStruct((n_seq, keep, d_model), F32)
    in_specs = [
        pl.BlockSpec((rows, d_model), lambda t: (work(t), 0)),
        _const_spec(w_qkv.shape),
        _const_spec(w_o.shape),
        _const_spec(bias.shape),
        _const_spec(ln_g.shape),
        _const_spec(ln_b.shape),
        _const_spec(wr_hi.shape),
        _const_spec(wr_lo.shape),
        _const_spec(br.shape),
    ]
    args = [x2d, w_qkv, w_o, bias, ln_g, ln_b, wr_hi, wr_lo, br]
    cast_shapes, cast_specs = _cast_plumbing(args, in_specs, cast, n_steps, work)
    return pl.pallas_call(
        functools.partial(_att_prompt_kernel, blk=blk, chain=chain, nblk=nblk, n_steps=n_steps, n_fill=n_fill,
                          n_cast=len(cast_shapes), alpha=alpha),
        grid=(n_steps + n_fill,),
        in_specs=in_specs,
        out_specs=_token_out_specs(rows, d_model, lambda t: (t, 0)) + (row_spec, row_spec) + cast_specs,
        out_shape=_token_out_shapes(n_all, d_model) + (rows_out, rows_out) + cast_shapes,
        scratch_shapes=[
            pltpu.VMEM((_RING, blk, d_model), BF16),
            pltpu.VMEM((_RING, ATT_HEADS, d_model // ATT_HEADS + BF16_SUBLANES, blk), BF16),
            pltpu.VMEM((chain, d_model, blk), BF16),
        ],
        compiler_params=pltpu.CompilerParams(
            dimension_semantics=("arbitrary",), vmem_limit_bytes=VMEM_LIMIT),
        name="att_mixer",
    )(*args)


def _att_sample_kernel(*refs, nb, blk, alpha):
    (x_ref, kc_ref, vc_ref, wqkv_ref, wo_ref, bias_c_ref, bias_n_ref, lng_ref, lnb_ref,
     wrh_ref, wrl_ref, br_ref) = refs[:12]
    h_ref, hp_ref, lgt_ref, krow_ref, vrow_ref, o_scr = refs[-6:]
    d_model = x_ref.shape[1]
    heads = ATT_HEADS
    dh = d_model // heads
    x = x_ref[...]
    xb = x.astype(BF16)
    q = _dot(xb, wqkv_ref[:, 0:d_model]) * (dh ** -0.5)
    k = _dot(xb, wqkv_ref[:, d_model:2 * d_model])
    v = _dot(xb, wqkv_ref[:, 2 * d_model:3 * d_model])
    krow_ref[...] = k.reshape(nb, blk, d_model)
    vrow_ref[...] = v.reshape(nb, blk, d_model)
    lane_head = lax.broadcasted_iota(jnp.int32, (heads, 1, d_model), 2) // dh
    head_id = lax.broadcasted_iota(jnp.int32, (heads, 1, d_model), 0)
    head_mask = (lane_head == head_id).astype(F32)

    for s in range(nb):
        r0 = s * blk
        qs = q[r0:r0 + blk]
        q_bd = (qs[None, :, :] * head_mask).reshape(heads * blk, d_model).astype(BF16)
        kn = k[r0:r0 + blk].astype(BF16)
        vn = v[r0:r0 + blk].astype(BF16)
        s_c = _dot(q_bd, kc_ref[s].astype(BF16)) + bias_c_ref[...]
        s_n = _dot_nt(q_bd, kn) + bias_n_ref[...]
        m = jnp.maximum(s_c.max(axis=-1, keepdims=True), s_n.max(axis=-1, keepdims=True))
        p_c = jnp.exp(s_c - m)
        p_n = jnp.exp(s_n - m)
        l = p_c.sum(axis=-1, keepdims=True) + p_n.sum(axis=-1, keepdims=True)
        o_full = _dot_nt(p_c.astype(BF16), vc_ref[s].astype(BF16)) + _dot(p_n.astype(BF16), vn)
        o_full = o_full * (1.0 / l)
        o = (o_full.reshape(heads, blk, d_model) * head_mask).sum(axis=0)
        o_scr[r0:r0 + blk, :] = o.astype(BF16)

    mix = _dot(o_scr[...], wo_ref[...])
    _finish_tokens(x, mix, alpha, lng_ref, lnb_ref, wrh_ref, wrl_ref, br_ref, h_ref, hp_ref, lgt_ref)


def _att_sample(x2d, in_row0, n_seq, seq_len, k_cache, v_cache, w_qkv, w_o, rel_table, ln_g, ln_b,
                wr_hi, wr_lo, br, alpha, nb, n_all, out_row0, dst):
    d_model = x2d.shape[1]
    heads = ATT_HEADS
    blk = seq_len
    n_cache = k_cache.shape[1]
    rows = nb * blk
    assert n_seq % nb == 0 and in_row0 % rows == 0 and out_row0 % rows == 0
    in_b0, out_b0 = in_row0 // rows, out_row0 // rows
    bias_c = _rel_bias(rel_table, blk, n_cache, n_cache, 1).reshape(heads * blk, n_cache)
    bias_n = _rel_bias(rel_table, blk, blk, 0, 1).reshape(heads * blk, blk)
    kc = jnp.transpose(k_cache, (0, 2, 3, 1)).reshape(n_seq, d_model, n_cache)
    vc = jnp.transpose(v_cache, (0, 2, 3, 1)).reshape(n_seq, d_model, n_cache)
    cache_spec = pl.BlockSpec((nb, d_model, n_cache), lambda g: (g, 0, 0))
    row_spec = pl.BlockSpec((nb, blk, d_model), lambda g: (g, 0, 0))
    rows_out = jax.ShapeDtypeStruct((n_seq, blk, d_model), F32)
    in_specs = [
        pl.BlockSpec((rows, d_model), lambda g: (in_b0 + g, 0)),
        cache_spec,
        cache_spec,
        _const_spec(w_qkv.shape),
        _const_spec(w_o.shape),
        _const_spec(bias_c.shape),
        _const_spec(bias_n.shape),
        _const_spec(ln_g.shape),
        _const_spec(ln_b.shape),
        _const_spec(wr_hi.shape),
        _const_spec(wr_lo.shape),
        _const_spec(br.shape),
    ]
    args = [x2d, kc, vc, w_qkv, w_o, bias_c, bias_n, ln_g, ln_b, wr_hi, wr_lo, br]
    aliases = _alias_dst(args, in_specs, dst)
    return pl.pallas_call(
        functools.partial(_att_sample_kernel, nb=nb, blk=blk, alpha=alpha),
        grid=(n_seq // nb,),
        in_specs=in_specs,
        out_specs=_token_out_specs(rows, d_model, lambda g: (out_b0 + g, 0)) + (row_spec, row_spec),
        out_shape=_token_out_shapes(n_all, d_model) + (rows_out, rows_out),
        input_output_aliases=aliases,
        scratch_shapes=[pltpu.VMEM((rows, d_model), BF16)],
        compiler_params=pltpu.CompilerParams(
            dimension_semantics=("arbitrary",), vmem_limit_bytes=VMEM_LIMIT),
        name="att_mixer_cache",
    )(*args)


ROUTE_ROWS = 40
ROUTE_TILES_PER_STEP = 5


def _route_kernel(lgt_ref, meta_t_ref, meta_w_ref, count_ref, carry_ref, tri_ref):
    j = pl.program_id(0)
    tm = tri_ref.shape[0]

    @pl.when(j == 0)
    def _():
        carry_ref[...] = jnp.zeros(carry_ref.shape, F32)
        r = lax.broadcasted_iota(jnp.int32, (tm, tm), 0)
        c = lax.broadcasted_iota(jnp.int32, (tm, tm), 1)
        tri_ref[...] = (r < c).astype(BF16)

    for sub in range(lgt_ref.shape[0] // tm):
        _route_tile(lgt_ref, meta_t_ref, meta_w_ref, carry_ref, tri_ref, slice(sub * tm, (sub + 1) * tm))
    count_ref[...] = carry_ref[...].astype(jnp.int32)


def _route_tile(lgt_ref, meta_t_ref, meta_w_ref, carry_ref, tri_ref, tok):
    tm = tri_ref.shape[0]
    a = jnp.transpose(lgt_ref[tok, :])[0:ROUTE_ROWS, :]
    row = lax.broadcasted_iota(jnp.int32, (ROUTE_ROWS, tm), 0)
    big = jnp.int32(LANES)
    glog = jnp.where(row < N_GROUPS, a, NEG_INF)
    gmax = glog.max(axis=0, keepdims=True)
    gsel = jnp.where(glog == gmax, row, big).min(axis=0, keepdims=True)
    gp = 1.0 / jnp.exp(glog - gmax).sum(axis=0, keepdims=True)
    first = N_GROUPS + gsel * EXP_PER_GROUP
    in_grp = (row >= first) & (row < first + EXP_PER_GROUP)
    elog = jnp.where(in_grp, a, NEG_INF)
    v1 = elog.max(axis=0, keepdims=True)
    l1 = jnp.where(elog == v1, row, big).min(axis=0, keepdims=True)
    elog2 = jnp.where(row == l1, NEG_INF, elog)
    v2 = elog2.max(axis=0, keepdims=True)
    l2 = jnp.where(elog2 == v2, row, big).min(axis=0, keepdims=True)
    e2 = jnp.exp(v2 - v1)
    w1 = gp / (1.0 + e2)
    w2 = gp * e2 / (1.0 + e2)

    is1 = row == l1
    is2 = row == l2
    oh = (is1 | is2).astype(F32)
    before = _dot(oh.astype(BF16), tri_ref[...]) + carry_ref[:, 0:1]
    rank1 = jnp.where(is1, before, 0.0).sum(axis=0, keepdims=True)
    rank2 = jnp.where(is2, before, 0.0).sum(axis=0, keepdims=True)
    carry_ref[...] = carry_ref[...] + oh.sum(axis=1, keepdims=True)

    pad = jnp.zeros((SUBLANES - 4, tm), jnp.int32)
    meta_t_ref[:, tok] = jnp.concatenate(
        [l1 - N_GROUPS, l2 - N_GROUPS, rank1.astype(jnp.int32), rank2.astype(jnp.int32), pad], axis=0)
    wt = jnp.concatenate([w1, w2, jnp.zeros((LANES - 2, tm), F32)], axis=0)
    meta_w_ref[tok, :] = jnp.transpose(wt)


def _route(logits):
    n_tok = logits.shape[0]
    tm = TOKEN_TILE
    assert n_tok % tm == 0
    per_step = max(d for d in range(1, ROUTE_TILES_PER_STEP + 1) if (n_tok // tm) % d == 0)
    rows = tm * per_step
    tile = pl.BlockSpec((rows, LANES), lambda j: (j, 0))
    return pl.pallas_call(
        _route_kernel,
        grid=(n_tok // rows,),
        in_specs=[tile],
        out_specs=(pl.BlockSpec((SUBLANES, rows), lambda j: (0, j)), tile,
                   pl.BlockSpec((ROUTE_ROWS, LANES), lambda j: (0, 0))),
        out_shape=(
            jax.ShapeDtypeStruct((SUBLANES, n_tok), jnp.int32),
            jax.ShapeDtypeStruct((n_tok, LANES), F32),
            jax.ShapeDtypeStruct((ROUTE_ROWS, LANES), jnp.int32),
        ),
        scratch_shapes=[pltpu.VMEM((ROUTE_ROWS, LANES), F32), pltpu.VMEM((tm, tm), BF16)],
        compiler_params=pltpu.CompilerParams(dimension_semantics=("arbitrary",)),
        name="moe_route",
    )(logits)


def _sc_gather_loop(table_hbm, idx_v, out_hbm, rows_v, sems, base, n_chunk, r):
    def gather(c, slot):
        off = pl.multiple_of(c * r, r)
        return pltpu.make_async_copy(table_hbm.at[idx_v.at[pl.ds(off, r)]], rows_v.at[slot], sems.at[slot])

    def finish(c, slot):
        gather(c, slot).wait()
        pltpu.sync_copy(rows_v.at[slot], out_hbm.at[pl.ds(base + pl.multiple_of(c * r, r), r)])

    gather(0, 0).start()
    if n_chunk > 1:
        gather(1, 1).start()

    def body(pair, carry):
        c = 2 * pair
        for slot in range(2):
            finish(c + slot, slot)

            @pl.when(c + slot + 2 < n_chunk)
            def _():
                gather(c + slot + 2, slot).start()
        return carry

    lax.fori_loop(0, n_chunk // 2, body, 0)
    if n_chunk % 2:
        finish(n_chunk - 1, 0)


def _gather_rows(table, idx):
    m = idx.shape[0]
    width = table.shape[1]
    r = SC_ROWS_PER_CHUNK
    assert m % (SC_WORKERS * r) == 0
    per_w = m // SC_WORKERS
    mesh = plsc.VectorSubcoreMesh(core_axis_name="c", subcore_axis_name="s")

    @functools.partial(
        pl.kernel,
        mesh=mesh,
        out_type=jax.ShapeDtypeStruct((m, width), table.dtype),
        scratch_types=[
            pltpu.VMEM((per_w,), jnp.int32),
            pltpu.VMEM((2, r, width), table.dtype),
            pltpu.SemaphoreType.DMA((2,)),
        ],
    )
    def gather(table_hbm, idx_hbm, out_hbm, idx_v, rows_v, sem):
        wid = lax.axis_index("s") * 2 + lax.axis_index("c")
        base = wid * per_w
        pltpu.sync_copy(idx_hbm.at[pl.ds(base, per_w)], idx_v)
        _sc_gather_loop(table_hbm, idx_v, out_hbm, rows_v, sem, base, per_w // r, r)

    return gather(table, idx)


def _dispatch_rows(table, pos, row0, n_rows):
    n_tok, width = table.shape
    n_pairs = pos.shape[0]
    r = SC_ROWS_PER_CHUNK
    lanes = SC_LANES
    assert n_rows % (SC_WORKERS * r) == 0
    per_w = n_rows // SC_WORKERS
    n_stage = 16
    stage = n_pairs // n_stage
    assert stage * n_stage == n_pairs and stage % lanes == 0 and per_w % lanes == 0
    assert row0 + n_rows < 3 * n_tok
    mesh = plsc.VectorSubcoreMesh(core_axis_name="c", subcore_axis_name="s")

    @functools.partial(
        pl.kernel,
        mesh=mesh,
        out_type=jax.ShapeDtypeStruct((n_rows, width), table.dtype),
        scratch_types=[
            pltpu.VMEM((per_w,), jnp.int32),
            pltpu.VMEM((stage,), jnp.int32),
            pltpu.VMEM((2, r, width), table.dtype),
            pltpu.SemaphoreType.DMA((2,)),
        ],
        compiler_params=pltpu.CompilerParams(needs_layout_passes=False),
    )
    def dispatch(table_hbm, pos_hbm, out_hbm, src_v, pos_v, rows_v, sem):
        wid = lax.axis_index("s") * 2 + lax.axis_index("c")
        out_base = wid * per_w
        base = row0 + out_base
        lane = lax.iota(jnp.int32, lanes)

        def wrap(t):
            t = jnp.where(t >= n_tok, t - n_tok, t)
            return jnp.where(t >= n_tok, t - n_tok, t)

        def init(i, carry):
            off = pl.multiple_of(i * lanes, lanes)
            src_v[pl.ds(off, lanes)] = wrap(base + off + lane)
            return carry

        lax.fori_loop(0, per_w // lanes, init, 0)

        def scan_stage(sidx, carry):
            pair0 = sidx * stage
            pltpu.sync_copy(pos_hbm.at[pl.ds(pl.multiple_of(pair0, 8), stage)], pos_v)

            @plsc.parallel_loop(0, stage // lanes, unroll=4)
            def _(i):
                off = pl.multiple_of(i * lanes, lanes)
                local = pos_v[pl.ds(off, lanes)] - base
                mine = (local >= 0) & (local < per_w)
                plsc.store_scatter(src_v, [jnp.where(mine, local, 0)], wrap(pair0 + off + lane), mask=mine)

            return carry

        lax.fori_loop(0, n_stage, scan_stage, 0)
        _sc_gather_loop(table_hbm, src_v, out_hbm, rows_v, sem, out_base, per_w // r, r)

    return dispatch(table, pos)


def _gemm_kernel(tile_expert_ref, n_used_ref, xs_ref, wg_ref, wu_ref, wd_ref, *rest, tile0):
    ys_ref = rest[-1]
    tile = tile0 + pl.program_id(0)

    @pl.when(tile < n_used_ref[0])
    def _():
        x = _unpack_rows(xs_ref[...]).astype(BF16)
        g = _dot(x, wg_ref[...])
        u = _dot(x, wu_ref[...])
        hmid = (jax.nn.silu(g) * u).astype(BF16)
        ys_ref[...] = _pack_rows(_dot(hmid, wd_ref[...]))


def _expert_gemm(xs, tile_expert, n_used, w_gate, w_up, w_down, tile0, n_rows_all, ys):
    n_rows, half = xs.shape
    d_model, d_exp = w_gate.shape[1], w_gate.shape[2]
    tm = GEMM_TILE
    n_tiles = n_rows // tm

    def last_used(j, nu):
        return jnp.minimum(tile0 + j, nu[0] - 1)

    def in_map(j, te, nu):
        return (jnp.maximum(last_used(j, nu) - tile0, 0), 0)

    def out_map(j, te, nu):
        return (jnp.maximum(last_used(j, nu), tile0), 0)

    def w_map(j, te, nu):
        return (te[tile0 + j], 0, 0)

    in_specs = [
        pl.BlockSpec((tm, half), in_map),
        pl.BlockSpec((None, d_model, d_exp), w_map),
        pl.BlockSpec((None, d_model, d_exp), w_map),
        pl.BlockSpec((None, d_exp, d_model), w_map),
    ]
    args = [tile_expert, n_used, xs, w_gate, w_up, w_down]
    aliases = {}
    if ys is not None:
        aliases[len(args)] = 0
        args.append(ys)
        in_specs.append(pl.BlockSpec(memory_space=pl.ANY))
    grid_spec = pltpu.PrefetchScalarGridSpec(
        num_scalar_prefetch=2,
        grid=(n_tiles,),
        in_specs=in_specs,
        out_specs=pl.BlockSpec((tm, half), out_map),
    )
    return pl.pallas_call(
        functools.partial(_gemm_kernel, tile0=tile0),
        grid_spec=grid_spec,
        out_shape=jax.ShapeDtypeStruct((n_rows_all, half), U32),
        input_output_aliases=aliases,
        compiler_params=pltpu.CompilerParams(
            dimension_semantics=("arbitrary",), vmem_limit_bytes=VMEM_LIMIT),
        name="moe_gemm",
    )(*args)


def _combine_kernel(y0_ref, y1_ref, mw_ref, h_ref, pp_ref, ps_ref, lng_ref, lnb_ref, wproj_ref, wgate_ref,
                    *out_refs, n_prompt_tiles, alpha):
    j = pl.program_id(0)
    tm = h_ref.shape[0]
    half = tm // 2

    def normed(rows):
        mw = mw_ref[rows, :]
        ffn = mw[:, 0:1] * _unpack_rows(y0_ref[rows, :]) + mw[:, 1:2] * _unpack_rows(y1_ref[rows, :])
        return _layer_norm(alpha * h_ref[rows, :] + ffn, lng_ref[...], lnb_ref[...])

    def gated(rows, h2):
        p = jnp.where(j < n_prompt_tiles, pp_ref[rows, :], ps_ref[rows, :]).astype(BF16)
        gate = jax.nn.sigmoid(_dot(h2.astype(BF16), wgate_ref[...]))
        return h2 + gate * _dot(p, wproj_ref[...])

    rows_a, rows_b = slice(0, half), slice(half, tm)
    h2_a = normed(rows_a)
    h2_b = normed(rows_b)
    out = jnp.concatenate([gated(rows_a, h2_a), gated(rows_b, h2_b)], axis=0)
    if len(out_refs) == 1:
        out_refs[0][...] = out
    else:
        @pl.when(j < n_prompt_tiles)
        def _():
            out_refs[0][...] = out

        @pl.when(j >= n_prompt_tiles)
        def _():
            out_refs[1][...] = out


def _combine(yg, meta_w, h, p_prompt, p_sample, layer, ln_g, ln_b, w_proj, w_gate, alpha, split_out):
    n_tok, d_model = h.shape
    tm = TOKEN_TILE
    n_t = n_tok // tm
    n_tp = p_prompt.shape[1] // tm
    n_ts = p_sample.shape[1] // tm
    assert n_tp * tm == p_prompt.shape[1] and n_ts * tm == p_sample.shape[1] and n_tp + n_ts == n_t
    ple = p_prompt.shape[2]
    tile = pl.BlockSpec((tm, d_model), lambda j: (j, 0))
    if split_out:
        out_specs = (pl.BlockSpec((tm, d_model), lambda j: (jnp.minimum(j, n_tp - 1), 0)),
                     pl.BlockSpec((tm, d_model), lambda j: (jnp.maximum(j - n_tp, 0), 0)))
        out_shape = (jax.ShapeDtypeStruct((n_tp * tm, d_model), F32),
                     jax.ShapeDtypeStruct((n_ts * tm, d_model), F32))
    else:
        out_specs = tile
        out_shape = jax.ShapeDtypeStruct((n_tok, d_model), F32)
    return pl.pallas_call(
        functools.partial(_combine_kernel, n_prompt_tiles=n_tp, alpha=alpha),
        grid=(n_t,),
        in_specs=[
            pl.BlockSpec((tm, d_model // 2), lambda j: (j, 0)),
            pl.BlockSpec((tm, d_model // 2), lambda j: (j + n_t, 0)),
            pl.BlockSpec((tm, LANES), lambda j: (j, 0)),
            tile,
            pl.BlockSpec((None, tm, ple), lambda j: (layer, jnp.minimum(j, n_tp - 1), 0)),
            pl.BlockSpec((None, tm, ple), lambda j: (layer, jnp.maximum(j - n_tp, 0), 0)),
            _const_spec(ln_g.shape),
            _const_spec(ln_b.shape),
            _const_spec(w_proj.shape),
            _const_spec(w_gate.shape),
        ],
        out_specs=out_specs,
        out_shape=out_shape,
        compiler_params=pltpu.CompilerParams(
            dimension_semantics=("arbitrary",), vmem_limit_bytes=VMEM_LIMIT),
        name="moe_combine",
    )(yg, yg, meta_w, h, p_prompt, p_sample, ln_g, ln_b, w_proj, w_gate)


def _router_weights(w_grp, b_grp, w_exp, b_exp):
    d_model = w_grp.shape[0]
    w = jnp.concatenate([w_grp, jnp.transpose(w_exp, (1, 0, 2)).reshape(d_model, N_EXPERTS)], axis=1)
    w = jnp.pad(w, ((0, 0), (0, LANES - w.shape[1])))
    b = jnp.concatenate([b_grp, b_exp.reshape(N_EXPERTS)])
    b = jnp.pad(b, (0, LANES - b.shape[0])).reshape(1, LANES)
    w_hi = w.astype(BF16)
    w_lo = (w - w_hi.astype(F32)).astype(BF16)
    return w_hi, w_lo, b


def _moe(h_packed, logits, w_gate, w_up, w_down):
    n_tok = h_packed.shape[0]
    tm = GEMM_TILE
    meta_t, meta_w, counts = _route(logits)
    counts = counts[N_GROUPS:N_GROUPS + N_EXPERTS, 0]
    tiles_per_expert = (counts + tm - 1) // tm
    tile_end = jnp.cumsum(tiles_per_expert)
    row_start = (tile_end - tiles_per_expert) * tm
    eid = meta_t[0:2]
    rank = meta_t[2:4]
    experts = jnp.arange(N_EXPERTS, dtype=jnp.int32)
    start = jnp.sum(jnp.where(eid[:, :, None] == experts, row_start, 0), axis=-1)
    pos = (start + rank).reshape(-1).astype(jnp.int32)
    part_quant = MOE_PARTS * SC_WORKERS * SC_ROWS_PER_CHUNK * (tm // math.gcd(tm, SC_WORKERS * SC_ROWS_PER_CHUNK))
    n_tiles = -(-(2 * n_tok) // tm) + N_EXPERTS
    n_rows = -(-(n_tiles * tm) // part_quant) * part_quant
    n_tiles = n_rows // tm
    n_used = tile_end[-1:].astype(jnp.int32)
    tile_ids = jnp.minimum(jnp.arange(n_tiles, dtype=jnp.int32), n_used[0] - 1)
    tile_expert = jnp.sum(tile_end[None, :] <= tile_ids[:, None], axis=1).astype(jnp.int32)
    part_rows = n_rows // MOE_PARTS
    xs_parts = [_dispatch_rows(h_packed, pos, part * part_rows, part_rows) for part in range(MOE_PARTS)]
    ys = None
    for part, xs in enumerate(xs_parts):
        ys = _expert_gemm(xs, tile_expert, n_used, w_gate, w_up, w_down,
                          part * (part_rows // tm), n_rows, ys)
    return _gather_rows(ys, pos), meta_w


def kernel(x_prompt, x_sample, p_prompt, p_sample, state_ret, cache_att_k, cache_att_v, ret_w_in, ret_gn_g,
           ret_w_o, att_w_qkv, att_rel_bias, att_w_o, ln1_g, ln1_b, ln2_g, ln2_b, moe_w_grp, moe_b_grp,
           moe_w_exp, moe_b_exp, moe_w_gate, moe_w_up, moe_w_down, ple_w_proj, ple_w_gate):
    n_p, len_p, d_model = x_prompt.shape
    n_s, len_s, _ = x_sample.shape
    depth = ln1_g.shape[0]
    alpha = float((2 * depth) ** 0.25)
    tok_p, tok_s = n_p * len_p, n_s * len_s
    n_all = tok_p + tok_s
    dh = d_model // ATT_HEADS
    pp = p_prompt.reshape(depth, tok_p, -1)
    ps = p_sample.reshape(depth, tok_s, -1)
    nb_s = 4

    x_all = None
    y_prompt = y_sample = None
    states_p, states_s, k_p, v_p, k_s, v_s = [], [], [], [], [], []
    for i in range(depth):
        jj = i // 2
        wr_hi, wr_lo, br = _router_weights(moe_w_grp[i], moe_b_grp[i], moe_w_exp[i], moe_b_exp[i])
        lng, lnb = ln1_g[i].reshape(1, d_model), ln1_b[i].reshape(1, d_model)
        moe_cast = (i, [moe_w_gate, moe_w_up, moe_w_down])
        if x_all is None:
            src_p, src_s, row_s = x_prompt.reshape(tok_p, d_model), x_sample.reshape(tok_s, d_model), 0
        else:
            src_p, src_s, row_s = x_all, x_all, tok_p
        if i % 2 == 0:
            w_in = ret_w_in[jj].astype(BF16)
            w_o = ret_w_o[jj].astype(BF16)
            gn = ret_gn_g[jj].reshape(1, -1)
            h, hp, lgt, st_p, *w_moe = _ret_mixer(
                src_p, 0, n_p, len_p, 0, None, w_in, w_o, gn, lng, lnb, wr_hi, wr_lo, br, alpha, nb=1,
                blk=min(RET_BLOCK, len_p), chain=RET_CHAIN, n_all=n_all, out_row0=0, dst=None, cast=moe_cast)
            h, hp, lgt, st_s = _ret_mixer(src_s, row_s, n_s, len_s, PAST_LEN, state_ret[jj], w_in, w_o, gn,
                                          lng, lnb, wr_hi, wr_lo, br, alpha, nb=nb_s, blk=len_s, chain=1,
                                          n_all=n_all, out_row0=tok_p, dst=(h, hp, lgt))
            states_p.append(st_p)
            states_s.append(st_s)
        else:
            w_qkv = att_w_qkv[jj].astype(BF16)
            w_o = att_w_o[jj].astype(BF16)
            h, hp, lgt, kr, vr, *w_moe = _att_prompt(src_p, n_p, len_p, w_qkv, w_o, att_rel_bias[jj], lng, lnb,
                                                     wr_hi, wr_lo, br, alpha, n_all, cast=moe_cast)
            k_p.append(kr.reshape(n_p, -1, ATT_HEADS, dh))
            v_p.append(vr.reshape(n_p, -1, ATT_HEADS, dh))
            h, hp, lgt, kr, vr = _att_sample(src_s, row_s, n_s, len_s, cache_att_k[jj], cache_att_v[jj],
                                             w_qkv, w_o, att_rel_bias[jj], lng, lnb, wr_hi, wr_lo, br, alpha,
                                             nb=nb_s, n_all=n_all, out_row0=tok_p, dst=(h, hp, lgt))
            k_s.append(kr.reshape(n_s, len_s, ATT_HEADS, dh))
            v_s.append(vr.reshape(n_s, len_s, ATT_HEADS, dh))
        n_exp = moe_w_gate.shape[1]
        w_moe = [w.reshape(n_exp, -1, w.shape[1]) for w in w_moe]
        yg, meta_w = _moe(hp, lgt, *w_moe)
        last = i == depth - 1
        out = _combine(yg, meta_w, h, pp, ps, i, ln2_g[i].reshape(1, d_model), ln2_b[i].reshape(1, d_model),
                       ple_w_proj[i].astype(BF16), ple_w_gate[i].astype(BF16), alpha, split_out=last)
        if last:
            y_prompt = out[0].reshape(n_p, len_p, d_model)
            y_sample = out[1].reshape(n_s, len_s, d_model)
        else:
            x_all = out

    return (y_prompt, y_sample, jnp.stack(states_p), jnp.stack(states_s),
            jnp.stack(k_p), jnp.stack(v_p), jnp.stack(k_s), jnp.stack(v_s))
```

```python
import functools
import math

import numpy as np
import jax
import jax.numpy as jnp
from jax import lax
from jax.experimental import pallas as pl
from jax.experimental.pallas import tpu as pltpu
from jax.experimental.pallas import tpu_sc as plsc

CHUNK = 64
PAST_LEN = 2048
RET_HEADS = 4
ROPE_BASE = 10000.0
ATT_HEADS = 16
BAND_CHUNKS = 8
REL_CLIP = 256
N_GROUPS = 4
EXP_PER_GROUP = 8
N_EXPERTS = N_GROUPS * EXP_PER_GROUP
LN_EPS = 1e-5
NEG_INF = -1e30

LANES = 128
SUBLANES = 8
BF16_SUBLANES = 16
SC_WORKERS = 32
SC_LANES = 16
SC_ROWS_PER_CHUNK = 32
VMEM_LIMIT = 58 * 1024 * 1024

RET_BLOCK = 256
RET_CHAIN = 2
ATT_BLOCK = 4 * CHUNK
ATT_CHAIN = 2
TOKEN_TILE = 512
GEMM_TILE = 512
MOE_PARTS = 4

F32 = jnp.float32
BF16 = jnp.bfloat16
U32 = jnp.uint32
HI_MASK = 0xFFFF0000
LOG2E = math.log2(math.e)


def _dot(a, b):
    return jnp.dot(a, b, preferred_element_type=F32)


def _dot_nt(a, b):
    return lax.dot_general(a, b, (((1,), (1,)), ((), ())), preferred_element_type=F32)


def _dot_tn(a, b):
    return lax.dot_general(a, b, (((0,), (0,)), ((), ())), preferred_element_type=F32)


def _layer_norm(x, g, b):
    mu = jnp.mean(x, axis=-1, keepdims=True)
    xc = x - mu
    var = jnp.mean(xc * xc, axis=-1, keepdims=True)
    return xc * lax.rsqrt(var + LN_EPS) * g + b


def _pack_rows(x):
    half = x.shape[1] // 2
    lo = lax.bitcast_convert_type(x[:, :half].astype(BF16).astype(F32), U32) >> 16
    hi = lax.bitcast_convert_type(x[:, half:].astype(BF16).astype(F32), U32) & U32(HI_MASK)
    return lo | hi


def _unpack_rows(p):
    lo = lax.bitcast_convert_type(p << 16, F32)
    hi = lax.bitcast_convert_type(p & U32(HI_MASK), F32)
    return jnp.concatenate([lo, hi], axis=1)


def _router_logits(h, wr_hi_ref, wr_lo_ref, br_ref):
    h_hi = h.astype(BF16)
    h_lo = (h - h_hi.astype(F32)).astype(BF16)
    w_hi = wr_hi_ref[...]
    return _dot(h_hi, w_hi) + _dot(h_lo, w_hi) + _dot(h_hi, wr_lo_ref[...]) + br_ref[...]


def _finish_tokens(x, mix, alpha, lng_ref, lnb_ref, wrh_ref, wrl_ref, br_ref, h_ref, hp_ref, lgt_ref):
    hh = _layer_norm(alpha * x + mix, lng_ref[...], lnb_ref[...])
    h_ref[...] = hh
    hp_ref[...] = _pack_rows(hh)
    lgt_ref[...] = _router_logits(hh, wrh_ref, wrl_ref, br_ref)


def _zero_tokens(h_ref, hp_ref, lgt_ref):
    h_ref[...] = jnp.zeros(h_ref.shape, h_ref.dtype)
    hp_ref[...] = jnp.zeros(hp_ref.shape, hp_ref.dtype)
    lgt_ref[...] = jnp.zeros(lgt_ref.shape, lgt_ref.dtype)


def _const_spec(shape):
    nd = len(shape)
    return pl.BlockSpec(shape, lambda *_: (0,) * nd, pipeline_mode=pl.Buffered(1))


def _token_out_shapes(n_all, d_model):
    return (
        jax.ShapeDtypeStruct((n_all, d_model), F32),
        jax.ShapeDtypeStruct((n_all, d_model // 2), U32),
        jax.ShapeDtypeStruct((n_all, LANES), F32),
    )


def _token_out_specs(rows, d_model, row_map):
    return (
        pl.BlockSpec((rows, d_model), row_map),
        pl.BlockSpec((rows, d_model // 2), row_map),
        pl.BlockSpec((rows, LANES), row_map),
    )


def _alias_dst(args, in_specs, dst):
    if dst is None:
        return {}
    aliases = {}
    for k, arr in enumerate(dst):
        aliases[len(args)] = k
        args.append(arr)
        in_specs.append(pl.BlockSpec(memory_space=pl.ANY))
    return aliases


def _cast_plumbing(args, in_specs, cast, n_steps, work):
    if cast is None:
        return (), ()
    layer, weights = cast
    shapes, specs = [], []
    for w in weights:
        depth, n_exp, rows, cols = w.shape
        assert (n_exp * rows) % n_steps == 0
        slab = (n_exp * rows) // n_steps
        args.append(w.reshape(depth, n_exp * rows, cols))
        in_specs.append(pl.BlockSpec((None, slab, cols), lambda t: (layer, work(t), 0)))
        shapes.append(jax.ShapeDtypeStruct((n_exp * rows, cols), BF16))
        specs.append(pl.BlockSpec((slab, cols), lambda t: (work(t), 0)))
    return tuple(shapes), tuple(specs)


def _cast_slabs(in_refs, out_refs):
    for src, dst in zip(in_refs, out_refs):
        dst[...] = src[...].astype(BF16)


def _ret_log_gamma():
    h = np.arange(RET_HEADS, dtype=np.float32)
    return np.log(np.float32(1.0) - np.float32(2.0) ** (np.float32(-5.0) - h)).astype(np.float32)


def _ret_kernel(*refs, nb, blk, chain, nblk, n_steps, n_fill, has_state, n_alias, n_cast, alpha):
    step = pl.program_id(0)
    n_in = 12 + int(has_state) + n_alias + n_cast
    h_ref, hp_ref, lgt_ref = refs[n_in:n_in + 3]
    core = refs[:12 + int(has_state)] + refs[n_in:n_in + 4] + refs[-1:]

    @pl.when(step < n_steps)
    def _():
        _cast_slabs(refs[n_in - n_cast:n_in], refs[n_in + 4:n_in + 4 + n_cast])
        _ret_step(*core, first_block=step % nblk == 0, nb=nb, blk=blk, chain=chain, has_state=has_state,
                  alpha=alpha)

    if n_fill:
        @pl.when(step >= n_steps)
        def _():
            _zero_tokens(h_ref, hp_ref, lgt_ref)


def _ret_step(*refs, first_block, nb, blk, chain, has_state, alpha):
    (x_ref, cos_ref, sin_ref, dmask_ref, win_ref, wo_ref, gn_ref, lng_ref, lnb_ref,
     wrh_ref, wrl_ref, br_ref) = refs[:12]
    s_in_ref = refs[12] if has_state else None
    h_ref, hp_ref, lgt_ref, s_out_ref, gated_ref = refs[12 + int(has_state):]
    heads = RET_HEADS
    d_model = x_ref.shape[1]
    dk = d_model // heads
    dv = 2 * d_model // heads
    hk, hv = heads * dk, heads * dv
    half = dk // 2
    lg = _ret_log_gamma()

    n_sub = nb * chain
    proj_rows = blk if chain > 1 else n_sub * blk
    rowf = lax.broadcasted_iota(jnp.int32, (blk, 1), 0).astype(F32)

    def proj(group):
        xb = x_ref[group * proj_rows:(group + 1) * proj_rows, :].astype(BF16)
        return (_dot(xb, win_ref[:, 0:hk]), _dot(xb, win_ref[:, hk:2 * hk]),
                _dot(xb, win_ref[:, 2 * hk:2 * hk + hv]), _dot(xb, win_ref[:, 2 * hk + hv:2 * hk + 2 * hv]))

    if not has_state:
        @pl.when(first_block)
        def _():
            s_out_ref[...] = jnp.zeros(s_out_ref.shape, F32)

    s_prev_ref = s_in_ref if has_state else s_out_ref

    def head(j, h, projected):
        q_all, k_all, v_all, g_all = projected
        s = j // chain
        c0 = (j % chain) * blk
        r0 = (j * blk) % proj_rows
        cos = cos_ref[c0:c0 + blk, :]
        sin = sin_ref[c0:c0 + blk, :]

        def rot(t):
            t1, t2 = t[:, :half], t[:, half:]
            return jnp.concatenate([t1 * cos - t2 * sin, t1 * sin + t2 * cos], axis=1)

        lgh = float(lg[h])
        q = rot(q_all[r0:r0 + blk, h * dk:(h + 1) * dk])
        k = rot(k_all[r0:r0 + blk, h * dk:(h + 1) * dk]) * (dk ** -0.5)
        v = v_all[r0:r0 + blk, h * dv:(h + 1) * dv]
        g = g_all[r0:r0 + blk, h * dv:(h + 1) * dv]
        vb = v.astype(BF16)
        scores = _dot_nt(q.astype(BF16), k.astype(BF16)) * dmask_ref[h]
        inner = _dot(scores.astype(BF16), vb)
        s_prev = s_prev_ref[s, h]
        q_dec = q * jnp.exp(lgh * (rowf + 1.0))
        cross = _dot(q_dec.astype(BF16), s_prev.astype(BF16))
        k_dec = k * jnp.exp(lgh * (float(blk - 1) - rowf))
        s_out_ref[s, h] = math.exp(lgh * blk) * s_prev + _dot_tn(k_dec.astype(BF16), vb)
        o = inner + cross
        mu = jnp.mean(o, axis=-1, keepdims=True)
        oc = o - mu
        var = jnp.mean(oc * oc, axis=-1, keepdims=True)
        on = oc * lax.rsqrt(var + LN_EPS) * gn_ref[:, h * dv:(h + 1) * dv]
        gated_ref[j * blk:(j + 1) * blk, h * dv:(h + 1) * dv] = (jax.nn.silu(g) * on).astype(BF16)

    def tail(group):
        rows = slice(group * proj_rows, (group + 1) * proj_rows)
        mix = _dot(gated_ref[rows, :], wo_ref[...])
        _finish_tokens(x_ref[rows, :], mix, alpha, lng_ref, lnb_ref, wrh_ref, wrl_ref, br_ref,
                       h_ref.at[rows, :], hp_ref.at[rows, :], lgt_ref.at[rows, :])

    n_groups = (n_sub * blk) // proj_rows
    subs_per_group = proj_rows // blk
    projected = proj(0)
    for group in range(n_groups):
        nxt = None
        for jj in range(subs_per_group):
            for h in range(heads):
                head(group * subs_per_group + jj, h, projected)
                if jj == 0 and h == 0 and group + 1 < n_groups:
                    nxt = proj(group + 1)
                if jj == 0 and h == 1 and group > 0:
                    tail(group - 1)
        projected = nxt
    tail(n_groups - 1)


def _ret_mixer(x2d, in_row0, n_seq, seq_len, pos0, state_in, w_in, w_o, gn_g, ln_g, ln_b,
               wr_hi, wr_lo, br, alpha, nb, blk, chain, n_all, out_row0, dst, cast=None):
    d_model = x2d.shape[1]
    heads = RET_HEADS
    dk, dv = d_model // heads, 2 * d_model // heads
    half = dk // 2
    nblk = seq_len // (blk * chain)
    has_state = state_in is not None
    rows = nb * blk * chain
    assert seq_len % (blk * chain) == 0 and n_seq % nb == 0
    assert (not has_state) or nblk == 1
    assert nb == 1 or (nblk == 1 and chain == 1)
    assert in_row0 % rows == 0 and out_row0 % rows == 0
    in_b0, out_b0 = in_row0 // rows, out_row0 // rows

    pos = (pos0 + np.arange(seq_len, dtype=np.int32)).astype(np.float32)
    inv_freq = np.float32(ROPE_BASE) ** (-np.arange(half, dtype=np.float32) / np.float32(half))
    ang = (pos[:, None] * inv_freq[None, :]).astype(np.float64)
    cos, sin = jnp.asarray(np.cos(ang), F32), jnp.asarray(np.sin(ang), F32)
    lg = jnp.asarray(_ret_log_gamma())
    ii = jnp.arange(blk, dtype=F32)
    diff = ii[:, None] - ii[None, :]
    dmask = jnp.where(diff >= 0, jnp.exp(lg[:, None, None] * jnp.maximum(diff, 0.0)), 0.0)

    n_steps = (n_seq // nb) * nblk
    n_fill = 0 if dst is not None else (n_all - n_seq * seq_len) // rows
    assert dst is not None or (out_row0 == 0 and n_fill * rows == n_all - n_seq * seq_len)

    def work(t):
        return jnp.minimum(t, n_steps - 1)

    in_specs = [
        pl.BlockSpec((rows, d_model), lambda t: (in_b0 + work(t), 0)),
        pl.BlockSpec((blk * chain, half), lambda t: (work(t) % nblk, 0)),
        pl.BlockSpec((blk * chain, half), lambda t: (work(t) % nblk, 0)),
        _const_spec(dmask.shape),
        _const_spec(w_in.shape),
        _const_spec(w_o.shape),
        _const_spec(gn_g.shape),
        _const_spec(ln_g.shape),
        _const_spec(ln_b.shape),
        _const_spec(wr_hi.shape),
        _const_spec(wr_lo.shape),
        _const_spec(br.shape),
    ]
    args = [x2d, cos, sin, dmask, w_in, w_o, gn_g, ln_g, ln_b, wr_hi, wr_lo, br]
    state_spec = pl.BlockSpec((nb, heads, dk, dv), lambda t: (work(t) // nblk, 0, 0, 0))
    if has_state:
        in_specs.append(state_spec)
        args.append(state_in)
    aliases = _alias_dst(args, in_specs, dst)
    cast_shapes, cast_specs = _cast_plumbing(args, in_specs, cast, n_steps, work)
    out_shape = (_token_out_shapes(n_all, d_model) + (jax.ShapeDtypeStruct((n_seq, heads, dk, dv), F32),)
                 + cast_shapes)
    out_specs = _token_out_specs(rows, d_model, lambda t: (out_b0 + t, 0)) + (state_spec,) + cast_specs
    return pl.pallas_call(
        functools.partial(_ret_kernel, nb=nb, blk=blk, chain=chain, nblk=nblk, n_steps=n_steps, n_fill=n_fill,
                          has_state=has_state, n_alias=len(aliases), n_cast=len(cast_shapes), alpha=alpha),
        grid=(n_steps + n_fill,),
        in_specs=in_specs,
        out_specs=out_specs,
        out_shape=out_shape,
        input_output_aliases=aliases,
        scratch_shapes=[pltpu.VMEM((rows, heads * dv), BF16)],
        compiler_params=pltpu.CompilerParams(
            dimension_semantics=("arbitrary",), vmem_limit_bytes=VMEM_LIMIT),
        name="ret_mixer_state" if has_state else "ret_mixer",
    )(*args)


_RING = 3


def _att_prompt_kernel(*refs, blk, chain, nblk, n_steps, n_fill, n_cast, alpha):
    step = pl.program_id(0)
    n_in = 9 + n_cast
    h_ref, hp_ref, lgt_ref = refs[n_in:n_in + 3]
    core = refs[:9] + refs[n_in:n_in + 5] + refs[-3:]

    @pl.when(step < n_steps)
    def _():
        _cast_slabs(refs[9:n_in], refs[n_in + 5:n_in + 5 + n_cast])
        _att_prompt_step(*core, i0=(step % nblk) * chain, blk=blk, chain=chain, alpha=alpha)

    if n_fill:
        @pl.when(step >= n_steps)
        def _():
            _zero_tokens(h_ref, hp_ref, lgt_ref)


def _att_prompt_step(x_ref, wqkv_ref, wo_ref, bias_ref, lng_ref, lnb_ref, wrh_ref, wrl_ref, br_ref,
                     h_ref, hp_ref, lgt_ref, krow_ref, vrow_ref, kring, vring, o_scr, *, i0, blk, chain, alpha):
    d_model = x_ref.shape[1]
    dh = d_model // ATT_HEADS
    heads_per_group = LANES // dh
    lane = lax.broadcasted_iota(jnp.int32, (1, LANES), 1)

    @pl.when(i0 == 0)
    def _():
        kring[...] = jnp.zeros(kring.shape, BF16)
        vring[...] = jnp.zeros(vring.shape, BF16)
        vring[:, :, dh:, :] = jnp.ones((_RING, ATT_HEADS, vring.shape[2] - dh, blk), BF16)

    def project(j):
        xb = x_ref[j * blk:(j + 1) * blk, :].astype(BF16)
        q = _dot(xb, wqkv_ref[:, 0:d_model]) * (dh ** -0.5 * LOG2E)
        k = _dot(xb, wqkv_ref[:, d_model:2 * d_model])
        v = _dot(xb, wqkv_ref[:, 2 * d_model:3 * d_model])
        return q, k, v

    def to_ring(j, k, v):
        slot = (i0 + j) % _RING
        krow_ref[j * blk:(j + 1) * blk, :] = k
        vrow_ref[j * blk:(j + 1) * blk, :] = v
        kring[slot] = k.astype(BF16)
        v_t = jnp.transpose(v).astype(BF16)
        for hd in range(ATT_HEADS):
            vring[slot, hd, 0:dh, :] = v_t[hd * dh:(hd + 1) * dh, :]

    def attend(j, q, between):
        i = i0 + j
        qb = q.astype(BF16)
        behind = [(i + _RING - k) % _RING for k in range(_RING)]
        slabs = [jnp.where(i >= behind[k], behind[k], _RING) for k in range(_RING)]

        def scores(hd):
            c0 = (hd // heads_per_group) * LANES
            sub = hd % heads_per_group
            in_head = (lane >= sub * dh) & (lane < (sub + 1) * dh)
            q_pair = qb[:, c0:c0 + LANES]
            qm = jnp.where(in_head, q_pair, jnp.zeros_like(q_pair))
            s_all = _dot_nt(kring[:, :, c0:c0 + LANES].reshape(_RING * blk, LANES), qm)
            return [s_all[k * blk:(k + 1) * blk] + bias_ref[slabs[k], hd] for k in range(_RING)]

        def probs(s_list):
            m = s_list[0].max(axis=0, keepdims=True)
            for k in range(1, _RING):
                m = jnp.maximum(m, s_list[k].max(axis=0, keepdims=True))
            return [jnp.exp2(sc - m).astype(BF16) for sc in s_list]

        def values(hd, p_list):
            o = _dot(vring[0, hd], p_list[0])
            for k in range(1, _RING):
                o = o + _dot(vring[k, hd], p_list[k])
            o_scr[j, hd * dh:(hd + 1) * dh, :] = (o[0:dh] * (1.0 / o[dh:dh + 1])).astype(BF16)

        s_next = scores(0)
        pending = None
        for hd in range(ATT_HEADS):
            s_cur = s_next
            if hd + 1 < ATT_HEADS:
                s_next = scores(hd + 1)
            p_list = probs(s_cur)
            if pending is not None:
                values(*pending)
            pending = (hd, p_list)
            if hd in between:
                between[hd]()
        values(*pending)

    def finish(j):
        rows = slice(j * blk, (j + 1) * blk)
        mix = _dot_tn(o_scr[j], wo_ref[...])
        _finish_tokens(x_ref[rows, :], mix, alpha, lng_ref, lnb_ref, wrh_ref, wrl_ref, br_ref,
                       h_ref.at[rows, :], hp_ref.at[rows, :], lgt_ref.at[rows, :])

    qkv = project(0)
    for j in range(chain):
        to_ring(j, qkv[1], qkv[2])
        nxt = []
        between = {}
        if j + 1 < chain:
            between[0] = lambda j=j: nxt.append(project(j + 1))
        attend(j, qkv[0], between)
        finish(j)
        qkv = nxt[0] if nxt else None


def _rel_bias(rel_table, n_rows, n_cols, offset, sign):
    heads = rel_table.shape[0]
    period = n_rows + n_cols
    m = jnp.arange(period)
    c_minus_r = jnp.where(m < n_cols, m, m - period)
    w = rel_table[:, jnp.clip(offset - sign * c_minus_r, -REL_CLIP, REL_CLIP) + REL_CLIP]
    flat = jnp.broadcast_to(w[:, None, :], (heads, n_rows, period)).reshape(heads, n_rows * period)
    return flat[:, :n_rows * (period - 1)].reshape(heads, n_rows, period - 1)[:, :, :n_cols]


def _att_prompt_bias(rel_table, blk):
    i = jnp.arange(blk)
    out = []
    wide = _rel_bias((rel_table * LOG2E).astype(BF16), blk, _RING * blk, 0, -1)
    masked = jnp.asarray(NEG_INF * LOG2E, BF16)
    for d in range(_RING):
        b = wide[:, :, d * blk:(d + 1) * blk]
        cd = (i[None, :] // CHUNK) - (i[:, None] // CHUNK) + d * (blk // CHUNK)
        ok = (cd >= 0) & (cd <= BAND_CHUNKS)
        out.append(jnp.where(ok[None], b, masked))
    out.append(jnp.full_like(out[0], masked))
    return jnp.stack(out)


def _att_prompt(x2d, n_seq, seq_len, w_qkv, w_o, rel_table, ln_g, ln_b, wr_hi, wr_lo, br, alpha, n_all,
                cast=None):
    d_model = x2d.shape[1]
    blk = ATT_BLOCK
    chain = ATT_CHAIN
    rows = blk * chain
    nblk = seq_len // rows
    keep = min(BAND_CHUNKS * CHUNK, seq_len)
    assert seq_len % rows == 0 and keep % rows == 0
    assert (_RING - 1) * blk >= BAND_CHUNKS * CHUNK
    kb = keep // rows
    bias = _att_prompt_bias(rel_table, blk)
    n_steps = n_seq * nblk
    n_fill = (n_all - n_seq * seq_len) // rows
    assert n_fill * rows == n_all - n_seq * seq_len

    def work(t):
        return jnp.minimum(t, n_steps - 1)

    row_spec = pl.BlockSpec(
        (None, rows, d_model), lambda t: (work(t) // nblk, jnp.maximum(work(t) % nblk - (nblk - kb), 0), 0),
        pipeline_mode=pl.Buffered(1))
    rows_out = jax.ShapeDtypeStruct((n_seq, keep, d_model), F32)
    in_specs = [
        pl.BlockSpec((rows, d_model), lambda t: (work(t), 0)),
        _const_spec(w_qkv.shape),
        _const_spec(w_o.shape),
        _const_spec(bias.shape),
        _const_spec(ln_g.shape),
        _const_spec(ln_b.shape),
        _const_spec(wr_hi.shape),
        _const_spec(wr_lo.shape),
        _const_spec(br.shape),
    ]
    args = [x2d, w_qkv, w_o, bias, ln_g, ln_b, wr_hi, wr_lo, br]
    cast_shapes, cast_specs = _cast_plumbing(args, in_specs, cast, n_steps, work)
    return pl.pallas_call(
        functools.partial(_att_prompt_kernel, blk=blk, chain=chain, nblk=nblk, n_steps=n_steps, n_fill=n_fill,
                          n_cast=len(cast_shapes), alpha=alpha),
        grid=(n_steps + n_fill,),
        in_specs=in_specs,
        out_specs=_token_out_specs(rows, d_model, lambda t: (t, 0)) + (row_spec, row_spec) + cast_specs,
        out_shape=_token_out_shapes(n_all, d_model) + (rows_out, rows_out) + cast_shapes,
        scratch_shapes=[
            pltpu.VMEM((_RING, blk, d_model), BF16),
            pltpu.VMEM((_RING, ATT_HEADS, d_model // ATT_HEADS + BF16_SUBLANES, blk), BF16),
            pltpu.VMEM((chain, d_model, blk), BF16),
        ],
        compiler_params=pltpu.CompilerParams(
            dimension_semantics=("arbitrary",), vmem_limit_bytes=VMEM_LIMIT),
        name="att_mixer",
    )(*args)


def _att_sample_kernel(*refs, nb, blk, alpha):
    (x_ref, kc_ref, vc_ref, wqkv_ref, wo_ref, bias_c_ref, bias_n_ref, lng_ref, lnb_ref,
     wrh_ref, wrl_ref, br_ref) = refs[:12]
    h_ref, hp_ref, lgt_ref, krow_ref, vrow_ref, o_scr = refs[-6:]
    d_model = x_ref.shape[1]
    heads = ATT_HEADS
    dh = d_model // heads
    x = x_ref[...]
    xb = x.astype(BF16)
    q = _dot(xb, wqkv_ref[:, 0:d_model]) * (dh ** -0.5)
    k = _dot(xb, wqkv_ref[:, d_model:2 * d_model])
    v = _dot(xb, wqkv_ref[:, 2 * d_model:3 * d_model])
    krow_ref[...] = k.reshape(nb, blk, d_model)
    vrow_ref[...] = v.reshape(nb, blk, d_model)
    lane_head = lax.broadcasted_iota(jnp.int32, (heads, 1, d_model), 2) // dh
    head_id = lax.broadcasted_iota(jnp.int32, (heads, 1, d_model), 0)
    head_mask = (lane_head == head_id).astype(F32)

    for s in range(nb):
        r0 = s * blk
        qs = q[r0:r0 + blk]
        q_bd = (qs[None, :, :] * head_mask).reshape(heads * blk, d_model).astype(BF16)
        kn = k[r0:r0 + blk].astype(BF16)
        vn = v[r0:r0 + blk].astype(BF16)
        s_c = _dot(q_bd, kc_ref[s].astype(BF16)) + bias_c_ref[...]
        s_n = _dot_nt(q_bd, kn) + bias_n_ref[...]
        m = jnp.maximum(s_c.max(axis=-1, keepdims=True), s_n.max(axis=-1, keepdims=True))
        p_c = jnp.exp(s_c - m)
        p_n = jnp.exp(s_n - m)
        l = p_c.sum(axis=-1, keepdims=True) + p_n.sum(axis=-1, keepdims=True)
        o_full = _dot_nt(p_c.astype(BF16), vc_ref[s].astype(BF16)) + _dot(p_n.astype(BF16), vn)
        o_full = o_full * (1.0 / l)
        o = (o_full.reshape(heads, blk, d_model) * head_mask).sum(axis=0)
        o_scr[r0:r0 + blk, :] = o.astype(BF16)

    mix = _dot(o_scr[...], wo_ref[...])
    _finish_tokens(x, mix, alpha, lng_ref, lnb_ref, wrh_ref, wrl_ref, br_ref, h_ref, hp_ref, lgt_ref)


def _att_sample(x2d, in_row0, n_seq, seq_len, k_cache, v_cache, w_qkv, w_o, rel_table, ln_g, ln_b,
                wr_hi, wr_lo, br, alpha, nb, n_all, out_row0, dst):
    d_model = x2d.shape[1]
    heads = ATT_HEADS
    blk = seq_len
    n_cache = k_cache.shape[1]
    rows = nb * blk
    assert n_seq % nb == 0 and in_row0 % rows == 0 and out_row0 % rows == 0
    in_b0, out_b0 = in_row0 // rows, out_row0 // rows
    bias_c = _rel_bias(rel_table, blk, n_cache, n_cache, 1).reshape(heads * blk, n_cache)
    bias_n = _rel_bias(rel_table, blk, blk, 0, 1).reshape(heads * blk, blk)
    kc = jnp.transpose(k_cache, (0, 2, 3, 1)).reshape(n_seq, d_model, n_cache)
    vc = jnp.transpose(v_cache, (0, 2, 3, 1)).reshape(n_seq, d_model, n_cache)
    cache_spec = pl.BlockSpec((nb, d_model, n_cache), lambda g: (g, 0, 0))
    row_spec = pl.BlockSpec((nb, blk, d_model), lambda g: (g, 0, 0))
    rows_out = jax.ShapeDtypeStruct((n_seq, blk, d_model), F32)
    in_specs = [
        pl.BlockSpec((rows, d_model), lambda g: (in_b0 + g, 0)),
        cache_spec,
        cache_spec,
        _const_spec(w_qkv.shape),
        _const_spec(w_o.shape),
        _const_spec(bias_c.shape),
        _const_spec(bias_n.shape),
        _const_spec(ln_g.shape),
        _const_spec(ln_b.shape),
        _const_spec(wr_hi.shape),
        _const_spec(wr_lo.shape),
        _const_spec(br.shape),
    ]
    args = [x2d, kc, vc, w_qkv, w_o, bias_c, bias_n, ln_g, ln_b, wr_hi, wr_lo, br]
    aliases = _alias_dst(args, in_specs, dst)
    return pl.pallas_call(
        functools.partial(_att_sample_kernel, nb=nb, blk=blk, alpha=alpha),
        grid=(n_seq // nb,),
        in_specs=in_specs,
        out_specs=_token_out_specs(rows, d_model, lambda g: (out_b0 + g, 0)) + (row_spec, row_spec),
        out_shape=_token_out_shapes(n_all, d_model) + (rows_out, rows_out),
        input_output_aliases=aliases,
        scratch_shapes=[pltpu.VMEM((rows, d_model), BF16)],
        compiler_params=pltpu.CompilerParams(
            dimension_semantics=("arbitrary",), vmem_limit_bytes=VMEM_LIMIT),
        name="att_mixer_cache",
    )(*args)


ROUTE_ROWS = 40
ROUTE_TILES_PER_STEP = 5


def _route_kernel(lgt_ref, meta_t_ref, meta_w_ref, count_ref, carry_ref, tri_ref):
    j = pl.program_id(0)
    tm = tri_ref.shape[0]

    @pl.when(j == 0)
    def _():
        carry_ref[...] = jnp.zeros(carry_ref.shape, F32)
        r = lax.broadcasted_iota(jnp.int32, (tm, tm), 0)
        c = lax.broadcasted_iota(jnp.int32, (tm, tm), 1)
        tri_ref[...] = (r < c).astype(BF16)

    for sub in range(lgt_ref.shape[0] // tm):
        _route_tile(lgt_ref, meta_t_ref, meta_w_ref, carry_ref, tri_ref, slice(sub * tm, (sub + 1) * tm))
    count_ref[...] = carry_ref[...].astype(jnp.int32)


def _route_tile(lgt_ref, meta_t_ref, meta_w_ref, carry_ref, tri_ref, tok):
    tm = tri_ref.shape[0]
    a = jnp.transpose(lgt_ref[tok, :])[0:ROUTE_ROWS, :]
    row = lax.broadcasted_iota(jnp.int32, (ROUTE_ROWS, tm), 0)
    big = jnp.int32(LANES)
    glog = jnp.where(row < N_GROUPS, a, NEG_INF)
    gmax = glog.max(axis=0, keepdims=True)
    gsel = jnp.where(glog == gmax, row, big).min(axis=0, keepdims=True)
    gp = 1.0 / jnp.exp(glog - gmax).sum(axis=0, keepdims=True)
    first = N_GROUPS + gsel * EXP_PER_GROUP
    in_grp = (row >= first) & (row < first + EXP_PER_GROUP)
    elog = jnp.where(in_grp, a, NEG_INF)
    v1 = elog.max(axis=0, keepdims=True)
    l1 = jnp.where(elog == v1, row, big).min(axis=0, keepdims=True)
    elog2 = jnp.where(row == l1, NEG_INF, elog)
    v2 = elog2.max(axis=0, keepdims=True)
    l2 = jnp.where(elog2 == v2, row, big).min(axis=0, keepdims=True)
    e2 = jnp.exp(v2 - v1)
    w1 = gp / (1.0 + e2)
    w2 = gp * e2 / (1.0 + e2)

    is1 = row == l1
    is2 = row == l2
    oh = (is1 | is2).astype(F32)
    before = _dot(oh.astype(BF16), tri_ref[...]) + carry_ref[:, 0:1]
    rank1 = jnp.where(is1, before, 0.0).sum(axis=0, keepdims=True)
    rank2 = jnp.where(is2, before, 0.0).sum(axis=0, keepdims=True)
    carry_ref[...] = carry_ref[...] + oh.sum(axis=1, keepdims=True)

    pad = jnp.zeros((SUBLANES - 4, tm), jnp.int32)
    meta_t_ref[:, tok] = jnp.concatenate(
        [l1 - N_GROUPS, l2 - N_GROUPS, rank1.astype(jnp.int32), rank2.astype(jnp.int32), pad], axis=0)
    wt = jnp.concatenate([w1, w2, jnp.zeros((LANES - 2, tm), F32)], axis=0)
    meta_w_ref[tok, :] = jnp.transpose(wt)


def _route(logits):
    n_tok = logits.shape[0]
    tm = TOKEN_TILE
    assert n_tok % tm == 0
    per_step = max(d for d in range(1, ROUTE_TILES_PER_STEP + 1) if (n_tok // tm) % d == 0)
    rows = tm * per_step
    tile = pl.BlockSpec((rows, LANES), lambda j: (j, 0))
    return pl.pallas_call(
        _route_kernel,
        grid=(n_tok // rows,),
        in_specs=[tile],
        out_specs=(pl.BlockSpec((SUBLANES, rows), lambda j: (0, j)), tile,
                   pl.BlockSpec((ROUTE_ROWS, LANES), lambda j: (0, 0))),
        out_shape=(
            jax.ShapeDtypeStruct((SUBLANES, n_tok), jnp.int32),
            jax.ShapeDtypeStruct((n_tok, LANES), F32),
            jax.ShapeDtypeStruct((ROUTE_ROWS, LANES), jnp.int32),
        ),
        scratch_shapes=[pltpu.VMEM((ROUTE_ROWS, LANES), F32), pltpu.VMEM((tm, tm), BF16)],
        compiler_params=pltpu.CompilerParams(dimension_semantics=("arbitrary",)),
        name="moe_route",
    )(logits)


def _sc_gather_loop(table_hbm, idx_v, out_hbm, rows_v, sems, base, n_chunk, r):
    def gather(c, slot):
        off = pl.multiple_of(c * r, r)
        return pltpu.make_async_copy(table_hbm.at[idx_v.at[pl.ds(off, r)]], rows_v.at[slot], sems.at[slot])

    def finish(c, slot):
        gather(c, slot).wait()
        pltpu.sync_copy(rows_v.at[slot], out_hbm.at[pl.ds(base + pl.multiple_of(c * r, r), r)])

    gather(0, 0).start()
    if n_chunk > 1:
        gather(1, 1).start()

    def body(pair, carry):
        c = 2 * pair
        for slot in range(2):
            finish(c + slot, slot)

            @pl.when(c + slot + 2 < n_chunk)
            def _():
                gather(c + slot + 2, slot).start()
        return carry

    lax.fori_loop(0, n_chunk // 2, body, 0)
    if n_chunk % 2:
        finish(n_chunk - 1, 0)


def _gather_rows(table, idx):
    m = idx.shape[0]
    width = table.shape[1]
    r = SC_ROWS_PER_CHUNK
    assert m % (SC_WORKERS * r) == 0
    per_w = m // SC_WORKERS
    mesh = plsc.VectorSubcoreMesh(core_axis_name="c", subcore_axis_name="s")

    @functools.partial(
        pl.kernel,
        mesh=mesh,
        out_type=jax.ShapeDtypeStruct((m, width), table.dtype),
        scratch_types=[
            pltpu.VMEM((per_w,), jnp.int32),
            pltpu.VMEM((2, r, width), table.dtype),
            pltpu.SemaphoreType.DMA((2,)),
        ],
    )
    def gather(table_hbm, idx_hbm, out_hbm, idx_v, rows_v, sem):
        wid = lax.axis_index("s") * 2 + lax.axis_index("c")
        base = wid * per_w
        pltpu.sync_copy(idx_hbm.at[pl.ds(base, per_w)], idx_v)
        _sc_gather_loop(table_hbm, idx_v, out_hbm, rows_v, sem, base, per_w // r, r)

    return gather(table, idx)


def _dispatch_rows(table, pos, row0, n_rows):
    n_tok, width = table.shape
    n_pairs = pos.shape[0]
    r = SC_ROWS_PER_CHUNK
    lanes = SC_LANES
    assert n_rows % (SC_WORKERS * r) == 0
    per_w = n_rows // SC_WORKERS
    n_stage = 16
    stage = n_pairs // n_stage
    assert stage * n_stage == n_pairs and stage % lanes == 0 and per_w % lanes == 0
    assert row0 + n_rows < 3 * n_tok
    mesh = plsc.VectorSubcoreMesh(core_axis_name="c", subcore_axis_name="s")

    @functools.partial(
        pl.kernel,
        mesh=mesh,
        out_type=jax.ShapeDtypeStruct((n_rows, width), table.dtype),
        scratch_types=[
            pltpu.VMEM((per_w,), jnp.int32),
            pltpu.VMEM((stage,), jnp.int32),
            pltpu.VMEM((2, r, width), table.dtype),
            pltpu.SemaphoreType.DMA((2,)),
        ],
        compiler_params=pltpu.CompilerParams(needs_layout_passes=False),
    )
    def dispatch(table_hbm, pos_hbm, out_hbm, src_v, pos_v, rows_v, sem):
        wid = lax.axis_index("s") * 2 + lax.axis_index("c")
        out_base = wid * per_w
        base = row0 + out_base
        lane = lax.iota(jnp.int32, lanes)

        def wrap(t):
            t = jnp.where(t >= n_tok, t - n_tok, t)
            return jnp.where(t >= n_tok, t - n_tok, t)

        def init(i, carry):
            off = pl.multiple_of(i * lanes, lanes)
            src_v[pl.ds(off, lanes)] = wrap(base + off + lane)
            return carry

        lax.fori_loop(0, per_w // lanes, init, 0)

        def scan_stage(sidx, carry):
            pair0 = sidx * stage
            pltpu.sync_copy(pos_hbm.at[pl.ds(pl.multiple_of(pair0, 8), stage)], pos_v)

            @plsc.parallel_loop(0, stage // lanes, unroll=4)
            def _(i):
                off = pl.multiple_of(i * lanes, lanes)
                local = pos_v[pl.ds(off, lanes)] - base
                mine = (local >= 0) & (local < per_w)
                plsc.store_scatter(src_v, [jnp.where(mine, local, 0)], wrap(pair0 + off + lane), mask=mine)

            return carry

        lax.fori_loop(0, n_stage, scan_stage, 0)
        _sc_gather_loop(table_hbm, src_v, out_hbm, rows_v, sem, out_base, per_w // r, r)

    return dispatch(table, pos)


def _gemm_kernel(tile_expert_ref, n_used_ref, xs_ref, wg_ref, wu_ref, wd_ref, *rest, tile0):
    ys_ref = rest[-1]
    tile = tile0 + pl.program_id(0)

    @pl.when(tile < n_used_ref[0])
    def _():
        x = _unpack_rows(xs_ref[...]).astype(BF16)
        g = _dot(x, wg_ref[...])
        u = _dot(x, wu_ref[...])
        hmid = (jax.nn.silu(g) * u).astype(BF16)
        ys_ref[...] = _pack_rows(_dot(hmid, wd_ref[...]))


def _expert_gemm(xs, tile_expert, n_used, w_gate, w_up, w_down, tile0, n_rows_all, ys):
    n_rows, half = xs.shape
    d_model, d_exp = w_gate.shape[1], w_gate.shape[2]
    tm = GEMM_TILE
    n_tiles = n_rows // tm

    def last_used(j, nu):
        return jnp.minimum(tile0 + j, nu[0] - 1)

    def in_map(j, te, nu):
        return (jnp.maximum(last_used(j, nu) - tile0, 0), 0)

    def out_map(j, te, nu):
        return (jnp.maximum(last_used(j, nu), tile0), 0)

    def w_map(j, te, nu):
        return (te[tile0 + j], 0, 0)

    in_specs = [
        pl.BlockSpec((tm, half), in_map),
        pl.BlockSpec((None, d_model, d_exp), w_map),
        pl.BlockSpec((None, d_model, d_exp), w_map),
        pl.BlockSpec((None, d_exp, d_model), w_map),
    ]
    args = [tile_expert, n_used, xs, w_gate, w_up, w_down]
    aliases = {}
    if ys is not None:
        aliases[len(args)] = 0
        args.append(ys)
        in_specs.append(pl.BlockSpec(memory_space=pl.ANY))
    grid_spec = pltpu.PrefetchScalarGridSpec(
        num_scalar_prefetch=2,
        grid=(n_tiles,),
        in_specs=in_specs,
        out_specs=pl.BlockSpec((tm, half), out_map),
    )
    return pl.pallas_call(
        functools.partial(_gemm_kernel, tile0=tile0),
        grid_spec=grid_spec,
        out_shape=jax.ShapeDtypeStruct((n_rows_all, half), U32),
        input_output_aliases=aliases,
        compiler_params=pltpu.CompilerParams(
            dimension_semantics=("arbitrary",), vmem_limit_bytes=VMEM_LIMIT),
        name="moe_gemm",
    )(*args)


def _combine_kernel(y0_ref, y1_ref, mw_ref, h_ref, pp_ref, ps_ref, lng_ref, lnb_ref, wproj_ref, wgate_ref,
                    *out_refs, n_prompt_tiles, alpha):
    j = pl.program_id(0)
    tm = h_ref.shape[0]
    half = tm // 2

    def normed(rows):
        mw = mw_ref[rows, :]
        ffn = mw[:, 0:1] * _unpack_rows(y0_ref[rows, :]) + mw[:, 1:2] * _unpack_rows(y1_ref[rows, :])
        return _layer_norm(alpha * h_ref[rows, :] + ffn, lng_ref[...], lnb_ref[...])

    def gated(rows, h2):
        p = jnp.where(j < n_prompt_tiles, pp_ref[rows, :], ps_ref[rows, :]).astype(BF16)
        gate = jax.nn.sigmoid(_dot(h2.astype(BF16), wgate_ref[...]))
        return h2 + gate * _dot(p, wproj_ref[...])

    rows_a, rows_b = slice(0, half), slice(half, tm)
    h2_a = normed(rows_a)
    h2_b = normed(rows_b)
    out = jnp.concatenate([gated(rows_a, h2_a), gated(rows_b, h2_b)], axis=0)
    if len(out_refs) == 1:
        out_refs[0][...] = out
    else:
        @pl.when(j < n_prompt_tiles)
        def _():
            out_refs[0][...] = out

        @pl.when(j >= n_prompt_tiles)
        def _():
            out_refs[1][...] = out


def _combine(yg, meta_w, h, p_prompt, p_sample, layer, ln_g, ln_b, w_proj, w_gate, alpha, split_out):
    n_tok, d_model = h.shape
    tm = TOKEN_TILE
    n_t = n_tok // tm
    n_tp = p_prompt.shape[1] // tm
    n_ts = p_sample.shape[1] // tm
    assert n_tp * tm == p_prompt.shape[1] and n_ts * tm == p_sample.shape[1] and n_tp + n_ts == n_t
    ple = p_prompt.shape[2]
    tile = pl.BlockSpec((tm, d_model), lambda j: (j, 0))
    if split_out:
        out_specs = (pl.BlockSpec((tm, d_model), lambda j: (jnp.minimum(j, n_tp - 1), 0)),
                     pl.BlockSpec((tm, d_model), lambda j: (jnp.maximum(j - n_tp, 0), 0)))
        out_shape = (jax.ShapeDtypeStruct((n_tp * tm, d_model), F32),
                     jax.ShapeDtypeStruct((n_ts * tm, d_model), F32))
    else:
        out_specs = tile
        out_shape = jax.ShapeDtypeStruct((n_tok, d_model), F32)
    return pl.pallas_call(
        functools.partial(_combine_kernel, n_prompt_tiles=n_tp, alpha=alpha),
        grid=(n_t,),
        in_specs=[
            pl.BlockSpec((tm, d_model // 2), lambda j: (j, 0)),
            pl.BlockSpec((tm, d_model // 2), lambda j: (j + n_t, 0)),
            pl.BlockSpec((tm, LANES), lambda j: (j, 0)),
            tile,
            pl.BlockSpec((None, tm, ple), lambda j: (layer, jnp.minimum(j, n_tp - 1), 0)),
            pl.BlockSpec((None, tm, ple), lambda j: (layer, jnp.maximum(j - n_tp, 0), 0)),
            _const_spec(ln_g.shape),
            _const_spec(ln_b.shape),
            _const_spec(w_proj.shape),
            _const_spec(w_gate.shape),
        ],
        out_specs=out_specs,
        out_shape=out_shape,
        compiler_params=pltpu.CompilerParams(
            dimension_semantics=("arbitrary",), vmem_limit_bytes=VMEM_LIMIT),
        name="moe_combine",
    )(yg, yg, meta_w, h, p_prompt, p_sample, ln_g, ln_b, w_proj, w_gate)


def _router_weights(w_grp, b_grp, w_exp, b_exp):
    d_model = w_grp.shape[0]
    w = jnp.concatenate([w_grp, jnp.transpose(w_exp, (1, 0, 2)).reshape(d_model, N_EXPERTS)], axis=1)
    w = jnp.pad(w, ((0, 0), (0, LANES - w.shape[1])))
    b = jnp.concatenate([b_grp, b_exp.reshape(N_EXPERTS)])
    b = jnp.pad(b, (0, LANES - b.shape[0])).reshape(1, LANES)
    w_hi = w.astype(BF16)
    w_lo = (w - w_hi.astype(F32)).astype(BF16)
    return w_hi, w_lo, b


def _moe(h_packed, logits, w_gate, w_up, w_down):
    n_tok = h_packed.shape[0]
    tm = GEMM_TILE
    meta_t, meta_w, counts = _route(logits)
    counts = counts[N_GROUPS:N_GROUPS + N_EXPERTS, 0]
    tiles_per_expert = (counts + tm - 1) // tm
    tile_end = jnp.cumsum(tiles_per_expert)
    row_start = (tile_end - tiles_per_expert) * tm
    eid = meta_t[0:2]
    rank = meta_t[2:4]
    experts = jnp.arange(N_EXPERTS, dtype=jnp.int32)
    start = jnp.sum(jnp.where(eid[:, :, None] == experts, row_start, 0), axis=-1)
    pos = (start + rank).reshape(-1).astype(jnp.int32)
    part_quant = MOE_PARTS * SC_WORKERS * SC_ROWS_PER_CHUNK * (tm // math.gcd(tm, SC_WORKERS * SC_ROWS_PER_CHUNK))
    n_tiles = -(-(2 * n_tok) // tm) + N_EXPERTS
    n_rows = -(-(n_tiles * tm) // part_quant) * part_quant
    n_tiles = n_rows // tm
    n_used = tile_end[-1:].astype(jnp.int32)
    tile_ids = jnp.minimum(jnp.arange(n_tiles, dtype=jnp.int32), n_used[0] - 1)
    tile_expert = jnp.sum(tile_end[None, :] <= tile_ids[:, None], axis=1).astype(jnp.int32)
    part_rows = n_rows // MOE_PARTS
    xs_parts = [_dispatch_rows(h_packed, pos, part * part_rows, part_rows) for part in range(MOE_PARTS)]
    ys = None
    for part, xs in enumerate(xs_parts):
        ys = _expert_gemm(xs, tile_expert, n_used, w_gate, w_up, w_down,
                          part * (part_rows // tm), n_rows, ys)
    return _gather_rows(ys, pos), meta_w


def kernel(x_prompt, x_sample, p_prompt, p_sample, state_ret, cache_att_k, cache_att_v, ret_w_in, ret_gn_g,
           ret_w_o, att_w_qkv, att_rel_bias, att_w_o, ln1_g, ln1_b, ln2_g, ln2_b, moe_w_grp, moe_b_grp,
           moe_w_exp, moe_b_exp, moe_w_gate, moe_w_up, moe_w_down, ple_w_proj, ple_w_gate):
    n_p, len_p, d_model = x_prompt.shape
    n_s, len_s, _ = x_sample.shape
    depth = ln1_g.shape[0]
    alpha = float((2 * depth) ** 0.25)
    tok_p, tok_s = n_p * len_p, n_s * len_s
    n_all = tok_p + tok_s
    dh = d_model // ATT_HEADS
    pp = p_prompt.reshape(depth, tok_p, -1)
    ps = p_sample.reshape(depth, tok_s, -1)
    nb_s = 4

    x_all = None
    y_prompt = y_sample = None
    states_p, states_s, k_p, v_p, k_s, v_s = [], [], [], [], [], []
    for i in range(depth):
        jj = i // 2
        wr_hi, wr_lo, br = _router_weights(moe_w_grp[i], moe_b_grp[i], moe_w_exp[i], moe_b_exp[i])
        lng, lnb = ln1_g[i].reshape(1, d_model), ln1_b[i].reshape(1, d_model)
        moe_cast = (i, [moe_w_gate, moe_w_up, moe_w_down])
        if x_all is None:
            src_p, src_s, row_s = x_prompt.reshape(tok_p, d_model), x_sample.reshape(tok_s, d_model), 0
        else:
            src_p, src_s, row_s = x_all, x_all, tok_p
        if i % 2 == 0:
            w_in = ret_w_in[jj].astype(BF16)
            w_o = ret_w_o[jj].astype(BF16)
            gn = ret_gn_g[jj].reshape(1, -1)
            h, hp, lgt, st_p, *w_moe = _ret_mixer(
                src_p, 0, n_p, len_p, 0, None, w_in, w_o, gn, lng, lnb, wr_hi, wr_lo, br, alpha, nb=1,
                blk=min(RET_BLOCK, len_p), chain=RET_CHAIN, n_all=n_all, out_row0=0, dst=None, cast=moe_cast)
            h, hp, lgt, st_s = _ret_mixer(src_s, row_s, n_s, len_s, PAST_LEN, state_ret[jj], w_in, w_o, gn,
                                          lng, lnb, wr_hi, wr_lo, br, alpha, nb=nb_s, blk=len_s, chain=1,
                                          n_all=n_all, out_row0=tok_p, dst=(h, hp, lgt))
            states_p.append(st_p)
            states_s.append(st_s)
        else:
            w_qkv = att_w_qkv[jj].astype(BF16)
            w_o = att_w_o[jj].astype(BF16)
            h, hp, lgt, kr, vr, *w_moe = _att_prompt(src_p, n_p, len_p, w_qkv, w_o, att_rel_bias[jj], lng, lnb,
                                                     wr_hi, wr_lo, br, alpha, n_all, cast=moe_cast)
            k_p.append(kr.reshape(n_p, -1, ATT_HEADS, dh))
            v_p.append(vr.reshape(n_p, -1, ATT_HEADS, dh))
            h, hp, lgt, kr, vr = _att_sample(src_s, row_s, n_s, len_s, cache_att_k[jj], cache_att_v[jj],
                                             w_qkv, w_o, att_rel_bias[jj], lng, lnb, wr_hi, wr_lo, br, alpha,
                                             nb=nb_s, n_all=n_all, out_row0=tok_p, dst=(h, hp, lgt))
            k_s.append(kr.reshape(n_s, len_s, ATT_HEADS, dh))
            v_s.append(vr.reshape(n_s, len_s, ATT_HEADS, dh))
        n_exp = moe_w_gate.shape[1]
        w_moe = [w.reshape(n_exp, -1, w.shape[1]) for w in w_moe]
        yg, meta_w = _moe(hp, lgt, *w_moe)
        last = i == depth - 1
        out = _combine(yg, meta_w, h, pp, ps, i, ln2_g[i].reshape(1, d_model), ln2_b[i].reshape(1, d_model),
                       ple_w_proj[i].astype(BF16), ple_w_gate[i].astype(BF16), alpha, split_out=last)
        if last:
            y_prompt = out[0].reshape(n_p, len_p, d_model)
            y_sample = out[1].reshape(n_s, len_s, d_model)
        else:
            x_all = out

    return (y_prompt, y_sample, jnp.stack(states_p), jnp.stack(states_s),
            jnp.stack(k_p), jnp.stack(v_p), jnp.stack(k_s), jnp.stack(v_s))
```

```python
import functools
import math

import numpy as np
import jax
import jax.numpy as jnp
from jax import lax
from jax.experimental import pallas as pl
from jax.experimental.pallas import tpu as pltpu
from jax.experimental.pallas import tpu_sc as plsc

CHUNK = 64
PAST_LEN = 2048
RET_HEADS = 4
ROPE_BASE = 10000.0
ATT_HEADS = 16
BAND_CHUNKS = 8
REL_CLIP = 256
N_GROUPS = 4
EXP_PER_GROUP = 8
N_EXPERTS = N_GROUPS * EXP_PER_GROUP
LN_EPS = 1e-5
NEG_INF = -1e30

LANES = 128
SUBLANES = 8
BF16_SUBLANES = 16
SC_WORKERS = 32
SC_LANES = 16
SC_ROWS_PER_CHUNK = 32
VMEM_LIMIT = 58 * 1024 * 1024

RET_BLOCK = 256
RET_CHAIN = 2
ATT_BLOCK = 4 * CHUNK
ATT_CHAIN = 2
TOKEN_TILE = 512
GEMM_TILE = 512
MOE_PARTS = 4
MOE_FIRST_PART_DIV = 8

F32 = jnp.float32
BF16 = jnp.bfloat16
U32 = jnp.uint32
HI_MASK = 0xFFFF0000
LOG2E = math.log2(math.e)


def _dot(a, b):
    return jnp.dot(a, b, preferred_element_type=F32)


def _dot_nt(a, b):
    return lax.dot_general(a, b, (((1,), (1,)), ((), ())), preferred_element_type=F32)


def _dot_tn(a, b):
    return lax.dot_general(a, b, (((0,), (0,)), ((), ())), preferred_element_type=F32)


def _layer_norm(x, g, b):
    mu = jnp.mean(x, axis=-1, keepdims=True)
    xc = x - mu
    var = jnp.mean(xc * xc, axis=-1, keepdims=True)
    return xc * lax.rsqrt(var + LN_EPS) * g + b


def _pack_rows(x):
    half = x.shape[1] // 2
    lo = lax.bitcast_convert_type(x[:, :half].astype(BF16).astype(F32), U32) >> 16
    hi = lax.bitcast_convert_type(x[:, half:].astype(BF16).astype(F32), U32) & U32(HI_MASK)
    return lo | hi


def _unpack_rows(p):
    lo = lax.bitcast_convert_type(p << 16, F32)
    hi = lax.bitcast_convert_type(p & U32(HI_MASK), F32)
    return jnp.concatenate([lo, hi], axis=1)


def _router_logits(h, wr_hi_ref, wr_lo_ref, br_ref):
    h_hi = h.astype(BF16)
    h_lo = (h - h_hi.astype(F32)).astype(BF16)
    w_hi = wr_hi_ref[...]
    return _dot(h_hi, w_hi) + _dot(h_lo, w_hi) + _dot(h_hi, wr_lo_ref[...]) + br_ref[...]


def _finish_tokens(x, mix, alpha, lng_ref, lnb_ref, wrh_ref, wrl_ref, br_ref, h_ref, hp_ref, lgt_ref):
    hh = _layer_norm(alpha * x + mix, lng_ref[...], lnb_ref[...])
    h_ref[...] = hh
    hp_ref[...] = _pack_rows(hh)
    lgt_ref[...] = _router_logits(hh, wrh_ref, wrl_ref, br_ref)


def _zero_tokens(h_ref, hp_ref, lgt_ref):
    h_ref[...] = jnp.zeros(h_ref.shape, h_ref.dtype)
    hp_ref[...] = jnp.zeros(hp_ref.shape, hp_ref.dtype)
    lgt_ref[...] = jnp.zeros(lgt_ref.shape, lgt_ref.dtype)


def _const_spec(shape):
    nd = len(shape)
    return pl.BlockSpec(shape, lambda *_: (0,) * nd, pipeline_mode=pl.Buffered(1))


def _token_out_shapes(n_all, d_model):
    return (
        jax.ShapeDtypeStruct((n_all, d_model), F32),
        jax.ShapeDtypeStruct((n_all, d_model // 2), U32),
        jax.ShapeDtypeStruct((n_all, LANES), F32),
    )


def _token_out_specs(rows, d_model, row_map):
    return (
        pl.BlockSpec((rows, d_model), row_map),
        pl.BlockSpec((rows, d_model // 2), row_map),
        pl.BlockSpec((rows, LANES), row_map),
    )


def _alias_dst(args, in_specs, dst):
    if dst is None:
        return {}
    aliases = {}
    for k, arr in enumerate(dst):
        aliases[len(args)] = k
        args.append(arr)
        in_specs.append(pl.BlockSpec(memory_space=pl.ANY))
    return aliases


def _cast_plumbing(args, in_specs, cast, n_steps, work):
    if cast is None:
        return (), ()
    layer, weights = cast
    shapes, specs = [], []
    for w in weights:
        depth, n_exp, rows, cols = w.shape
        assert (n_exp * rows) % n_steps == 0
        slab = (n_exp * rows) // n_steps
        args.append(w.reshape(depth, n_exp * rows, cols))
        in_specs.append(pl.BlockSpec((None, slab, cols), lambda t: (layer, work(t), 0)))
        shapes.append(jax.ShapeDtypeStruct((n_exp * rows, cols), BF16))
        specs.append(pl.BlockSpec((slab, cols), lambda t: (work(t), 0)))
    return tuple(shapes), tuple(specs)


def _cast_slabs(in_refs, out_refs):
    for src, dst in zip(in_refs, out_refs):
        dst[...] = src[...].astype(BF16)


def _ret_log_gamma():
    h = np.arange(RET_HEADS, dtype=np.float32)
    return np.log(np.float32(1.0) - np.float32(2.0) ** (np.float32(-5.0) - h)).astype(np.float32)


def _ret_kernel(*refs, nb, blk, chain, nblk, n_steps, n_fill, has_state, n_alias, n_cast, alpha):
    step = pl.program_id(0)
    n_in = 12 + int(has_state) + n_alias + n_cast
    h_ref, hp_ref, lgt_ref = refs[n_in:n_in + 3]
    core = refs[:12 + int(has_state)] + refs[n_in:n_in + 4] + refs[-1:]

    @pl.when(step < n_steps)
    def _():
        _cast_slabs(refs[n_in - n_cast:n_in], refs[n_in + 4:n_in + 4 + n_cast])
        _ret_step(*core, first_block=step % nblk == 0, nb=nb, blk=blk, chain=chain, has_state=has_state,
                  alpha=alpha)

    if n_fill:
        @pl.when(step >= n_steps)
        def _():
            _zero_tokens(h_ref, hp_ref, lgt_ref)


def _ret_step(*refs, first_block, nb, blk, chain, has_state, alpha):
    (x_ref, cos_ref, sin_ref, dmask_ref, win_ref, wo_ref, gn_ref, lng_ref, lnb_ref,
     wrh_ref, wrl_ref, br_ref) = refs[:12]
    s_in_ref = refs[12] if has_state else None
    h_ref, hp_ref, lgt_ref, s_out_ref, gated_ref = refs[12 + int(has_state):]
    heads = RET_HEADS
    d_model = x_ref.shape[1]
    dk = d_model // heads
    dv = 2 * d_model // heads
    hk, hv = heads * dk, heads * dv
    half = dk // 2
    lg = _ret_log_gamma()

    n_sub = nb * chain
    proj_rows = blk if chain > 1 else n_sub * blk
    rowf = lax.broadcasted_iota(jnp.int32, (blk, 1), 0).astype(F32)

    def proj(group):
        xb = x_ref[group * proj_rows:(group + 1) * proj_rows, :].astype(BF16)
        return (_dot(xb, win_ref[:, 0:hk]), _dot(xb, win_ref[:, hk:2 * hk]),
                _dot(xb, win_ref[:, 2 * hk:2 * hk + hv]), _dot(xb, win_ref[:, 2 * hk + hv:2 * hk + 2 * hv]))

    if not has_state:
        @pl.when(first_block)
        def _():
            s_out_ref[...] = jnp.zeros(s_out_ref.shape, F32)

    s_prev_ref = s_in_ref if has_state else s_out_ref

    def head(j, h, projected):
        q_all, k_all, v_all, g_all = projected
        s = j // chain
        c0 = (j % chain) * blk
        r0 = (j * blk) % proj_rows
        cos = cos_ref[c0:c0 + blk, :]
        sin = sin_ref[c0:c0 + blk, :]

        def rot(t):
            t1, t2 = t[:, :half], t[:, half:]
            return jnp.concatenate([t1 * cos - t2 * sin, t1 * sin + t2 * cos], axis=1)

        lgh = float(lg[h])
        q = rot(q_all[r0:r0 + blk, h * dk:(h + 1) * dk])
        k = rot(k_all[r0:r0 + blk, h * dk:(h + 1) * dk]) * (dk ** -0.5)
        v = v_all[r0:r0 + blk, h * dv:(h + 1) * dv]
        g = g_all[r0:r0 + blk, h * dv:(h + 1) * dv]
        vb = v.astype(BF16)
        scores = _dot_nt(q.astype(BF16), k.astype(BF16)) * dmask_ref[h]
        inner = _dot(scores.astype(BF16), vb)
        s_prev = s_prev_ref[s, h]
        q_dec = q * jnp.exp(lgh * (rowf + 1.0))
        cross = _dot(q_dec.astype(BF16), s_prev.astype(BF16))
        k_dec = k * jnp.exp(lgh * (float(blk - 1) - rowf))
        s_out_ref[s, h] = math.exp(lgh * blk) * s_prev + _dot_tn(k_dec.astype(BF16), vb)
        o = inner + cross
        mu = jnp.mean(o, axis=-1, keepdims=True)
        oc = o - mu
        var = jnp.mean(oc * oc, axis=-1, keepdims=True)
        on = oc * lax.rsqrt(var + LN_EPS) * gn_ref[:, h * dv:(h + 1) * dv]
        gated_ref[j * blk:(j + 1) * blk, h * dv:(h + 1) * dv] = (jax.nn.silu(g) * on).astype(BF16)

    def tail(group):
        rows = slice(group * proj_rows, (group + 1) * proj_rows)
        mix = _dot(gated_ref[rows, :], wo_ref[...])
        _finish_tokens(x_ref[rows, :], mix, alpha, lng_ref, lnb_ref, wrh_ref, wrl_ref, br_ref,
                       h_ref.at[rows, :], hp_ref.at[rows, :], lgt_ref.at[rows, :])

    n_groups = (n_sub * blk) // proj_rows
    subs_per_group = proj_rows // blk
    projected = proj(0)
    for group in range(n_groups):
        nxt = None
        for jj in range(subs_per_group):
            for h in range(heads):
                head(group * subs_per_group + jj, h, projected)
                if jj == 0 and h == 0 and group + 1 < n_groups:
                    nxt = proj(group + 1)
                if jj == 0 and h == 1 and group > 0:
                    tail(group - 1)
        projected = nxt
    tail(n_groups - 1)


def _ret_mixer(x2d, in_row0, n_seq, seq_len, pos0, state_in, w_in, w_o, gn_g, ln_g, ln_b,
               wr_hi, wr_lo, br, alpha, nb, blk, chain, n_all, out_row0, dst, cast=None):
    d_model = x2d.shape[1]
    heads = RET_HEADS
    dk, dv = d_model // heads, 2 * d_model // heads
    half = dk // 2
    nblk = seq_len // (blk * chain)
    has_state = state_in is not None
    rows = nb * blk * chain
    assert seq_len % (blk * chain) == 0 and n_seq % nb == 0
    assert (not has_state) or nblk == 1
    assert nb == 1 or (nblk == 1 and chain == 1)
    assert in_row0 % rows == 0 and out_row0 % rows == 0
    in_b0, out_b0 = in_row0 // rows, out_row0 // rows

    pos = (pos0 + np.arange(seq_len, dtype=np.int32)).astype(np.float32)
    inv_freq = np.float32(ROPE_BASE) ** (-np.arange(half, dtype=np.float32) / np.float32(half))
    ang = (pos[:, None] * inv_freq[None, :]).astype(np.float64)
    cos, sin = jnp.asarray(np.cos(ang), F32), jnp.asarray(np.sin(ang), F32)
    lg = jnp.asarray(_ret_log_gamma())
    ii = jnp.arange(blk, dtype=F32)
    diff = ii[:, None] - ii[None, :]
    dmask = jnp.where(diff >= 0, jnp.exp(lg[:, None, None] * jnp.maximum(diff, 0.0)), 0.0)

    n_steps = (n_seq // nb) * nblk
    n_fill = 0 if dst is not None else (n_all - n_seq * seq_len) // rows
    assert dst is not None or (out_row0 == 0 and n_fill * rows == n_all - n_seq * seq_len)

    def work(t):
        return jnp.minimum(t, n_steps - 1)

    in_specs = [
        pl.BlockSpec((rows, d_model), lambda t: (in_b0 + work(t), 0)),
        pl.BlockSpec((blk * chain, half), lambda t: (work(t) % nblk, 0)),
        pl.BlockSpec((blk * chain, half), lambda t: (work(t) % nblk, 0)),
        _const_spec(dmask.shape),
        _const_spec(w_in.shape),
        _const_spec(w_o.shape),
        _const_spec(gn_g.shape),
        _const_spec(ln_g.shape),
        _const_spec(ln_b.shape),
        _const_spec(wr_hi.shape),
        _const_spec(wr_lo.shape),
        _const_spec(br.shape),
    ]
    args = [x2d, cos, sin, dmask, w_in, w_o, gn_g, ln_g, ln_b, wr_hi, wr_lo, br]
    state_spec = pl.BlockSpec((nb, heads, dk, dv), lambda t: (work(t) // nblk, 0, 0, 0))
    if has_state:
        in_specs.append(state_spec)
        args.append(state_in)
    aliases = _alias_dst(args, in_specs, dst)
    cast_shapes, cast_specs = _cast_plumbing(args, in_specs, cast, n_steps, work)
    out_shape = (_token_out_shapes(n_all, d_model) + (jax.ShapeDtypeStruct((n_seq, heads, dk, dv), F32),)
                 + cast_shapes)
    out_specs = _token_out_specs(rows, d_model, lambda t: (out_b0 + t, 0)) + (state_spec,) + cast_specs
    return pl.pallas_call(
        functools.partial(_ret_kernel, nb=nb, blk=blk, chain=chain, nblk=nblk, n_steps=n_steps, n_fill=n_fill,
                          has_state=has_state, n_alias=len(aliases), n_cast=len(cast_shapes), alpha=alpha),
        grid=(n_steps + n_fill,),
        in_specs=in_specs,
        out_specs=out_specs,
        out_shape=out_shape,
        input_output_aliases=aliases,
        scratch_shapes=[pltpu.VMEM((rows, heads * dv), BF16)],
        compiler_params=pltpu.CompilerParams(
            dimension_semantics=("arbitrary",), vmem_limit_bytes=VMEM_LIMIT),
        name="ret_mixer_state" if has_state else "ret_mixer",
    )(*args)


_RING = 3


def _att_prompt_kernel(*refs, blk, chain, nblk, n_steps, n_fill, n_cast, alpha):
    step = pl.program_id(0)
    n_in = 9 + n_cast
    h_ref, hp_ref, lgt_ref = refs[n_in:n_in + 3]
    core = refs[:9] + refs[n_in:n_in + 5] + refs[-3:]

    @pl.when(step < n_steps)
    def _():
        _cast_slabs(refs[9:n_in], refs[n_in + 5:n_in + 5 + n_cast])
        _att_prompt_step(*core, i0=(step % nblk) * chain, blk=blk, chain=chain, alpha=alpha)

    if n_fill:
        @pl.when(step >= n_steps)
        def _():
            _zero_tokens(h_ref, hp_ref, lgt_ref)


def _att_prompt_step(x_ref, wqkv_ref, wo_ref, bias_ref, lng_ref, lnb_ref, wrh_ref, wrl_ref, br_ref,
                     h_ref, hp_ref, lgt_ref, krow_ref, vrow_ref, kring, vring, o_scr, *, i0, blk, chain, alpha):
    d_model = x_ref.shape[1]
    dh = d_model // ATT_HEADS
    heads_per_group = LANES // dh
    lane = lax.broadcasted_iota(jnp.int32, (1, LANES), 1)

    @pl.when(i0 == 0)
    def _():
        kring[...] = jnp.zeros(kring.shape, BF16)
        vring[...] = jnp.zeros(vring.shape, BF16)
        vring[:, :, dh:, :] = jnp.ones((_RING, ATT_HEADS, vring.shape[2] - dh, blk), BF16)

    def project(j):
        xb = x_ref[j * blk:(j + 1) * blk, :].astype(BF16)
        q = _dot(xb, wqkv_ref[:, 0:d_model]) * (dh ** -0.5 * LOG2E)
        k = _dot(xb, wqkv_ref[:, d_model:2 * d_model])
        v = _dot(xb, wqkv_ref[:, 2 * d_model:3 * d_model])
        return q, k, v

    def to_ring(j, k, v):
        slot = (i0 + j) % _RING
        krow_ref[j * blk:(j + 1) * blk, :] = k
        vrow_ref[j * blk:(j + 1) * blk, :] = v
        kring[slot] = k.astype(BF16)
        v_t = jnp.transpose(v).astype(BF16)
        for hd in range(ATT_HEADS):
            vring[slot, hd, 0:dh, :] = v_t[hd * dh:(hd + 1) * dh, :]

    def attend(j, q, between):
        i = i0 + j
        qb = q.astype(BF16)
        behind = [(i + _RING - k) % _RING for k in range(_RING)]
        slabs = [jnp.where(i >= behind[k], behind[k], _RING) for k in range(_RING)]

        def scores(hd):
            c0 = (hd // heads_per_group) * LANES
            sub = hd % heads_per_group
            in_head = (lane >= sub * dh) & (lane < (sub + 1) * dh)
            q_pair = qb[:, c0:c0 + LANES]
            qm = jnp.where(in_head, q_pair, jnp.zeros_like(q_pair))
            s_all = _dot_nt(kring[:, :, c0:c0 + LANES].reshape(_RING * blk, LANES), qm)
            return [s_all[k * blk:(k + 1) * blk] + bias_ref[slabs[k], hd] for k in range(_RING)]

        def probs(s_list):
            m = s_list[0].max(axis=0, keepdims=True)
            for k in range(1, _RING):
                m = jnp.maximum(m, s_list[k].max(axis=0, keepdims=True))
            return [jnp.exp2(sc - m).astype(BF16) for sc in s_list]

        def values(hd, p_list):
            o = _dot(vring[0, hd], p_list[0])
            for k in range(1, _RING):
                o = o + _dot(vring[k, hd], p_list[k])
            o_scr[j, hd * dh:(hd + 1) * dh, :] = (o[0:dh] * (1.0 / o[dh:dh + 1])).astype(BF16)

        s_next = scores(0)
        pending = None
        for hd in range(ATT_HEADS):
            s_cur = s_next
            if hd + 1 < ATT_HEADS:
                s_next = scores(hd + 1)
            p_list = probs(s_cur)
            if pending is not None:
                values(*pending)
            pending = (hd, p_list)
            if hd in between:
                between[hd]()
        values(*pending)

    def finish(j):
        rows = slice(j * blk, (j + 1) * blk)
        mix = _dot_tn(o_scr[j], wo_ref[...])
        _finish_tokens(x_ref[rows, :], mix, alpha, lng_ref, lnb_ref, wrh_ref, wrl_ref, br_ref,
                       h_ref.at[rows, :], hp_ref.at[rows, :], lgt_ref.at[rows, :])

    qkv = project(0)
    for j in range(chain):
        to_ring(j, qkv[1], qkv[2])
        nxt = []
        between = {}
        if j + 1 < chain:
            between[0] = lambda j=j: nxt.append(project(j + 1))
        attend(j, qkv[0], between)
        finish(j)
        qkv = nxt[0] if nxt else None


def _rel_bias(rel_table, n_rows, n_cols, offset, sign):
    heads = rel_table.shape[0]
    period = n_rows + n_cols
    m = jnp.arange(period)
    c_minus_r = jnp.where(m < n_cols, m, m - period)
    w = rel_table[:, jnp.clip(offset - sign * c_minus_r, -REL_CLIP, REL_CLIP) + REL_CLIP]
    flat = jnp.broadcast_to(w[:, None, :], (heads, n_rows, period)).reshape(heads, n_rows * period)
    return flat[:, :n_rows * (period - 1)].reshape(heads, n_rows, period - 1)[:, :, :n_cols]


def _att_prompt_bias(rel_table, blk):
    i = jnp.arange(blk)
    out = []
    wide = _rel_bias((rel_table * LOG2E).astype(BF16), blk, _RING * blk, 0, -1)
    masked = jnp.asarray(NEG_INF * LOG2E, BF16)
    for d in range(_RING):
        b = wide[:, :, d * blk:(d + 1) * blk]
        cd = (i[None, :] // CHUNK) - (i[:, None] // CHUNK) + d * (blk // CHUNK)
        ok = (cd >= 0) & (cd <= BAND_CHUNKS)
        out.append(jnp.where(ok[None], b, masked))
    out.append(jnp.full_like(out[0], masked))
    return jnp.stack(out)


def _att_prompt(x2d, n_seq, seq_len, w_qkv, w_o, rel_table, ln_g, ln_b, wr_hi, wr_lo, br, alpha, n_all,
                cast=None):
    d_model = x2d.shape[1]
    blk = ATT_BLOCK
    chain = ATT_CHAIN
    rows = blk * chain
    nblk = seq_len // rows
    keep = min(BAND_CHUNKS * CHUNK, seq_len)
    assert seq_len % rows == 0 and keep % rows == 0
    assert (_RING - 1) * blk >= BAND_CHUNKS * CHUNK
    kb = keep // rows
    bias = _att_prompt_bias(rel_table, blk)
    n_steps = n_seq * nblk
    n_fill = (n_all - n_seq * seq_len) // rows
    assert n_fill * rows == n_all - n_seq * seq_len

    def work(t):
        return jnp.minimum(t, n_steps - 1)

    row_spec = pl.BlockSpec(
        (None, rows, d_model), lambda t: (work(t) // nblk, jnp.maximum(work(t) % nblk - (nblk - kb), 0), 0),
        pipeline_mode=pl.Buffered(1))
    rows_out = jax.ShapeDtypeStruct((n_seq, keep, d_model), F32)
    in_specs = [
        pl.BlockSpec((rows, d_model), lambda t: (work(t), 0)),
        _const_spec(w_qkv.shape),
        _const_spec(w_o.shape),
        _const_spec(bias.shape),
        _const_spec(ln_g.shape),
        _const_spec(ln_b.shape),
        _const_spec(wr_hi.shape),
        _const_spec(wr_lo.shape),
        _const_spec(br.shape),
    ]
    args = [x2d, w_qkv, w_o, bias, ln_g, ln_b, wr_hi, wr_lo, br]
    cast_shapes, cast_specs = _cast_plumbing(args, in_specs, cast, n_steps, work)
    return pl.pallas_call(
        functools.partial(_att_prompt_kernel, blk=blk, chain=chain, nblk=nblk, n_steps=n_steps, n_fill=n_fill,
                          n_cast=len(cast_shapes), alpha=alpha),
        grid=(n_steps + n_fill,),
        in_specs=in_specs,
        out_specs=_token_out_specs(rows, d_model, lambda t: (t, 0)) + (row_spec, row_spec) + cast_specs,
        out_shape=_token_out_shapes(n_all, d_model) + (rows_out, rows_out) + cast_shapes,
        scratch_shapes=[
            pltpu.VMEM((_RING, blk, d_model), BF16),
            pltpu.VMEM((_RING, ATT_HEADS, d_model // ATT_HEADS + BF16_SUBLANES, blk), BF16),
            pltpu.VMEM((chain, d_model, blk), BF16),
        ],
        compiler_params=pltpu.CompilerParams(
            dimension_semantics=("arbitrary",), vmem_limit_bytes=VMEM_LIMIT),
        name="att_mixer",
    )(*args)


def _att_sample_kernel(*refs, nb, blk, alpha):
    (x_ref, kc_ref, vc_ref, wqkv_ref, wo_ref, bias_c_ref, bias_n_ref, lng_ref, lnb_ref,
     wrh_ref, wrl_ref, br_ref) = refs[:12]
    h_ref, hp_ref, lgt_ref, krow_ref, vrow_ref, o_scr = refs[-6:]
    d_model = x_ref.shape[1]
    heads = ATT_HEADS
    dh = d_model // heads
    x = x_ref[...]
    xb = x.astype(BF16)
    q = _dot(xb, wqkv_ref[:, 0:d_model]) * (dh ** -0.5)
    k = _dot(xb, wqkv_ref[:, d_model:2 * d_model])
    v = _dot(xb, wqkv_ref[:, 2 * d_model:3 * d_model])
    krow_ref[...] = k.reshape(nb, blk, d_model)
    vrow_ref[...] = v.reshape(nb, blk, d_model)
    lane_head = lax.broadcasted_iota(jnp.int32, (heads, 1, d_model), 2) // dh
    head_id = lax.broadcasted_iota(jnp.int32, (heads, 1, d_model), 0)
    head_mask = (lane_head == head_id).astype(F32)

    for s in range(nb):
        r0 = s * blk
        qs = q[r0:r0 + blk]
        q_bd = (qs[None, :, :] * head_mask).reshape(heads * blk, d_model).astype(BF16)
        kn = k[r0:r0 + blk].astype(BF16)
        vn = v[r0:r0 + blk].astype(BF16)
        s_c = _dot(q_bd, kc_ref[s].astype(BF16)) + bias_c_ref[...]
        s_n = _dot_nt(q_bd, kn) + bias_n_ref[...]
        m = jnp.maximum(s_c.max(axis=-1, keepdims=True), s_n.max(axis=-1, keepdims=True))
        p_c = jnp.exp(s_c - m)
        p_n = jnp.exp(s_n - m)
        l = p_c.sum(axis=-1, keepdims=True) + p_n.sum(axis=-1, keepdims=True)
        o_full = _dot_nt(p_c.astype(BF16), vc_ref[s].astype(BF16)) + _dot(p_n.astype(BF16), vn)
        o_full = o_full * (1.0 / l)
        o = (o_full.reshape(heads, blk, d_model) * head_mask).sum(axis=0)
        o_scr[r0:r0 + blk, :] = o.astype(BF16)

    mix = _dot(o_scr[...], wo_ref[...])
    _finish_tokens(x, mix, alpha, lng_ref, lnb_ref, wrh_ref, wrl_ref, br_ref, h_ref, hp_ref, lgt_ref)


def _att_sample(x2d, in_row0, n_seq, seq_len, k_cache, v_cache, w_qkv, w_o, rel_table, ln_g, ln_b,
                wr_hi, wr_lo, br, alpha, nb, n_all, out_row0, dst):
    d_model = x2d.shape[1]
    heads = ATT_HEADS
    blk = seq_len
    n_cache = k_cache.shape[1]
    rows = nb * blk
    assert n_seq % nb == 0 and in_row0 % rows == 0 and out_row0 % rows == 0
    in_b0, out_b0 = in_row0 // rows, out_row0 // rows
    bias_c = _rel_bias(rel_table, blk, n_cache, n_cache, 1).reshape(heads * blk, n_cache)
    bias_n = _rel_bias(rel_table, blk, blk, 0, 1).reshape(heads * blk, blk)
    kc = jnp.transpose(k_cache, (0, 2, 3, 1)).reshape(n_seq, d_model, n_cache)
    vc = jnp.transpose(v_cache, (0, 2, 3, 1)).reshape(n_seq, d_model, n_cache)
    cache_spec = pl.BlockSpec((nb, d_model, n_cache), lambda g: (g, 0, 0))
    row_spec = pl.BlockSpec((nb, blk, d_model), lambda g: (g, 0, 0))
    rows_out = jax.ShapeDtypeStruct((n_seq, blk, d_model), F32)
    in_specs = [
        pl.BlockSpec((rows, d_model), lambda g: (in_b0 + g, 0)),
        cache_spec,
        cache_spec,
        _const_spec(w_qkv.shape),
        _const_spec(w_o.shape),
        _const_spec(bias_c.shape),
        _const_spec(bias_n.shape),
        _const_spec(ln_g.shape),
        _const_spec(ln_b.shape),
        _const_spec(wr_hi.shape),
        _const_spec(wr_lo.shape),
        _const_spec(br.shape),
    ]
    args = [x2d, kc, vc, w_qkv, w_o, bias_c, bias_n, ln_g, ln_b, wr_hi, wr_lo, br]
    aliases = _alias_dst(args, in_specs, dst)
    return pl.pallas_call(
        functools.partial(_att_sample_kernel, nb=nb, blk=blk, alpha=alpha),
        grid=(n_seq // nb,),
        in_specs=in_specs,
        out_specs=_token_out_specs(rows, d_model, lambda g: (out_b0 + g, 0)) + (row_spec, row_spec),
        out_shape=_token_out_shapes(n_all, d_model) + (rows_out, rows_out),
        input_output_aliases=aliases,
        scratch_shapes=[pltpu.VMEM((rows, d_model), BF16)],
        compiler_params=pltpu.CompilerParams(
            dimension_semantics=("arbitrary",), vmem_limit_bytes=VMEM_LIMIT),
        name="att_mixer_cache",
    )(*args)


ROUTE_ROWS = 40
ROUTE_TILES_PER_STEP = 5


def _route_kernel(lgt_ref, meta_t_ref, meta_w_ref, count_ref, carry_ref, tri_ref):
    j = pl.program_id(0)
    tm = tri_ref.shape[0]

    @pl.when(j == 0)
    def _():
        carry_ref[...] = jnp.zeros(carry_ref.shape, F32)
        r = lax.broadcasted_iota(jnp.int32, (tm, tm), 0)
        c = lax.broadcasted_iota(jnp.int32, (tm, tm), 1)
        tri_ref[...] = (r < c).astype(BF16)

    for sub in range(lgt_ref.shape[0] // tm):
        _route_tile(lgt_ref, meta_t_ref, meta_w_ref, carry_ref, tri_ref, slice(sub * tm, (sub + 1) * tm))
    count_ref[...] = carry_ref[...].astype(jnp.int32)


def _route_tile(lgt_ref, meta_t_ref, meta_w_ref, carry_ref, tri_ref, tok):
    tm = tri_ref.shape[0]
    a = jnp.transpose(lgt_ref[tok, :])[0:ROUTE_ROWS, :]
    row = lax.broadcasted_iota(jnp.int32, (ROUTE_ROWS, tm), 0)
    big = jnp.int32(LANES)
    glog = jnp.where(row < N_GROUPS, a, NEG_INF)
    gmax = glog.max(axis=0, keepdims=True)
    gsel = jnp.where(glog == gmax, row, big).min(axis=0, keepdims=True)
    gp = 1.0 / jnp.exp(glog - gmax).sum(axis=0, keepdims=True)
    first = N_GROUPS + gsel * EXP_PER_GROUP
    in_grp = (row >= first) & (row < first + EXP_PER_GROUP)
    elog = jnp.where(in_grp, a, NEG_INF)
    v1 = elog.max(axis=0, keepdims=True)
    l1 = jnp.where(elog == v1, row, big).min(axis=0, keepdims=True)
    elog2 = jnp.where(row == l1, NEG_INF, elog)
    v2 = elog2.max(axis=0, keepdims=True)
    l2 = jnp.where(elog2 == v2, row, big).min(axis=0, keepdims=True)
    e2 = jnp.exp(v2 - v1)
    w1 = gp / (1.0 + e2)
    w2 = gp * e2 / (1.0 + e2)

    is1 = row == l1
    is2 = row == l2
    oh = (is1 | is2).astype(F32)
    before = _dot(oh.astype(BF16), tri_ref[...]) + carry_ref[:, 0:1]
    rank1 = jnp.where(is1, before, 0.0).sum(axis=0, keepdims=True)
    rank2 = jnp.where(is2, before, 0.0).sum(axis=0, keepdims=True)
    carry_ref[...] = carry_ref[...] + oh.sum(axis=1, keepdims=True)

    pad = jnp.zeros((SUBLANES - 4, tm), jnp.int32)
    meta_t_ref[:, tok] = jnp.concatenate(
        [l1 - N_GROUPS, l2 - N_GROUPS, rank1.astype(jnp.int32), rank2.astype(jnp.int32), pad], axis=0)
    wt = jnp.concatenate([w1, w2, jnp.zeros((LANES - 2, tm), F32)], axis=0)
    meta_w_ref[tok, :] = jnp.transpose(wt)


def _route(logits):
    n_tok = logits.shape[0]
    tm = TOKEN_TILE
    assert n_tok % tm == 0
    per_step = max(d for d in range(1, ROUTE_TILES_PER_STEP + 1) if (n_tok // tm) % d == 0)
    rows = tm * per_step
    tile = pl.BlockSpec((rows, LANES), lambda j: (j, 0))
    return pl.pallas_call(
        _route_kernel,
        grid=(n_tok // rows,),
        in_specs=[tile],
        out_specs=(pl.BlockSpec((SUBLANES, rows), lambda j: (0, j)), tile,
                   pl.BlockSpec((ROUTE_ROWS, LANES), lambda j: (0, 0))),
        out_shape=(
            jax.ShapeDtypeStruct((SUBLANES, n_tok), jnp.int32),
            jax.ShapeDtypeStruct((n_tok, LANES), F32),
            jax.ShapeDtypeStruct((ROUTE_ROWS, LANES), jnp.int32),
        ),
        scratch_shapes=[pltpu.VMEM((ROUTE_ROWS, LANES), F32), pltpu.VMEM((tm, tm), BF16)],
        compiler_params=pltpu.CompilerParams(dimension_semantics=("arbitrary",)),
        name="moe_route",
    )(logits)


def _sc_gather_loop(table_hbm, idx_v, out_hbm, rows_v, sems, base, n_chunk, r):
    def gather(c, slot):
        off = pl.multiple_of(c * r, r)
        return pltpu.make_async_copy(table_hbm.at[idx_v.at[pl.ds(off, r)]], rows_v.at[slot], sems.at[slot])

    def finish(c, slot):
        gather(c, slot).wait()
        pltpu.sync_copy(rows_v.at[slot], out_hbm.at[pl.ds(base + pl.multiple_of(c * r, r), r)])

    gather(0, 0).start()
    if n_chunk > 1:
        gather(1, 1).start()

    def body(pair, carry):
        c = 2 * pair
        for slot in range(2):
            finish(c + slot, slot)

            @pl.when(c + slot + 2 < n_chunk)
            def _():
                gather(c + slot + 2, slot).start()
        return carry

    lax.fori_loop(0, n_chunk // 2, body, 0)
    if n_chunk % 2:
        finish(n_chunk - 1, 0)


def _gather_rows(table, idx):
    m = idx.shape[0]
    width = table.shape[1]
    r = SC_ROWS_PER_CHUNK
    assert m % (SC_WORKERS * r) == 0
    per_w = m // SC_WORKERS
    mesh = plsc.VectorSubcoreMesh(core_axis_name="c", subcore_axis_name="s")

    @functools.partial(
        pl.kernel,
        mesh=mesh,
        out_type=jax.ShapeDtypeStruct((m, width), table.dtype),
        scratch_types=[
            pltpu.VMEM((per_w,), jnp.int32),
            pltpu.VMEM((2, r, width), table.dtype),
            pltpu.SemaphoreType.DMA((2,)),
        ],
    )
    def gather(table_hbm, idx_hbm, out_hbm, idx_v, rows_v, sem):
        wid = lax.axis_index("s") * 2 + lax.axis_index("c")
        base = wid * per_w
        pltpu.sync_copy(idx_hbm.at[pl.ds(base, per_w)], idx_v)
        _sc_gather_loop(table_hbm, idx_v, out_hbm, rows_v, sem, base, per_w // r, r)

    return gather(table, idx)


def _dispatch_rows(table, pos, row0, n_rows):
    n_tok, width = table.shape
    n_pairs = pos.shape[0]
    r = SC_ROWS_PER_CHUNK
    lanes = SC_LANES
    assert n_rows % (SC_WORKERS * r) == 0
    per_w = n_rows // SC_WORKERS
    n_stage = 16
    stage = n_pairs // n_stage
    assert stage * n_stage == n_pairs and stage % lanes == 0 and per_w % lanes == 0
    assert row0 + n_rows < 3 * n_tok
    mesh = plsc.VectorSubcoreMesh(core_axis_name="c", subcore_axis_name="s")

    @functools.partial(
        pl.kernel,
        mesh=mesh,
        out_type=jax.ShapeDtypeStruct((n_rows, width), table.dtype),
        scratch_types=[
            pltpu.VMEM((per_w,), jnp.int32),
            pltpu.VMEM((stage,), jnp.int32),
            pltpu.VMEM((2, r, width), table.dtype),
            pltpu.SemaphoreType.DMA((2,)),
        ],
        compiler_params=pltpu.CompilerParams(needs_layout_passes=False),
    )
    def dispatch(table_hbm, pos_hbm, out_hbm, src_v, pos_v, rows_v, sem):
        wid = lax.axis_index("s") * 2 + lax.axis_index("c")
        out_base = wid * per_w
        base = row0 + out_base
        lane = lax.iota(jnp.int32, lanes)

        def wrap(t):
            t = jnp.where(t >= n_tok, t - n_tok, t)
            return jnp.where(t >= n_tok, t - n_tok, t)

        def init(i, carry):
            off = pl.multiple_of(i * lanes, lanes)
            src_v[pl.ds(off, lanes)] = wrap(base + off + lane)
            return carry

        lax.fori_loop(0, per_w // lanes, init, 0)

        def scan_stage(sidx, carry):
            pair0 = sidx * stage
            pltpu.sync_copy(pos_hbm.at[pl.ds(pl.multiple_of(pair0, 8), stage)], pos_v)

            @plsc.parallel_loop(0, stage // lanes, unroll=4)
            def _(i):
                off = pl.multiple_of(i * lanes, lanes)
                local = pos_v[pl.ds(off, lanes)] - base
                mine = (local >= 0) & (local < per_w)
                plsc.store_scatter(src_v, [jnp.where(mine, local, 0)], wrap(pair0 + off + lane), mask=mine)

            return carry

        lax.fori_loop(0, n_stage, scan_stage, 0)
        _sc_gather_loop(table_hbm, src_v, out_hbm, rows_v, sem, out_base, per_w // r, r)

    return dispatch(table, pos)


def _gemm_kernel(tile_expert_ref, n_used_ref, xs_ref, wg_ref, wu_ref, wd_ref, *rest, tile0):
    ys_ref = rest[-1]
    tile = tile0 + pl.program_id(0)

    @pl.when(tile < n_used_ref[0])
    def _():
        x = _unpack_rows(xs_ref[...]).astype(BF16)
        g = _dot(x, wg_ref[...])
        u = _dot(x, wu_ref[...])
        hmid = (jax.nn.silu(g) * u).astype(BF16)
        ys_ref[...] = _pack_rows(_dot(hmid, wd_ref[...]))


def _expert_gemm(xs, tile_expert, n_used, w_gate, w_up, w_down, tile0, n_rows_all, ys):
    n_rows, half = xs.shape
    d_model, d_exp = w_gate.shape[1], w_gate.shape[2]
    tm = GEMM_TILE
    n_tiles = n_rows // tm

    def last_used(j, nu):
        return jnp.minimum(tile0 + j, nu[0] - 1)

    def in_map(j, te, nu):
        return (jnp.maximum(last_used(j, nu) - tile0, 0), 0)

    def out_map(j, te, nu):
        return (jnp.maximum(last_used(j, nu), tile0), 0)

    def w_map(j, te, nu):
        return (te[tile0 + j], 0, 0)

    in_specs = [
        pl.BlockSpec((tm, half), in_map),
        pl.BlockSpec((None, d_model, d_exp), w_map),
        pl.BlockSpec((None, d_model, d_exp), w_map),
        pl.BlockSpec((None, d_exp, d_model), w_map),
    ]
    args = [tile_expert, n_used, xs, w_gate, w_up, w_down]
    aliases = {}
    if ys is not None:
        aliases[len(args)] = 0
        args.append(ys)
        in_specs.append(pl.BlockSpec(memory_space=pl.ANY))
    grid_spec = pltpu.PrefetchScalarGridSpec(
        num_scalar_prefetch=2,
        grid=(n_tiles,),
        in_specs=in_specs,
        out_specs=pl.BlockSpec((tm, half), out_map),
    )
    return pl.pallas_call(
        functools.partial(_gemm_kernel, tile0=tile0),
        grid_spec=grid_spec,
        out_shape=jax.ShapeDtypeStruct((n_rows_all, half), U32),
        input_output_aliases=aliases,
        compiler_params=pltpu.CompilerParams(
            dimension_semantics=("arbitrary",), vmem_limit_bytes=VMEM_LIMIT),
        name="moe_gemm",
    )(*args)


def _combine_kernel(y0_ref, y1_ref, mw_ref, h_ref, pp_ref, ps_ref, lng_ref, lnb_ref, wproj_ref, wgate_ref,
                    *out_refs, n_prompt_tiles, alpha):
    j = pl.program_id(0)
    tm = h_ref.shape[0]
    half = tm // 2

    def normed(rows):
        mw = mw_ref[rows, :]
        ffn = mw[:, 0:1] * _unpack_rows(y0_ref[rows, :]) + mw[:, 1:2] * _unpack_rows(y1_ref[rows, :])
        return _layer_norm(alpha * h_ref[rows, :] + ffn, lng_ref[...], lnb_ref[...])

    def gated(rows, h2):
        p = jnp.where(j < n_prompt_tiles, pp_ref[rows, :], ps_ref[rows, :]).astype(BF16)
        gate = jax.nn.sigmoid(_dot(h2.astype(BF16), wgate_ref[...]))
        return h2 + gate * _dot(p, wproj_ref[...])

    rows_a, rows_b = slice(0, half), slice(half, tm)
    h2_a = normed(rows_a)
    h2_b = normed(rows_b)
    out = jnp.concatenate([gated(rows_a, h2_a), gated(rows_b, h2_b)], axis=0)
    if len(out_refs) == 1:
        out_refs[0][...] = out
    else:
        @pl.when(j < n_prompt_tiles)
        def _():
            out_refs[0][...] = out

        @pl.when(j >= n_prompt_tiles)
        def _():
            out_refs[1][...] = out


def _combine(yg, meta_w, h, p_prompt, p_sample, layer, ln_g, ln_b, w_proj, w_gate, alpha, split_out):
    n_tok, d_model = h.shape
    tm = TOKEN_TILE
    n_t = n_tok // tm
    n_tp = p_prompt.shape[1] // tm
    n_ts = p_sample.shape[1] // tm
    assert n_tp * tm == p_prompt.shape[1] and n_ts * tm == p_sample.shape[1] and n_tp + n_ts == n_t
    ple = p_prompt.shape[2]
    tile = pl.BlockSpec((tm, d_model), lambda j: (j, 0))
    if split_out:
        out_specs = (pl.BlockSpec((tm, d_model), lambda j: (jnp.minimum(j, n_tp - 1), 0)),
                     pl.BlockSpec((tm, d_model), lambda j: (jnp.maximum(j - n_tp, 0), 0)))
        out_shape = (jax.ShapeDtypeStruct((n_tp * tm, d_model), F32),
                     jax.ShapeDtypeStruct((n_ts * tm, d_model), F32))
    else:
        out_specs = tile
        out_shape = jax.ShapeDtypeStruct((n_tok, d_model), F32)
    return pl.pallas_call(
        functools.partial(_combine_kernel, n_prompt_tiles=n_tp, alpha=alpha),
        grid=(n_t,),
        in_specs=[
            pl.BlockSpec((tm, d_model // 2), lambda j: (j, 0)),
            pl.BlockSpec((tm, d_model // 2), lambda j: (j + n_t, 0)),
            pl.BlockSpec((tm, LANES), lambda j: (j, 0)),
            tile,
            pl.BlockSpec((None, tm, ple), lambda j: (layer, jnp.minimum(j, n_tp - 1), 0)),
            pl.BlockSpec((None, tm, ple), lambda j: (layer, jnp.maximum(j - n_tp, 0), 0)),
            _const_spec(ln_g.shape),
            _const_spec(ln_b.shape),
            _const_spec(w_proj.shape),
            _const_spec(w_gate.shape),
        ],
        out_specs=out_specs,
        out_shape=out_shape,
        compiler_params=pltpu.CompilerParams(
            dimension_semantics=("arbitrary",), vmem_limit_bytes=VMEM_LIMIT),
        name="moe_combine",
    )(yg, yg, meta_w, h, p_prompt, p_sample, ln_g, ln_b, w_proj, w_gate)


def _router_weights(w_grp, b_grp, w_exp, b_exp):
    d_model = w_grp.shape[0]
    w = jnp.concatenate([w_grp, jnp.transpose(w_exp, (1, 0, 2)).reshape(d_model, N_EXPERTS)], axis=1)
    w = jnp.pad(w, ((0, 0), (0, LANES - w.shape[1])))
    b = jnp.concatenate([b_grp, b_exp.reshape(N_EXPERTS)])
    b = jnp.pad(b, (0, LANES - b.shape[0])).reshape(1, LANES)
    w_hi = w.astype(BF16)
    w_lo = (w - w_hi.astype(F32)).astype(BF16)
    return w_hi, w_lo, b


def _moe(h_packed, logits, w_gate, w_up, w_down):
    n_tok = h_packed.shape[0]
    tm = GEMM_TILE
    meta_t, meta_w, counts = _route(logits)
    counts = counts[N_GROUPS:N_GROUPS + N_EXPERTS, 0]
    tiles_per_expert = (counts + tm - 1) // tm
    tile_end = jnp.cumsum(tiles_per_expert)
    row_start = (tile_end - tiles_per_expert) * tm
    eid = meta_t[0:2]
    rank = meta_t[2:4]
    experts = jnp.arange(N_EXPERTS, dtype=jnp.int32)
    start = jnp.sum(jnp.where(eid[:, :, None] == experts, row_start, 0), axis=-1)
    pos = (start + rank).reshape(-1).astype(jnp.int32)
    quant = SC_WORKERS * SC_ROWS_PER_CHUNK * (tm // math.gcd(tm, SC_WORKERS * SC_ROWS_PER_CHUNK))
    n_tiles = -(-(2 * n_tok) // tm) + N_EXPERTS
    n_units = -(-(n_tiles * tm) // quant)
    n_rows = n_units * quant
    n_tiles = n_rows // tm
    first = max(1, n_units // MOE_FIRST_PART_DIV)
    rest, later = n_units - first, MOE_PARTS - 1
    part_rows = [first * quant] + [(rest // later + (1 if k < rest % later else 0)) * quant for k in range(later)]
    n_used = tile_end[-1:].astype(jnp.int32)
    tile_ids = jnp.minimum(jnp.arange(n_tiles, dtype=jnp.int32), n_used[0] - 1)
    tile_expert = jnp.sum(tile_end[None, :] <= tile_ids[:, None], axis=1).astype(jnp.int32)
    row0 = [sum(part_rows[:k]) for k in range(MOE_PARTS)]
    xs_parts = [_dispatch_rows(h_packed, pos, row0[k], part_rows[k]) for k in range(MOE_PARTS)]
    ys = None
    for k, xs in enumerate(xs_parts):
        ys = _expert_gemm(xs, tile_expert, n_used, w_gate, w_up, w_down, row0[k] // tm, n_rows, ys)
    return _gather_rows(ys, pos), meta_w


def kernel(x_prompt, x_sample, p_prompt, p_sample, state_ret, cache_att_k, cache_att_v, ret_w_in, ret_gn_g,
           ret_w_o, att_w_qkv, att_rel_bias, att_w_o, ln1_g, ln1_b, ln2_g, ln2_b, moe_w_grp, moe_b_grp,
           moe_w_exp, moe_b_exp, moe_w_gate, moe_w_up, moe_w_down, ple_w_proj, ple_w_gate):
    n_p, len_p, d_model = x_prompt.shape
    n_s, len_s, _ = x_sample.shape
    depth = ln1_g.shape[0]
    alpha = float((2 * depth) ** 0.25)
    tok_p, tok_s = n_p * len_p, n_s * len_s
    n_all = tok_p + tok_s
    dh = d_model // ATT_HEADS
    pp = p_prompt.reshape(depth, tok_p, -1)
    ps = p_sample.reshape(depth, tok_s, -1)
    nb_s = 4

    x_all = None
    y_prompt = y_sample = None
    states_p, states_s, k_p, v_p, k_s, v_s = [], [], [], [], [], []
    for i in range(depth):
        jj = i // 2
        wr_hi, wr_lo, br = _router_weights(moe_w_grp[i], moe_b_grp[i], moe_w_exp[i], moe_b_exp[i])
        lng, lnb = ln1_g[i].reshape(1, d_model), ln1_b[i].reshape(1, d_model)
        moe_cast = (i, [moe_w_gate, moe_w_up, moe_w_down])
        if x_all is None:
            src_p, src_s, row_s = x_prompt.reshape(tok_p, d_model), x_sample.reshape(tok_s, d_model), 0
        else:
            src_p, src_s, row_s = x_all, x_all, tok_p
        if i % 2 == 0:
            w_in = ret_w_in[jj].astype(BF16)
            w_o = ret_w_o[jj].astype(BF16)
            gn = ret_gn_g[jj].reshape(1, -1)
            h, hp, lgt, st_p, *w_moe = _ret_mixer(
                src_p, 0, n_p, len_p, 0, None, w_in, w_o, gn, lng, lnb, wr_hi, wr_lo, br, alpha, nb=1,
                blk=min(RET_BLOCK, len_p), chain=RET_CHAIN, n_all=n_all, out_row0=0, dst=None, cast=moe_cast)
            h, hp, lgt, st_s = _ret_mixer(src_s, row_s, n_s, len_s, PAST_LEN, state_ret[jj], w_in, w_o, gn,
                                          lng, lnb, wr_hi, wr_lo, br, alpha, nb=nb_s, blk=len_s, chain=1,
                                          n_all=n_all, out_row0=tok_p, dst=(h, hp, lgt))
            states_p.append(st_p)
            states_s.append(st_s)
        else:
            w_qkv = att_w_qkv[jj].astype(BF16)
            w_o = att_w_o[jj].astype(BF16)
            h, hp, lgt, kr, vr, *w_moe = _att_prompt(src_p, n_p, len_p, w_qkv, w_o, att_rel_bias[jj], lng, lnb,
                                                     wr_hi, wr_lo, br, alpha, n_all, cast=moe_cast)
            k_p.append(kr.reshape(n_p, -1, ATT_HEADS, dh))
            v_p.append(vr.reshape(n_p, -1, ATT_HEADS, dh))
            h, hp, lgt, kr, vr = _att_sample(src_s, row_s, n_s, len_s, cache_att_k[jj], cache_att_v[jj],
                                             w_qkv, w_o, att_rel_bias[jj], lng, lnb, wr_hi, wr_lo, br, alpha,
                                             nb=nb_s, n_all=n_all, out_row0=tok_p, dst=(h, hp, lgt))
            k_s.append(kr.reshape(n_s, len_s, ATT_HEADS, dh))
            v_s.append(vr.reshape(n_s, len_s, ATT_HEADS, dh))
        n_exp = moe_w_gate.shape[1]
        w_moe = [w.reshape(n_exp, -1, w.shape[1]) for w in w_moe]
        yg, meta_w = _moe(hp, lgt, *w_moe)
        last = i == depth - 1
        out = _combine(yg, meta_w, h, pp, ps, i, ln2_g[i].reshape(1, d_model), ln2_b[i].reshape(1, d_model),
                       ple_w_proj[i].astype(BF16), ple_w_gate[i].astype(BF16), alpha, split_out=last)
        if last:
            y_prompt = out[0].reshape(n_p, len_p, d_model)
            y_sample = out[1].reshape(n_s, len_s, d_model)
        else:
            x_all = out

    return (y_prompt, y_sample, jnp.stack(states_p), jnp.stack(states_s),
            jnp.stack(k_p), jnp.stack(v_p), jnp.stack(k_s), jnp.stack(v_s))
```

```python
import functools
import math

import numpy as np
import jax
import jax.numpy as jnp
from jax import lax
from jax.experimental import pallas as pl
from jax.experimental.pallas import tpu as pltpu
from jax.experimental.pallas import tpu_sc as plsc

CHUNK = 64
PAST_LEN = 2048
RET_HEADS = 4
ROPE_BASE = 10000.0
ATT_HEADS = 16
BAND_CHUNKS = 8
REL_CLIP = 256
N_GROUPS = 4
EXP_PER_GROUP = 8
N_EXPERTS = N_GROUPS * EXP_PER_GROUP
LN_EPS = 1e-5
NEG_INF = -1e30

LANES = 128
SUBLANES = 8
BF16_SUBLANES = 16
SC_WORKERS = 32
SC_LANES = 16
SC_ROWS_PER_CHUNK = 32
VMEM_LIMIT = 58 * 1024 * 1024

RET_BLOCK = 256
RET_CHAIN = 2
ATT_BLOCK = 4 * CHUNK
ATT_CHAIN = 2
TOKEN_TILE = 512
GEMM_TILE = 512
MOE_PARTS = 4

F32 = jnp.float32
BF16 = jnp.bfloat16
U32 = jnp.uint32
HI_MASK = 0xFFFF0000
LOG2E = math.log2(math.e)


def _dot(a, b):
    return jnp.dot(a, b, preferred_element_type=F32)


def _dot_nt(a, b):
    return lax.dot_general(a, b, (((1,), (1,)), ((), ())), preferred_element_type=F32)


def _dot_tn(a, b):
    return lax.dot_general(a, b, (((0,), (0,)), ((), ())), preferred_element_type=F32)


def _layer_norm(x, g, b):
    mu = jnp.mean(x, axis=-1, keepdims=True)
    xc = x - mu
    var = jnp.mean(xc * xc, axis=-1, keepdims=True)
    return xc * lax.rsqrt(var + LN_EPS) * g + b


def _pack_rows(x):
    half = x.shape[1] // 2
    lo = lax.bitcast_convert_type(x[:, :half].astype(BF16).astype(F32), U32) >> 16
    hi = lax.bitcast_convert_type(x[:, half:].astype(BF16).astype(F32), U32) & U32(HI_MASK)
    return lo | hi


def _unpack_rows(p):
    lo = lax.bitcast_convert_type(p << 16, F32)
    hi = lax.bitcast_convert_type(p & U32(HI_MASK), F32)
    return jnp.concatenate([lo, hi], axis=1)


def _router_logits(h, wr_both_ref, wr_hi_ref, br_ref):
    h_hi = h.astype(BF16)
    h_lo = (h - h_hi.astype(F32)).astype(BF16)
    both = _dot(h_hi, wr_both_ref[...])
    return both[:, :LANES] + both[:, LANES:] + _dot(h_lo, wr_hi_ref[...]) + br_ref[...]


def _finish_tokens(x, mix, alpha, lng_ref, lnb_ref, wrb_ref, wrh_ref, br_ref, h_ref, hp_ref, lgt_ref):
    hh = _layer_norm(alpha * x + mix, lng_ref[...], lnb_ref[...])
    h_ref[...] = hh
    hp_ref[...] = _pack_rows(hh)
    lgt_ref[...] = _router_logits(hh, wrb_ref, wrh_ref, br_ref)


def _zero_tokens(h_ref, hp_ref, lgt_ref):
    h_ref[...] = jnp.zeros(h_ref.shape, h_ref.dtype)
    hp_ref[...] = jnp.zeros(hp_ref.shape, hp_ref.dtype)
    lgt_ref[...] = jnp.zeros(lgt_ref.shape, lgt_ref.dtype)


def _const_spec(shape):
    nd = len(shape)
    return pl.BlockSpec(shape, lambda *_: (0,) * nd, pipeline_mode=pl.Buffered(1))


def _token_out_shapes(n_all, d_model):
    return (
        jax.ShapeDtypeStruct((n_all, d_model), F32),
        jax.ShapeDtypeStruct((n_all, d_model // 2), U32),
        jax.ShapeDtypeStruct((n_all, LANES), F32),
    )


def _token_out_specs(rows, d_model, row_map):
    return (
        pl.BlockSpec((rows, d_model), row_map),
        pl.BlockSpec((rows, d_model // 2), row_map),
        pl.BlockSpec((rows, LANES), row_map),
    )


def _alias_dst(args, in_specs, dst):
    if dst is None:
        return {}
    aliases = {}
    for k, arr in enumerate(dst):
        aliases[len(args)] = k
        args.append(arr)
        in_specs.append(pl.BlockSpec(memory_space=pl.ANY))
    return aliases


def _cast_plumbing(args, in_specs, cast, n_steps, work):
    if cast is None:
        return (), ()
    layer, weights = cast
    shapes, specs = [], []
    for w in weights:
        depth, n_exp, rows, cols = w.shape
        assert (n_exp * rows) % n_steps == 0
        slab = (n_exp * rows) // n_steps
        args.append(w.reshape(depth, n_exp * rows, cols))
        in_specs.append(pl.BlockSpec((None, slab, cols), lambda t: (layer, work(t), 0)))
        shapes.append(jax.ShapeDtypeStruct((n_exp * rows, cols), BF16))
        specs.append(pl.BlockSpec((slab, cols), lambda t: (work(t), 0)))
    return tuple(shapes), tuple(specs)


def _cast_slabs(in_refs, out_refs):
    for src, dst in zip(in_refs, out_refs):
        dst[...] = src[...].astype(BF16)


def _ret_log_gamma():
    h = np.arange(RET_HEADS, dtype=np.float32)
    return np.log(np.float32(1.0) - np.float32(2.0) ** (np.float32(-5.0) - h)).astype(np.float32)


def _ret_kernel(*refs, nb, blk, chain, nblk, n_steps, n_fill, has_state, n_alias, n_cast, alpha):
    step = pl.program_id(0)
    n_in = 12 + int(has_state) + n_alias + n_cast
    h_ref, hp_ref, lgt_ref = refs[n_in:n_in + 3]
    core = refs[:12 + int(has_state)] + refs[n_in:n_in + 4] + refs[-1:]

    @pl.when(step < n_steps)
    def _():
        _cast_slabs(refs[n_in - n_cast:n_in], refs[n_in + 4:n_in + 4 + n_cast])
        _ret_step(*core, first_block=step % nblk == 0, nb=nb, blk=blk, chain=chain, has_state=has_state,
                  alpha=alpha)

    if n_fill:
        @pl.when(step >= n_steps)
        def _():
            _zero_tokens(h_ref, hp_ref, lgt_ref)


def _ret_step(*refs, first_block, nb, blk, chain, has_state, alpha):
    (x_ref, cos_ref, sin_ref, dmask_ref, win_ref, wo_ref, gn_ref, lng_ref, lnb_ref,
     wrb_ref, wrh_ref, br_ref) = refs[:12]
    s_in_ref = refs[12] if has_state else None
    h_ref, hp_ref, lgt_ref, s_out_ref, gated_ref = refs[12 + int(has_state):]
    heads = RET_HEADS
    d_model = x_ref.shape[1]
    dk = d_model // heads
    dv = 2 * d_model // heads
    hk, hv = heads * dk, heads * dv
    half = dk // 2
    lg = _ret_log_gamma()

    n_sub = nb * chain
    proj_rows = blk if chain > 1 else n_sub * blk
    rowf = lax.broadcasted_iota(jnp.int32, (blk, 1), 0).astype(F32)

    def proj(group):
        xb = x_ref[group * proj_rows:(group + 1) * proj_rows, :].astype(BF16)
        return (_dot(xb, win_ref[:, 0:hk]), _dot(xb, win_ref[:, hk:2 * hk]),
                _dot(xb, win_ref[:, 2 * hk:2 * hk + hv]), _dot(xb, win_ref[:, 2 * hk + hv:2 * hk + 2 * hv]))

    if not has_state:
        @pl.when(first_block)
        def _():
            s_out_ref[...] = jnp.zeros(s_out_ref.shape, F32)

    s_prev_ref = s_in_ref if has_state else s_out_ref

    def head(j, h, projected):
        q_all, k_all, v_all, g_all = projected
        s = j // chain
        c0 = (j % chain) * blk
        r0 = (j * blk) % proj_rows
        cos = cos_ref[c0:c0 + blk, :]
        sin = sin_ref[c0:c0 + blk, :]

        def rot(t):
            t1, t2 = t[:, :half], t[:, half:]
            return jnp.concatenate([t1 * cos - t2 * sin, t1 * sin + t2 * cos], axis=1)

        lgh = float(lg[h])
        q = rot(q_all[r0:r0 + blk, h * dk:(h + 1) * dk])
        k = rot(k_all[r0:r0 + blk, h * dk:(h + 1) * dk]) * (dk ** -0.5)
        v = v_all[r0:r0 + blk, h * dv:(h + 1) * dv]
        g = g_all[r0:r0 + blk, h * dv:(h + 1) * dv]
        vb = v.astype(BF16)
        scores = _dot_nt(q.astype(BF16), k.astype(BF16)) * dmask_ref[h]
        inner = _dot(scores.astype(BF16), vb)
        s_prev = s_prev_ref[s, h]
        q_dec = q * jnp.exp(lgh * (rowf + 1.0))
        cross = _dot(q_dec.astype(BF16), s_prev.astype(BF16))
        k_dec = k * jnp.exp(lgh * (float(blk - 1) - rowf))
        s_out_ref[s, h] = math.exp(lgh * blk) * s_prev + _dot_tn(k_dec.astype(BF16), vb)
        o = inner + cross
        mu = jnp.mean(o, axis=-1, keepdims=True)
        oc = o - mu
        var = jnp.mean(oc * oc, axis=-1, keepdims=True)
        on = oc * lax.rsqrt(var + LN_EPS) * gn_ref[:, h * dv:(h + 1) * dv]
        gated_ref[j * blk:(j + 1) * blk, h * dv:(h + 1) * dv] = (jax.nn.silu(g) * on).astype(BF16)

    def tail(group):
        rows = slice(group * proj_rows, (group + 1) * proj_rows)
        mix = _dot(gated_ref[rows, :], wo_ref[...])
        _finish_tokens(x_ref[rows, :], mix, alpha, lng_ref, lnb_ref, wrb_ref, wrh_ref, br_ref,
                       h_ref.at[rows, :], hp_ref.at[rows, :], lgt_ref.at[rows, :])

    n_groups = (n_sub * blk) // proj_rows
    subs_per_group = proj_rows // blk
    projected = proj(0)
    for group in range(n_groups):
        nxt = None
        for jj in range(subs_per_group):
            for h in range(heads):
                head(group * subs_per_group + jj, h, projected)
                if jj == 0 and h == 0 and group + 1 < n_groups:
                    nxt = proj(group + 1)
                if jj == 0 and h == 1 and group > 0:
                    tail(group - 1)
        projected = nxt
    tail(n_groups - 1)


def _ret_mixer(x2d, in_row0, n_seq, seq_len, pos0, state_in, w_in, w_o, gn_g, ln_g, ln_b,
               wr_both, wr_hi, br, alpha, nb, blk, chain, n_all, out_row0, dst, cast=None):
    d_model = x2d.shape[1]
    heads = RET_HEADS
    dk, dv = d_model // heads, 2 * d_model // heads
    half = dk // 2
    nblk = seq_len // (blk * chain)
    has_state = state_in is not None
    rows = nb * blk * chain
    assert seq_len % (blk * chain) == 0 and n_seq % nb == 0
    assert (not has_state) or nblk == 1
    assert nb == 1 or (nblk == 1 and chain == 1)
    assert in_row0 % rows == 0 and out_row0 % rows == 0
    in_b0, out_b0 = in_row0 // rows, out_row0 // rows

    pos = (pos0 + np.arange(seq_len, dtype=np.int32)).astype(np.float32)
    inv_freq = np.float32(ROPE_BASE) ** (-np.arange(half, dtype=np.float32) / np.float32(half))
    ang = (pos[:, None] * inv_freq[None, :]).astype(np.float64)
    cos, sin = jnp.asarray(np.cos(ang), F32), jnp.asarray(np.sin(ang), F32)
    lg = jnp.asarray(_ret_log_gamma())
    ii = jnp.arange(blk, dtype=F32)
    diff = ii[:, None] - ii[None, :]
    dmask = jnp.where(diff >= 0, jnp.exp(lg[:, None, None] * jnp.maximum(diff, 0.0)), 0.0)

    n_steps = (n_seq // nb) * nblk
    n_fill = 0 if dst is not None else (n_all - n_seq * seq_len) // rows
    assert dst is not None or (out_row0 == 0 and n_fill * rows == n_all - n_seq * seq_len)

    def work(t):
        return jnp.minimum(t, n_steps - 1)

    in_specs = [
        pl.BlockSpec((rows, d_model), lambda t: (in_b0 + work(t), 0)),
        pl.BlockSpec((blk * chain, half), lambda t: (work(t) % nblk, 0)),
        pl.BlockSpec((blk * chain, half), lambda t: (work(t) % nblk, 0)),
        _const_spec(dmask.shape),
        _const_spec(w_in.shape),
        _const_spec(w_o.shape),
        _const_spec(gn_g.shape),
        _const_spec(ln_g.shape),
        _const_spec(ln_b.shape),
        _const_spec(wr_both.shape),
        _const_spec(wr_hi.shape),
        _const_spec(br.shape),
    ]
    args = [x2d, cos, sin, dmask, w_in, w_o, gn_g, ln_g, ln_b, wr_both, wr_hi, br]
    state_spec = pl.BlockSpec((nb, heads, dk, dv), lambda t: (work(t) // nblk, 0, 0, 0))
    if has_state:
        in_specs.append(state_spec)
        args.append(state_in)
    aliases = _alias_dst(args, in_specs, dst)
    cast_shapes, cast_specs = _cast_plumbing(args, in_specs, cast, n_steps, work)
    out_shape = (_token_out_shapes(n_all, d_model) + (jax.ShapeDtypeStruct((n_seq, heads, dk, dv), F32),)
                 + cast_shapes)
    out_specs = _token_out_specs(rows, d_model, lambda t: (out_b0 + t, 0)) + (state_spec,) + cast_specs
    return pl.pallas_call(
        functools.partial(_ret_kernel, nb=nb, blk=blk, chain=chain, nblk=nblk, n_steps=n_steps, n_fill=n_fill,
                          has_state=has_state, n_alias=len(aliases), n_cast=len(cast_shapes), alpha=alpha),
        grid=(n_steps + n_fill,),
        in_specs=in_specs,
        out_specs=out_specs,
        out_shape=out_shape,
        input_output_aliases=aliases,
        scratch_shapes=[pltpu.VMEM((rows, heads * dv), BF16)],
        compiler_params=pltpu.CompilerParams(
            dimension_semantics=("arbitrary",), vmem_limit_bytes=VMEM_LIMIT),
        name="ret_mixer_state" if has_state else "ret_mixer",
    )(*args)


_RING = 3


def _att_prompt_kernel(*refs, blk, chain, nblk, n_steps, n_fill, n_cast, alpha):
    step = pl.program_id(0)
    n_in = 9 + n_cast
    h_ref, hp_ref, lgt_ref = refs[n_in:n_in + 3]
    core = refs[:9] + refs[n_in:n_in + 5] + refs[-3:]

    @pl.when(step < n_steps)
    def _():
        _cast_slabs(refs[9:n_in], refs[n_in + 5:n_in + 5 + n_cast])
        _att_prompt_step(*core, i0=(step % nblk) * chain, blk=blk, chain=chain, alpha=alpha)

    if n_fill:
        @pl.when(step >= n_steps)
        def _():
            _zero_tokens(h_ref, hp_ref, lgt_ref)


def _att_prompt_step(x_ref, wqkv_ref, wo_ref, bias_ref, lng_ref, lnb_ref, wrb_ref, wrh_ref, br_ref,
                     h_ref, hp_ref, lgt_ref, krow_ref, vrow_ref, kring, vring, o_scr, *, i0, blk, chain, alpha):
    d_model = x_ref.shape[1]
    dh = d_model // ATT_HEADS
    heads_per_group = LANES // dh
    lane = lax.broadcasted_iota(jnp.int32, (1, LANES), 1)

    @pl.when(i0 == 0)
    def _():
        kring[...] = jnp.zeros(kring.shape, BF16)
        vring[...] = jnp.zeros(vring.shape, BF16)
        vring[:, :, dh:, :] = jnp.ones((_RING, ATT_HEADS, vring.shape[2] - dh, blk), BF16)

    def project(j):
        xb = x_ref[j * blk:(j + 1) * blk, :].astype(BF16)
        q = _dot(xb, wqkv_ref[:, 0:d_model]) * (dh ** -0.5 * LOG2E)
        k = _dot(xb, wqkv_ref[:, d_model:2 * d_model])
        v = _dot(xb, wqkv_ref[:, 2 * d_model:3 * d_model])
        return q, k, v

    def to_ring(j, k, v):
        slot = (i0 + j) % _RING
        krow_ref[j * blk:(j + 1) * blk, :] = k
        vrow_ref[j * blk:(j + 1) * blk, :] = v
        kring[slot] = k.astype(BF16)
        v_t = jnp.transpose(v).astype(BF16)
        for hd in range(ATT_HEADS):
            vring[slot, hd, 0:dh, :] = v_t[hd * dh:(hd + 1) * dh, :]

    def attend(j, q, between):
        i = i0 + j
        qb = q.astype(BF16)
        behind = [(i + _RING - k) % _RING for k in range(_RING)]
        slabs = [jnp.where(i >= behind[k], behind[k], _RING) for k in range(_RING)]

        def scores(hd):
            c0 = (hd // heads_per_group) * LANES
            sub = hd % heads_per_group
            in_head = (lane >= sub * dh) & (lane < (sub + 1) * dh)
            q_pair = qb[:, c0:c0 + LANES]
            qm = jnp.where(in_head, q_pair, jnp.zeros_like(q_pair))
            s_all = _dot_nt(kring[:, :, c0:c0 + LANES].reshape(_RING * blk, LANES), qm)
            return [s_all[k * blk:(k + 1) * blk] + bias_ref[slabs[k], hd] for k in range(_RING)]

        def probs(s_list):
            m = s_list[0].max(axis=0, keepdims=True)
            for k in range(1, _RING):
                m = jnp.maximum(m, s_list[k].max(axis=0, keepdims=True))
            return [jnp.exp2(sc - m).astype(BF16) for sc in s_list]

        def values(hd, p_list):
            o = _dot(vring[0, hd], p_list[0])
            for k in range(1, _RING):
                o = o + _dot(vring[k, hd], p_list[k])
            o_scr[j, hd * dh:(hd + 1) * dh, :] = (o[0:dh] * (1.0 / o[dh:dh + 1])).astype(BF16)

        s_next = scores(0)
        pending = None
        for hd in range(ATT_HEADS):
            s_cur = s_next
            if hd + 1 < ATT_HEADS:
                s_next = scores(hd + 1)
            p_list = probs(s_cur)
            if pending is not None:
                values(*pending)
            pending = (hd, p_list)
            if hd in between:
                between[hd]()
        values(*pending)

    def finish(j):
        rows = slice(j * blk, (j + 1) * blk)
        mix = _dot_tn(o_scr[j], wo_ref[...])
        _finish_tokens(x_ref[rows, :], mix, alpha, lng_ref, lnb_ref, wrb_ref, wrh_ref, br_ref,
                       h_ref.at[rows, :], hp_ref.at[rows, :], lgt_ref.at[rows, :])

    qkv = project(0)
    for j in range(chain):
        to_ring(j, qkv[1], qkv[2])
        nxt = []
        between = {}
        if j + 1 < chain:
            between[0] = lambda j=j: nxt.append(project(j + 1))
        attend(j, qkv[0], between)
        finish(j)
        qkv = nxt[0] if nxt else None


def _rel_bias(rel_table, n_rows, n_cols, offset, sign):
    heads = rel_table.shape[0]
    period = n_rows + n_cols
    m = jnp.arange(period)
    c_minus_r = jnp.where(m < n_cols, m, m - period)
    w = rel_table[:, jnp.clip(offset - sign * c_minus_r, -REL_CLIP, REL_CLIP) + REL_CLIP]
    flat = jnp.broadcast_to(w[:, None, :], (heads, n_rows, period)).reshape(heads, n_rows * period)
    return flat[:, :n_rows * (period - 1)].reshape(heads, n_rows, period - 1)[:, :, :n_cols]


def _att_prompt_bias(rel_table, blk):
    i = jnp.arange(blk)
    out = []
    wide = _rel_bias((rel_table * LOG2E).astype(BF16), blk, _RING * blk, 0, -1)
    masked = jnp.asarray(NEG_INF * LOG2E, BF16)
    for d in range(_RING):
        b = wide[:, :, d * blk:(d + 1) * blk]
        cd = (i[None, :] // CHUNK) - (i[:, None] // CHUNK) + d * (blk // CHUNK)
        ok = (cd >= 0) & (cd <= BAND_CHUNKS)
        out.append(jnp.where(ok[None], b, masked))
    out.append(jnp.full_like(out[0], masked))
    return jnp.stack(out)


def _att_prompt(x2d, n_seq, seq_len, w_qkv, w_o, rel_table, ln_g, ln_b, wr_both, wr_hi, br, alpha, n_all,
                cast=None):
    d_model = x2d.shape[1]
    blk = ATT_BLOCK
    chain = ATT_CHAIN
    rows = blk * chain
    nblk = seq_len // rows
    keep = min(BAND_CHUNKS * CHUNK, seq_len)
    assert seq_len % rows == 0 and keep % rows == 0
    assert (_RING - 1) * blk >= BAND_CHUNKS * CHUNK
    kb = keep // rows
    bias = _att_prompt_bias(rel_table, blk)
    n_steps = n_seq * nblk
    n_fill = (n_all - n_seq * seq_len) // rows
    assert n_fill * rows == n_all - n_seq * seq_len

    def work(t):
        return jnp.minimum(t, n_steps - 1)

    row_spec = pl.BlockSpec(
        (None, rows, d_model), lambda t: (work(t) // nblk, jnp.maximum(work(t) % nblk - (nblk - kb), 0), 0),
        pipeline_mode=pl.Buffered(1))
    rows_out = jax.ShapeDtypeStruct((n_seq, keep, d_model), F32)
    in_specs = [
        pl.BlockSpec((rows, d_model), lambda t: (work(t), 0)),
        _const_spec(w_qkv.shape),
        _const_spec(w_o.shape),
        _const_spec(bias.shape),
        _const_spec(ln_g.shape),
        _const_spec(ln_b.shape),
        _const_spec(wr_both.shape),
        _const_spec(wr_hi.shape),
        _const_spec(br.shape),
    ]
    args = [x2d, w_qkv, w_o, bias, ln_g, ln_b, wr_both, wr_hi, br]
    cast_shapes, cast_specs = _cast_plumbing(args, in_specs, cast, n_steps, work)
    return pl.pallas_call(
        functools.partial(_att_prompt_kernel, blk=blk, chain=chain, nblk=nblk, n_steps=n_steps, n_fill=n_fill,
                          n_cast=len(cast_shapes), alpha=alpha),
        grid=(n_steps + n_fill,),
        in_specs=in_specs,
        out_specs=_token_out_specs(rows, d_model, lambda t: (t, 0)) + (row_spec, row_spec) + cast_specs,
        out_shape=_token_out_shapes(n_all, d_model) + (rows_out, rows_out) + cast_shapes,
        scratch_shapes=[
            pltpu.VMEM((_RING, blk, d_model), BF16),
            pltpu.VMEM((_RING, ATT_HEADS, d_model // ATT_HEADS + BF16_SUBLANES, blk), BF16),
            pltpu.VMEM((chain, d_model, blk), BF16),
        ],
        compiler_params=pltpu.CompilerParams(
            dimension_semantics=("arbitrary",), vmem_limit_bytes=VMEM_LIMIT),
        name="att_mixer",
    )(*args)


def _att_sample_kernel(*refs, nb, blk, alpha):
    (x_ref, kc_ref, vc_ref, wqkv_ref, wo_ref, bias_c_ref, bias_n_ref, lng_ref, lnb_ref,
     wrb_ref, wrh_ref, br_ref) = refs[:12]
    h_ref, hp_ref, lgt_ref, krow_ref, vrow_ref, o_scr = refs[-6:]
    d_model = x_ref.shape[1]
    heads = ATT_HEADS
    dh = d_model // heads
    x = x_ref[...]
    xb = x.astype(BF16)
    q = _dot(xb, wqkv_ref[:, 0:d_model]) * (dh ** -0.5)
    k = _dot(xb, wqkv_ref[:, d_model:2 * d_model])
    v = _dot(xb, wqkv_ref[:, 2 * d_model:3 * d_model])
    krow_ref[...] = k.reshape(nb, blk, d_model)
    vrow_ref[...] = v.reshape(nb, blk, d_model)
    lane_head = lax.broadcasted_iota(jnp.int32, (heads, 1, d_model), 2) // dh
    head_id = lax.broadcasted_iota(jnp.int32, (heads, 1, d_model), 0)
    head_mask = (lane_head == head_id).astype(F32)

    for s in range(nb):
        r0 = s * blk
        qs = q[r0:r0 + blk]
        q_bd = (qs[None, :, :] * head_mask).reshape(heads * blk, d_model).astype(BF16)
        kn = k[r0:r0 + blk].astype(BF16)
        vn = v[r0:r0 + blk].astype(BF16)
        s_c = _dot(q_bd, kc_ref[s].astype(BF16)) + bias_c_ref[...]
        s_n = _dot_nt(q_bd, kn) + bias_n_ref[...]
        m = jnp.maximum(s_c.max(axis=-1, keepdims=True), s_n.max(axis=-1, keepdims=True))
        p_c = jnp.exp(s_c - m)
        p_n = jnp.exp(s_n - m)
        l = p_c.sum(axis=-1, keepdims=True) + p_n.sum(axis=-1, keepdims=True)
        o_full = _dot_nt(p_c.astype(BF16), vc_ref[s].astype(BF16)) + _dot(p_n.astype(BF16), vn)
        o_full = o_full * (1.0 / l)
        o = (o_full.reshape(heads, blk, d_model) * head_mask).sum(axis=0)
        o_scr[r0:r0 + blk, :] = o.astype(BF16)

    mix = _dot(o_scr[...], wo_ref[...])
    _finish_tokens(x, mix, alpha, lng_ref, lnb_ref, wrb_ref, wrh_ref, br_ref, h_ref, hp_ref, lgt_ref)


def _att_sample(x2d, in_row0, n_seq, seq_len, k_cache, v_cache, w_qkv, w_o, rel_table, ln_g, ln_b,
                wr_both, wr_hi, br, alpha, nb, n_all, out_row0, dst):
    d_model = x2d.shape[1]
    heads = ATT_HEADS
    blk = seq_len
    n_cache = k_cache.shape[1]
    rows = nb * blk
    assert n_seq % nb == 0 and in_row0 % rows == 0 and out_row0 % rows == 0
    in_b0, out_b0 = in_row0 // rows, out_row0 // rows
    bias_c = _rel_bias(rel_table, blk, n_cache, n_cache, 1).reshape(heads * blk, n_cache)
    bias_n = _rel_bias(rel_table, blk, blk, 0, 1).reshape(heads * blk, blk)
    kc = jnp.transpose(k_cache, (0, 2, 3, 1)).reshape(n_seq, d_model, n_cache)
    vc = jnp.transpose(v_cache, (0, 2, 3, 1)).reshape(n_seq, d_model, n_cache)
    cache_spec = pl.BlockSpec((nb, d_model, n_cache), lambda g: (g, 0, 0))
    row_spec = pl.BlockSpec((nb, blk, d_model), lambda g: (g, 0, 0))
    rows_out = jax.ShapeDtypeStruct((n_seq, blk, d_model), F32)
    in_specs = [
        pl.BlockSpec((rows, d_model), lambda g: (in_b0 + g, 0)),
        cache_spec,
        cache_spec,
        _const_spec(w_qkv.shape),
        _const_spec(w_o.shape),
        _const_spec(bias_c.shape),
        _const_spec(bias_n.shape),
        _const_spec(ln_g.shape),
        _const_spec(ln_b.shape),
        _const_spec(wr_both.shape),
        _const_spec(wr_hi.shape),
        _const_spec(br.shape),
    ]
    args = [x2d, kc, vc, w_qkv, w_o, bias_c, bias_n, ln_g, ln_b, wr_both, wr_hi, br]
    aliases = _alias_dst(args, in_specs, dst)
    return pl.pallas_call(
        functools.partial(_att_sample_kernel, nb=nb, blk=blk, alpha=alpha),
        grid=(n_seq // nb,),
        in_specs=in_specs,
        out_specs=_token_out_specs(rows, d_model, lambda g: (out_b0 + g, 0)) + (row_spec, row_spec),
        out_shape=_token_out_shapes(n_all, d_model) + (rows_out, rows_out),
        input_output_aliases=aliases,
        scratch_shapes=[pltpu.VMEM((rows, d_model), BF16)],
        compiler_params=pltpu.CompilerParams(
            dimension_semantics=("arbitrary",), vmem_limit_bytes=VMEM_LIMIT),
        name="att_mixer_cache",
    )(*args)


ROUTE_ROWS = 40
ROUTE_TILES_PER_STEP = 5


def _route_kernel(lgt_ref, meta_t_ref, meta_w_ref, count_ref, carry_ref, tri_ref):
    j = pl.program_id(0)
    tm = tri_ref.shape[0]

    @pl.when(j == 0)
    def _():
        carry_ref[...] = jnp.zeros(carry_ref.shape, F32)
        r = lax.broadcasted_iota(jnp.int32, (tm, tm), 0)
        c = lax.broadcasted_iota(jnp.int32, (tm, tm), 1)
        tri_ref[...] = (r < c).astype(BF16)

    for sub in range(lgt_ref.shape[0] // tm):
        _route_tile(lgt_ref, meta_t_ref, meta_w_ref, carry_ref, tri_ref, slice(sub * tm, (sub + 1) * tm))
    count_ref[...] = carry_ref[...].astype(jnp.int32)


def _route_tile(lgt_ref, meta_t_ref, meta_w_ref, carry_ref, tri_ref, tok):
    tm = tri_ref.shape[0]
    a = jnp.transpose(lgt_ref[tok, :])[0:ROUTE_ROWS, :]
    row = lax.broadcasted_iota(jnp.int32, (ROUTE_ROWS, tm), 0)
    big = jnp.int32(LANES)
    glog = jnp.where(row < N_GROUPS, a, NEG_INF)
    gmax = glog.max(axis=0, keepdims=True)
    gsel = jnp.where(glog == gmax, row, big).min(axis=0, keepdims=True)
    gp = 1.0 / jnp.exp(glog - gmax).sum(axis=0, keepdims=True)
    first = N_GROUPS + gsel * EXP_PER_GROUP
    in_grp = (row >= first) & (row < first + EXP_PER_GROUP)
    elog = jnp.where(in_grp, a, NEG_INF)
    v1 = elog.max(axis=0, keepdims=True)
    l1 = jnp.where(elog == v1, row, big).min(axis=0, keepdims=True)
    elog2 = jnp.where(row == l1, NEG_INF, elog)
    v2 = elog2.max(axis=0, keepdims=True)
    l2 = jnp.where(elog2 == v2, row, big).min(axis=0, keepdims=True)
    e2 = jnp.exp(v2 - v1)
    w1 = gp / (1.0 + e2)
    w2 = gp * e2 / (1.0 + e2)

    is1 = row == l1
    is2 = row == l2
    oh = (is1 | is2).astype(F32)
    before = _dot(oh.astype(BF16), tri_ref[...]) + carry_ref[:, 0:1]
    rank1 = jnp.where(is1, before, 0.0).sum(axis=0, keepdims=True)
    rank2 = jnp.where(is2, before, 0.0).sum(axis=0, keepdims=True)
    carry_ref[...] = carry_ref[...] + oh.sum(axis=1, keepdims=True)

    pad = jnp.zeros((SUBLANES - 4, tm), jnp.int32)
    meta_t_ref[:, tok] = jnp.concatenate(
        [l1 - N_GROUPS, l2 - N_GROUPS, rank1.astype(jnp.int32), rank2.astype(jnp.int32), pad], axis=0)
    wt = jnp.concatenate([w1, w2, jnp.zeros((LANES - 2, tm), F32)], axis=0)
    meta_w_ref[tok, :] = jnp.transpose(wt)


def _route(logits):
    n_tok = logits.shape[0]
    tm = TOKEN_TILE
    assert n_tok % tm == 0
    per_step = max(d for d in range(1, ROUTE_TILES_PER_STEP + 1) if (n_tok // tm) % d == 0)
    rows = tm * per_step
    tile = pl.BlockSpec((rows, LANES), lambda j: (j, 0))
    return pl.pallas_call(
        _route_kernel,
        grid=(n_tok // rows,),
        in_specs=[tile],
        out_specs=(pl.BlockSpec((SUBLANES, rows), lambda j: (0, j)), tile,
                   pl.BlockSpec((ROUTE_ROWS, LANES), lambda j: (0, 0))),
        out_shape=(
            jax.ShapeDtypeStruct((SUBLANES, n_tok), jnp.int32),
            jax.ShapeDtypeStruct((n_tok, LANES), F32),
            jax.ShapeDtypeStruct((ROUTE_ROWS, LANES), jnp.int32),
        ),
        scratch_shapes=[pltpu.VMEM((ROUTE_ROWS, LANES), F32), pltpu.VMEM((tm, tm), BF16)],
        compiler_params=pltpu.CompilerParams(dimension_semantics=("arbitrary",)),
        name="moe_route",
    )(logits)


def _sc_gather_loop(table_hbm, idx_v, out_hbm, rows_v, sems, base, n_chunk, r):
    def gather(c, slot):
        off = pl.multiple_of(c * r, r)
        return pltpu.make_async_copy(table_hbm.at[idx_v.at[pl.ds(off, r)]], rows_v.at[slot], sems.at[slot])

    def finish(c, slot):
        gather(c, slot).wait()
        pltpu.sync_copy(rows_v.at[slot], out_hbm.at[pl.ds(base + pl.multiple_of(c * r, r), r)])

    gather(0, 0).start()
    if n_chunk > 1:
        gather(1, 1).start()

    def body(pair, carry):
        c = 2 * pair
        for slot in range(2):
            finish(c + slot, slot)

            @pl.when(c + slot + 2 < n_chunk)
            def _():
                gather(c + slot + 2, slot).start()
        return carry

    lax.fori_loop(0, n_chunk // 2, body, 0)
    if n_chunk % 2:
        finish(n_chunk - 1, 0)


def _gather_rows(table, idx):
    m = idx.shape[0]
    width = table.shape[1]
    r = SC_ROWS_PER_CHUNK
    assert m % (SC_WORKERS * r) == 0
    per_w = m // SC_WORKERS
    mesh = plsc.VectorSubcoreMesh(core_axis_name="c", subcore_axis_name="s")

    @functools.partial(
        pl.kernel,
        mesh=mesh,
        out_type=jax.ShapeDtypeStruct((m, width), table.dtype),
        scratch_types=[
            pltpu.VMEM((per_w,), jnp.int32),
            pltpu.VMEM((2, r, width), table.dtype),
            pltpu.SemaphoreType.DMA((2,)),
        ],
    )
    def gather(table_hbm, idx_hbm, out_hbm, idx_v, rows_v, sem):
        wid = lax.axis_index("s") * 2 + lax.axis_index("c")
        base = wid * per_w
        pltpu.sync_copy(idx_hbm.at[pl.ds(base, per_w)], idx_v)
        _sc_gather_loop(table_hbm, idx_v, out_hbm, rows_v, sem, base, per_w // r, r)

    return gather(table, idx)


def _dispatch_rows(table, pos, row0, n_rows):
    n_tok, width = table.shape
    n_pairs = pos.shape[0]
    r = SC_ROWS_PER_CHUNK
    lanes = SC_LANES
    assert n_rows % (SC_WORKERS * r) == 0
    per_w = n_rows // SC_WORKERS
    n_stage = 16
    stage = n_pairs // n_stage
    assert stage * n_stage == n_pairs and stage % lanes == 0 and per_w % lanes == 0
    assert row0 + n_rows < 3 * n_tok
    mesh = plsc.VectorSubcoreMesh(core_axis_name="c", subcore_axis_name="s")

    @functools.partial(
        pl.kernel,
        mesh=mesh,
        out_type=jax.ShapeDtypeStruct((n_rows, width), table.dtype),
        scratch_types=[
            pltpu.VMEM((per_w,), jnp.int32),
            pltpu.VMEM((stage,), jnp.int32),
            pltpu.VMEM((2, r, width), table.dtype),
            pltpu.SemaphoreType.DMA((2,)),
        ],
        compiler_params=pltpu.CompilerParams(needs_layout_passes=False),
    )
    def dispatch(table_hbm, pos_hbm, out_hbm, src_v, pos_v, rows_v, sem):
        wid = lax.axis_index("s") * 2 + lax.axis_index("c")
        out_base = wid * per_w
        base = row0 + out_base
        lane = lax.iota(jnp.int32, lanes)

        def wrap(t):
            t = jnp.where(t >= n_tok, t - n_tok, t)
            return jnp.where(t >= n_tok, t - n_tok, t)

        def init(i, carry):
            off = pl.multiple_of(i * lanes, lanes)
            src_v[pl.ds(off, lanes)] = wrap(base + off + lane)
            return carry

        lax.fori_loop(0, per_w // lanes, init, 0)

        def scan_stage(sidx, carry):
            pair0 = sidx * stage
            pltpu.sync_copy(pos_hbm.at[pl.ds(pl.multiple_of(pair0, 8), stage)], pos_v)

            @plsc.parallel_loop(0, stage // lanes, unroll=4)
            def _(i):
                off = pl.multiple_of(i * lanes, lanes)
                local = pos_v[pl.ds(off, lanes)] - base
                mine = (local >= 0) & (local < per_w)
                plsc.store_scatter(src_v, [jnp.where(mine, local, 0)], wrap(pair0 + off + lane), mask=mine)

            return carry

        lax.fori_loop(0, n_stage, scan_stage, 0)
        _sc_gather_loop(table_hbm, src_v, out_hbm, rows_v, sem, out_base, per_w // r, r)

    return dispatch(table, pos)


def _gemm_kernel(tile_expert_ref, n_used_ref, xs_ref, wg_ref, wu_ref, wd_ref, *rest, tile0):
    ys_ref = rest[-1]
    tile = tile0 + pl.program_id(0)

    @pl.when(tile < n_used_ref[0])
    def _():
        x = _unpack_rows(xs_ref[...]).astype(BF16)
        g = _dot(x, wg_ref[...])
        u = _dot(x, wu_ref[...])
        hmid = (jax.nn.silu(g) * u).astype(BF16)
        ys_ref[...] = _pack_rows(_dot(hmid, wd_ref[...]))


def _expert_gemm(xs, tile_expert, n_used, w_gate, w_up, w_down, tile0, n_rows_all, ys):
    n_rows, half = xs.shape
    d_model, d_exp = w_gate.shape[1], w_gate.shape[2]
    tm = GEMM_TILE
    n_tiles = n_rows // tm

    def last_used(j, nu):
        return jnp.minimum(tile0 + j, nu[0] - 1)

    def in_map(j, te, nu):
        return (jnp.maximum(last_used(j, nu) - tile0, 0), 0)

    def out_map(j, te, nu):
        return (jnp.maximum(last_used(j, nu), tile0), 0)

    def w_map(j, te, nu):
        return (te[tile0 + j], 0, 0)

    in_specs = [
        pl.BlockSpec((tm, half), in_map),
        pl.BlockSpec((None, d_model, d_exp), w_map),
        pl.BlockSpec((None, d_model, d_exp), w_map),
        pl.BlockSpec((None, d_exp, d_model), w_map),
    ]
    args = [tile_expert, n_used, xs, w_gate, w_up, w_down]
    aliases = {}
    if ys is not None:
        aliases[len(args)] = 0
        args.append(ys)
        in_specs.append(pl.BlockSpec(memory_space=pl.ANY))
    grid_spec = pltpu.PrefetchScalarGridSpec(
        num_scalar_prefetch=2,
        grid=(n_tiles,),
        in_specs=in_specs,
        out_specs=pl.BlockSpec((tm, half), out_map),
    )
    return pl.pallas_call(
        functools.partial(_gemm_kernel, tile0=tile0),
        grid_spec=grid_spec,
        out_shape=jax.ShapeDtypeStruct((n_rows_all, half), U32),
        input_output_aliases=aliases,
        compiler_params=pltpu.CompilerParams(
            dimension_semantics=("arbitrary",), vmem_limit_bytes=VMEM_LIMIT),
        name="moe_gemm",
    )(*args)


def _combine_kernel(y0_ref, y1_ref, mw_ref, h_ref, pp_ref, ps_ref, lng_ref, lnb_ref, wproj_ref, wgate_ref,
                    *out_refs, n_prompt_tiles, alpha):
    j = pl.program_id(0)
    tm = h_ref.shape[0]
    half = tm // 2

    def normed(rows):
        mw = mw_ref[rows, :]
        ffn = mw[:, 0:1] * _unpack_rows(y0_ref[rows, :]) + mw[:, 1:2] * _unpack_rows(y1_ref[rows, :])
        return _layer_norm(alpha * h_ref[rows, :] + ffn, lng_ref[...], lnb_ref[...])

    def gated(rows, h2):
        p = jnp.where(j < n_prompt_tiles, pp_ref[rows, :], ps_ref[rows, :]).astype(BF16)
        gate = jax.nn.sigmoid(_dot(h2.astype(BF16), wgate_ref[...]))
        return h2 + gate * _dot(p, wproj_ref[...])

    rows_a, rows_b = slice(0, half), slice(half, tm)
    h2_a = normed(rows_a)
    h2_b = normed(rows_b)
    out = jnp.concatenate([gated(rows_a, h2_a), gated(rows_b, h2_b)], axis=0)
    if len(out_refs) == 1:
        out_refs[0][...] = out
    else:
        @pl.when(j < n_prompt_tiles)
        def _():
            out_refs[0][...] = out

        @pl.when(j >= n_prompt_tiles)
        def _():
            out_refs[1][...] = out


def _combine(yg, meta_w, h, p_prompt, p_sample, layer, ln_g, ln_b, w_proj, w_gate, alpha, split_out):
    n_tok, d_model = h.shape
    tm = TOKEN_TILE
    n_t = n_tok // tm
    n_tp = p_prompt.shape[1] // tm
    n_ts = p_sample.shape[1] // tm
    assert n_tp * tm == p_prompt.shape[1] and n_ts * tm == p_sample.shape[1] and n_tp + n_ts == n_t
    ple = p_prompt.shape[2]
    tile = pl.BlockSpec((tm, d_model), lambda j: (j, 0))
    if split_out:
        out_specs = (pl.BlockSpec((tm, d_model), lambda j: (jnp.minimum(j, n_tp - 1), 0)),
                     pl.BlockSpec((tm, d_model), lambda j: (jnp.maximum(j - n_tp, 0), 0)))
        out_shape = (jax.ShapeDtypeStruct((n_tp * tm, d_model), F32),
                     jax.ShapeDtypeStruct((n_ts * tm, d_model), F32))
    else:
        out_specs = tile
        out_shape = jax.ShapeDtypeStruct((n_tok, d_model), F32)
    return pl.pallas_call(
        functools.partial(_combine_kernel, n_prompt_tiles=n_tp, alpha=alpha),
        grid=(n_t,),
        in_specs=[
            pl.BlockSpec((tm, d_model // 2), lambda j: (j, 0)),
            pl.BlockSpec((tm, d_model // 2), lambda j: (j + n_t, 0)),
            pl.BlockSpec((tm, LANES), lambda j: (j, 0)),
            tile,
            pl.BlockSpec((None, tm, ple), lambda j: (layer, jnp.minimum(j, n_tp - 1), 0)),
            pl.BlockSpec((None, tm, ple), lambda j: (layer, jnp.maximum(j - n_tp, 0), 0)),
            _const_spec(ln_g.shape),
            _const_spec(ln_b.shape),
            _const_spec(w_proj.shape),
            _const_spec(w_gate.shape),
        ],
        out_specs=out_specs,
        out_shape=out_shape,
        compiler_params=pltpu.CompilerParams(
            dimension_semantics=("arbitrary",), vmem_limit_bytes=VMEM_LIMIT),
        name="moe_combine",
    )(yg, yg, meta_w, h, p_prompt, p_sample, ln_g, ln_b, w_proj, w_gate)


def _router_weights(w_grp, b_grp, w_exp, b_exp):
    d_model = w_grp.shape[0]
    w = jnp.concatenate([w_grp, jnp.transpose(w_exp, (1, 0, 2)).reshape(d_model, N_EXPERTS)], axis=1)
    w = jnp.pad(w, ((0, 0), (0, LANES - w.shape[1])))
    b = jnp.concatenate([b_grp, b_exp.reshape(N_EXPERTS)])
    b = jnp.pad(b, (0, LANES - b.shape[0])).reshape(1, LANES)
    w_hi = w.astype(BF16)
    w_lo = (w - w_hi.astype(F32)).astype(BF16)
    return jnp.concatenate([w_hi, w_lo], axis=1), w_hi, b


def _moe(h_packed, logits, w_gate, w_up, w_down):
    n_tok = h_packed.shape[0]
    tm = GEMM_TILE
    meta_t, meta_w, counts = _route(logits)
    counts = counts[N_GROUPS:N_GROUPS + N_EXPERTS, 0]
    tiles_per_expert = (counts + tm - 1) // tm
    tile_end = jnp.cumsum(tiles_per_expert)
    row_start = (tile_end - tiles_per_expert) * tm
    eid = meta_t[0:2]
    rank = meta_t[2:4]
    experts = jnp.arange(N_EXPERTS, dtype=jnp.int32)
    start = jnp.sum(jnp.where(eid[:, :, None] == experts, row_start, 0), axis=-1)
    pos = (start + rank).reshape(-1).astype(jnp.int32)
    part_quant = MOE_PARTS * SC_WORKERS * SC_ROWS_PER_CHUNK * (tm // math.gcd(tm, SC_WORKERS * SC_ROWS_PER_CHUNK))
    n_tiles = -(-(2 * n_tok) // tm) + N_EXPERTS
    n_rows = -(-(n_tiles * tm) // part_quant) * part_quant
    n_tiles = n_rows // tm
    n_used = tile_end[-1:].astype(jnp.int32)
    tile_ids = jnp.minimum(jnp.arange(n_tiles, dtype=jnp.int32), n_used[0] - 1)
    tile_expert = jnp.sum(tile_end[None, :] <= tile_ids[:, None], axis=1).astype(jnp.int32)
    part_rows = n_rows // MOE_PARTS
    xs_parts = [_dispatch_rows(h_packed, pos, part * part_rows, part_rows) for part in range(MOE_PARTS)]
    ys = None
    for part, xs in enumerate(xs_parts):
        ys = _expert_gemm(xs, tile_expert, n_used, w_gate, w_up, w_down,
                          part * (part_rows // tm), n_rows, ys)
    return _gather_rows(ys, pos), meta_w


def kernel(x_prompt, x_sample, p_prompt, p_sample, state_ret, cache_att_k, cache_att_v, ret_w_in, ret_gn_g,
           ret_w_o, att_w_qkv, att_rel_bias, att_w_o, ln1_g, ln1_b, ln2_g, ln2_b, moe_w_grp, moe_b_grp,
           moe_w_exp, moe_b_exp, moe_w_gate, moe_w_up, moe_w_down, ple_w_proj, ple_w_gate):
    n_p, len_p, d_model = x_prompt.shape
    n_s, len_s, _ = x_sample.shape
    depth = ln1_g.shape[0]
    alpha = float((2 * depth) ** 0.25)
    tok_p, tok_s = n_p * len_p, n_s * len_s
    n_all = tok_p + tok_s
    dh = d_model // ATT_HEADS
    pp = p_prompt.reshape(depth, tok_p, -1)
    ps = p_sample.reshape(depth, tok_s, -1)
    nb_s = 4

    x_all = None
    y_prompt = y_sample = None
    states_p, states_s, k_p, v_p, k_s, v_s = [], [], [], [], [], []
    for i in range(depth):
        jj = i // 2
        wr_both, wr_hi, br = _router_weights(moe_w_grp[i], moe_b_grp[i], moe_w_exp[i], moe_b_exp[i])
        lng, lnb = ln1_g[i].reshape(1, d_model), ln1_b[i].reshape(1, d_model)
        moe_cast = (i, [moe_w_gate, moe_w_up, moe_w_down])
        if x_all is None:
            src_p, src_s, row_s = x_prompt.reshape(tok_p, d_model), x_sample.reshape(tok_s, d_model), 0
        else:
            src_p, src_s, row_s = x_all, x_all, tok_p
        if i % 2 == 0:
            w_in = ret_w_in[jj].astype(BF16)
            w_o = ret_w_o[jj].astype(BF16)
            gn = ret_gn_g[jj].reshape(1, -1)
            h, hp, lgt, st_p, *w_moe = _ret_mixer(
                src_p, 0, n_p, len_p, 0, None, w_in, w_o, gn, lng, lnb, wr_both, wr_hi, br, alpha, nb=1,
                blk=min(RET_BLOCK, len_p), chain=RET_CHAIN, n_all=n_all, out_row0=0, dst=None, cast=moe_cast)
            h, hp, lgt, st_s = _ret_mixer(src_s, row_s, n_s, len_s, PAST_LEN, state_ret[jj], w_in, w_o, gn,
                                          lng, lnb, wr_both, wr_hi, br, alpha, nb=nb_s, blk=len_s, chain=1,
                                          n_all=n_all, out_row0=tok_p, dst=(h, hp, lgt))
            states_p.append(st_p)
            states_s.append(st_s)
        else:
            w_qkv = att_w_qkv[jj].astype(BF16)
            w_o = att_w_o[jj].astype(BF16)
            h, hp, lgt, kr, vr, *w_moe = _att_prompt(src_p, n_p, len_p, w_qkv, w_o, att_rel_bias[jj], lng, lnb,
                                                     wr_both, wr_hi, br, alpha, n_all, cast=moe_cast)
            k_p.append(kr.reshape(n_p, -1, ATT_HEADS, dh))
            v_p.append(vr.reshape(n_p, -1, ATT_HEADS, dh))
            h, hp, lgt, kr, vr = _att_sample(src_s, row_s, n_s, len_s, cache_att_k[jj], cache_att_v[jj],
                                             w_qkv, w_o, att_rel_bias[jj], lng, lnb, wr_both, wr_hi, br, alpha,
                                             nb=nb_s, n_all=n_all, out_row0=tok_p, dst=(h, hp, lgt))
            k_s.append(kr.reshape(n_s, len_s, ATT_HEADS, dh))
            v_s.append(vr.reshape(n_s, len_s, ATT_HEADS, dh))
        n_exp = moe_w_gate.shape[1]
        w_moe = [w.reshape(n_exp, -1, w.shape[1]) for w in w_moe]
        yg, meta_w = _moe(hp, lgt, *w_moe)
        last = i == depth - 1
        out = _combine(yg, meta_w, h, pp, ps, i, ln2_g[i].reshape(1, d_model), ln2_b[i].reshape(1, d_model),
                       ple_w_proj[i].astype(BF16), ple_w_gate[i].astype(BF16), alpha, split_out=last)
        if last:
            y_prompt = out[0].reshape(n_p, len_p, d_model)
            y_sample = out[1].reshape(n_s, len_s, d_model)
        else:
            x_all = out

    return (y_prompt, y_sample, jnp.stack(states_p), jnp.stack(states_s),
            jnp.stack(k_p), jnp.stack(v_p), jnp.stack(k_s), jnp.stack(v_s))
```
